```python
import math
import jax
import jax.numpy as jnp
from jax import lax
import numpy as np


D_MODEL = 2048
BATCH = 2
SEQ = 4096
DEPTH = 4

GRID_W = 64
CTX_LEN = 256
N_EVEN = (DEPTH + 1) // 2
N_ODD = DEPTH // 2
EPS = 1e-6

GLA_HEADS = 4
GLA_DK = 128
GLA_DV = 256
GLA_GATE_RANK = 16
GLA_TAU = 16.0
GLA_CHUNK = 64
SC_WIDTH = 1024
DA_HEADS = 8
DA_DQK = 64
DA_DV = 128
ROPE_BASE = 10000.0
ROPE_PAIRS = DA_DQK // 4
Q_BLOCK = 128
ML_HEADS = 4
ML_DK = 128
ML_DV = 256
ML_CHUNK = 64
D_FF = 5632

EVEN_SIZES = (GLA_HEADS * GLA_DK, GLA_HEADS * GLA_DK, GLA_HEADS * GLA_DV, GLA_HEADS * GLA_DV,
              2 * GLA_GATE_RANK, SC_WIDTH, SC_WIDTH, SC_WIDTH)
ODD_SIZES = (DA_HEADS * 2 * DA_DQK, DA_HEADS * 2 * DA_DQK, DA_HEADS * DA_DV,
             ML_HEADS * ML_DK, ML_HEADS * ML_DK, ML_HEADS * ML_DV, ML_HEADS * ML_DV, 4 * ML_HEADS)
EVEN_IN = sum(EVEN_SIZES)
ODD_IN = sum(ODD_SIZES)
EVEN_MIX = GLA_HEADS * GLA_DV + SC_WIDTH
ODD_MIX = DA_HEADS * DA_DV + ML_HEADS * ML_DV

kernel_name = 'hybrid_prefix_diffusion_trunk'


def _rmsnorm(x, g):
    xf = x.astype(jnp.float32)
    y = xf * lax.rsqrt(jnp.mean(xf * xf, axis=-1, keepdims=True) + EPS)
    return (y * g.astype(jnp.float32)).astype(x.dtype)


def _dwconv3(x, w):
    xp = jnp.pad(x, ((0, 0), (1, 1), (0, 0)))
    return xp[:, :-2] * w[0] + xp[:, 1:-1] * w[1] + xp[:, 2:] * w[2]


def _split(a, sizes):
    idx = [int(i) for i in np.cumsum(sizes)[:-1]]
    return jnp.split(a, idx, axis=-1)


def _to_heads(a, h):
    b, n, _ = a.shape
    return a.reshape(b, n, h, -1).transpose(0, 2, 1, 3)


def _from_heads(a):
    b, h, n, d = a.shape
    return a.transpose(0, 2, 1, 3).reshape(b, n, h * d)


def _to_chunks(a, size):
    n = a.shape[2]
    a = a.reshape(a.shape[:2] + (n // size, size) + a.shape[3:])
    return jnp.moveaxis(a, 2, 0).astype(jnp.float32)


def _from_chunks(o):
    o = jnp.moveaxis(o, 0, 2)
    return o.reshape(o.shape[:2] + (-1, o.shape[-1]))


def _gla_scan(q, k, v, g, s0):
    L = GLA_CHUNK
    mask = jnp.tril(jnp.ones((L, L), dtype=bool))[:, :, None]

    def step(S, inp):
        qc, kc, vc, gc = inp
        b = jnp.cumsum(gc, axis=2)
        o_inter = jnp.einsum('bhld,bhde->bhle', qc * jnp.exp(b), S)
        rel = jnp.where(mask, b[:, :, :, None, :] - b[:, :, None, :, :], -jnp.inf)
        A = jnp.einsum('bhijd,bhjd->bhij', qc[:, :, :, None, :] * jnp.exp(rel), kc)
        o_intra = jnp.einsum('bhij,bhje->bhie', A, vc)
        b_last = b[:, :, -1:, :]
        S_new = jnp.exp(b_last[:, :, 0, :])[..., None] * S + jnp.einsum(
            'bhld,bhle->bhde', kc * jnp.exp(b_last - b), vc)
        return S_new, o_inter + o_intra

    S, o = lax.scan(step, s0, tuple(_to_chunks(a, L) for a in (q, k, v, g)))
    return _from_chunks(o), S


def _mlstm_scan(q, k, v, ig, fg, state):
    L = ML_CHUNK
    mask = jnp.tril(jnp.ones((L, L), dtype=bool))

    def step(carry, inp):
        C, n, m = carry
        qc, kc, vc, ic, fc = inp
        b = jnp.cumsum(jax.nn.log_sigmoid(fc), axis=-1)
        a = b + m[..., None]
        dmat = jnp.where(mask, b[..., :, None] - b[..., None, :] + ic[..., None, :], -jnp.inf)
        m_t = jnp.maximum(a, jnp.max(dmat, axis=-1))
        w_inter = jnp.exp(a - m_t)
        s = jnp.einsum('bhtd,bhsd->bhts', qc, kc) * jnp.exp(dmat - m_t[..., None])
        num = w_inter[..., None] * jnp.einsum('bhtd,bhde->bhte', qc, C) + jnp.einsum('bhts,bhse->bhte', s, vc)
        den = w_inter * jnp.einsum('bhtd,bhd->bht', qc, n) + jnp.sum(s, axis=-1)
        h = num / jnp.maximum(jnp.abs(den), jnp.exp(-m_t))[..., None]
        b_last = b[..., -1]
        g_s = b_last[..., None] - b + ic
        m_new = jnp.maximum(b_last + m, jnp.max(g_s, axis=-1))
        carry_decay = jnp.exp(b_last + m - m_new)
        w_s = jnp.exp(g_s - m_new[..., None])
        C_new = carry_decay[..., None, None] * C + jnp.einsum('bhs,bhsd,bhse->bhde', w_s, kc, vc)
        n_new = carry_decay[..., None] * n + jnp.einsum('bhs,bhsd->bhd', w_s, kc)
        return (C_new, n_new, m_new), h

    state, h = lax.scan(step, state, tuple(_to_chunks(a, L) for a in (q, k, v, ig, fg)))
    return _from_chunks(h), state


def _bidir_prefix_scan(scan_fn, init, ctx_fwd, lat_fwd, ctx_bwd, lat_bwd):
    flip = lambda t: tuple(jnp.flip(a, axis=2) for a in t)
    oc_f, st_f = scan_fn(*ctx_fwd, init)
    ol_f, _ = scan_fn(*lat_fwd, st_f)
    oc_b, st_b = scan_fn(*flip(ctx_bwd), init)
    ol_b, _ = scan_fn(*flip(lat_bwd), st_b)
    return oc_f + jnp.flip(oc_b, axis=2), ol_f + jnp.flip(ol_b, axis=2)


def _axial_rope_tables(n_lat):
    rows = n_lat // GRID_W
    row = jnp.repeat(jnp.arange(rows, dtype=jnp.float32), GRID_W)
    col = jnp.tile(jnp.arange(GRID_W, dtype=jnp.float32), rows)
    inv = jnp.power(ROPE_BASE, -jnp.arange(ROPE_PAIRS, dtype=jnp.float32) / ROPE_PAIRS)
    ang_r = row[:, None] * inv
    ang_c = col[:, None] * inv
    return (jnp.cos(ang_r), jnp.sin(ang_r), jnp.cos(ang_c), jnp.sin(ang_c))


def _rope_half(x, cos, sin):
    x1, x2 = jnp.split(x, 2, axis=-1)
    return jnp.concatenate([x1 * cos - x2 * sin, x1 * sin + x2 * cos], axis=-1)


def _axial_rope(x, tabs):
    cr, sr, cc, sc = (t[None, :, None, None, :] for t in tabs)
    xr, xc = jnp.split(x, 2, axis=-1)
    return jnp.concatenate([_rope_half(xr, cr, sr), _rope_half(xc, cc, sc)], axis=-1).astype(x.dtype)


def _diff_softmax(q, k, v, lam):
    s = jnp.einsum('bqhmd,bkhmd->bhmqk', q, k).astype(jnp.float32) * (DA_DQK ** -0.5)
    p = jax.nn.softmax(s, axis=-1)
    w = p[:, :, 0] - lam * p[:, :, 1]
    return jnp.einsum('bhqk,bkhe->bqhe', w.astype(v.dtype), v)


def _even_mixer(hc, hl, w_in, w_out, gate_w2, gate_b, gla_norm_g, sc_conv_w):
    def project(h):
        q, k, v, r, glr, sx, sb, scg = _split(h @ w_in, EVEN_SIZES)
        glr_f, glr_b = jnp.split(glr, 2, axis=-1)
        g_f = jax.nn.log_sigmoid(glr_f @ gate_w2[0] + gate_b[0]) / GLA_TAU
        g_b = jax.nn.log_sigmoid(glr_b @ gate_w2[1] + gate_b[1]) / GLA_TAU
        qkv = (_to_heads(q, GLA_HEADS) * (GLA_DK ** -0.5), _to_heads(k, GLA_HEADS), _to_heads(v, GLA_HEADS))
        conv_out = sb * _dwconv3(scg * sx, sc_conv_w)
        return qkv + (_to_heads(g_f, GLA_HEADS),), qkv + (_to_heads(g_b, GLA_HEADS),), r, conv_out

    fc, bc, rc, cc = project(hc)
    fl, bl, rl, cl = project(hl)
    bsz = hl.shape[0]
    init = jnp.zeros((bsz, GLA_HEADS, GLA_DK, GLA_DV), jnp.float32)
    oc, ol = _bidir_prefix_scan(_gla_scan, init, fc, fl, bc, bl)

    def finish(o, r, conv_out):
        o = _from_heads(_rmsnorm(o.astype(r.dtype), gla_norm_g)) * jax.nn.silu(r)
        return jnp.concatenate([o, conv_out], axis=-1) @ w_out

    return finish(oc, rc, cc), finish(ol, rl, cl)


def _odd_mixer(hc, hl, rope_tabs, layer, w_in, w_out, qn_g, kn_g, lam_p, subln_g,
               ml_conv_w, ml_gate_b, ml_norm_g):
    lam_init = 0.8 - 0.6 * math.exp(-0.3 * layer)
    lam = (jnp.exp(jnp.sum(lam_p[0] * lam_p[1]).astype(jnp.float32))
           - jnp.exp(jnp.sum(lam_p[2] * lam_p[3]).astype(jnp.float32)) + lam_init)

    def project(h, tabs):
        bsz, n, _ = h.shape
        dq, dk, dv, mq, mk, mv, mo, mg = _split(h @ w_in, ODD_SIZES)
        q = _rmsnorm(dq.reshape(bsz, n, DA_HEADS, 2, DA_DQK), qn_g)
        k = _rmsnorm(dk.reshape(bsz, n, DA_HEADS, 2, DA_DQK), kn_g)
        if tabs is not None:
            q = _axial_rope(q, tabs)
            k = _axial_rope(k, tabs)
        v = dv.reshape(bsz, n, DA_HEADS, DA_DV)
        mqk = jax.nn.silu(_dwconv3(jnp.concatenate([mq, mk], axis=-1), ml_conv_w))
        mq, mk = jnp.split(mqk, 2, axis=-1)
        gates = (mg + ml_gate_b).reshape(bsz, n, 2, 2, ML_HEADS)
        gates = jnp.moveaxis(gates, 1, -1).astype(jnp.float32)
        hq, hk, hv = _to_heads(mq, ML_HEADS), _to_heads(mk, ML_HEADS) * (ML_DK ** -0.5), _to_heads(mv, ML_HEADS)
        fwd = (hq, hk, hv, gates[:, 0, 0], gates[:, 0, 1])
        bwd = (hq, hk, hv, gates[:, 1, 0], gates[:, 1, 1])
        return (q, k, v), fwd, bwd, mo

    (qc, kc, vc), fc, bc, moc = project(hc, None)
    (ql, kl, vl), fl, bl, mol = project(hl, rope_tabs)

    da_c = _diff_softmax(qc, kc, vc, lam)
    bsz, n_lat = ql.shape[:2]
    k_all = jnp.concatenate([kc, kl], axis=1)
    v_all = jnp.concatenate([vc, vl], axis=1)
    qb = jnp.moveaxis(ql.reshape((bsz, n_lat // Q_BLOCK, Q_BLOCK) + ql.shape[2:]), 1, 0)
    da_l = lax.map(lambda qq: _diff_softmax(qq, k_all, v_all, lam), qb)
    da_l = jnp.moveaxis(da_l, 0, 1).reshape(bsz, n_lat, DA_HEADS, DA_DV)

    init = (jnp.zeros((bsz, ML_HEADS, ML_DK, ML_DV), jnp.float32),
            jnp.zeros((bsz, ML_HEADS, ML_DK), jnp.float32),
            jnp.zeros((bsz, ML_HEADS), jnp.float32))
    mc, ml = _bidir_prefix_scan(_mlstm_scan, init, fc, fl, bc, bl)

    def finish(da, m, mo):
        b_, n_ = da.shape[:2]
        da = (_rmsnorm(da, subln_g) * (1.0 - lam_init)).reshape(b_, n_, DA_HEADS * DA_DV)
        m = _from_heads(_rmsnorm(m.astype(mo.dtype), ml_norm_g)) * jax.nn.sigmoid(mo)
        return jnp.concatenate([da, m], axis=-1) @ w_out

    return finish(da_c, mc, moc), finish(da_l, ml, mol)


def _conv_ffn(h, w_up, conv_w, conv_b, w_down):
    gate, val = jnp.split(h @ w_up, 2, axis=-1)
    return (jax.nn.silu(_dwconv3(gate, conv_w) + conv_b) * val) @ w_down


def setup_inputs(seed: int = 0) -> dict:
    key = jax.random.key(seed)
    ks = iter(jax.random.split(key, 40))
    f32 = jnp.float32
    D = D_MODEL
    nrm = lambda shape, scale: scale * jax.random.normal(next(ks), shape, f32)
    gain = lambda shape: 1.0 + 0.02 * jax.random.normal(next(ks), shape, f32)
    ib = 0.1 * jax.random.normal(next(ks), (N_ODD, 2, 1, ML_HEADS), f32)
    fb = jax.random.uniform(next(ks), (N_ODD, 2, 1, ML_HEADS), f32, 3.0, 6.0)
    return {
        'x': nrm((BATCH, SEQ, D), 1.0),
        'c': nrm((BATCH, D), 1.0),
        'ctx': nrm((BATCH, CTX_LEN, D), 1.0),
        'c_ctx': nrm((D,), 1.0),
        'ada_w': nrm((DEPTH, D, 6 * D), 0.3 * D ** -0.5),
        'ada_b': nrm((DEPTH, 6 * D), 0.02),
        'norm1_g': gain((DEPTH, D)),
        'norm2_g': gain((DEPTH, D)),
        'ev_w_in': nrm((N_EVEN, D, EVEN_IN), D ** -0.5),
        'ev_w_out': nrm((N_EVEN, EVEN_MIX, D), EVEN_MIX ** -0.5),
        'gla_gate_w2': nrm((N_EVEN, 2, GLA_GATE_RANK, GLA_HEADS * GLA_DK), GLA_GATE_RANK ** -0.5),
        'gla_gate_b': nrm((N_EVEN, 2, GLA_HEADS * GLA_DK), 0.1),
        'gla_norm_g': gain((N_EVEN, GLA_DV)),
        'sc_conv_w': nrm((N_EVEN, 3, SC_WIDTH), 3 ** -0.5),
        'od_w_in': nrm((N_ODD, D, ODD_IN), D ** -0.5),
        'od_w_out': nrm((N_ODD, ODD_MIX, D), ODD_MIX ** -0.5),
        'da_qnorm_g': gain((N_ODD, DA_DQK)),
        'da_knorm_g': gain((N_ODD, DA_DQK)),
        'da_lambda': nrm((N_ODD, 4, DA_DQK), 0.1),
        'da_subln_g': gain((N_ODD, DA_DV)),
        'ml_conv_w': nrm((N_ODD, 3, 2 * ML_HEADS * ML_DK), 3 ** -0.5),
        'ml_gate_b': jnp.concatenate([ib, fb], axis=2).reshape(N_ODD, 4 * ML_HEADS),
        'ml_norm_g': gain((N_ODD, ML_DV)),
        'ffn_w_up': nrm((DEPTH, D, 2 * D_FF), D ** -0.5),
        'ffn_conv_w': nrm((DEPTH, 3, D_FF), 3 ** -0.5),
        'ffn_conv_b': nrm((DEPTH, D_FF), 0.02),
        'ffn_w_down': nrm((DEPTH, D_FF, D), D_FF ** -0.5),
    }


def reference(x, c, ctx, c_ctx, ada_w, ada_b, norm1_g, norm2_g, ev_w_in, ev_w_out, gla_gate_w2,
              gla_gate_b, gla_norm_g, sc_conv_w, od_w_in, od_w_out, da_qnorm_g, da_knorm_g, da_lambda,
              da_subln_g, ml_conv_w, ml_gate_b, ml_norm_g, ffn_w_up, ffn_conv_w, ffn_conv_b, ffn_w_down):
    n_lat = x.shape[1]
    rope_tabs = _axial_rope_tables(n_lat)
    xl, xc = x, ctx
    for layer in range(DEPTH):
        last = layer == DEPTH - 1
        mod_l = (jax.nn.silu(c) @ ada_w[layer] + ada_b[layer])[:, None, :]
        mod_c = (jax.nn.silu(c_ctx) @ ada_w[layer] + ada_b[layer])[None, None, :]
        sh1_l, sc1_l, g1_l, sh2_l, sc2_l, g2_l = jnp.split(mod_l, 6, axis=-1)
        sh1_c, sc1_c, g1_c, sh2_c, sc2_c, g2_c = jnp.split(mod_c, 6, axis=-1)

        hl = _rmsnorm(xl, norm1_g[layer]) * (1.0 + sc1_l) + sh1_l
        hc = _rmsnorm(xc, norm1_g[layer]) * (1.0 + sc1_c) + sh1_c
        if layer % 2 == 0:
            i = layer // 2
            oc, ol = _even_mixer(hc, hl, ev_w_in[i], ev_w_out[i], gla_gate_w2[i], gla_gate_b[i],
                                 gla_norm_g[i], sc_conv_w[i])
        else:
            i = layer // 2
            oc, ol = _odd_mixer(hc, hl, rope_tabs, layer, od_w_in[i], od_w_out[i], da_qnorm_g[i],
                                da_knorm_g[i], da_lambda[i], da_subln_g[i], ml_conv_w[i],
                                ml_gate_b[i], ml_norm_g[i])
        xl = xl + g1_l * ol
        hl = _rmsnorm(xl, norm2_g[layer]) * (1.0 + sc2_l) + sh2_l
        xl = xl + g2_l * _conv_ffn(hl, ffn_w_up[layer], ffn_conv_w[layer], ffn_conv_b[layer], ffn_w_down[layer])
        if not last:
            xc = xc + g1_c * oc
            hc = _rmsnorm(xc, norm2_g[layer]) * (1.0 + sc2_c) + sh2_c
            xc = xc + g2_c * _conv_ffn(hc, ffn_w_up[layer], ffn_conv_w[layer], ffn_conv_b[layer], ffn_w_down[layer])
    return xl
```

```python
import functools
import math

import jax
import jax.numpy as jnp
import numpy as np
from jax import lax
from jax.experimental import pallas as pl
from jax.experimental.pallas import tpu as pltpu

GRID_W = 64
CTX_LEN = 256
EPS = 1e-6
GLA_HEADS, GLA_DK, GLA_DV, GLA_GATE_RANK, GLA_TAU = 4, 128, 256, 16, 16.0
SC_WIDTH = 1024
DA_HEADS, DA_DQK, DA_DV = 8, 64, 128
ROPE_BASE = 10000.0
ROPE_PAIRS = DA_DQK // 4
ML_HEADS, ML_DK, ML_DV = 4, 128, 256

GLA_CHUNK = 128
ML_CHUNK = 256
ATTN_BQ = 256
LANES = 128
VMEM_LIMIT = 56 * 1024 * 1024

F32 = jnp.float32
BF16 = jnp.bfloat16
HI = lax.Precision.HIGHEST


def _mm_kernel(x_ref, w_ref, o_ref):
    o_ref[...] = jnp.dot(x_ref[...], w_ref[...], preferred_element_type=F32)


def _pick_block(n, target):
    b = min(n, target)
    while n % b:
        b -= LANES
    return b


def _matmul(x, w, *, bm=1088, bn=1024):
    m, k = x.shape
    n = w.shape[1]
    bm = bm if m % bm == 0 else m
    bn = _pick_block(n, bn)
    return pl.pallas_call(
        _mm_kernel,
        grid=(m // bm, n // bn),
        in_specs=[pl.BlockSpec((bm, k), lambda i, j: (i, 0)),
                  pl.BlockSpec((k, bn), lambda i, j: (0, j))],
        out_specs=pl.BlockSpec((bm, bn), lambda i, j: (i, j)),
        out_shape=jax.ShapeDtypeStruct((m, n), F32),
        compiler_params=pltpu.CompilerParams(
            dimension_semantics=("parallel", "arbitrary"), vmem_limit_bytes=VMEM_LIMIT),
        name="matmul",
    )(x, w)


def _attn_kernel(lam_ref, q_ref, k_ref, v_ref, o_ref, *, n_ctx):
    qi = pl.program_id(2)
    lam = lam_ref[0]

    def attend(nk):
        q = q_ref[0]
        bq = q.shape[0]
        lane = lax.broadcasted_iota(jnp.int32, q.shape, 1)
        zero = jnp.zeros_like(q)
        q2 = jnp.concatenate([jnp.where(lane < DA_DQK, q, zero),
                              jnp.where(lane >= DA_DQK, q, zero)], axis=0)
        k = k_ref[0, :nk, :]
        v = v_ref[0, :nk, :]
        s = lax.dot_general(q2, k, (((1,), (1,)), ((), ())), preferred_element_type=F32)
        m = jnp.max(s, axis=-1, keepdims=True)
        p = jnp.exp(s - m)
        l = jnp.sum(p, axis=-1, keepdims=True)
        o = jnp.dot(p.astype(BF16), v, preferred_element_type=F32) / l
        o_ref[0] = o[:bq] - lam * o[bq:]

    n_ctx_blocks = n_ctx // q_ref.shape[1]

    @pl.when(qi < n_ctx_blocks)
    def _():
        attend(n_ctx)

    @pl.when(qi >= n_ctx_blocks)
    def _():
        attend(k_ref.shape[1])


def _diff_attention(q, k, v, lam, *, n_ctx, bq=ATTN_BQ):
    b, n, hd = q.shape
    h = hd // LANES
    return pl.pallas_call(
        functools.partial(_attn_kernel, n_ctx=n_ctx),
        grid=(b, h, n // bq),
        in_specs=[pl.BlockSpec(memory_space=pltpu.SMEM),
                  pl.BlockSpec((1, bq, LANES), lambda bi, hi, qi: (bi, qi, hi)),
                  pl.BlockSpec((1, n, LANES), lambda bi, hi, qi: (bi, 0, hi)),
                  pl.BlockSpec((1, n, LANES), lambda bi, hi, qi: (bi, 0, hi))],
        out_specs=pl.BlockSpec((1, bq, LANES), lambda bi, hi, qi: (bi, qi, hi)),
        out_shape=jax.ShapeDtypeStruct((b, n, hd), F32),
        compiler_params=pltpu.CompilerParams(
            dimension_semantics=("parallel", "parallel", "arbitrary"), vmem_limit_bytes=VMEM_LIMIT),
        name="diff_attention",
    )(lam, q, k, v)


def _scan_chunk(t, n_ctx_chunks, n_chunks, reverse):
    if not reverse:
        return t
    return jnp.where(t < n_ctx_chunks, n_ctx_chunks - 1 - t, n_chunks - 1 - (t - n_ctx_chunks))


def _gla_constants(L, reverse):
    nlev = int(math.log2(L))
    idx = np.arange(L)
    i, t = idx[:, None], idx[None, :]
    if reverse:
        i, t = L - 1 - i, L - 1 - t
    rs = [(t <= i), (t > i)]
    am = [(i == t)]
    for lev in range(nlev):
        m = L >> (lev + 1)
        blk_i, blk_t = i // (2 * m), t // (2 * m)
        mid = blk_i * 2 * m + m
        q_role = i >= mid
        rs.append(np.where(q_role, (t >= mid) & (t <= i), (t > i) & (t < mid)) & (blk_i == blk_t))
        am.append((blk_i == blk_t) & q_role & (t < mid))
    return (np.stack(rs).astype(np.float32).reshape((nlev + 2) * L, L),
            np.stack(am).astype(np.float32))


def _gla_kernel(q_ref, k_ref, v_ref, glr_ref, w2_ref, gb_ref, rsum_ref, amask_ref, o_ref, st_ref):
    L = q_ref.shape[1]
    nlev = amask_ref.shape[0] - 1

    @pl.when(pl.program_id(1) == 0)
    def _():
        st_ref[...] = jnp.zeros_like(st_ref)

    z = jnp.dot(glr_ref[0], w2_ref[...], preferred_element_type=F32) + gb_ref[...]
    g = (jnp.minimum(z, 0.0) - jnp.log(1.0 + jnp.exp(-jnp.abs(z)))) * (1.0 / GLA_TAU)
    e_all = jnp.dot(rsum_ref[...], g, precision=HI, preferred_element_type=F32)
    b_tot = jnp.sum(g, axis=0, keepdims=True)

    nt = (((1,), (1,)), ((), ()))
    for h in range(GLA_HEADS):
        ck = slice(h * GLA_DK, (h + 1) * GLA_DK)
        cv = slice(h * GLA_DV, (h + 1) * GLA_DV)
        q = q_ref[0, :, ck].astype(F32)
        k = k_ref[0, :, ck].astype(F32)
        v = v_ref[0, :, cv]
        st = st_ref[h]
        a = amask_ref[0] * lax.dot_general(q.astype(BF16), k.astype(BF16), nt, preferred_element_type=F32)
        for lev in range(nlev):
            e = jnp.exp(e_all[(2 + lev) * L:(3 + lev) * L, ck])
            a = a + amask_ref[1 + lev] * lax.dot_general(
                (q * e).astype(BF16), (k * e).astype(BF16), nt, preferred_element_type=F32)
        qe = (q * jnp.exp(e_all[0:L, ck])).astype(BF16)
        o = lax.dot_general(qe, st.astype(BF16), nt, preferred_element_type=F32)
        o = o + jnp.dot(a.astype(BF16), v, preferred_element_type=F32)
        o_ref[0, :, cv] = o
        kd = (k * jnp.exp(e_all[L:2 * L, ck])).astype(BF16)
        st_ref[h] = jnp.exp(b_tot[:, ck]) * st + jnp.dot(
            v.T, kd, preferred_element_type=F32)


def _gla_scan(q, k, v, glr, w2, gb, *, n_ctx, reverse, L=GLA_CHUNK):
    b, n, _ = q.shape
    nc, ncc = n // L, n_ctx // L
    rsum, amask = _gla_constants(L, reverse)
    chunk = functools.partial(_scan_chunk, n_ctx_chunks=ncc, n_chunks=nc, reverse=reverse)
    row = lambda bi, t: (bi, chunk(t), 0)
    const2 = lambda bi, t: (0, 0)
    const3 = lambda bi, t: (0, 0, 0)
    hk, hv = q.shape[2], v.shape[2]
    return pl.pallas_call(
        _gla_kernel,
        grid=(b, nc),
        in_specs=[pl.BlockSpec((1, L, hk), row), pl.BlockSpec((1, L, hk), row),
                  pl.BlockSpec((1, L, hv), row), pl.BlockSpec((1, L, LANES), row),
                  pl.BlockSpec(w2.shape, const2), pl.BlockSpec(gb.shape, const2),
                  pl.BlockSpec(rsum.shape, const2), pl.BlockSpec(amask.shape, const3)],
        out_specs=pl.BlockSpec((1, L, hv), row),
        out_shape=jax.ShapeDtypeStruct((b, n, hv), F32),
        scratch_shapes=[pltpu.VMEM((GLA_HEADS, GLA_DV, GLA_DK), F32)],
        compiler_params=pltpu.CompilerParams(
            dimension_semantics=("parallel", "arbitrary"), vmem_limit_bytes=VMEM_LIMIT),
        name="gla_scan_bwd" if reverse else "gla_scan_fwd",
    )(q, k, v, glr, w2, gb, jnp.asarray(rsum), jnp.asarray(amask))


def _log_sigmoid(x):
    return jnp.minimum(x, 0.0) - jnp.log(1.0 + jnp.exp(-jnp.abs(x)))


def _mlstm_kernel(q_ref, k_ref, v_ref, gc_ref, gr_ref, tri_ref, o_ref, c_ref, n_ref, m_ref, *, reverse):
    L = q_ref.shape[1]
    H = ML_HEADS

    @pl.when(pl.program_id(1) == 0)
    def _():
        c_ref[...] = jnp.zeros_like(c_ref)
        n_ref[...] = jnp.zeros_like(n_ref)
        m_ref[...] = jnp.zeros_like(m_ref)

    tri = tri_ref[...]
    gc = gc_ref[0]
    gr = gr_ref[0]
    ic_col, ic_row = gc[:, :H], gr[:H, :]
    b_col = jnp.dot(tri, _log_sigmoid(gc[:, H:]), precision=HI, preferred_element_type=F32)
    b_row = lax.dot_general(_log_sigmoid(gr[H:, :]), tri, (((1,), (1,)), ((), ())),
                            precision=HI, preferred_element_type=F32)
    last = 0 if reverse else L - 1
    causal = tri > 0.5
    nt = (((1,), (1,)), ((), ()))

    for h in range(H):
        ck = slice(h * ML_DK, (h + 1) * ML_DK)
        cv = slice(h * ML_DV, (h + 1) * ML_DV)
        q = q_ref[0, :, ck]
        k = k_ref[0, :, ck]
        v = v_ref[0, :, cv]
        c, n, m = c_ref[h], n_ref[h], m_ref[h]
        bc, br = b_col[:, h:h + 1], b_row[h:h + 1, :]
        icc, icr = ic_col[:, h:h + 1], ic_row[h:h + 1, :]
        b_last = bc[last:last + 1, :]

        a = bc + m
        dmat = jnp.where(causal, bc - br + icr, -jnp.inf)
        m_t = jnp.maximum(a, jnp.max(dmat, axis=-1, keepdims=True))
        w_inter = jnp.exp(a - m_t)
        s = lax.dot_general(q, k, nt, preferred_element_type=F32) * jnp.exp(dmat - m_t)
        num = w_inter * jnp.dot(q, c.astype(BF16), preferred_element_type=F32) + jnp.dot(
            s.astype(BF16), v, preferred_element_type=F32)
        den = w_inter * jnp.sum(q.astype(F32) * n, axis=-1, keepdims=True) + jnp.sum(
            s, axis=-1, keepdims=True)
        o_ref[0, :, cv] = num / jnp.maximum(jnp.abs(den), jnp.exp(-m_t))

        gs_col = b_last - bc + icc
        gs_row = b_last - br + icr
        m_new = jnp.maximum(b_last + m, jnp.max(gs_row, axis=-1, keepdims=True))
        decay = jnp.exp(b_last + m - m_new)
        wk = jnp.exp(gs_col - m_new) * k.astype(F32)
        c_ref[h] = decay * c + jnp.dot(wk.astype(BF16).T, v, preferred_element_type=F32)
        n_ref[h] = decay * n + jnp.sum(wk, axis=0, keepdims=True)
        m_ref[h] = m_new


def _mlstm_scan(q, k, v, gates, *, n_ctx, reverse, L=ML_CHUNK):
    b, n, hk = q.shape
    hv = v.shape[2]
    nc, ncc = n // L, n_ctx // L
    idx = np.arange(L)
    tri = (idx[None, :] >= idx[:, None]) if reverse else (idx[None, :] <= idx[:, None])
    tri = jnp.asarray(tri.astype(np.float32))
    gates_t = jnp.swapaxes(gates, 1, 2)
    chunk = functools.partial(_scan_chunk, n_ctx_chunks=ncc, n_chunks=nc, reverse=reverse)
    row = lambda bi, t: (bi, chunk(t), 0)
    return pl.pallas_call(
        functools.partial(_mlstm_kernel, reverse=reverse),
        grid=(b, nc),
        in_specs=[pl.BlockSpec((1, L, hk), row), pl.BlockSpec((1, L, hk), row),
                  pl.BlockSpec((1, L, hv), row), pl.BlockSpec((1, L, 2 * ML_HEADS), row),
                  pl.BlockSpec((1, 2 * ML_HEADS, L), lambda bi, t: (bi, 0, chunk(t))),
                  pl.BlockSpec((L, L), lambda bi, t: (0, 0))],
        out_specs=pl.BlockSpec((1, L, hv), row),
        out_shape=jax.ShapeDtypeStruct((b, n, hv), F32),
        scratch_shapes=[pltpu.VMEM((ML_HEADS, ML_DK, ML_DV), F32),
                        pltpu.VMEM((ML_HEADS, 1, ML_DK), F32),
                        pltpu.VMEM((ML_HEADS, 1, 1), F32)],
        compiler_params=pltpu.CompilerParams(
            dimension_semantics=("parallel", "arbitrary"), vmem_limit_bytes=VMEM_LIMIT),
        name="mlstm_scan_bwd" if reverse else "mlstm_scan_fwd",
    )(q, k, v, gates, gates_t, tri)


def _rmsnorm(x, g):
    return x * lax.rsqrt(jnp.mean(x * x, axis=-1, keepdims=True) + EPS) * g


def _silu(x):
    return x * jax.nn.sigmoid(x)


def _dwconv3(x, w, first, last):
    zero = jnp.zeros_like(x[:, :1])
    prev = jnp.concatenate([zero, x[:, :-1]], axis=1)
    nxt = jnp.concatenate([x[:, 1:], zero], axis=1)
    prev = jnp.where(first[None, :, None], 0.0, prev)
    nxt = jnp.where(last[None, :, None], 0.0, nxt)
    return prev * w[0] + x * w[1] + nxt * w[2]


def _rope_tables(n_lat, n_ctx):
    rows = n_lat // GRID_W
    row = jnp.repeat(jnp.arange(rows, dtype=F32), GRID_W)
    col = jnp.tile(jnp.arange(GRID_W, dtype=F32), rows)
    inv = jnp.power(ROPE_BASE, -jnp.arange(ROPE_PAIRS, dtype=F32) / ROPE_PAIRS)
    ang_r, ang_c = row[:, None] * inv, col[:, None] * inv
    cos = jnp.concatenate([jnp.cos(ang_r)] * 2 + [jnp.cos(ang_c)] * 2, axis=-1)
    sin = jnp.concatenate([-jnp.sin(ang_r), jnp.sin(ang_r), -jnp.sin(ang_c), jnp.sin(ang_c)], axis=-1)
    pad = ((n_ctx, 0), (0, 0))
    return jnp.pad(cos, pad, constant_values=1.0), jnp.pad(sin, pad)


def _rope(x, cos, sin):
    p = ROPE_PAIRS
    xs = [x[..., i * p:(i + 1) * p] for i in range(4)]
    swapped = jnp.concatenate([xs[1], xs[0], xs[3], xs[2]], axis=-1)
    return x * cos[None, :, None, None, :] + swapped * sin[None, :, None, None, :]


def _even_mixer(h, first, last, w_in, w_out, gate_w2, gate_b, gla_norm_g, sc_conv_w, n_ctx):
    b, n, d = h.shape
    hb = h.reshape(b * n, d).astype(BF16)
    nq = GLA_HEADS * GLA_DK
    nv = GLA_HEADS * GLA_DV
    r2 = 2 * GLA_GATE_RANK
    off = np.cumsum([0, nq, nq, nv, nv, r2, SC_WIDTH, SC_WIDTH, SC_WIDTH])
    w_main = jnp.concatenate([w_in[:, :off[4]], w_in[:, off[5]:]], axis=1).astype(BF16)
    w_glr = jnp.pad(w_in[:, off[4]:off[5]], ((0, 0), (0, LANES - r2))).astype(BF16)
    p = _matmul(hb, w_main).reshape(b, n, -1)
    glr = _matmul(hb, w_glr).reshape(b, n, LANES).astype(BF16)
    q = (p[..., :nq] * (GLA_DK ** -0.5)).astype(BF16)
    k = p[..., nq:2 * nq].astype(BF16)
    v = p[..., 2 * nq:2 * nq + nv].astype(BF16)
    r = p[..., 2 * nq + nv:2 * nq + 2 * nv]
    sx, sb, scg = jnp.split(p[..., 2 * nq + 2 * nv:], 3, axis=-1)

    outs = []
    for direction in range(2):
        w2 = jnp.zeros((LANES, nq), F32).at[
            direction * GLA_GATE_RANK:(direction + 1) * GLA_GATE_RANK].set(gate_w2[direction])
        outs.append(_gla_scan(q, k, v, glr, w2.astype(BF16), gate_b[direction][None, :],
                              n_ctx=n_ctx, reverse=bool(direction)))
    o = (outs[0] + outs[1]).reshape(b, n, GLA_HEADS, GLA_DV)
    o = _rmsnorm(o, gla_norm_g).reshape(b, n, nv) * _silu(r)
    conv_out = sb * _dwconv3(scg * sx, sc_conv_w, first, last)
    mix = jnp.concatenate([o, conv_out], axis=-1).reshape(b * n, -1).astype(BF16)
    return _matmul(mix, w_out.astype(BF16)).reshape(b, n, d)


def _odd_mixer(h, first, last, rope, layer, w_in, w_out, qn_g, kn_g, lam_p, subln_g,
               ml_conv_w, ml_gate_b, ml_norm_g, n_ctx):
    b, n, d = h.shape
    lam_init = 0.8 - 0.6 * math.exp(-0.3 * layer)
    lam = (jnp.exp(jnp.sum(lam_p[0] * lam_p[1])) - jnp.exp(jnp.sum(lam_p[2] * lam_p[3])) + lam_init)
    hb = h.reshape(b * n, d).astype(BF16)
    na = DA_HEADS * 2 * DA_DQK
    nav = DA_HEADS * DA_DV
    nk = ML_HEADS * ML_DK
    nv = ML_HEADS * ML_DV
    ng = 4 * ML_HEADS
    n_main = 2 * na + nav + 2 * nk + 2 * nv
    w_main = w_in[:, :n_main].astype(BF16)
    w_gate = jnp.pad(w_in[:, n_main:], ((0, 0), (0, LANES - ng))).astype(BF16)
    p = _matmul(hb, w_main).reshape(b, n, -1)
    mg = _matmul(hb, w_gate).reshape(b, n, LANES)[..., :ng]
    dq, dk, dv, mq, mk, mv, mo = jnp.split(p, list(np.cumsum([na, na, nav, nk, nk, nv])), axis=-1)

    cos, sin = rope
    q = _rope(_rmsnorm(dq.reshape(b, n, DA_HEADS, 2, DA_DQK), qn_g), cos, sin)
    k = _rope(_rmsnorm(dk.reshape(b, n, DA_HEADS, 2, DA_DQK), kn_g), cos, sin)
    q = (q * (DA_DQK ** -0.5)).reshape(b, n, na).astype(BF16)
    k = k.reshape(b, n, na).astype(BF16)
    da = _diff_attention(q, k, dv.astype(BF16), lam.reshape(1).astype(F32), n_ctx=n_ctx)

    mqk = _silu(_dwconv3(jnp.concatenate([mq, mk], axis=-1), ml_conv_w, first, last))
    hq = mqk[..., :nk].astype(BF16)
    hk = (mqk[..., nk:] * (ML_DK ** -0.5)).astype(BF16)
    hv = mv.astype(BF16)
    gates = mg + ml_gate_b
    outs = []
    for direction in range(2):
        g_dir = gates[..., direction * 2 * ML_HEADS:(direction + 1) * 2 * ML_HEADS]
        outs.append(_mlstm_scan(hq, hk, hv, g_dir, n_ctx=n_ctx, reverse=bool(direction)))
    m = (outs[0] + outs[1]).reshape(b, n, ML_HEADS, ML_DV)
    m = _rmsnorm(m, ml_norm_g).reshape(b, n, nv) * jax.nn.sigmoid(mo)
    da = (_rmsnorm(da.reshape(b, n, DA_HEADS, DA_DV), subln_g) * (1.0 - lam_init)).reshape(b, n, nav)
    mix = jnp.concatenate([da, m], axis=-1).reshape(b * n, -1).astype(BF16)
    return _matmul(mix, w_out.astype(BF16)).reshape(b, n, d)


def _conv_ffn(h, first, last, w_up, conv_w, conv_b, w_down):
    b, n, d = h.shape
    up = _matmul(h.reshape(b * n, d).astype(BF16), w_up.astype(BF16)).reshape(b, n, -1)
    gate, val = jnp.split(up, 2, axis=-1)
    a = _silu(_dwconv3(gate, conv_w, first, last) + conv_b) * val
    return _matmul(a.reshape(b * n, -1).astype(BF16), w_down.astype(BF16), bn=512).reshape(b, n, d)


def kernel(x, c, ctx, c_ctx, ada_w, ada_b, norm1_g, norm2_g, ev_w_in, ev_w_out, gla_gate_w2, gla_gate_b, gla_norm_g, sc_conv_w, od_w_in, od_w_out, da_qnorm_g, da_knorm_g, da_lambda, da_subln_g, ml_conv_w, ml_gate_b, ml_norm_g, ffn_w_up, ffn_conv_w, ffn_conv_b, ffn_w_down):
    bsz, n_lat, d = x.shape
    n_ctx = ctx.shape[1]
    n = n_ctx + n_lat
    depth = ada_w.shape[0]
    pos = jnp.arange(n)
    is_ctx = (pos < n_ctx)[None, :, None]
    first = (pos == 0) | (pos == n_ctx)
    last = (pos == n_ctx - 1) | (pos == n - 1)
    rope = _rope_tables(n_lat, n_ctx)

    xs = jnp.concatenate([ctx, x], axis=1)
    for layer in range(depth):
        mod_l = (_silu(c) @ ada_w[layer] + ada_b[layer])[:, None, :]
        mod_c = (_silu(c_ctx) @ ada_w[layer] + ada_b[layer])[None, None, :]
        sh1, sc1, g1, sh2, sc2, g2 = (jnp.where(is_ctx, mc, ml) for mc, ml in zip(
            jnp.split(mod_c, 6, axis=-1), jnp.split(mod_l, 6, axis=-1)))
        h = _rmsnorm(xs, norm1_g[layer]) * (1.0 + sc1) + sh1
        i = layer // 2
        if layer % 2 == 0:
            o = _even_mixer(h, first, last, ev_w_in[i], ev_w_out[i], gla_gate_w2[i], gla_gate_b[i],
                            gla_norm_g[i], sc_conv_w[i], n_ctx)
        else:
            o = _odd_mixer(h, first, last, rope, layer, od_w_in[i], od_w_out[i], da_qnorm_g[i],
                           da_knorm_g[i], da_lambda[i], da_subln_g[i], ml_conv_w[i], ml_gate_b[i],
                           ml_norm_g[i], n_ctx)
        xs = xs + g1 * o
        h = _rmsnorm(xs, norm2_g[layer]) * (1.0 + sc2) + sh2
        xs = xs + g2 * _conv_ffn(h, first, last, ffn_w_up[layer], ffn_conv_w[layer],
                                 ffn_conv_b[layer], ffn_w_down[layer])
    return xs[:, n_ctx:, :]
```

```python
import functools
import math

import jax
import jax.numpy as jnp
import numpy as np
from jax import lax
from jax.experimental import pallas as pl
from jax.experimental.pallas import tpu as pltpu

GRID_W = 64
CTX_LEN = 256
EPS = 1e-6
GLA_HEADS, GLA_DK, GLA_DV, GLA_GATE_RANK, GLA_TAU = 4, 128, 256, 16, 16.0
SC_WIDTH = 1024
DA_HEADS, DA_DQK, DA_DV = 8, 64, 128
ROPE_BASE = 10000.0
ROPE_PAIRS = DA_DQK // 4
ML_HEADS, ML_DK, ML_DV = 4, 128, 256

GLA_CHUNK = 128
ML_CHUNK = 256
ATTN_BQ = 256
ATTN_CHUNK = 512
ROW_TILE = 1088
ROW_CHUNK = 272
LANES = 128
SUBLANES = 8
LOG2E = 1.4426950408889634
VMEM_LIMIT = 56 * 1024 * 1024

F32 = jnp.float32
BF16 = jnp.bfloat16
HI = lax.Precision.HIGHEST


def _mm_kernel(x_ref, w_ref, o_ref, wb_ref):
    @pl.when(pl.program_id(1) == 0)
    def _():
        wb_ref[...] = w_ref[...].astype(BF16)

    o_ref[...] = jnp.dot(x_ref[...], wb_ref[...], preferred_element_type=F32)


def _matmul(x, w, *, bn, bm=ROW_TILE):
    m, k = x.shape
    n = w.shape[1]
    return pl.pallas_call(
        _mm_kernel,
        grid=(pl.cdiv(n, bn), m // bm),
        in_specs=[pl.BlockSpec((bm, k), lambda j, i: (i, 0)),
                  pl.BlockSpec((k, bn), lambda j, i: (0, j))],
        out_specs=pl.BlockSpec((bm, bn), lambda j, i: (i, j)),
        out_shape=jax.ShapeDtypeStruct((m, n), F32),
        scratch_shapes=[pltpu.VMEM((k, bn), BF16)],
        compiler_params=pltpu.CompilerParams(
            dimension_semantics=("arbitrary", "arbitrary"), vmem_limit_bytes=VMEM_LIMIT),
        name="matmul",
    )(x, w)


def _norm_mod(x, gain, scale, shift):
    return x * lax.rsqrt(jnp.mean(x * x, axis=-1, keepdims=True) + EPS) * gain * (1.0 + scale) + shift


def _mod_row(is_ctx, modb_ref, modc_ref, idx):
    return jnp.where(is_ctx, modc_ref[0, idx:idx + 1, :], modb_ref[0, idx:idx + 1, :])


def _tile_rows(bm, tiles_per_batch):
    rows = lax.broadcasted_iota(jnp.int32, (bm, 1), 0)
    return rows, (pl.program_id(0) % tiles_per_batch) * bm + rows


def _norm_kernel(x_ref, gain_ref, modb_ref, modc_ref, o_ref, *, n_ctx, tiles_per_batch):
    _, pos = _tile_rows(x_ref.shape[0], tiles_per_batch)
    is_ctx = pos < n_ctx
    o_ref[...] = _norm_mod(x_ref[...], gain_ref[...], _mod_row(is_ctx, modb_ref, modc_ref, 1),
                           _mod_row(is_ctx, modb_ref, modc_ref, 0)).astype(BF16)


def _chunk_rows(r0, rc, tiles_per_batch, bm):
    rows = r0 + lax.broadcasted_iota(jnp.int32, (rc, 1), 0)
    return rows, (pl.program_id(0) % tiles_per_batch) * bm + rows


def _project_rows(rs, a, wb_ref, xo_ref, hn_ref, x_ref, modb_ref, modc_ref, gain_ref, nmodb_ref,
                  nmodc_ref, is_ctx, gate_idx, shift_idx):
    xo_ref[rs, :] += jnp.dot(a, wb_ref[...], preferred_element_type=F32)

    @pl.when(pl.program_id(1) == pl.num_programs(1) - 1)
    def _():
        x_new = x_ref[rs, :] + _mod_row(is_ctx, modb_ref, modc_ref, gate_idx) * xo_ref[rs, :]
        xo_ref[rs, :] = x_new
        hn_ref[rs, :] = _norm_mod(
            x_new, gain_ref[...], _mod_row(is_ctx, nmodb_ref, nmodc_ref, shift_idx + 1),
            _mod_row(is_ctx, nmodb_ref, nmodc_ref, shift_idx)).astype(BF16)


def _start_k_step(w_ref, wb_ref, xo_ref):
    wb_ref[...] = w_ref[...].astype(BF16)

    @pl.when(pl.program_id(1) == 0)
    def _():
        xo_ref[...] = jnp.zeros_like(xo_ref)


def _out_kernel(a_ref, w_ref, x_ref, modb_ref, modc_ref, gain_ref, xo_ref, hn_ref, wb_ref, *,
                n_ctx, tiles_per_batch):
    bm = x_ref.shape[0]
    _start_k_step(w_ref, wb_ref, xo_ref)
    for r0 in range(0, bm, ROW_CHUNK):
        rs = slice(r0, r0 + ROW_CHUNK)
        _, pos = _chunk_rows(r0, ROW_CHUNK, tiles_per_batch, bm)
        _project_rows(rs, a_ref[rs, :], wb_ref, xo_ref, hn_ref, x_ref, modb_ref, modc_ref, gain_ref,
                      modb_ref, modc_ref, pos < n_ctx, 2, 3)


def _down_kernel(gate_ref, val_ref, hprev_ref, hnext_ref, cw_ref, cb_ref, w_ref, x_ref, modb_ref,
                 modc_ref, gain_ref, nmodb_ref, nmodc_ref, xo_ref, hn_ref, wb_ref, *, n_ctx, n_seq,
                 tiles_per_batch):
    bm = gate_ref.shape[0]
    rc = ROW_CHUNK
    _start_k_step(w_ref, wb_ref, xo_ref)
    for r0 in range(0, bm, rc):
        rs = slice(r0, r0 + rc)
        rows, pos = _chunk_rows(r0, rc, tiles_per_batch, bm)
        g = gate_ref[rs, :]
        before = hprev_ref[SUBLANES - 1:SUBLANES, :] if r0 == 0 else gate_ref[r0 - 1:r0, :]
        after = hnext_ref[0:1, :] if r0 + rc == bm else gate_ref[r0 + rc:r0 + rc + 1, :]
        prev = jnp.where(rows == r0, before, pltpu.roll(g, 1, 0))
        prev = jnp.where((pos == 0) | (pos == n_ctx), 0.0, prev)
        nxt = jnp.where(rows == r0 + rc - 1, after, pltpu.roll(g, rc - 1, 0))
        nxt = jnp.where((pos == n_ctx - 1) | (pos == n_seq - 1), 0.0, nxt)
        z = prev * cw_ref[0:1, :] + g * cw_ref[1:2, :] + nxt * cw_ref[2:3, :] + cb_ref[...]
        a = (z * jax.nn.sigmoid(z) * val_ref[rs, :]).astype(BF16)
        _project_rows(rs, a, wb_ref, xo_ref, hn_ref, x_ref, modb_ref, modc_ref, gain_ref, nmodb_ref,
                      nmodc_ref, pos < n_ctx, 5, 0)


def _mod_specs(n_batch, tiles_per_batch, d):
    return [pl.BlockSpec((1, 6, d), lambda i, *_: (i // tiles_per_batch, 0, 0)),
            pl.BlockSpec((1, 6, d), lambda i, *_: (n_batch, 0, 0))]


def _first_norm(x, gain, mod, *, n_ctx, n_seq, bm=ROW_TILE):
    m, d = x.shape
    tpb = n_seq // bm
    return pl.pallas_call(
        functools.partial(_norm_kernel, n_ctx=n_ctx, tiles_per_batch=tpb),
        grid=(m // bm,),
        in_specs=[pl.BlockSpec((bm, d), lambda i: (i, 0)), pl.BlockSpec((1, d), lambda i: (0, 0))]
        + _mod_specs(mod.shape[0] - 1, tpb, d),
        out_specs=pl.BlockSpec((bm, d), lambda i: (i, 0)),
        out_shape=jax.ShapeDtypeStruct((m, d), BF16),
        compiler_params=pltpu.CompilerParams(vmem_limit_bytes=VMEM_LIMIT),
        name="first_norm",
    )(x, gain, mod, mod)


def _row_spec(bm, d, single=False):
    del single
    return pl.BlockSpec((bm, d), lambda i, k: (i, 0), pipeline_mode=pl.Buffered(1))


def _out_proj(a, w, x, mod, gain, *, n_ctx, n_seq, bm=ROW_TILE, bk=512):
    m, kk = a.shape
    d = w.shape[1]
    tpb = n_seq // bm
    return pl.pallas_call(
        functools.partial(_out_kernel, n_ctx=n_ctx, tiles_per_batch=tpb),
        grid=(m // bm, kk // bk),
        in_specs=[pl.BlockSpec((bm, bk), lambda i, k: (i, k)), pl.BlockSpec((bk, d), lambda i, k: (k, 0)),
                  _row_spec(bm, d, single=True)] + _mod_specs(mod.shape[0] - 1, tpb, d)
        + [pl.BlockSpec((1, d), lambda i, k: (0, 0))],
        out_specs=[_row_spec(bm, d), _row_spec(bm, d)],
        out_shape=[jax.ShapeDtypeStruct((m, d), F32), jax.ShapeDtypeStruct((m, d), BF16)],
        scratch_shapes=[pltpu.VMEM((bk, d), BF16)],
        compiler_params=pltpu.CompilerParams(
            dimension_semantics=("arbitrary", "arbitrary"), vmem_limit_bytes=VMEM_LIMIT),
        name="out_proj",
    )(a, w, x, mod, mod, gain)


def _down_proj(up, conv_w, conv_b, w, x, mod, gain, next_mod, *, n_ctx, n_seq, bm=ROW_TILE, bk=256):
    m = up.shape[0]
    f, d = w.shape
    nkb = f // bk
    tpb = n_seq // bm
    hb = bm // SUBLANES
    last_hblk = m // SUBLANES - 1
    return pl.pallas_call(
        functools.partial(_down_kernel, n_ctx=n_ctx, n_seq=n_seq, tiles_per_batch=tpb),
        grid=(m // bm, nkb),
        in_specs=[pl.BlockSpec((bm, bk), lambda i, k: (i, k)),
                  pl.BlockSpec((bm, bk), lambda i, k: (i, k + nkb)),
                  pl.BlockSpec((SUBLANES, bk), lambda i, k: (jnp.maximum(i * hb - 1, 0), k)),
                  pl.BlockSpec((SUBLANES, bk), lambda i, k: (jnp.minimum((i + 1) * hb, last_hblk), k)),
                  pl.BlockSpec((3, bk), lambda i, k: (0, k)), pl.BlockSpec((1, bk), lambda i, k: (0, k)),
                  pl.BlockSpec((bk, d), lambda i, k: (k, 0)), _row_spec(bm, d, single=True)]
        + _mod_specs(mod.shape[0] - 1, tpb, d) + [pl.BlockSpec((1, d), lambda i, k: (0, 0))]
        + _mod_specs(mod.shape[0] - 1, tpb, d),
        out_specs=[_row_spec(bm, d), _row_spec(bm, d)],
        out_shape=[jax.ShapeDtypeStruct((m, d), F32), jax.ShapeDtypeStruct((m, d), BF16)],
        scratch_shapes=[pltpu.VMEM((bk, d), BF16)],
        compiler_params=pltpu.CompilerParams(
            dimension_semantics=("arbitrary", "arbitrary"), vmem_limit_bytes=VMEM_LIMIT),
        name="down_proj",
    )(up, up, up, up, conv_w, conv_b, w, x, mod, mod, gain, next_mod, next_mod)


def _attn_kernel(lam_ref, q_ref, k_ref, v_ref, o_ref, *, n_ctx, chunk):
    qi = pl.program_id(2)
    lam = lam_ref[0]

    def attend(nk):
        q = q_ref[0]
        bq = q.shape[0]
        lane = lax.broadcasted_iota(jnp.int32, q.shape, 1)
        zero = jnp.zeros_like(q)
        q2 = jnp.concatenate([jnp.where(lane < DA_DQK, q, zero),
                              jnp.where(lane >= DA_DQK, q, zero)], axis=0)
        m = jnp.full((2 * bq, 1), -jnp.inf, F32)
        l = jnp.zeros((2 * bq, 1), F32)
        acc = jnp.zeros((2 * bq, v_ref.shape[2]), F32)
        for c0 in range(0, nk, chunk):
            c1 = min(c0 + chunk, nk)
            s = lax.dot_general(q2, k_ref[0, c0:c1, :], (((1,), (1,)), ((), ())),
                                preferred_element_type=F32)
            m_new = jnp.maximum(m, jnp.max(s, axis=-1, keepdims=True))
            alpha = jnp.exp2(m - m_new)
            p = jnp.exp2(s - m_new)
            l = alpha * l + jnp.sum(p, axis=-1, keepdims=True)
            acc = alpha * acc + jnp.dot(p.astype(BF16), v_ref[0, c0:c1, :], preferred_element_type=F32)
            m = m_new
        o = acc / l
        o_ref[0] = o[:bq] - lam * o[bq:]

    n_ctx_blocks = n_ctx // q_ref.shape[1]

    @pl.when(qi < n_ctx_blocks)
    def _():
        attend(n_ctx)

    @pl.when(qi >= n_ctx_blocks)
    def _():
        attend(k_ref.shape[1])


def _diff_attention(q, k, v, lam, *, n_ctx, bq=ATTN_BQ, chunk=ATTN_CHUNK):
    b, n, hd = q.shape
    h = hd // LANES
    return pl.pallas_call(
        functools.partial(_attn_kernel, n_ctx=n_ctx, chunk=chunk),
        grid=(b, h, n // bq),
        in_specs=[pl.BlockSpec(memory_space=pltpu.SMEM),
                  pl.BlockSpec((1, bq, LANES), lambda bi, hi, qi: (bi, qi, hi)),
                  pl.BlockSpec((1, n, LANES), lambda bi, hi, qi: (bi, 0, hi)),
                  pl.BlockSpec((1, n, LANES), lambda bi, hi, qi: (bi, 0, hi))],
        out_specs=pl.BlockSpec((1, bq, LANES), lambda bi, hi, qi: (bi, qi, hi)),
        out_shape=jax.ShapeDtypeStruct((b, n, hd), F32),
        compiler_params=pltpu.CompilerParams(
            dimension_semantics=("parallel", "parallel", "arbitrary"), vmem_limit_bytes=VMEM_LIMIT),
        name="diff_attention",
    )(lam, q, k, v)


def _scan_chunk(t, n_ctx_chunks, n_chunks, reverse):
    if not reverse:
        return t
    return jnp.where(t < n_ctx_chunks, n_ctx_chunks - 1 - t, n_chunks - 1 - (t - n_ctx_chunks))


def _gla_constants(L, reverse):
    nlev = int(math.log2(L))
    idx = np.arange(L)
    i, t = idx[:, None], idx[None, :]
    if reverse:
        i, t = L - 1 - i, L - 1 - t
    rs = [(t <= i), (t > i)]
    am = [(i == t)]
    for lev in range(nlev):
        m = L >> (lev + 1)
        blk_i, blk_t = i // (2 * m), t // (2 * m)
        mid = blk_i * 2 * m + m
        q_role = i >= mid
        rs.append(np.where(q_role, (t >= mid) & (t <= i), (t > i) & (t < mid)) & (blk_i == blk_t))
        am.append((blk_i == blk_t) & q_role & (t < mid))
    return (np.stack(rs).astype(np.float32).reshape((nlev + 2) * L, L),
            np.stack(am).astype(np.float32))


def _gla_kernel(q_ref, k_ref, v_ref, glr_ref, w2_ref, gb_ref, rsum_ref, amask_ref, o_ref, st_ref):
    L = q_ref.shape[1]
    nlev = amask_ref.shape[0] - 1

    @pl.when(pl.program_id(1) == 0)
    def _():
        st_ref[...] = jnp.zeros_like(st_ref)

    z = jnp.dot(glr_ref[0], w2_ref[...], preferred_element_type=F32) + gb_ref[...]
    g = (jnp.minimum(z, 0.0) - jnp.log(1.0 + jnp.exp(-jnp.abs(z)))) * (1.0 / GLA_TAU)
    e_all = jnp.dot(rsum_ref[...], g, precision=HI, preferred_element_type=F32)
    b_tot = jnp.sum(g, axis=0, keepdims=True)

    nt = (((1,), (1,)), ((), ()))
    for h in range(GLA_HEADS):
        ck = slice(h * GLA_DK, (h + 1) * GLA_DK)
        cv = slice(h * GLA_DV, (h + 1) * GLA_DV)
        q = q_ref[0, :, ck].astype(F32)
        k = k_ref[0, :, ck].astype(F32)
        v = v_ref[0, :, cv]
        st = st_ref[h]
        a = amask_ref[0] * lax.dot_general(q.astype(BF16), k.astype(BF16), nt, preferred_element_type=F32)
        for lev in range(nlev):
            e = jnp.exp(e_all[(2 + lev) * L:(3 + lev) * L, ck])
            a = a + amask_ref[1 + lev] * lax.dot_general(
                (q * e).astype(BF16), (k * e).astype(BF16), nt, preferred_element_type=F32)
        qe = (q * jnp.exp(e_all[0:L, ck])).astype(BF16)
        o = lax.dot_general(qe, st.astype(BF16), nt, preferred_element_type=F32)
        o = o + jnp.dot(a.astype(BF16), v, preferred_element_type=F32)
        o_ref[0, :, cv] = o
        kd = (k * jnp.exp(e_all[L:2 * L, ck])).astype(BF16)
        st_ref[h] = jnp.exp(b_tot[:, ck]) * st + jnp.dot(
            v.T, kd, preferred_element_type=F32)


def _gla_scan(q, k, v, glr, w2, gb, *, n_ctx, reverse, L=GLA_CHUNK):
    b, n, _ = q.shape
    nc, ncc = n // L, n_ctx // L
    rsum, amask = _gla_constants(L, reverse)
    chunk = functools.partial(_scan_chunk, n_ctx_chunks=ncc, n_chunks=nc, reverse=reverse)
    row = lambda bi, t: (bi, chunk(t), 0)
    const2 = lambda bi, t: (0, 0)
    const3 = lambda bi, t: (0, 0, 0)
    hk, hv = q.shape[2], v.shape[2]
    return pl.pallas_call(
        _gla_kernel,
        grid=(b, nc),
        in_specs=[pl.BlockSpec((1, L, hk), row), pl.BlockSpec((1, L, hk), row),
                  pl.BlockSpec((1, L, hv), row), pl.BlockSpec((1, L, LANES), row),
                  pl.BlockSpec(w2.shape, const2), pl.BlockSpec(gb.shape, const2),
                  pl.BlockSpec(rsum.shape, const2), pl.BlockSpec(amask.shape, const3)],
        out_specs=pl.BlockSpec((1, L, hv), row),
        out_shape=jax.ShapeDtypeStruct((b, n, hv), F32),
        scratch_shapes=[pltpu.VMEM((GLA_HEADS, GLA_DV, GLA_DK), F32)],
        compiler_params=pltpu.CompilerParams(
            dimension_semantics=("parallel", "arbitrary"), vmem_limit_bytes=VMEM_LIMIT),
        name="gla_scan_bwd" if reverse else "gla_scan_fwd",
    )(q, k, v, glr, w2, gb, jnp.asarray(rsum), jnp.asarray(amask))


def _log_sigmoid(x):
    return jnp.minimum(x, 0.0) - jnp.log(1.0 + jnp.exp(-jnp.abs(x)))


def _mlstm_kernel(q_ref, k_ref, v_ref, gc_ref, gr_ref, tri_ref, o_ref, c_ref, n_ref, m_ref, *, reverse):
    L = q_ref.shape[1]
    H = ML_HEADS

    @pl.when(pl.program_id(1) == 0)
    def _():
        c_ref[...] = jnp.zeros_like(c_ref)
        n_ref[...] = jnp.zeros_like(n_ref)
        m_ref[...] = jnp.zeros_like(m_ref)

    tri = tri_ref[...]
    gc = gc_ref[0]
    gr = gr_ref[0]
    ic_col, ic_row = gc[:, :H], gr[:H, :]
    b_col = jnp.dot(tri, _log_sigmoid(gc[:, H:]), precision=HI, preferred_element_type=F32)
    b_row = lax.dot_general(_log_sigmoid(gr[H:, :]), tri, (((1,), (1,)), ((), ())),
                            precision=HI, preferred_element_type=F32)
    last = 0 if reverse else L - 1
    causal = tri > 0.5
    nt = (((1,), (1,)), ((), ()))

    for h in range(H):
        ck = slice(h * ML_DK, (h + 1) * ML_DK)
        cv = slice(h * ML_DV, (h + 1) * ML_DV)
        q = q_ref[0, :, ck]
        k = k_ref[0, :, ck]
        v = v_ref[0, :, cv]
        c, n, m = c_ref[h], n_ref[h], m_ref[h]
        bc, br = b_col[:, h:h + 1], b_row[h:h + 1, :]
        icc, icr = ic_col[:, h:h + 1], ic_row[h:h + 1, :]
        b_last = bc[last:last + 1, :]

        a = bc + m
        dmat = jnp.where(causal, bc - br + icr, -jnp.inf)
        m_t = jnp.maximum(a, jnp.max(dmat, axis=-1, keepdims=True))
        w_inter = jnp.exp(a - m_t)
        s = lax.dot_general(q, k, nt, preferred_element_type=F32) * jnp.exp(dmat - m_t)
        num = w_inter * jnp.dot(q, c.astype(BF16), preferred_element_type=F32) + jnp.dot(
            s.astype(BF16), v, preferred_element_type=F32)
        den = w_inter * jnp.sum(q.astype(F32) * n, axis=-1, keepdims=True) + jnp.sum(
            s, axis=-1, keepdims=True)
        o_ref[0, :, cv] = num / jnp.maximum(jnp.abs(den), jnp.exp(-m_t))

        gs_col = b_last - bc + icc
        gs_row = b_last - br + icr
        m_new = jnp.maximum(b_last + m, jnp.max(gs_row, axis=-1, keepdims=True))
        decay = jnp.exp(b_last + m - m_new)
        wk = jnp.exp(gs_col - m_new) * k.astype(F32)
        c_ref[h] = decay * c + jnp.dot(wk.astype(BF16).T, v, preferred_element_type=F32)
        n_ref[h] = decay * n + jnp.sum(wk, axis=0, keepdims=True)
        m_ref[h] = m_new


def _mlstm_scan(q, k, v, gates, *, n_ctx, reverse, L=ML_CHUNK):
    b, n, hk = q.shape
    hv = v.shape[2]
    nc, ncc = n // L, n_ctx // L
    idx = np.arange(L)
    tri = (idx[None, :] >= idx[:, None]) if reverse else (idx[None, :] <= idx[:, None])
    tri = jnp.asarray(tri.astype(np.float32))
    gates_t = jnp.swapaxes(gates, 1, 2)
    chunk = functools.partial(_scan_chunk, n_ctx_chunks=ncc, n_chunks=nc, reverse=reverse)
    row = lambda bi, t: (bi, chunk(t), 0)
    return pl.pallas_call(
        functools.partial(_mlstm_kernel, reverse=reverse),
        grid=(b, nc),
        in_specs=[pl.BlockSpec((1, L, hk), row), pl.BlockSpec((1, L, hk), row),
                  pl.BlockSpec((1, L, hv), row), pl.BlockSpec((1, L, 2 * ML_HEADS), row),
                  pl.BlockSpec((1, 2 * ML_HEADS, L), lambda bi, t: (bi, 0, chunk(t))),
                  pl.BlockSpec((L, L), lambda bi, t: (0, 0))],
        out_specs=pl.BlockSpec((1, L, hv), row),
        out_shape=jax.ShapeDtypeStruct((b, n, hv), F32),
        scratch_shapes=[pltpu.VMEM((ML_HEADS, ML_DK, ML_DV), F32),
                        pltpu.VMEM((ML_HEADS, 1, ML_DK), F32),
                        pltpu.VMEM((ML_HEADS, 1, 1), F32)],
        compiler_params=pltpu.CompilerParams(
            dimension_semantics=("parallel", "arbitrary"), vmem_limit_bytes=VMEM_LIMIT),
        name="mlstm_scan_bwd" if reverse else "mlstm_scan_fwd",
    )(q, k, v, gates, gates_t, tri)


def _rmsnorm(x, g):
    return x * lax.rsqrt(jnp.mean(x * x, axis=-1, keepdims=True) + EPS) * g


def _silu(x):
    return x * jax.nn.sigmoid(x)


def _dwconv3(x, w, first, last):
    zero = jnp.zeros_like(x[:, :1])
    prev = jnp.concatenate([zero, x[:, :-1]], axis=1)
    nxt = jnp.concatenate([x[:, 1:], zero], axis=1)
    prev = jnp.where(first[None, :, None], 0.0, prev)
    nxt = jnp.where(last[None, :, None], 0.0, nxt)
    return prev * w[0] + x * w[1] + nxt * w[2]


def _rope_tables(n_lat, n_ctx):
    rows = n_lat // GRID_W
    row = jnp.repeat(jnp.arange(rows, dtype=F32), GRID_W)
    col = jnp.tile(jnp.arange(GRID_W, dtype=F32), rows)
    inv = jnp.power(ROPE_BASE, -jnp.arange(ROPE_PAIRS, dtype=F32) / ROPE_PAIRS)
    ang_r, ang_c = row[:, None] * inv, col[:, None] * inv
    cos = jnp.concatenate([jnp.cos(ang_r)] * 2 + [jnp.cos(ang_c)] * 2, axis=-1)
    sin = jnp.concatenate([-jnp.sin(ang_r), jnp.sin(ang_r), -jnp.sin(ang_c), jnp.sin(ang_c)], axis=-1)
    pad = ((n_ctx, 0), (0, 0))
    return jnp.pad(cos, pad, constant_values=1.0), jnp.pad(sin, pad)


def _rope(x, cos, sin):
    p = ROPE_PAIRS
    xs = [x[..., i * p:(i + 1) * p] for i in range(4)]
    swapped = jnp.concatenate([xs[1], xs[0], xs[3], xs[2]], axis=-1)
    return x * cos[None, :, None, None, :] + swapped * sin[None, :, None, None, :]


def _even_mixer(hn, bsz, first, last, w_in, gate_w2, gate_b, gla_norm_g, sc_conv_w, n_ctx):
    nq = GLA_HEADS * GLA_DK
    nv = GLA_HEADS * GLA_DV
    r2 = 2 * GLA_GATE_RANK
    p = _matmul(hn, w_in, bn=896).reshape(bsz, -1, w_in.shape[1])
    n = p.shape[1]
    off = [int(o) for o in np.cumsum([0, nq, nq, nv, nv, r2, SC_WIDTH, SC_WIDTH, SC_WIDTH])]
    q = (p[..., :off[1]] * (GLA_DK ** -0.5)).astype(BF16)
    k = p[..., off[1]:off[2]].astype(BF16)
    v = p[..., off[2]:off[3]].astype(BF16)
    r = p[..., off[3]:off[4]]
    glr = p[..., off[4]:off[4] + LANES].astype(BF16)
    sx, sb, scg = p[..., off[5]:off[6]], p[..., off[6]:off[7]], p[..., off[7]:off[8]]

    outs = []
    for direction in range(2):
        w2 = jnp.zeros((LANES, nq), F32).at[
            direction * GLA_GATE_RANK:(direction + 1) * GLA_GATE_RANK].set(gate_w2[direction])
        outs.append(_gla_scan(q, k, v, glr, w2.astype(BF16), gate_b[direction][None, :],
                              n_ctx=n_ctx, reverse=bool(direction)))
    o = (outs[0] + outs[1]).reshape(bsz, n, GLA_HEADS, GLA_DV)
    o = _rmsnorm(o, gla_norm_g).reshape(bsz, n, nv) * _silu(r)
    conv_out = sb * _dwconv3(scg * sx, sc_conv_w, first, last)
    return jnp.concatenate([o, conv_out], axis=-1).reshape(bsz * n, -1).astype(BF16)


def _odd_mixer(hn, bsz, first, last, rope, layer, w_in, qn_g, kn_g, lam_p, subln_g,
               ml_conv_w, ml_gate_b, ml_norm_g, n_ctx):
    lam_init = 0.8 - 0.6 * math.exp(-0.3 * layer)
    lam = (jnp.exp(jnp.sum(lam_p[0] * lam_p[1])) - jnp.exp(jnp.sum(lam_p[2] * lam_p[3])) + lam_init)
    na = DA_HEADS * 2 * DA_DQK
    nav = DA_HEADS * DA_DV
    nk = ML_HEADS * ML_DK
    nv = ML_HEADS * ML_DV
    p = _matmul(hn, w_in, bn=896).reshape(bsz, -1, w_in.shape[1])
    n = p.shape[1]
    dq, dk, dv, mq, mk, mv, mo, mg = jnp.split(
        p, [int(o) for o in np.cumsum([na, na, nav, nk, nk, nv, nv])], axis=-1)

    cos, sin = rope
    q = _rope(_rmsnorm(dq.reshape(bsz, n, DA_HEADS, 2, DA_DQK), qn_g), cos, sin)
    k = _rope(_rmsnorm(dk.reshape(bsz, n, DA_HEADS, 2, DA_DQK), kn_g), cos, sin)
    q = (q * (DA_DQK ** -0.5 * LOG2E)).reshape(bsz, n, na).astype(BF16)
    k = k.reshape(bsz, n, na).astype(BF16)
    da = _diff_attention(q, k, dv.astype(BF16), lam.reshape(1).astype(F32), n_ctx=n_ctx)

    mqk = _silu(_dwconv3(jnp.concatenate([mq, mk], axis=-1), ml_conv_w, first, last))
    hq = mqk[..., :nk].astype(BF16)
    hk = (mqk[..., nk:] * (ML_DK ** -0.5)).astype(BF16)
    hv = mv.astype(BF16)
    gates = mg + ml_gate_b
    outs = []
    for direction in range(2):
        g_dir = gates[..., direction * 2 * ML_HEADS:(direction + 1) * 2 * ML_HEADS]
        outs.append(_mlstm_scan(hq, hk, hv, g_dir, n_ctx=n_ctx, reverse=bool(direction)))
    m = (outs[0] + outs[1]).reshape(bsz, n, ML_HEADS, ML_DV)
    m = _rmsnorm(m, ml_norm_g).reshape(bsz, n, nv) * jax.nn.sigmoid(mo)
    da = (_rmsnorm(da.reshape(bsz, n, DA_HEADS, DA_DV), subln_g) * (1.0 - lam_init)).reshape(bsz, n, nav)
    return jnp.concatenate([da, m], axis=-1).reshape(bsz * n, -1).astype(BF16)


def _modulation(c, c_ctx, ada_w, ada_b):
    cc = jnp.concatenate([c, c_ctx[None, :]], axis=0)
    return (_silu(cc) @ ada_w + ada_b).reshape(cc.shape[0], 6, -1)


def kernel(x, c, ctx, c_ctx, ada_w, ada_b, norm1_g, norm2_g, ev_w_in, ev_w_out, gla_gate_w2, gla_gate_b, gla_norm_g, sc_conv_w, od_w_in, od_w_out, da_qnorm_g, da_knorm_g, da_lambda, da_subln_g, ml_conv_w, ml_gate_b, ml_norm_g, ffn_w_up, ffn_conv_w, ffn_conv_b, ffn_w_down):
    bsz, n_lat, d = x.shape
    n_ctx = ctx.shape[1]
    n = n_ctx + n_lat
    depth = ada_w.shape[0]
    assert n % ROW_TILE == 0 and n_ctx <= ROW_TILE
    pos = jnp.arange(n)
    first = (pos == 0) | (pos == n_ctx)
    last = (pos == n_ctx - 1) | (pos == n - 1)
    rope = _rope_tables(n_lat, n_ctx)
    seq = dict(n_ctx=n_ctx, n_seq=n)

    mods = [_modulation(c, c_ctx, ada_w[layer], ada_b[layer]) for layer in range(depth)]
    xs = jnp.concatenate([ctx, x], axis=1).reshape(bsz * n, d)
    hn = _first_norm(xs, norm1_g[0][None, :], mods[0], **seq)
    for layer in range(depth):
        i = layer // 2
        if layer % 2 == 0:
            mix = _even_mixer(hn, bsz, first, last, ev_w_in[i], gla_gate_w2[i], gla_gate_b[i],
                              gla_norm_g[i], sc_conv_w[i], n_ctx)
            w_out = ev_w_out[i]
        else:
            mix = _odd_mixer(hn, bsz, first, last, rope, layer, od_w_in[i], da_qnorm_g[i],
                             da_knorm_g[i], da_lambda[i], da_subln_g[i], ml_conv_w[i], ml_gate_b[i],
                             ml_norm_g[i], n_ctx)
            w_out = od_w_out[i]
        xs, hn = _out_proj(mix, w_out, xs, mods[layer], norm2_g[layer][None, :], **seq)
        up = _matmul(hn, ffn_w_up[layer], bn=1024)
        nxt = min(layer + 1, depth - 1)
        xs, hn = _down_proj(up, ffn_conv_w[layer], ffn_conv_b[layer][None, :], ffn_w_down[layer], xs,
                            mods[layer], norm1_g[nxt][None, :], mods[nxt], **seq)
    return xs.reshape(bsz, n, d)[:, n_ctx:, :]
```

```python
import functools
import math

import jax
import jax.numpy as jnp
import numpy as np
from jax import lax
from jax.experimental import pallas as pl
from jax.experimental.pallas import tpu as pltpu

GRID_W = 64
CTX_LEN = 256
EPS = 1e-6
GLA_HEADS, GLA_DK, GLA_DV, GLA_GATE_RANK, GLA_TAU = 4, 128, 256, 16, 16.0
SC_WIDTH = 1024
DA_HEADS, DA_DQK, DA_DV = 8, 64, 128
ROPE_BASE = 10000.0
ROPE_PAIRS = DA_DQK // 4
ML_HEADS, ML_DK, ML_DV = 4, 128, 256

GLA_CHUNK = 128
ML_CHUNK = 256
ATTN_BQ = 256
ATTN_CHUNK = 512
ROW_TILE = 1088
ROW_CHUNK = 272
LANES = 128
SUBLANES = 8
BF16_ROWS = 16
LOG2E = 1.4426950408889634
VMEM_LIMIT = 56 * 1024 * 1024

F32 = jnp.float32
BF16 = jnp.bfloat16
HI = lax.Precision.HIGHEST


def _mm_kernel(x_ref, w_ref, o_ref, wb_ref):
    @pl.when(pl.program_id(1) == 0)
    def _():
        wb_ref[...] = w_ref[...].astype(BF16)

    o_ref[...] = jnp.dot(x_ref[...], wb_ref[...], preferred_element_type=F32).astype(o_ref.dtype)


def _matmul(x, w, *, bn, bm=ROW_TILE, out_dtype=F32):
    m, k = x.shape
    n = w.shape[1]
    return pl.pallas_call(
        _mm_kernel,
        grid=(pl.cdiv(n, bn), m // bm),
        in_specs=[pl.BlockSpec((bm, k), lambda j, i: (i, 0)),
                  pl.BlockSpec((k, bn), lambda j, i: (0, j))],
        out_specs=pl.BlockSpec((bm, bn), lambda j, i: (i, j)),
        out_shape=jax.ShapeDtypeStruct((m, n), out_dtype),
        scratch_shapes=[pltpu.VMEM((k, bn), BF16)],
        compiler_params=pltpu.CompilerParams(
            dimension_semantics=("arbitrary", "arbitrary"), vmem_limit_bytes=VMEM_LIMIT),
        name="matmul",
    )(x, w)


def _cast_kernel(x_ref, o_ref):
    o_ref[...] = x_ref[...].astype(o_ref.dtype)


def _to_bf16(w, *, rows):
    r, c = w.shape
    return pl.pallas_call(
        _cast_kernel,
        grid=(r // rows,),
        in_specs=[pl.BlockSpec((rows, c), lambda i: (i, 0))],
        out_specs=pl.BlockSpec((rows, c), lambda i: (i, 0)),
        out_shape=jax.ShapeDtypeStruct((r, c), BF16),
        compiler_params=pltpu.CompilerParams(vmem_limit_bytes=VMEM_LIMIT),
        name="to_bf16",
    )(w)


def _norm_mod(x, gain, scale, shift):
    return x * lax.rsqrt(jnp.mean(x * x, axis=-1, keepdims=True) + EPS) * gain * (1.0 + scale) + shift


def _mod_row(is_ctx, modb_ref, modc_ref, idx):
    if is_ctx is None:
        return modb_ref[0, idx:idx + 1, :]
    return jnp.where(is_ctx, modc_ref[0, idx:idx + 1, :], modb_ref[0, idx:idx + 1, :])


def _tile_rows(bm, tiles_per_batch):
    rows = lax.broadcasted_iota(jnp.int32, (bm, 1), 0)
    return rows, (pl.program_id(0) % tiles_per_batch) * bm + rows


def _norm_kernel(x_ref, gain_ref, modb_ref, modc_ref, o_ref, *, n_ctx, tiles_per_batch):
    _, pos = _tile_rows(x_ref.shape[0], tiles_per_batch)
    is_ctx = pos < n_ctx
    o_ref[...] = _norm_mod(x_ref[...], gain_ref[...], _mod_row(is_ctx, modb_ref, modc_ref, 1),
                           _mod_row(is_ctx, modb_ref, modc_ref, 0)).astype(BF16)


def _chunk_rows(r0, rc, tiles_per_batch, bm):
    rows = r0 + lax.broadcasted_iota(jnp.int32, (rc, 1), 0)
    return rows, (pl.program_id(0) % tiles_per_batch) * bm + rows


def _residual_norm(rs, acc, xo_ref, hn_ref, x_ref, modb_ref, modc_ref, gain_ref, nmodb_ref, nmodc_ref,
                   is_ctx, gate_idx, shift_idx):
    x_new = x_ref[rs, :] + _mod_row(is_ctx, modb_ref, modc_ref, gate_idx) * acc
    xo_ref[rs, :] = x_new
    hn_ref[rs, :] = _norm_mod(
        x_new, gain_ref[...], _mod_row(is_ctx, nmodb_ref, nmodc_ref, shift_idx + 1),
        _mod_row(is_ctx, nmodb_ref, nmodc_ref, shift_idx)).astype(BF16)


def _chunk_is_ctx(r0, pos, n_ctx):
    return pos < n_ctx if r0 < n_ctx else None


def _out_kernel(a_ref, w_ref, x_ref, modb_ref, modc_ref, gain_ref, xo_ref, hn_ref, *,
                n_ctx, tiles_per_batch):
    bm = x_ref.shape[0]
    for r0 in range(0, bm, ROW_CHUNK):
        rs = slice(r0, r0 + ROW_CHUNK)
        _, pos = _chunk_rows(r0, ROW_CHUNK, tiles_per_batch, bm)
        acc = jnp.dot(a_ref[rs, :], w_ref[...], preferred_element_type=F32)
        _residual_norm(rs, acc, xo_ref, hn_ref, x_ref, modb_ref, modc_ref, gain_ref, modb_ref, modc_ref,
                       _chunk_is_ctx(r0, pos, n_ctx), 2, 3)


def _down_kernel(gate_ref, val_ref, hprev_ref, hnext_ref, cw_ref, cb_ref, w_ref, x_ref, modb_ref,
                 modc_ref, gain_ref, nmodb_ref, nmodc_ref, xo_ref, hn_ref, *, n_ctx, n_seq,
                 tiles_per_batch):
    bm = gate_ref.shape[0]
    rc, hr = ROW_CHUNK, hprev_ref.shape[0]

    @pl.when(pl.program_id(1) == 0)
    def _():
        xo_ref[...] = jnp.zeros_like(xo_ref)

    for r0 in range(0, bm, rc):
        rs = slice(r0, r0 + rc)
        rows, pos = _chunk_rows(r0, rc, tiles_per_batch, bm)
        g = gate_ref[rs, :].astype(F32)
        before = (hprev_ref[...] if r0 == 0 else gate_ref[r0 - hr:r0, :]).astype(F32)[hr - 1:hr, :]
        after = (hnext_ref[...] if r0 + rc == bm else gate_ref[r0 + rc:r0 + rc + hr, :]).astype(F32)[0:1, :]
        prev = jnp.where(rows == r0, before, pltpu.roll(g, 1, 0))
        prev = jnp.where((pos == 0) | (pos == n_ctx), 0.0, prev)
        nxt = jnp.where(rows == r0 + rc - 1, after, pltpu.roll(g, rc - 1, 0))
        nxt = jnp.where((pos == n_ctx - 1) | (pos == n_seq - 1), 0.0, nxt)
        z = prev * cw_ref[0:1, :] + g * cw_ref[1:2, :] + nxt * cw_ref[2:3, :] + cb_ref[...]
        a = (z * jax.nn.sigmoid(z) * val_ref[rs, :].astype(F32)).astype(BF16)
        xo_ref[rs, :] += jnp.dot(a, w_ref[...], preferred_element_type=F32)

        @pl.when(pl.program_id(1) == pl.num_programs(1) - 1)
        def _():
            _residual_norm(rs, xo_ref[rs, :], xo_ref, hn_ref, x_ref, modb_ref, modc_ref, gain_ref,
                           nmodb_ref, nmodc_ref, _chunk_is_ctx(r0, pos, n_ctx), 5, 0)


def _mod_specs(n_batch, tiles_per_batch, d):
    return [pl.BlockSpec((1, 6, d), lambda i, *_: (i // tiles_per_batch, 0, 0)),
            pl.BlockSpec((1, 6, d), lambda i, *_: (n_batch, 0, 0))]


def _first_norm(x, gain, mod, *, n_ctx, n_seq, bm=ROW_TILE):
    m, d = x.shape
    tpb = n_seq // bm
    return pl.pallas_call(
        functools.partial(_norm_kernel, n_ctx=n_ctx, tiles_per_batch=tpb),
        grid=(m // bm,),
        in_specs=[pl.BlockSpec((bm, d), lambda i: (i, 0)), pl.BlockSpec((1, d), lambda i: (0, 0))]
        + _mod_specs(mod.shape[0] - 1, tpb, d),
        out_specs=pl.BlockSpec((bm, d), lambda i: (i, 0)),
        out_shape=jax.ShapeDtypeStruct((m, d), BF16),
        compiler_params=pltpu.CompilerParams(vmem_limit_bytes=VMEM_LIMIT),
        name="first_norm",
    )(x, gain, mod, mod)


def _row_spec(bm, d):
    return pl.BlockSpec((bm, d), lambda i, k: (i, 0))


def _out_proj(a, w, x, mod, gain, *, n_ctx, n_seq, bm=ROW_TILE // 2):
    m, kk = a.shape
    d = w.shape[1]
    tpb = n_seq // bm
    rows = lambda width: pl.BlockSpec((bm, width), lambda i: (i, 0))
    return pl.pallas_call(
        functools.partial(_out_kernel, n_ctx=n_ctx, tiles_per_batch=tpb),
        grid=(m // bm,),
        in_specs=[rows(kk), pl.BlockSpec((kk, d), lambda i: (0, 0), pipeline_mode=pl.Buffered(1)), rows(d)]
        + _mod_specs(mod.shape[0] - 1, tpb, d) + [pl.BlockSpec((1, d), lambda i: (0, 0))],
        out_specs=[rows(d), rows(d)],
        out_shape=[jax.ShapeDtypeStruct((m, d), F32), jax.ShapeDtypeStruct((m, d), BF16)],
        compiler_params=pltpu.CompilerParams(vmem_limit_bytes=VMEM_LIMIT),
        name="out_proj",
    )(a, w, x, mod, mod, gain)


def _down_proj(up, conv_w, conv_b, w, x, mod, gain, next_mod, *, n_ctx, n_seq, bm=ROW_TILE // 2,
               bk=1408):
    m = up.shape[0]
    f, d = w.shape
    nkb = f // bk
    tpb = n_seq // bm
    hr = BF16_ROWS
    hb = bm // hr
    last_hblk = m // hr - 1
    return pl.pallas_call(
        functools.partial(_down_kernel, n_ctx=n_ctx, n_seq=n_seq, tiles_per_batch=tpb),
        grid=(m // bm, nkb),
        in_specs=[pl.BlockSpec((bm, bk), lambda i, k: (i, k)),
                  pl.BlockSpec((bm, bk), lambda i, k: (i, k + nkb)),
                  pl.BlockSpec((hr, bk), lambda i, k: (jnp.maximum(i * hb - 1, 0), k)),
                  pl.BlockSpec((hr, bk), lambda i, k: (jnp.minimum((i + 1) * hb, last_hblk), k)),
                  pl.BlockSpec((3, bk), lambda i, k: (0, k)), pl.BlockSpec((1, bk), lambda i, k: (0, k)),
                  pl.BlockSpec((bk, d), lambda i, k: (k, 0)), _row_spec(bm, d)]
        + _mod_specs(mod.shape[0] - 1, tpb, d) + [pl.BlockSpec((1, d), lambda i, k: (0, 0))]
        + _mod_specs(mod.shape[0] - 1, tpb, d),
        out_specs=[_row_spec(bm, d), _row_spec(bm, d)],
        out_shape=[jax.ShapeDtypeStruct((m, d), F32), jax.ShapeDtypeStruct((m, d), BF16)],
        compiler_params=pltpu.CompilerParams(
            dimension_semantics=("arbitrary", "arbitrary"), vmem_limit_bytes=VMEM_LIMIT),
        name="down_proj",
    )(up, up, up, up, conv_w, conv_b, w, x, mod, mod, gain, next_mod, next_mod)


def _attn_kernel(lam_ref, q_ref, k_ref, v_ref, o_ref, *, n_ctx, chunk):
    qi = pl.program_id(2)
    lam = lam_ref[0]

    def attend(nk):
        q = q_ref[0]
        bq = q.shape[0]
        lane = lax.broadcasted_iota(jnp.int32, q.shape, 1)
        zero = jnp.zeros_like(q)
        q2 = jnp.concatenate([jnp.where(lane < DA_DQK, q, zero),
                              jnp.where(lane >= DA_DQK, q, zero)], axis=0)
        m = jnp.full((2 * bq, 1), -jnp.inf, F32)
        l = jnp.zeros((2 * bq, 1), F32)
        acc = jnp.zeros((2 * bq, v_ref.shape[2]), F32)
        for c0 in range(0, nk, chunk):
            c1 = min(c0 + chunk, nk)
            s = lax.dot_general(q2, k_ref[0, c0:c1, :], (((1,), (1,)), ((), ())),
                                preferred_element_type=F32)
            m_new = jnp.maximum(m, jnp.max(s, axis=-1, keepdims=True))
            alpha = jnp.exp2(m - m_new)
            p = jnp.exp2(s - m_new)
            l = alpha * l + jnp.sum(p, axis=-1, keepdims=True)
            acc = alpha * acc + jnp.dot(p.astype(BF16), v_ref[0, c0:c1, :], preferred_element_type=F32)
            m = m_new
        o = acc / l
        o_ref[0] = o[:bq] - lam * o[bq:]

    n_ctx_blocks = n_ctx // q_ref.shape[1]

    @pl.when(qi < n_ctx_blocks)
    def _():
        attend(n_ctx)

    @pl.when(qi >= n_ctx_blocks)
    def _():
        attend(k_ref.shape[1])


def _diff_attention(q, k, v, lam, *, n_ctx, bq=ATTN_BQ, chunk=ATTN_CHUNK):
    b, n, hd = q.shape
    h = hd // LANES
    return pl.pallas_call(
        functools.partial(_attn_kernel, n_ctx=n_ctx, chunk=chunk),
        grid=(b, h, n // bq),
        in_specs=[pl.BlockSpec(memory_space=pltpu.SMEM),
                  pl.BlockSpec((1, bq, LANES), lambda bi, hi, qi: (bi, qi, hi)),
                  pl.BlockSpec((1, n, LANES), lambda bi, hi, qi: (bi, 0, hi)),
                  pl.BlockSpec((1, n, LANES), lambda bi, hi, qi: (bi, 0, hi))],
        out_specs=pl.BlockSpec((1, bq, LANES), lambda bi, hi, qi: (bi, qi, hi)),
        out_shape=jax.ShapeDtypeStruct((b, n, hd), F32),
        compiler_params=pltpu.CompilerParams(
            dimension_semantics=("parallel", "parallel", "arbitrary"), vmem_limit_bytes=VMEM_LIMIT),
        name="diff_attention",
    )(lam, q, k, v)


def _scan_chunk(t, n_ctx_chunks, n_chunks, reverse):
    if not reverse:
        return t
    return jnp.where(t < n_ctx_chunks, n_ctx_chunks - 1 - t, n_chunks - 1 - (t - n_ctx_chunks))


def _gla_constants(L, reverse):
    nlev = int(math.log2(L))
    idx = np.arange(L)
    i, t = idx[:, None], idx[None, :]
    if reverse:
        i, t = L - 1 - i, L - 1 - t
    rs = [(t <= i), (t > i)]
    am = [(i == t)]
    for lev in range(nlev):
        m = L >> (lev + 1)
        blk_i, blk_t = i // (2 * m), t // (2 * m)
        mid = blk_i * 2 * m + m
        q_role = i >= mid
        rs.append(np.where(q_role, (t >= mid) & (t <= i), (t > i) & (t < mid)) & (blk_i == blk_t))
        am.append((blk_i == blk_t) & q_role & (t < mid))
    return (np.stack(rs).astype(np.float32).reshape((nlev + 2) * L, L),
            np.stack(am).astype(np.float32))


def _gla_kernel(q_ref, k_ref, v_ref, glr_ref, w2_ref, gb_ref, rsum_ref, amask_ref, o_ref, st_ref):
    L = q_ref.shape[1]
    nlev = amask_ref.shape[0] - 1

    @pl.when(pl.program_id(1) == 0)
    def _():
        st_ref[...] = jnp.zeros_like(st_ref)

    z = jnp.dot(glr_ref[0].astype(BF16), w2_ref[...], preferred_element_type=F32) + gb_ref[...]
    g = (jnp.minimum(z, 0.0) - jnp.log(1.0 + jnp.exp(-jnp.abs(z)))) * (1.0 / GLA_TAU)
    e_all = jnp.dot(rsum_ref[...], g, precision=HI, preferred_element_type=F32)
    b_tot = jnp.sum(g, axis=0, keepdims=True)

    nt = (((1,), (1,)), ((), ()))
    for h in range(GLA_HEADS):
        ck = slice(h * GLA_DK, (h + 1) * GLA_DK)
        cv = slice(h * GLA_DV, (h + 1) * GLA_DV)
        q = q_ref[0, :, ck] * (GLA_DK ** -0.5)
        k = k_ref[0, :, ck]
        v = v_ref[0, :, cv].astype(BF16)
        st = st_ref[h]
        a = amask_ref[0] * lax.dot_general(q.astype(BF16), k.astype(BF16), nt, preferred_element_type=F32)
        for lev in range(nlev):
            e = jnp.exp(e_all[(2 + lev) * L:(3 + lev) * L, ck])
            a = a + amask_ref[1 + lev] * lax.dot_general(
                (q * e).astype(BF16), (k * e).astype(BF16), nt, preferred_element_type=F32)
        qe = (q * jnp.exp(e_all[0:L, ck])).astype(BF16)
        o = lax.dot_general(qe, st.astype(BF16), nt, preferred_element_type=F32)
        o = o + jnp.dot(a.astype(BF16), v, preferred_element_type=F32)
        o_ref[0, :, cv] = o
        kd = (k * jnp.exp(e_all[L:2 * L, ck])).astype(BF16)
        st_ref[h] = jnp.exp(b_tot[:, ck]) * st + jnp.dot(
            v.T, kd, preferred_element_type=F32)


def _gla_scan(p, w2, gb, *, n_ctx, reverse, L=GLA_CHUNK):
    b, n, _ = p.shape
    nc, ncc = n // L, n_ctx // L
    rsum, amask = _gla_constants(L, reverse)
    chunk = functools.partial(_scan_chunk, n_ctx_chunks=ncc, n_chunks=nc, reverse=reverse)
    hk, hv = GLA_HEADS * GLA_DK, GLA_HEADS * GLA_DV
    cols = lambda width, start: pl.BlockSpec((1, L, width), lambda bi, t: (bi, chunk(t), start // width))
    row = lambda bi, t: (bi, chunk(t), 0)
    const2 = lambda bi, t: (0, 0)
    const3 = lambda bi, t: (0, 0, 0)
    q, k, v, glr = p, p, p, p
    return pl.pallas_call(
        _gla_kernel,
        grid=(b, nc),
        in_specs=[cols(hk, 0), cols(hk, hk), cols(hv, 2 * hk), cols(LANES, 2 * hk + 2 * hv),
                  pl.BlockSpec(w2.shape, const2), pl.BlockSpec(gb.shape, const2),
                  pl.BlockSpec(rsum.shape, const2), pl.BlockSpec(amask.shape, const3)],
        out_specs=pl.BlockSpec((1, L, hv), row),
        out_shape=jax.ShapeDtypeStruct((b, n, hv), F32),
        scratch_shapes=[pltpu.VMEM((GLA_HEADS, GLA_DV, GLA_DK), F32)],
        compiler_params=pltpu.CompilerParams(
            dimension_semantics=("parallel", "arbitrary"), vmem_limit_bytes=VMEM_LIMIT),
        name="gla_scan_bwd" if reverse else "gla_scan_fwd",
    )(q, k, v, glr, w2, gb, jnp.asarray(rsum), jnp.asarray(amask))


def _log_sigmoid(x):
    return jnp.minimum(x, 0.0) - jnp.log(1.0 + jnp.exp(-jnp.abs(x)))


def _mlstm_kernel(q_ref, k_ref, v_ref, gc_ref, gr_ref, tri_ref, o_ref, c_ref, n_ref, m_ref, *, reverse):
    L = q_ref.shape[1]
    H = ML_HEADS

    @pl.when(pl.program_id(1) == 0)
    def _():
        c_ref[...] = jnp.zeros_like(c_ref)
        n_ref[...] = jnp.zeros_like(n_ref)
        m_ref[...] = jnp.zeros_like(m_ref)

    tri = tri_ref[...]
    gc = gc_ref[0]
    gr = gr_ref[0]
    ic_col, ic_row = gc[:, :H], gr[:H, :]
    b_col = jnp.dot(tri, _log_sigmoid(gc[:, H:]), precision=HI, preferred_element_type=F32)
    b_row = lax.dot_general(_log_sigmoid(gr[H:, :]), tri, (((1,), (1,)), ((), ())),
                            precision=HI, preferred_element_type=F32)
    last = 0 if reverse else L - 1
    causal = tri > 0.5
    nt = (((1,), (1,)), ((), ()))

    for h in range(H):
        ck = slice(h * ML_DK, (h + 1) * ML_DK)
        cv = slice(h * ML_DV, (h + 1) * ML_DV)
        q = q_ref[0, :, ck]
        k = k_ref[0, :, ck]
        v = v_ref[0, :, cv]
        c, n, m = c_ref[h], n_ref[h], m_ref[h]
        bc, br = b_col[:, h:h + 1], b_row[h:h + 1, :]
        icc, icr = ic_col[:, h:h + 1], ic_row[h:h + 1, :]
        b_last = bc[last:last + 1, :]

        a = bc + m
        dmat = jnp.where(causal, bc - br + icr, -jnp.inf)
        m_t = jnp.maximum(a, jnp.max(dmat, axis=-1, keepdims=True))
        w_inter = jnp.exp(a - m_t)
        s = lax.dot_general(q, k, nt, preferred_element_type=F32) * jnp.exp(dmat - m_t)
        num = w_inter * jnp.dot(q, c.astype(BF16), preferred_element_type=F32) + jnp.dot(
            s.astype(BF16), v, preferred_element_type=F32)
        den = w_inter * jnp.sum(q.astype(F32) * n, axis=-1, keepdims=True) + jnp.sum(
            s, axis=-1, keepdims=True)
        o_ref[0, :, cv] = num / jnp.maximum(jnp.abs(den), jnp.exp(-m_t))

        gs_col = b_last - bc + icc
        gs_row = b_last - br + icr
        m_new = jnp.maximum(b_last + m, jnp.max(gs_row, axis=-1, keepdims=True))
        decay = jnp.exp(b_last + m - m_new)
        wk = jnp.exp(gs_col - m_new) * k.astype(F32)
        c_ref[h] = decay * c + jnp.dot(wk.astype(BF16).T, v, preferred_element_type=F32)
        n_ref[h] = decay * n + jnp.sum(wk, axis=0, keepdims=True)
        m_ref[h] = m_new


def _mlstm_scan(q, k, v, gates, *, n_ctx, reverse, L=ML_CHUNK):
    b, n, hk = q.shape
    hv = v.shape[2]
    nc, ncc = n // L, n_ctx // L
    idx = np.arange(L)
    tri = (idx[None, :] >= idx[:, None]) if reverse else (idx[None, :] <= idx[:, None])
    tri = jnp.asarray(tri.astype(np.float32))
    gates_t = jnp.swapaxes(gates, 1, 2)
    chunk = functools.partial(_scan_chunk, n_ctx_chunks=ncc, n_chunks=nc, reverse=reverse)
    row = lambda bi, t: (bi, chunk(t), 0)
    return pl.pallas_call(
        functools.partial(_mlstm_kernel, reverse=reverse),
        grid=(b, nc),
        in_specs=[pl.BlockSpec((1, L, hk), row), pl.BlockSpec((1, L, hk), row),
                  pl.BlockSpec((1, L, hv), row), pl.BlockSpec((1, L, 2 * ML_HEADS), row),
                  pl.BlockSpec((1, 2 * ML_HEADS, L), lambda bi, t: (bi, 0, chunk(t))),
                  pl.BlockSpec((L, L), lambda bi, t: (0, 0))],
        out_specs=pl.BlockSpec((1, L, hv), row),
        out_shape=jax.ShapeDtypeStruct((b, n, hv), F32),
        scratch_shapes=[pltpu.VMEM((ML_HEADS, ML_DK, ML_DV), F32),
                        pltpu.VMEM((ML_HEADS, 1, ML_DK), F32),
                        pltpu.VMEM((ML_HEADS, 1, 1), F32)],
        compiler_params=pltpu.CompilerParams(
            dimension_semantics=("parallel", "arbitrary"), vmem_limit_bytes=VMEM_LIMIT),
        name="mlstm_scan_bwd" if reverse else "mlstm_scan_fwd",
    )(q, k, v, gates, gates_t, tri)


def _rmsnorm(x, g):
    return x * lax.rsqrt(jnp.mean(x * x, axis=-1, keepdims=True) + EPS) * g


def _silu(x):
    return x * jax.nn.sigmoid(x)


def _dwconv3(x, w, first, last):
    zero = jnp.zeros_like(x[:, :1])
    prev = jnp.concatenate([zero, x[:, :-1]], axis=1)
    nxt = jnp.concatenate([x[:, 1:], zero], axis=1)
    prev = jnp.where(first[None, :, None], 0.0, prev)
    nxt = jnp.where(last[None, :, None], 0.0, nxt)
    return prev * w[0] + x * w[1] + nxt * w[2]


def _rope_tables(n_lat, n_ctx):
    rows = n_lat // GRID_W
    row = jnp.repeat(jnp.arange(rows, dtype=F32), GRID_W)
    col = jnp.tile(jnp.arange(GRID_W, dtype=F32), rows)
    inv = jnp.power(ROPE_BASE, -jnp.arange(ROPE_PAIRS, dtype=F32) / ROPE_PAIRS)
    ang_r, ang_c = row[:, None] * inv, col[:, None] * inv
    cos = jnp.concatenate([jnp.cos(ang_r)] * 2 + [jnp.cos(ang_c)] * 2, axis=-1)
    sin = jnp.concatenate([-jnp.sin(ang_r), jnp.sin(ang_r), -jnp.sin(ang_c), jnp.sin(ang_c)], axis=-1)
    pad = ((n_ctx, 0), (0, 0))
    return jnp.pad(cos, pad, constant_values=1.0), jnp.pad(sin, pad)


def _rope(x, cos, sin):
    p = ROPE_PAIRS
    xs = [x[..., i * p:(i + 1) * p] for i in range(4)]
    swapped = jnp.concatenate([xs[1], xs[0], xs[3], xs[2]], axis=-1)
    return x * cos[None, :, None, None, :] + swapped * sin[None, :, None, None, :]


def _even_mixer(hn, bsz, first, last, w_in, gate_w2, gate_b, gla_norm_g, sc_conv_w, n_ctx):
    nq = GLA_HEADS * GLA_DK
    nv = GLA_HEADS * GLA_DV
    r2 = 2 * GLA_GATE_RANK
    p = _matmul(hn, w_in, bn=896).reshape(bsz, -1, w_in.shape[1])
    n = p.shape[1]
    off = [int(o) for o in np.cumsum([0, nq, nq, nv, nv, r2, SC_WIDTH, SC_WIDTH, SC_WIDTH])]
    r = p[..., off[3]:off[4]]
    sx, sb, scg = p[..., off[5]:off[6]], p[..., off[6]:off[7]], p[..., off[7]:off[8]]

    outs = []
    for direction in range(2):
        w2 = jnp.zeros((LANES, nq), F32).at[
            direction * GLA_GATE_RANK:(direction + 1) * GLA_GATE_RANK].set(gate_w2[direction])
        outs.append(_gla_scan(p, w2.astype(BF16), gate_b[direction][None, :],
                              n_ctx=n_ctx, reverse=bool(direction)))
    o = (outs[0] + outs[1]).reshape(bsz, n, GLA_HEADS, GLA_DV)
    o = _rmsnorm(o, gla_norm_g).reshape(bsz, n, nv) * _silu(r)
    conv_out = sb * _dwconv3(scg * sx, sc_conv_w, first, last)
    return jnp.concatenate([o, conv_out], axis=-1).reshape(bsz * n, -1).astype(BF16)


def _odd_mixer(hn, bsz, first, last, rope, layer, w_in, qn_g, kn_g, lam_p, subln_g,
               ml_conv_w, ml_gate_b, ml_norm_g, n_ctx):
    lam_init = 0.8 - 0.6 * math.exp(-0.3 * layer)
    lam = (jnp.exp(jnp.sum(lam_p[0] * lam_p[1])) - jnp.exp(jnp.sum(lam_p[2] * lam_p[3])) + lam_init)
    na = DA_HEADS * 2 * DA_DQK
    nav = DA_HEADS * DA_DV
    nk = ML_HEADS * ML_DK
    nv = ML_HEADS * ML_DV
    p = _matmul(hn, w_in, bn=896).reshape(bsz, -1, w_in.shape[1])
    n = p.shape[1]
    dq, dk, dv, mq, mk, mv, mo, mg = jnp.split(
        p, [int(o) for o in np.cumsum([na, na, nav, nk, nk, nv, nv])], axis=-1)

    cos, sin = rope
    q = _rope(_rmsnorm(dq.reshape(bsz, n, DA_HEADS, 2, DA_DQK), qn_g), cos, sin)
    k = _rope(_rmsnorm(dk.reshape(bsz, n, DA_HEADS, 2, DA_DQK), kn_g), cos, sin)
    q = (q * (DA_DQK ** -0.5 * LOG2E)).reshape(bsz, n, na).astype(BF16)
    k = k.reshape(bsz, n, na).astype(BF16)
    da = _diff_attention(q, k, dv.astype(BF16), lam.reshape(1).astype(F32), n_ctx=n_ctx)

    mqk = _silu(_dwconv3(jnp.concatenate([mq, mk], axis=-1), ml_conv_w, first, last))
    hq = mqk[..., :nk].astype(BF16)
    hk = (mqk[..., nk:] * (ML_DK ** -0.5)).astype(BF16)
    hv = mv.astype(BF16)
    gates = mg + ml_gate_b
    outs = []
    for direction in range(2):
        g_dir = gates[..., direction * 2 * ML_HEADS:(direction + 1) * 2 * ML_HEADS]
        outs.append(_mlstm_scan(hq, hk, hv, g_dir, n_ctx=n_ctx, reverse=bool(direction)))
    m = (outs[0] + outs[1]).reshape(bsz, n, ML_HEADS, ML_DV)
    m = _rmsnorm(m, ml_norm_g).reshape(bsz, n, nv) * jax.nn.sigmoid(mo)
    da = (_rmsnorm(da.reshape(bsz, n, DA_HEADS, DA_DV), subln_g) * (1.0 - lam_init)).reshape(bsz, n, nav)
    return jnp.concatenate([da, m], axis=-1).reshape(bsz * n, -1).astype(BF16)


def _modulation(c, c_ctx, ada_w, ada_b):
    cc = jnp.concatenate([c, c_ctx[None, :]], axis=0)
    return (_silu(cc) @ ada_w + ada_b).reshape(cc.shape[0], 6, -1)


def kernel(x, c, ctx, c_ctx, ada_w, ada_b, norm1_g, norm2_g, ev_w_in, ev_w_out, gla_gate_w2, gla_gate_b, gla_norm_g, sc_conv_w, od_w_in, od_w_out, da_qnorm_g, da_knorm_g, da_lambda, da_subln_g, ml_conv_w, ml_gate_b, ml_norm_g, ffn_w_up, ffn_conv_w, ffn_conv_b, ffn_w_down):
    bsz, n_lat, d = x.shape
    n_ctx = ctx.shape[1]
    n = n_ctx + n_lat
    depth = ada_w.shape[0]
    assert n % ROW_TILE == 0 and n_ctx <= ROW_TILE
    pos = jnp.arange(n)
    first = (pos == 0) | (pos == n_ctx)
    last = (pos == n_ctx - 1) | (pos == n - 1)
    rope = _rope_tables(n_lat, n_ctx)
    seq = dict(n_ctx=n_ctx, n_seq=n)

    mods = [_modulation(c, c_ctx, ada_w[layer], ada_b[layer]) for layer in range(depth)]
    xs = jnp.concatenate([ctx, x], axis=1).reshape(bsz * n, d)
    hn = _first_norm(xs, norm1_g[0][None, :], mods[0], **seq)
    for layer in range(depth):
        i = layer // 2
        if layer % 2 == 0:
            mix = _even_mixer(hn, bsz, first, last, ev_w_in[i], gla_gate_w2[i], gla_gate_b[i],
                              gla_norm_g[i], sc_conv_w[i], n_ctx)
            w_out = ev_w_out[i]
        else:
            mix = _odd_mixer(hn, bsz, first, last, rope, layer, od_w_in[i], da_qnorm_g[i],
                             da_knorm_g[i], da_lambda[i], da_subln_g[i], ml_conv_w[i], ml_gate_b[i],
                             ml_norm_g[i], n_ctx)
            w_out = od_w_out[i]
        xs, hn = _out_proj(mix, _to_bf16(w_out, rows=512), xs, mods[layer], norm2_g[layer][None, :], **seq)
        up = _matmul(hn, ffn_w_up[layer], bn=1024, out_dtype=BF16)
        nxt = min(layer + 1, depth - 1)
        xs, hn = _down_proj(up, ffn_conv_w[layer], ffn_conv_b[layer][None, :],
                            _to_bf16(ffn_w_down[layer], rows=704), xs,
                            mods[layer], norm1_g[nxt][None, :], mods[nxt], **seq)
    return xs.reshape(bsz, n, d)[:, n_ctx:, :]
```

```python
import functools
import math

import jax
import jax.numpy as jnp
import numpy as np
from jax import lax
from jax.experimental import pallas as pl
from jax.experimental.pallas import tpu as pltpu

GRID_W = 64
EPS = 1e-6
GLA_HEADS, GLA_DK, GLA_DV, GLA_GATE_RANK, GLA_TAU = 4, 128, 256, 16, 16.0
SC_WIDTH = 1024
DA_HEADS, DA_DQK, DA_DV = 8, 64, 128
ROPE_BASE = 10000.0
ROPE_PAIRS = DA_DQK // 4
ML_HEADS, ML_DK, ML_DV = 4, 128, 256

GLA_CHUNK = 128
ML_CHUNK = 256
ATTN_BQ = 256
ATTN_CHUNK = 512
ATTN_PREP = 544
ROW_TILE = 1088
ROW_CHUNK = 272
LANES = 128
SUBLANES = 8
BF16_ROWS = 16
LOG2E = 1.4426950408889634
VMEM_LIMIT = 56 * 1024 * 1024

F32 = jnp.float32
BF16 = jnp.bfloat16
HI = lax.Precision.HIGHEST
NT = (((1,), (1,)), ((), ()))


def _mm_kernel(x_ref, w_ref, *rest, lane_shift):
    o_ref, wb_ref = rest[-2:]

    @pl.when(pl.program_id(1) == 0)
    def _():
        if lane_shift:
            wide = jnp.concatenate([w_ref[...], rest[0][...]], axis=1)
            wide = pltpu.roll(wide, wide.shape[1] - lane_shift, 1)
            wb_ref[...] = wide[:, :wb_ref.shape[1]].astype(BF16)
        else:
            wb_ref[...] = w_ref[...].astype(BF16)

    o_ref[...] = jnp.dot(x_ref[...], wb_ref[...], preferred_element_type=F32).astype(o_ref.dtype)


def _matmul(x, w, *, bn, col0=0, ncols=None, lane_shift=0, bm=ROW_TILE, out_dtype=F32):
    m, k = x.shape
    ncols = w.shape[1] - col0 if ncols is None else ncols
    j0 = col0 // bn
    in_specs = [pl.BlockSpec((bm, k), lambda j, i: (i, 0)),
                pl.BlockSpec((k, bn), lambda j, i: (0, j0 + j))]
    operands = [x, w]
    if lane_shift:
        per = bn // LANES
        in_specs.append(pl.BlockSpec((k, LANES), lambda j, i: (0, (j0 + j + 1) * per)))
        operands.append(w)
    return pl.pallas_call(
        functools.partial(_mm_kernel, lane_shift=lane_shift),
        grid=(pl.cdiv(ncols, bn), m // bm),
        in_specs=in_specs,
        out_specs=pl.BlockSpec((bm, bn), lambda j, i: (i, j)),
        out_shape=jax.ShapeDtypeStruct((m, ncols), out_dtype),
        scratch_shapes=[pltpu.VMEM((k, bn), BF16)],
        compiler_params=pltpu.CompilerParams(
            dimension_semantics=("arbitrary", "arbitrary"), vmem_limit_bytes=VMEM_LIMIT),
        name="matmul",
    )(*operands)


def _cast_kernel(x_ref, o_ref):
    o_ref[...] = x_ref[...].astype(o_ref.dtype)


def _to_bf16(w, *, rows):
    r, c = w.shape
    return pl.pallas_call(
        _cast_kernel,
        grid=(r // rows,),
        in_specs=[pl.BlockSpec((rows, c), lambda i: (i, 0))],
        out_specs=pl.BlockSpec((rows, c), lambda i: (i, 0)),
        out_shape=jax.ShapeDtypeStruct((r, c), BF16),
        compiler_params=pltpu.CompilerParams(vmem_limit_bytes=VMEM_LIMIT),
        name="to_bf16",
    )(w)


def _norm_mod(x, gain, scale, shift):
    return x * lax.rsqrt(jnp.mean(x * x, axis=-1, keepdims=True) + EPS) * gain * (1.0 + scale) + shift


def _head_rmsnorm(x, width):
    parts = []
    for c0 in range(0, x.shape[1], width):
        xh = x[:, c0:c0 + width]
        parts.append(xh * lax.rsqrt(jnp.mean(xh * xh, axis=-1, keepdims=True) + EPS))
    return parts[0] if len(parts) == 1 else jnp.concatenate(parts, axis=1)


def _sigmoid(x):
    return 1.0 / (1.0 + jnp.exp(-x))


def _mod_row(is_ctx, modb_ref, modc_ref, idx):
    if is_ctx is None:
        return modb_ref[0, idx:idx + 1, :]
    return jnp.where(is_ctx, modc_ref[0, idx:idx + 1, :], modb_ref[0, idx:idx + 1, :])


def _chunk_rows(r0, rc, tiles_per_batch, bm):
    rows = r0 + lax.broadcasted_iota(jnp.int32, (rc, 1), 0)
    return rows, (pl.program_id(0) % tiles_per_batch) * bm + rows


def _chunk_is_ctx(r0, pos, n_ctx):
    return pos < n_ctx if r0 < n_ctx else None


def _edge_rows(r0, rc, bm, tiles_per_batch, n_ctx, n_seq, gate_ref, hprev_ref, hnext_ref):
    hr = hprev_ref.shape[0]
    pos0 = (pl.program_id(0) % tiles_per_batch) * bm
    if r0 == 0:
        before = hprev_ref[...].astype(F32)[hr - 1:hr, :]
        before = jnp.where((pos0 == 0) | (pos0 == n_ctx), 0.0, before)
    else:
        before = gate_ref[r0 - hr:r0, :].astype(F32)[hr - 1:hr, :]
    if r0 + rc == bm:
        after = hnext_ref[...].astype(F32)[0:1, :]
        after = jnp.where((pos0 + bm == n_ctx) | (pos0 + bm == n_seq), 0.0, after)
    else:
        after = gate_ref[r0 + rc:r0 + rc + hr, :].astype(F32)[0:1, :]
    return before, after


def _splits_rows(r0, rc, bm, tiles_per_batch, n_ctx):
    return any(r0 <= n_ctx - k * bm <= r0 + rc for k in range(tiles_per_batch))


def _dwconv3(s_ref, g, before, after, cw_ref, pos, n_ctx):
    rc = g.shape[0]
    s_ref[SUBLANES:SUBLANES + rc, :] = g
    s_ref[SUBLANES - 1:SUBLANES, :] = before
    s_ref[SUBLANES + rc:SUBLANES + rc + 1, :] = after
    prev = s_ref[SUBLANES - 1:SUBLANES - 1 + rc, :]
    nxt = s_ref[SUBLANES + 1:SUBLANES + 1 + rc, :]
    if pos is not None:
        prev = jnp.where(pos == n_ctx, 0.0, prev)
        nxt = jnp.where(pos == n_ctx - 1, 0.0, nxt)
    return prev * cw_ref[0:1, :] + g * cw_ref[1:2, :] + nxt * cw_ref[2:3, :]


def _mod_specs(n_batch, tiles_per_batch, d):
    return [pl.BlockSpec((1, 6, d), lambda i, *_: (i // tiles_per_batch, 0, 0)),
            pl.BlockSpec((1, 6, d), lambda i, *_: (n_batch, 0, 0))]


def _halo_specs(bm, width, col, n_rows, hr):
    hb = bm // hr
    last = n_rows // hr - 1
    return [pl.BlockSpec((hr, width), lambda i, *k: (jnp.maximum(i * hb - 1, 0), col(*k))),
            pl.BlockSpec((hr, width), lambda i, *k: (jnp.minimum((i + 1) * hb, last), col(*k)))]


def _norm_kernel(x_ref, gain_ref, modb_ref, modc_ref, o_ref, *, n_ctx, tiles_per_batch):
    _, pos = _chunk_rows(0, x_ref.shape[0], tiles_per_batch, x_ref.shape[0])
    is_ctx = pos < n_ctx
    o_ref[...] = _norm_mod(x_ref[...], gain_ref[...], _mod_row(is_ctx, modb_ref, modc_ref, 1),
                           _mod_row(is_ctx, modb_ref, modc_ref, 0)).astype(BF16)


def _first_norm(x, gain, mod, *, n_ctx, n_seq, bm=ROW_TILE):
    m, d = x.shape
    tpb = n_seq // bm
    return pl.pallas_call(
        functools.partial(_norm_kernel, n_ctx=n_ctx, tiles_per_batch=tpb),
        grid=(m // bm,),
        in_specs=[pl.BlockSpec((bm, d), lambda i: (i, 0)), pl.BlockSpec((1, d), lambda i: (0, 0))]
        + _mod_specs(mod.shape[0] - 1, tpb, d),
        out_specs=pl.BlockSpec((bm, d), lambda i: (i, 0)),
        out_shape=jax.ShapeDtypeStruct((m, d), BF16),
        compiler_params=pltpu.CompilerParams(vmem_limit_bytes=VMEM_LIMIT),
        name="first_norm",
    )(x, gain, mod, mod)


def _residual_norm(rs, acc, xo_ref, hn_ref, x_ref, modb_ref, modc_ref, gain_ref, nmodb_ref, nmodc_ref,
                   is_ctx, gate_idx, shift_idx):
    x_new = x_ref[rs, :] + _mod_row(is_ctx, modb_ref, modc_ref, gate_idx) * acc
    xo_ref[rs, :] = x_new
    hn_ref[rs, :] = _norm_mod(
        x_new, gain_ref[...], _mod_row(is_ctx, nmodb_ref, nmodc_ref, shift_idx + 1),
        _mod_row(is_ctx, nmodb_ref, nmodc_ref, shift_idx)).astype(BF16)


def _out_kernel(*refs, n_in, n_ctx, tiles_per_batch):
    a_refs = refs[:n_in]
    w_ref, x_ref, modb_ref, modc_ref, gain_ref, xo_ref, hn_ref = refs[n_in:]
    bm = x_ref.shape[0]
    for r0 in range(0, bm, ROW_CHUNK):
        rs = slice(r0, r0 + ROW_CHUNK)
        _, pos = _chunk_rows(r0, ROW_CHUNK, tiles_per_batch, bm)
        acc, k0 = None, 0
        for a_ref in a_refs:
            part = jnp.dot(a_ref[rs, :], w_ref[k0:k0 + a_ref.shape[1], :], preferred_element_type=F32)
            acc = part if acc is None else acc + part
            k0 += a_ref.shape[1]
        _residual_norm(rs, acc, xo_ref, hn_ref, x_ref, modb_ref, modc_ref, gain_ref, modb_ref, modc_ref,
                       _chunk_is_ctx(r0, pos, n_ctx), 2, 3)


def _out_proj(acts, w, x, mod, gain, *, n_ctx, n_seq, bm=ROW_TILE // 2):
    m, d = x.shape
    tpb = n_seq // bm
    rows = lambda width: pl.BlockSpec((bm, width), lambda i: (i, 0))
    return pl.pallas_call(
        functools.partial(_out_kernel, n_in=len(acts), n_ctx=n_ctx, tiles_per_batch=tpb),
        grid=(m // bm,),
        in_specs=[rows(a.shape[1]) for a in acts]
        + [pl.BlockSpec(w.shape, lambda i: (0, 0), pipeline_mode=pl.Buffered(1)), rows(d)]
        + _mod_specs(mod.shape[0] - 1, tpb, d) + [pl.BlockSpec((1, d), lambda i: (0, 0))],
        out_specs=[rows(d), rows(d)],
        out_shape=[jax.ShapeDtypeStruct((m, d), F32), jax.ShapeDtypeStruct((m, d), BF16)],
        compiler_params=pltpu.CompilerParams(vmem_limit_bytes=VMEM_LIMIT),
        name="out_proj",
    )(*acts, w, x, mod, mod, gain)


def _down_kernel(gate_ref, val_ref, hprev_ref, hnext_ref, cw_ref, cb_ref, w_ref, x_ref, modb_ref,
                 modc_ref, gain_ref, nmodb_ref, nmodc_ref, xo_ref, hn_ref, s_ref, *, n_ctx, n_seq,
                 tiles_per_batch):
    bm = gate_ref.shape[0]
    rc = ROW_CHUNK

    @pl.when(pl.program_id(1) == 0)
    def _():
        xo_ref[...] = jnp.zeros_like(xo_ref)

    for r0 in range(0, bm, rc):
        rs = slice(r0, r0 + rc)
        _, pos = _chunk_rows(r0, rc, tiles_per_batch, bm)
        before, after = _edge_rows(r0, rc, bm, tiles_per_batch, n_ctx, n_seq, gate_ref, hprev_ref, hnext_ref)
        split = pos if _splits_rows(r0, rc, bm, tiles_per_batch, n_ctx) else None
        z = _dwconv3(s_ref, gate_ref[rs, :].astype(F32), before, after, cw_ref, split, n_ctx) + cb_ref[...]
        a = (z * _sigmoid(z) * val_ref[rs, :].astype(F32)).astype(BF16)
        xo_ref[rs, :] += jnp.dot(a, w_ref[...], preferred_element_type=F32)

        @pl.when(pl.program_id(1) == pl.num_programs(1) - 1)
        def _():
            _residual_norm(rs, xo_ref[rs, :], xo_ref, hn_ref, x_ref, modb_ref, modc_ref, gain_ref,
                           nmodb_ref, nmodc_ref, _chunk_is_ctx(r0, pos, n_ctx), 5, 0)


def _down_proj(up, conv_w, conv_b, w, x, mod, gain, next_mod, *, n_ctx, n_seq, bm=ROW_TILE // 2,
               bk=1408):
    m = up.shape[0]
    f, d = w.shape
    nkb = f // bk
    tpb = n_seq // bm
    row = lambda i, k: (i, 0)
    return pl.pallas_call(
        functools.partial(_down_kernel, n_ctx=n_ctx, n_seq=n_seq, tiles_per_batch=tpb),
        grid=(m // bm, nkb),
        in_specs=[pl.BlockSpec((bm, bk), lambda i, k: (i, k)),
                  pl.BlockSpec((bm, bk), lambda i, k: (i, k + nkb))]
        + _halo_specs(bm, bk, lambda k: k, m, BF16_ROWS)
        + [pl.BlockSpec((3, bk), lambda i, k: (0, k)), pl.BlockSpec((1, bk), lambda i, k: (0, k)),
           pl.BlockSpec((bk, d), lambda i, k: (k, 0)), pl.BlockSpec((bm, d), row)]
        + _mod_specs(mod.shape[0] - 1, tpb, d) + [pl.BlockSpec((1, d), lambda i, k: (0, 0))]
        + _mod_specs(mod.shape[0] - 1, tpb, d),
        out_specs=[pl.BlockSpec((bm, d), row), pl.BlockSpec((bm, d), row)],
        out_shape=[jax.ShapeDtypeStruct((m, d), F32), jax.ShapeDtypeStruct((m, d), BF16)],
        scratch_shapes=[pltpu.VMEM((ROW_CHUNK + 2 * SUBLANES, bk), F32)],
        compiler_params=pltpu.CompilerParams(
            dimension_semantics=("arbitrary", "arbitrary"), vmem_limit_bytes=VMEM_LIMIT),
        name="down_proj",
    )(up, up, up, up, conv_w, conv_b, w, x, mod, mod, gain, next_mod, next_mod)


def _even_finish_kernel(of_ref, ob_ref, r_ref, sx_ref, sb_ref, sg_ref, sxp_ref, sxn_ref, sgp_ref, sgn_ref,
                        gain_ref, cw_ref, o_ref, s_ref, *, n_ctx, n_seq, tiles_per_batch):
    bm = of_ref.shape[0]
    nv = of_ref.shape[1]
    r = r_ref[...]
    o_ref[:, :nv] = (_head_rmsnorm(of_ref[...] + ob_ref[...], GLA_DV) * gain_ref[...]
                     * (r * _sigmoid(r))).astype(BF16)

    _, pos = _chunk_rows(0, bm, tiles_per_batch, bm)
    pos0 = (pl.program_id(0) % tiles_per_batch) * bm
    hr = sxp_ref.shape[0]
    before = (sxp_ref[...] * sgp_ref[...])[hr - 1:hr, :]
    before = jnp.where((pos0 == 0) | (pos0 == n_ctx), 0.0, before)
    after = (sxn_ref[...] * sgn_ref[...])[0:1, :]
    after = jnp.where((pos0 + bm == n_ctx) | (pos0 + bm == n_seq), 0.0, after)
    split = pos if _splits_rows(0, bm, bm, tiles_per_batch, n_ctx) else None
    conv = _dwconv3(s_ref, sg_ref[...] * sx_ref[...], before, after, cw_ref, split, n_ctx)
    o_ref[:, nv:] = (sb_ref[...] * conv).astype(BF16)


def _even_finish(o_f, o_b, pa, pb, gain, conv_w, *, n_ctx, n_seq, bm=ROW_CHUNK):
    m, nv = o_f.shape
    sc = conv_w.shape[1]
    tpb = n_seq // bm
    col = lambda j: (lambda i: (i, j))
    const = lambda i: (0, 0)
    return pl.pallas_call(
        functools.partial(_even_finish_kernel, n_ctx=n_ctx, n_seq=n_seq, tiles_per_batch=tpb),
        grid=(m // bm,),
        in_specs=[pl.BlockSpec((bm, nv), col(0)), pl.BlockSpec((bm, nv), col(0)),
                  pl.BlockSpec((bm, nv), col(2)),
                  pl.BlockSpec((bm, sc), col(0)), pl.BlockSpec((bm, sc), col(1)), pl.BlockSpec((bm, sc), col(2))]
        + _halo_specs(bm, sc, lambda: 0, m, SUBLANES) + _halo_specs(bm, sc, lambda: 2, m, SUBLANES)
        + [pl.BlockSpec((1, nv), const), pl.BlockSpec((3, sc), const)],
        out_specs=pl.BlockSpec((bm, nv + sc), col(0)),
        out_shape=jax.ShapeDtypeStruct((m, nv + sc), BF16),
        scratch_shapes=[pltpu.VMEM((bm + 2 * SUBLANES, sc), F32)],
        compiler_params=pltpu.CompilerParams(vmem_limit_bytes=VMEM_LIMIT),
        name="even_finish",
    )(o_f, o_b, pa, pb, pb, pb, pb, pb, pb, pb, gain, conv_w)


def _mlstm_prep_kernel(x_ref, xp_ref, xn_ref, cw_ref, scale_ref, o_ref, s_ref, *, n_ctx, n_seq,
                       tiles_per_batch):
    bm = x_ref.shape[0]
    _, pos = _chunk_rows(0, bm, tiles_per_batch, bm)
    before, after = _edge_rows(0, bm, bm, tiles_per_batch, n_ctx, n_seq, x_ref, xp_ref, xn_ref)
    split = pos if _splits_rows(0, bm, bm, tiles_per_batch, n_ctx) else None
    z = _dwconv3(s_ref, x_ref[...], before, after, cw_ref, split, n_ctx)
    o_ref[...] = (z * _sigmoid(z) * scale_ref[...]).astype(BF16)


def _mlstm_prep(p, conv_w, scale, *, col, n_ctx, n_seq, bm=ROW_CHUNK):
    m = p.shape[0]
    width = conv_w.shape[1]
    tpb = n_seq // bm
    const = lambda i: (0, 0)
    return pl.pallas_call(
        functools.partial(_mlstm_prep_kernel, n_ctx=n_ctx, n_seq=n_seq, tiles_per_batch=tpb),
        grid=(m // bm,),
        in_specs=[pl.BlockSpec((bm, width), lambda i: (i, col))]
        + _halo_specs(bm, width, lambda: col, m, SUBLANES)
        + [pl.BlockSpec((3, width), const), pl.BlockSpec((1, width), const)],
        out_specs=pl.BlockSpec((bm, width), lambda i: (i, 0)),
        out_shape=jax.ShapeDtypeStruct((m, width), BF16),
        scratch_shapes=[pltpu.VMEM((bm + 2 * SUBLANES, width), F32)],
        compiler_params=pltpu.CompilerParams(vmem_limit_bytes=VMEM_LIMIT),
        name="mlstm_prep",
    )(p, p, p, conv_w, scale)


def _mlstm_finish_kernel(hf_ref, hb_ref, mo_ref, gain_ref, o_ref):
    o_ref[...] = (_head_rmsnorm(hf_ref[...] + hb_ref[...], ML_DV) * gain_ref[...]
                  * _sigmoid(mo_ref[...])).astype(BF16)


def _mlstm_finish(h_f, h_b, p, gain, *, col, bm=ROW_TILE // 2):
    m, nv = h_f.shape
    return pl.pallas_call(
        _mlstm_finish_kernel,
        grid=(m // bm,),
        in_specs=[pl.BlockSpec((bm, nv), lambda i: (i, 0)), pl.BlockSpec((bm, nv), lambda i: (i, 0)),
                  pl.BlockSpec((bm, nv), lambda i: (i, col)), pl.BlockSpec((1, nv), lambda i: (0, 0))],
        out_specs=pl.BlockSpec((bm, nv), lambda i: (i, 0)),
        out_shape=jax.ShapeDtypeStruct((m, nv), BF16),
        compiler_params=pltpu.CompilerParams(vmem_limit_bytes=VMEM_LIMIT),
        name="mlstm_finish",
    )(h_f, h_b, p, gain)


def _qk_prep(x, cos, sin, gain):
    lane = lax.broadcasted_iota(jnp.int32, x.shape, 1)
    lo = lane < DA_DQK
    x2 = x * x
    ss_lo = jnp.sum(jnp.where(lo, x2, 0.0), axis=-1, keepdims=True)
    ss_hi = jnp.sum(jnp.where(lo, 0.0, x2), axis=-1, keepdims=True)
    y = x * lax.rsqrt(jnp.where(lo, ss_lo, ss_hi) * (1.0 / DA_DQK) + EPS) * gain
    swapped = jnp.where((lane & ROPE_PAIRS) == 0, pltpu.roll(y, LANES - ROPE_PAIRS, 1),
                        pltpu.roll(y, ROPE_PAIRS, 1))
    return y * cos + swapped * sin


def _attn_kernel(lam_ref, q_ref, k_ref, v_ref, cos_ref, sin_ref, qg_ref, kg_ref, og_ref, o_ref,
                 kb_ref, vb_ref, *, n_ctx, chunk, prep_rows, out_scale):
    qi = pl.program_id(2)
    lam = lam_ref[0]
    n = k_ref.shape[1]
    bq = q_ref.shape[1]

    @pl.when(qi == 0)
    def _():
        for r0 in range(0, n, prep_rows):
            rs = slice(r0, r0 + prep_rows)
            kb_ref[rs, :] = _qk_prep(k_ref[0, rs, :], cos_ref[rs, :], sin_ref[rs, :], kg_ref[...]).astype(BF16)
            vb_ref[rs, :] = v_ref[0, rs, :].astype(BF16)

    def attend(nk):
        rows = pl.ds(pl.multiple_of(qi * bq, bq), bq)
        q = _qk_prep(q_ref[0], cos_ref[rows, :], sin_ref[rows, :], qg_ref[...]) * (DA_DQK ** -0.5 * LOG2E)
        q = q.astype(BF16)
        lane = lax.broadcasted_iota(jnp.int32, q.shape, 1)
        zero = jnp.zeros_like(q)
        q2 = jnp.concatenate([jnp.where(lane < DA_DQK, q, zero),
                              jnp.where(lane >= DA_DQK, q, zero)], axis=0)
        m = jnp.full((2 * bq, 1), -jnp.inf, F32)
        l = jnp.zeros((2 * bq, 1), F32)
        acc = jnp.zeros((2 * bq, vb_ref.shape[1]), F32)
        for c0 in range(0, nk, chunk):
            c1 = min(c0 + chunk, nk)
            s = lax.dot_general(q2, kb_ref[c0:c1, :], NT, preferred_element_type=F32)
            m_new = jnp.maximum(m, jnp.max(s, axis=-1, keepdims=True))
            alpha = jnp.exp2(m - m_new)
            p = jnp.exp2(s - m_new)
            l = alpha * l + jnp.sum(p, axis=-1, keepdims=True)
            acc = alpha * acc + jnp.dot(p.astype(BF16), vb_ref[c0:c1, :], preferred_element_type=F32)
            m = m_new
        o = acc / l
        o = o[:bq] - lam * o[bq:]
        o = o * lax.rsqrt(jnp.mean(o * o, axis=-1, keepdims=True) + EPS) * og_ref[...] * out_scale
        o_ref[0] = o.astype(BF16)

    n_ctx_blocks = n_ctx // bq

    @pl.when(qi < n_ctx_blocks)
    def _():
        attend(n_ctx)

    @pl.when(qi >= n_ctx_blocks)
    def _():
        attend(n)


def _diff_attention(p, lam, cos, sin, q_gain, k_gain, out_gain, *, n_ctx, out_scale, bq=ATTN_BQ,
                    chunk=ATTN_CHUNK, prep_rows=ATTN_PREP):
    b, n, _ = p.shape
    h = DA_HEADS
    hd = h * LANES
    const = lambda bi, hi, qi: (0, 0)
    return pl.pallas_call(
        functools.partial(_attn_kernel, n_ctx=n_ctx, chunk=chunk, prep_rows=prep_rows, out_scale=out_scale),
        grid=(b, h, n // bq),
        in_specs=[pl.BlockSpec(memory_space=pltpu.SMEM),
                  pl.BlockSpec((1, bq, LANES), lambda bi, hi, qi: (bi, qi, hi)),
                  pl.BlockSpec((1, n, LANES), lambda bi, hi, qi: (bi, 0, h + hi)),
                  pl.BlockSpec((1, n, LANES), lambda bi, hi, qi: (bi, 0, 2 * h + hi)),
                  pl.BlockSpec((n, LANES), const), pl.BlockSpec((n, LANES), const),
                  pl.BlockSpec((1, LANES), const), pl.BlockSpec((1, LANES), const),
                  pl.BlockSpec((1, LANES), const)],
        out_specs=pl.BlockSpec((1, bq, LANES), lambda bi, hi, qi: (bi, qi, hi)),
        out_shape=jax.ShapeDtypeStruct((b, n, hd), BF16),
        scratch_shapes=[pltpu.VMEM((n, LANES), BF16), pltpu.VMEM((n, LANES), BF16)],
        compiler_params=pltpu.CompilerParams(
            dimension_semantics=("arbitrary", "arbitrary", "arbitrary"), vmem_limit_bytes=VMEM_LIMIT),
        name="diff_attention",
    )(lam, p, p, p, cos, sin, q_gain, k_gain, out_gain)


def _scan_chunk(t, n_ctx_chunks, n_chunks, reverse):
    if not reverse:
        return t
    return jnp.where(t < n_ctx_chunks, n_ctx_chunks - 1 - t, n_chunks - 1 - (t - n_ctx_chunks))


def _gla_constants(L, reverse):
    nlev = int(math.log2(L))
    idx = np.arange(L)
    i, t = idx[:, None], idx[None, :]
    if reverse:
        i, t = L - 1 - i, L - 1 - t
    rs = [(t <= i), (t > i)]
    am = [(i == t)]
    for lev in range(nlev):
        m = L >> (lev + 1)
        blk_i, blk_t = i // (2 * m), t // (2 * m)
        mid = blk_i * 2 * m + m
        q_role = i >= mid
        rs.append(np.where(q_role, (t >= mid) & (t <= i), (t > i) & (t < mid)) & (blk_i == blk_t))
        am.append((blk_i == blk_t) & q_role & (t < mid))
    return (np.stack(rs).astype(np.float32).reshape((nlev + 2) * L, L),
            np.stack(am).astype(np.float32))


def _gla_kernel(q_ref, k_ref, v_ref, glr_ref, w2_ref, gb_ref, rsum_ref, amask_ref, o_ref, st_ref):
    L = q_ref.shape[1]
    nlev = amask_ref.shape[0] - 1

    @pl.when(pl.program_id(1) == 0)
    def _():
        st_ref[...] = jnp.zeros_like(st_ref)

    z = jnp.dot(glr_ref[0].astype(BF16), w2_ref[...], preferred_element_type=F32) + gb_ref[...]
    g = (jnp.minimum(z, 0.0) - jnp.log(1.0 + jnp.exp(-jnp.abs(z)))) * (1.0 / GLA_TAU)
    e_all = jnp.dot(rsum_ref[...], g, precision=HI, preferred_element_type=F32)
    b_tot = jnp.sum(g, axis=0, keepdims=True)

    for h in range(GLA_HEADS):
        ck = slice(h * GLA_DK, (h + 1) * GLA_DK)
        cv = slice(h * GLA_DV, (h + 1) * GLA_DV)
        q = q_ref[0, :, ck] * (GLA_DK ** -0.5)
        k = k_ref[0, :, ck]
        v = v_ref[0, :, cv].astype(BF16)
        st = st_ref[h]
        a = amask_ref[0] * lax.dot_general(q.astype(BF16), k.astype(BF16), NT, preferred_element_type=F32)
        for lev in range(nlev):
            e = jnp.exp(e_all[(2 + lev) * L:(3 + lev) * L, ck])
            a = a + amask_ref[1 + lev] * lax.dot_general(
                (q * e).astype(BF16), (k * e).astype(BF16), NT, preferred_element_type=F32)
        qe = (q * jnp.exp(e_all[0:L, ck])).astype(BF16)
        o = lax.dot_general(qe, st.astype(BF16), NT, preferred_element_type=F32)
        o = o + jnp.dot(a.astype(BF16), v, preferred_element_type=F32)
        o_ref[0, :, cv] = o
        kd = (k * jnp.exp(e_all[L:2 * L, ck])).astype(BF16)
        st_ref[h] = jnp.exp(b_tot[:, ck]) * st + jnp.dot(
            v.T, kd, preferred_element_type=F32)


def _gla_scan(p, w2, gb, *, n_ctx, reverse, L=GLA_CHUNK):
    b, n, _ = p.shape
    nc, ncc = n // L, n_ctx // L
    rsum, amask = _gla_constants(L, reverse)
    chunk = functools.partial(_scan_chunk, n_ctx_chunks=ncc, n_chunks=nc, reverse=reverse)
    hk, hv = GLA_HEADS * GLA_DK, GLA_HEADS * GLA_DV
    cols = lambda width, start: pl.BlockSpec((1, L, width), lambda bi, t: (bi, chunk(t), start // width))
    row = lambda bi, t: (bi, chunk(t), 0)
    const2 = lambda bi, t: (0, 0)
    const3 = lambda bi, t: (0, 0, 0)
    return pl.pallas_call(
        _gla_kernel,
        grid=(b, nc),
        in_specs=[cols(hk, 0), cols(hk, hk), cols(hv, 2 * hk), cols(LANES, 2 * hk + 2 * hv),
                  pl.BlockSpec(w2.shape, const2), pl.BlockSpec(gb.shape, const2),
                  pl.BlockSpec(rsum.shape, const2), pl.BlockSpec(amask.shape, const3)],
        out_specs=pl.BlockSpec((1, L, hv), row),
        out_shape=jax.ShapeDtypeStruct((b, n, hv), F32),
        scratch_shapes=[pltpu.VMEM((GLA_HEADS, GLA_DV, GLA_DK), F32)],
        compiler_params=pltpu.CompilerParams(
            dimension_semantics=("arbitrary", "arbitrary"), vmem_limit_bytes=VMEM_LIMIT),
        name="gla_scan_bwd" if reverse else "gla_scan_fwd",
    )(p, p, p, p, w2, gb, jnp.asarray(rsum), jnp.asarray(amask))


def _log_sigmoid(x):
    return jnp.minimum(x, 0.0) - jnp.log(1.0 + jnp.exp(-jnp.abs(x)))


def _mlstm_kernel(q_ref, k_ref, v_ref, gc_ref, gr_ref, tri_ref, o_ref, c_ref, n_ref, m_ref, *, reverse):
    L = q_ref.shape[1]
    H = ML_HEADS

    @pl.when(pl.program_id(1) == 0)
    def _():
        c_ref[...] = jnp.zeros_like(c_ref)
        n_ref[...] = jnp.zeros_like(n_ref)
        m_ref[...] = jnp.zeros_like(m_ref)

    tri = tri_ref[...]
    gc = gc_ref[0]
    gr = gr_ref[0]
    ic_col, ic_row = gc[:, :H], gr[:H, :]
    b_col = jnp.dot(tri, _log_sigmoid(gc[:, H:]), precision=HI, preferred_element_type=F32)
    b_row = lax.dot_general(_log_sigmoid(gr[H:, :]), tri, NT, precision=HI, preferred_element_type=F32)
    last = 0 if reverse else L - 1
    causal = tri > 0.5

    for h in range(H):
        ck = slice(h * ML_DK, (h + 1) * ML_DK)
        cv = slice(h * ML_DV, (h + 1) * ML_DV)
        q = q_ref[0, :, ck]
        k = k_ref[0, :, ck]
        v = v_ref[0, :, cv].astype(BF16)
        c, n, m = c_ref[h], n_ref[h], m_ref[h]
        bc, br = b_col[:, h:h + 1], b_row[h:h + 1, :]
        icc, icr = ic_col[:, h:h + 1], ic_row[h:h + 1, :]
        b_last = bc[last:last + 1, :]

        a = bc + m
        dmat = jnp.where(causal, bc - br + icr, -jnp.inf)
        m_t = jnp.maximum(a, jnp.max(dmat, axis=-1, keepdims=True))
        w_inter = jnp.exp(a - m_t)
        s = lax.dot_general(q, k, NT, preferred_element_type=F32) * jnp.exp(dmat - m_t)
        num = w_inter * jnp.dot(q, c.astype(BF16), preferred_element_type=F32) + jnp.dot(
            s.astype(BF16), v, preferred_element_type=F32)
        den = w_inter * jnp.sum(q.astype(F32) * n, axis=-1, keepdims=True) + jnp.sum(
            s, axis=-1, keepdims=True)
        o_ref[0, :, cv] = num / jnp.maximum(jnp.abs(den), jnp.exp(-m_t))

        gs_col = b_last - bc + icc
        gs_row = b_last - br + icr
        m_new = jnp.maximum(b_last + m, jnp.max(gs_row, axis=-1, keepdims=True))
        decay = jnp.exp(b_last + m - m_new)
        wk = jnp.exp(gs_col - m_new) * k.astype(F32)
        c_ref[h] = decay * c + jnp.dot(wk.astype(BF16).T, v, preferred_element_type=F32)
        n_ref[h] = decay * n + jnp.sum(wk, axis=0, keepdims=True)
        m_ref[h] = m_new


def _mlstm_scan(qk, p, gates, *, v_col, n_ctx, reverse, L=ML_CHUNK):
    b, n, _ = qk.shape
    hk, hv = ML_HEADS * ML_DK, ML_HEADS * ML_DV
    nc, ncc = n // L, n_ctx // L
    idx = np.arange(L)
    tri = (idx[None, :] >= idx[:, None]) if reverse else (idx[None, :] <= idx[:, None])
    tri = jnp.asarray(tri.astype(np.float32))
    gates_t = jnp.swapaxes(gates, 1, 2)
    chunk = functools.partial(_scan_chunk, n_ctx_chunks=ncc, n_chunks=nc, reverse=reverse)
    col = lambda j: (lambda bi, t: (bi, chunk(t), j))
    return pl.pallas_call(
        functools.partial(_mlstm_kernel, reverse=reverse),
        grid=(b, nc),
        in_specs=[pl.BlockSpec((1, L, hk), col(0)), pl.BlockSpec((1, L, hk), col(1)),
                  pl.BlockSpec((1, L, hv), col(v_col)), pl.BlockSpec((1, L, 2 * ML_HEADS), col(0)),
                  pl.BlockSpec((1, 2 * ML_HEADS, L), lambda bi, t: (bi, 0, chunk(t))),
                  pl.BlockSpec((L, L), lambda bi, t: (0, 0))],
        out_specs=pl.BlockSpec((1, L, hv), col(0)),
        out_shape=jax.ShapeDtypeStruct((b, n, hv), F32),
        scratch_shapes=[pltpu.VMEM((ML_HEADS, ML_DK, ML_DV), F32),
                        pltpu.VMEM((ML_HEADS, 1, ML_DK), F32),
                        pltpu.VMEM((ML_HEADS, 1, 1), F32)],
        compiler_params=pltpu.CompilerParams(
            dimension_semantics=("arbitrary", "arbitrary"), vmem_limit_bytes=VMEM_LIMIT),
        name="mlstm_scan_bwd" if reverse else "mlstm_scan_fwd",
    )(qk, qk, p, gates, gates_t, tri)


def _silu(x):
    return x * jax.nn.sigmoid(x)


def _rope_tables(n_lat, n_ctx):
    rows = n_lat // GRID_W
    row = jnp.repeat(jnp.arange(rows, dtype=F32), GRID_W)
    col = jnp.tile(jnp.arange(GRID_W, dtype=F32), rows)
    inv = jnp.power(ROPE_BASE, -jnp.arange(ROPE_PAIRS, dtype=F32) / ROPE_PAIRS)
    ang_r, ang_c = row[:, None] * inv, col[:, None] * inv
    cos = jnp.concatenate([jnp.cos(ang_r)] * 2 + [jnp.cos(ang_c)] * 2, axis=-1)
    sin = jnp.concatenate([-jnp.sin(ang_r), jnp.sin(ang_r), -jnp.sin(ang_c), jnp.sin(ang_c)], axis=-1)
    pad = ((n_ctx, 0), (0, 0))
    cos, sin = jnp.pad(cos, pad, constant_values=1.0), jnp.pad(sin, pad)
    return jnp.tile(cos, (1, 2)), jnp.tile(sin, (1, 2))


def _even_mixer(hn, bsz, w_in, gate_w2, gate_b, gla_norm_g, sc_conv_w, seq):
    nq = GLA_HEADS * GLA_DK
    nv = GLA_HEADS * GLA_DV
    gate_col = 2 * nq + 2 * nv
    pa = _matmul(hn, w_in, bn=640, ncols=gate_col + LANES)
    pb = _matmul(hn, w_in, bn=SC_WIDTH, col0=gate_col, ncols=3 * SC_WIDTH, lane_shift=2 * GLA_GATE_RANK)
    pa3 = pa.reshape(bsz, -1, pa.shape[1])
    outs = []
    for direction in range(2):
        w2 = jnp.zeros((LANES, nq), F32).at[
            direction * GLA_GATE_RANK:(direction + 1) * GLA_GATE_RANK].set(gate_w2[direction])
        outs.append(_gla_scan(pa3, w2.astype(BF16), gate_b[direction][None, :],
                              n_ctx=seq["n_ctx"], reverse=bool(direction)).reshape(-1, nv))
    gain = jnp.tile(gla_norm_g, GLA_HEADS)[None, :]
    return [_even_finish(outs[0], outs[1], pa, pb, gain, sc_conv_w, **seq)]


def _odd_mixer(hn, bsz, rope, layer, w_in, qn_g, kn_g, lam_p, subln_g, ml_conv_w, ml_gate_b,
               ml_norm_g, seq):
    lam_init = 0.8 - 0.6 * math.exp(-0.3 * layer)
    lam = (jnp.exp(jnp.sum(lam_p[0] * lam_p[1])) - jnp.exp(jnp.sum(lam_p[2] * lam_p[3])) + lam_init)
    na = DA_HEADS * 2 * DA_DQK
    nk = ML_HEADS * ML_DK
    nv = ML_HEADS * ML_DV
    n_main = 3 * na + 2 * nk + 2 * nv
    p = _matmul(hn, w_in, bn=1024, ncols=n_main)
    mg = _matmul(hn, w_in, bn=LANES, col0=n_main, ncols=LANES)[:, :4 * ML_HEADS]
    p3 = p.reshape(bsz, -1, n_main)
    n = p3.shape[1]

    cos, sin = rope
    tile2 = lambda g: jnp.tile(g, 2)[None, :]
    da = _diff_attention(p3, lam.reshape(1).astype(F32), cos, sin, tile2(qn_g), tile2(kn_g),
                         subln_g[None, :], n_ctx=seq["n_ctx"], out_scale=1.0 - lam_init)

    scale = jnp.concatenate([jnp.ones((nk,), F32), jnp.full((nk,), ML_DK ** -0.5, F32)])[None, :]
    hqk = _mlstm_prep(p, ml_conv_w, scale, col=3 * na // (2 * nk), **seq).reshape(bsz, n, 2 * nk)
    gates = (mg + ml_gate_b).reshape(bsz, n, 4 * ML_HEADS)
    outs = []
    for direction in range(2):
        g_dir = gates[..., direction * 2 * ML_HEADS:(direction + 1) * 2 * ML_HEADS]
        outs.append(_mlstm_scan(hqk, p3, g_dir, v_col=(3 * na + 2 * nk) // nv, n_ctx=seq["n_ctx"],
                                reverse=bool(direction)).reshape(-1, nv))
    m = _mlstm_finish(outs[0], outs[1], p, jnp.tile(ml_norm_g, ML_HEADS)[None, :],
                      col=(3 * na + 2 * nk + nv) // nv)
    return [da.reshape(-1, na), m]


def _modulation(c, c_ctx, ada_w, ada_b):
    cc = jnp.concatenate([c, c_ctx[None, :]], axis=0)
    return (_silu(cc) @ ada_w + ada_b).reshape(cc.shape[0], 6, -1)


def kernel(x, c, ctx, c_ctx, ada_w, ada_b, norm1_g, norm2_g, ev_w_in, ev_w_out, gla_gate_w2, gla_gate_b, gla_norm_g, sc_conv_w, od_w_in, od_w_out, da_qnorm_g, da_knorm_g, da_lambda, da_subln_g, ml_conv_w, ml_gate_b, ml_norm_g, ffn_w_up, ffn_conv_w, ffn_conv_b, ffn_w_down):
    bsz, n_lat, d = x.shape
    n_ctx = ctx.shape[1]
    n = n_ctx + n_lat
    depth = ada_w.shape[0]
    assert n % ROW_TILE == 0 and n_ctx <= ROW_CHUNK
    rope = _rope_tables(n_lat, n_ctx)
    seq = dict(n_ctx=n_ctx, n_seq=n)

    mods = [_modulation(c, c_ctx, ada_w[layer], ada_b[layer]) for layer in range(depth)]
    xs = jnp.concatenate([ctx, x], axis=1).reshape(bsz * n, d)
    hn = _first_norm(xs, norm1_g[0][None, :], mods[0], **seq)
    for layer in range(depth):
        i = layer // 2
        if layer % 2 == 0:
            mix = _even_mixer(hn, bsz, ev_w_in[i], gla_gate_w2[i], gla_gate_b[i], gla_norm_g[i],
                              sc_conv_w[i], seq)
            w_out = ev_w_out[i]
        else:
            mix = _odd_mixer(hn, bsz, rope, layer, od_w_in[i], da_qnorm_g[i], da_knorm_g[i],
                             da_lambda[i], da_subln_g[i], ml_conv_w[i], ml_gate_b[i], ml_norm_g[i], seq)
            w_out = od_w_out[i]
        xs, hn = _out_proj(mix, _to_bf16(w_out, rows=512), xs, mods[layer], norm2_g[layer][None, :], **seq)
        up = _matmul(hn, ffn_w_up[layer], bn=1024, out_dtype=BF16)
        nxt = min(layer + 1, depth - 1)
        xs, hn = _down_proj(up, ffn_conv_w[layer], ffn_conv_b[layer][None, :],
                            _to_bf16(ffn_w_down[layer], rows=704), xs,
                            mods[layer], norm1_g[nxt][None, :], mods[nxt], **seq)
    return xs.reshape(bsz, n, d)[:, n_ctx:, :]
```

```python
import functools
import math

import jax
import jax.numpy as jnp
import numpy as np
from jax import lax
from jax.experimental import pallas as pl
from jax.experimental.pallas import tpu as pltpu

GRID_W = 64
EPS = 1e-6
GLA_HEADS, GLA_DK, GLA_DV, GLA_GATE_RANK, GLA_TAU = 4, 128, 256, 16, 16.0
SC_WIDTH = 1024
DA_HEADS, DA_DQK, DA_DV = 8, 64, 128
ROPE_BASE = 10000.0
ROPE_PAIRS = DA_DQK // 4
ML_HEADS, ML_DK, ML_DV = 4, 128, 256

GLA_CHUNK = 128
ML_CHUNK = 256
ATTN_BQ = 256
ATTN_CHUNK = 1024
ATTN_PREP = 544
ROW_TILE = 1088
ROW_CHUNK = 272
LANES = 128
SUBLANES = 8
BF16_ROWS = 16
LOG2E = 1.4426950408889634
VMEM_LIMIT = 56 * 1024 * 1024

F32 = jnp.float32
BF16 = jnp.bfloat16
HI = lax.Precision.HIGHEST
NT = (((1,), (1,)), ((), ()))


def _mm_kernel(x_ref, w_ref, *rest, lane_shift):
    o_ref, wb_ref = rest[-2:]

    @pl.when(pl.program_id(1) == 0)
    def _():
        if lane_shift:
            wide = jnp.concatenate([w_ref[...], rest[0][...]], axis=1)
            wide = pltpu.roll(wide, wide.shape[1] - lane_shift, 1)
            wb_ref[...] = wide[:, :wb_ref.shape[1]].astype(BF16)
        else:
            wb_ref[...] = w_ref[...].astype(BF16)

    o_ref[...] = jnp.dot(x_ref[...], wb_ref[...], preferred_element_type=F32).astype(o_ref.dtype)


def _matmul(x, w, layer, *, bn, col0=0, ncols=None, lane_shift=0, bm=ROW_TILE, out_dtype=F32):
    m, k = x.shape
    ncols = w.shape[2] - col0 if ncols is None else ncols
    j0 = col0 // bn
    in_specs = [pl.BlockSpec((bm, k), lambda j, i: (i, 0)),
                pl.BlockSpec((None, k, bn), lambda j, i: (layer, 0, j0 + j))]
    operands = [x, w]
    if lane_shift:
        per = bn // LANES
        in_specs.append(pl.BlockSpec((None, k, LANES), lambda j, i: (layer, 0, (j0 + j + 1) * per)))
        operands.append(w)
    return pl.pallas_call(
        functools.partial(_mm_kernel, lane_shift=lane_shift),
        grid=(pl.cdiv(ncols, bn), m // bm),
        in_specs=in_specs,
        out_specs=pl.BlockSpec((bm, bn), lambda j, i: (i, j)),
        out_shape=jax.ShapeDtypeStruct((m, ncols), out_dtype),
        scratch_shapes=[pltpu.VMEM((k, bn), BF16)],
        compiler_params=pltpu.CompilerParams(
            dimension_semantics=("arbitrary", "arbitrary"), vmem_limit_bytes=VMEM_LIMIT),
        name="matmul",
    )(*operands)


def _cast_kernel(x_ref, o_ref):
    o_ref[...] = x_ref[...].astype(o_ref.dtype)


def _to_bf16(w, layer, *, rows):
    _, r, c = w.shape
    return pl.pallas_call(
        _cast_kernel,
        grid=(r // rows,),
        in_specs=[pl.BlockSpec((None, rows, c), lambda i: (layer, i, 0))],
        out_specs=pl.BlockSpec((rows, c), lambda i: (i, 0)),
        out_shape=jax.ShapeDtypeStruct((r, c), BF16),
        compiler_params=pltpu.CompilerParams(vmem_limit_bytes=VMEM_LIMIT),
        name="to_bf16",
    )(w)


def _norm_mod(x, gain, scale, shift):
    return x * lax.rsqrt(jnp.mean(x * x, axis=-1, keepdims=True) + EPS) * gain * (1.0 + scale) + shift


def _head_rmsnorm(x, width):
    parts = []
    for c0 in range(0, x.shape[1], width):
        xh = x[:, c0:c0 + width]
        parts.append(xh * lax.rsqrt(jnp.mean(xh * xh, axis=-1, keepdims=True) + EPS))
    return parts[0] if len(parts) == 1 else jnp.concatenate(parts, axis=1)


def _sigmoid(x):
    return 1.0 / (1.0 + jnp.exp(-x))


def _mod_row(is_ctx, modb_ref, modc_ref, idx):
    if is_ctx is None:
        return modb_ref[0, idx:idx + 1, :]
    return jnp.where(is_ctx, modc_ref[0, idx:idx + 1, :], modb_ref[0, idx:idx + 1, :])


def _chunk_rows(r0, rc, tiles_per_batch, bm):
    rows = r0 + lax.broadcasted_iota(jnp.int32, (rc, 1), 0)
    return rows, (pl.program_id(0) % tiles_per_batch) * bm + rows


def _chunk_is_ctx(r0, pos, n_ctx):
    return pos < n_ctx if r0 < n_ctx else None


def _edge_rows(r0, rc, bm, tiles_per_batch, n_ctx, n_seq, gate_ref, hprev_ref, hnext_ref):
    hr = hprev_ref.shape[0]
    pos0 = (pl.program_id(0) % tiles_per_batch) * bm
    if r0 == 0:
        before = hprev_ref[...].astype(F32)[hr - 1:hr, :]
        before = jnp.where((pos0 == 0) | (pos0 == n_ctx), 0.0, before)
    else:
        before = gate_ref[r0 - hr:r0, :].astype(F32)[hr - 1:hr, :]
    if r0 + rc == bm:
        after = hnext_ref[...].astype(F32)[0:1, :]
        after = jnp.where((pos0 + bm == n_ctx) | (pos0 + bm == n_seq), 0.0, after)
    else:
        after = gate_ref[r0 + rc:r0 + rc + hr, :].astype(F32)[0:1, :]
    return before, after


def _splits_rows(r0, rc, bm, tiles_per_batch, n_ctx):
    return any(r0 <= n_ctx - k * bm <= r0 + rc for k in range(tiles_per_batch))


def _dwconv3(s_ref, g, before, after, cw_ref, pos, n_ctx):
    rc = g.shape[0]
    s_ref[SUBLANES:SUBLANES + rc, :] = g
    s_ref[SUBLANES - 1:SUBLANES, :] = before
    s_ref[SUBLANES + rc:SUBLANES + rc + 1, :] = after
    prev = s_ref[SUBLANES - 1:SUBLANES - 1 + rc, :]
    nxt = s_ref[SUBLANES + 1:SUBLANES + 1 + rc, :]
    if pos is not None:
        prev = jnp.where(pos == n_ctx, 0.0, prev)
        nxt = jnp.where(pos == n_ctx - 1, 0.0, nxt)
    return prev * cw_ref[0:1, :] + g * cw_ref[1:2, :] + nxt * cw_ref[2:3, :]


def _mod_specs(n_batch, tiles_per_batch, d):
    return [pl.BlockSpec((1, 6, d), lambda i, *_: (i // tiles_per_batch, 0, 0)),
            pl.BlockSpec((1, 6, d), lambda i, *_: (n_batch, 0, 0))]


def _halo_specs(bm, width, col, n_rows, hr):
    hb = bm // hr
    last = n_rows // hr - 1
    return [pl.BlockSpec((hr, width), lambda i, *k: (jnp.maximum(i * hb - 1, 0), col(*k))),
            pl.BlockSpec((hr, width), lambda i, *k: (jnp.minimum((i + 1) * hb, last), col(*k)))]


def _norm_kernel(x_ref, gain_ref, modb_ref, modc_ref, o_ref, *, n_ctx, tiles_per_batch):
    _, pos = _chunk_rows(0, x_ref.shape[0], tiles_per_batch, x_ref.shape[0])
    is_ctx = pos < n_ctx
    o_ref[...] = _norm_mod(x_ref[...], gain_ref[...], _mod_row(is_ctx, modb_ref, modc_ref, 1),
                           _mod_row(is_ctx, modb_ref, modc_ref, 0)).astype(BF16)


def _first_norm(x, gain, mod, *, n_ctx, n_seq, bm=ROW_TILE):
    m, d = x.shape
    tpb = n_seq // bm
    return pl.pallas_call(
        functools.partial(_norm_kernel, n_ctx=n_ctx, tiles_per_batch=tpb),
        grid=(m // bm,),
        in_specs=[pl.BlockSpec((bm, d), lambda i: (i, 0)), pl.BlockSpec((1, d), lambda i: (0, 0))]
        + _mod_specs(mod.shape[0] - 1, tpb, d),
        out_specs=pl.BlockSpec((bm, d), lambda i: (i, 0)),
        out_shape=jax.ShapeDtypeStruct((m, d), BF16),
        compiler_params=pltpu.CompilerParams(vmem_limit_bytes=VMEM_LIMIT),
        name="first_norm",
    )(x, gain, mod, mod)


def _residual_norm(rs, acc, xo_ref, hn_ref, x_ref, modb_ref, modc_ref, gain_ref, nmodb_ref, nmodc_ref,
                   is_ctx, gate_idx, shift_idx):
    x_new = x_ref[rs, :] + _mod_row(is_ctx, modb_ref, modc_ref, gate_idx) * acc
    xo_ref[rs, :] = x_new
    hn_ref[rs, :] = _norm_mod(
        x_new, gain_ref[...], _mod_row(is_ctx, nmodb_ref, nmodc_ref, shift_idx + 1),
        _mod_row(is_ctx, nmodb_ref, nmodc_ref, shift_idx)).astype(BF16)


def _out_kernel(*refs, n_in, n_ctx, tiles_per_batch):
    a_refs = refs[:n_in]
    w_ref, x_ref, modb_ref, modc_ref, gain_ref, xo_ref, hn_ref = refs[n_in:]
    bm = x_ref.shape[0]
    for r0 in range(0, bm, ROW_CHUNK):
        rs = slice(r0, r0 + ROW_CHUNK)
        _, pos = _chunk_rows(r0, ROW_CHUNK, tiles_per_batch, bm)
        acc, k0 = None, 0
        for a_ref in a_refs:
            part = jnp.dot(a_ref[rs, :], w_ref[k0:k0 + a_ref.shape[1], :], preferred_element_type=F32)
            acc = part if acc is None else acc + part
            k0 += a_ref.shape[1]
        _residual_norm(rs, acc, xo_ref, hn_ref, x_ref, modb_ref, modc_ref, gain_ref, modb_ref, modc_ref,
                       _chunk_is_ctx(r0, pos, n_ctx), 2, 3)


def _out_proj(acts, w, x, mod, gain, *, n_ctx, n_seq, bm=ROW_TILE // 2):
    m, d = x.shape
    tpb = n_seq // bm
    rows = lambda width: pl.BlockSpec((bm, width), lambda i: (i, 0))
    return pl.pallas_call(
        functools.partial(_out_kernel, n_in=len(acts), n_ctx=n_ctx, tiles_per_batch=tpb),
        grid=(m // bm,),
        in_specs=[rows(a.shape[1]) for a in acts]
        + [pl.BlockSpec(w.shape, lambda i: (0, 0), pipeline_mode=pl.Buffered(1)), rows(d)]
        + _mod_specs(mod.shape[0] - 1, tpb, d) + [pl.BlockSpec((1, d), lambda i: (0, 0))],
        out_specs=[rows(d), rows(d)],
        out_shape=[jax.ShapeDtypeStruct((m, d), F32), jax.ShapeDtypeStruct((m, d), BF16)],
        compiler_params=pltpu.CompilerParams(vmem_limit_bytes=VMEM_LIMIT),
        name="out_proj",
    )(*acts, w, x, mod, mod, gain)


def _up_kernel(x_ref, xp_ref, xn_ref, wg_ref, wv_ref, cw_ref, cb_ref, o_ref, wgb_ref, wvb_ref, s_ref, *,
               n_ctx, n_seq, tiles_per_batch):
    i = pl.program_id(1)
    bm, rc, hr = x_ref.shape[0], ROW_CHUNK, xp_ref.shape[0]

    @pl.when(i == 0)
    def _():
        wgb_ref[...] = wg_ref[...].astype(BF16)
        wvb_ref[...] = wv_ref[...].astype(BF16)

    pos0 = (i % tiles_per_batch) * bm
    before = jnp.dot(xp_ref[...], wgb_ref[...], preferred_element_type=F32)[hr - 1:hr, :]
    after = jnp.dot(xn_ref[...], wgb_ref[...], preferred_element_type=F32)[0:1, :]
    s_ref[SUBLANES - 1:SUBLANES, :] = jnp.where((pos0 == 0) | (pos0 == n_ctx), 0.0, before)
    s_ref[SUBLANES + bm:SUBLANES + bm + 1, :] = jnp.where(
        (pos0 + bm == n_ctx) | (pos0 + bm == n_seq), 0.0, after)
    for r0 in range(0, bm, rc):
        s_ref[SUBLANES + r0:SUBLANES + r0 + rc, :] = jnp.dot(
            x_ref[r0:r0 + rc, :], wgb_ref[...], preferred_element_type=F32)
    for r0 in range(0, bm, rc):
        rs = slice(r0, r0 + rc)
        prev = s_ref[SUBLANES - 1 + r0:SUBLANES - 1 + r0 + rc, :]
        nxt = s_ref[SUBLANES + 1 + r0:SUBLANES + 1 + r0 + rc, :]
        if _splits_rows(r0, rc, bm, tiles_per_batch, n_ctx):
            pos = pos0 + r0 + lax.broadcasted_iota(jnp.int32, (rc, 1), 0)
            prev = jnp.where(pos == n_ctx, 0.0, prev)
            nxt = jnp.where(pos == n_ctx - 1, 0.0, nxt)
        z = (prev * cw_ref[0:1, :] + s_ref[SUBLANES + r0:SUBLANES + r0 + rc, :] * cw_ref[1:2, :]
             + nxt * cw_ref[2:3, :] + cb_ref[...])
        val = jnp.dot(x_ref[rs, :], wvb_ref[...], preferred_element_type=F32)
        o_ref[rs, :] = (z * _sigmoid(z) * val).astype(BF16)


def _ffn_up(x, w, layer, conv_w, conv_b, *, n_ctx, n_seq, bn=512, bm=ROW_TILE):
    m, k = x.shape
    f = w.shape[2] // 2
    nj = f // bn
    tpb = n_seq // bm
    hb = bm // BF16_ROWS
    last = m // BF16_ROWS - 1
    return pl.pallas_call(
        functools.partial(_up_kernel, n_ctx=n_ctx, n_seq=n_seq, tiles_per_batch=tpb),
        grid=(nj, m // bm),
        in_specs=[pl.BlockSpec((bm, k), lambda j, i: (i, 0)),
                  pl.BlockSpec((BF16_ROWS, k), lambda j, i: (jnp.maximum(i * hb - 1, 0), 0)),
                  pl.BlockSpec((BF16_ROWS, k), lambda j, i: (jnp.minimum((i + 1) * hb, last), 0)),
                  pl.BlockSpec((None, k, bn), lambda j, i: (layer, 0, j)),
                  pl.BlockSpec((None, k, bn), lambda j, i: (layer, 0, j + nj)),
                  pl.BlockSpec((3, bn), lambda j, i: (0, j)), pl.BlockSpec((1, bn), lambda j, i: (0, j))],
        out_specs=pl.BlockSpec((bm, bn), lambda j, i: (i, j)),
        out_shape=jax.ShapeDtypeStruct((m, f), BF16),
        scratch_shapes=[pltpu.VMEM((k, bn), BF16), pltpu.VMEM((k, bn), BF16),
                        pltpu.VMEM((bm + 2 * SUBLANES, bn), F32)],
        compiler_params=pltpu.CompilerParams(
            dimension_semantics=("arbitrary", "arbitrary"), vmem_limit_bytes=VMEM_LIMIT),
        name="ffn_up",
    )(x, x, x, w, w, conv_w, conv_b)


def _down_kernel(a_ref, w_ref, x_ref, modb_ref, modc_ref, gain_ref, nmodb_ref, nmodc_ref, xo_ref, hn_ref, *,
                 n_ctx, tiles_per_batch):
    bm = a_ref.shape[0]
    rc = ROW_CHUNK

    @pl.when(pl.program_id(1) == 0)
    def _():
        xo_ref[...] = jnp.zeros_like(xo_ref)

    for r0 in range(0, bm, rc):
        xo_ref[r0:r0 + rc, :] += jnp.dot(a_ref[r0:r0 + rc, :], w_ref[...], preferred_element_type=F32)

    @pl.when(pl.program_id(1) == pl.num_programs(1) - 1)
    def _():
        for r0 in range(0, bm, rc):
            rs = slice(r0, r0 + rc)
            _, pos = _chunk_rows(r0, rc, tiles_per_batch, bm)
            _residual_norm(rs, xo_ref[rs, :], xo_ref, hn_ref, x_ref, modb_ref, modc_ref, gain_ref,
                           nmodb_ref, nmodc_ref, _chunk_is_ctx(r0, pos, n_ctx), 5, 0)


def _down_proj(a, w, x, mod, gain, next_mod, *, n_ctx, n_seq, bm=ROW_TILE // 2, bk=1408):
    m, f = a.shape
    d = w.shape[1]
    tpb = n_seq // bm
    row = lambda i, k: (i, 0)
    return pl.pallas_call(
        functools.partial(_down_kernel, n_ctx=n_ctx, tiles_per_batch=tpb),
        grid=(m // bm, f // bk),
        in_specs=[pl.BlockSpec((bm, bk), lambda i, k: (i, k)), pl.BlockSpec((bk, d), lambda i, k: (k, 0)),
                  pl.BlockSpec((bm, d), row)]
        + _mod_specs(mod.shape[0] - 1, tpb, d) + [pl.BlockSpec((1, d), lambda i, k: (0, 0))]
        + _mod_specs(mod.shape[0] - 1, tpb, d),
        out_specs=[pl.BlockSpec((bm, d), row), pl.BlockSpec((bm, d), row)],
        out_shape=[jax.ShapeDtypeStruct((m, d), F32), jax.ShapeDtypeStruct((m, d), BF16)],
        compiler_params=pltpu.CompilerParams(
            dimension_semantics=("arbitrary", "arbitrary"), vmem_limit_bytes=VMEM_LIMIT),
        name="down_proj",
    )(a, w, x, mod, mod, gain, next_mod, next_mod)


def _even_finish_kernel(of_ref, ob_ref, r_ref, sx_ref, sb_ref, sg_ref, sxp_ref, sxn_ref, sgp_ref, sgn_ref,
                        gain_ref, cw_ref, o_ref, s_ref, *, n_ctx, n_seq, tiles_per_batch):
    bm = of_ref.shape[0]
    nv = of_ref.shape[1]
    r = r_ref[...]
    o_ref[:, :nv] = (_head_rmsnorm(of_ref[...] + ob_ref[...], GLA_DV) * gain_ref[...]
                     * (r * _sigmoid(r))).astype(BF16)

    _, pos = _chunk_rows(0, bm, tiles_per_batch, bm)
    pos0 = (pl.program_id(0) % tiles_per_batch) * bm
    hr = sxp_ref.shape[0]
    before = (sxp_ref[...] * sgp_ref[...])[hr - 1:hr, :]
    before = jnp.where((pos0 == 0) | (pos0 == n_ctx), 0.0, before)
    after = (sxn_ref[...] * sgn_ref[...])[0:1, :]
    after = jnp.where((pos0 + bm == n_ctx) | (pos0 + bm == n_seq), 0.0, after)
    split = pos if _splits_rows(0, bm, bm, tiles_per_batch, n_ctx) else None
    conv = _dwconv3(s_ref, sg_ref[...] * sx_ref[...], before, after, cw_ref, split, n_ctx)
    o_ref[:, nv:] = (sb_ref[...] * conv).astype(BF16)


def _even_finish(o_f, o_b, pa, pb, gain, conv_w, *, n_ctx, n_seq, bm=ROW_CHUNK):
    m, nv = o_f.shape
    sc = conv_w.shape[1]
    tpb = n_seq // bm
    col = lambda j: (lambda i: (i, j))
    const = lambda i: (0, 0)
    return pl.pallas_call(
        functools.partial(_even_finish_kernel, n_ctx=n_ctx, n_seq=n_seq, tiles_per_batch=tpb),
        grid=(m // bm,),
        in_specs=[pl.BlockSpec((bm, nv), col(0)), pl.BlockSpec((bm, nv), col(0)),
                  pl.BlockSpec((bm, nv), col(2)),
                  pl.BlockSpec((bm, sc), col(0)), pl.BlockSpec((bm, sc), col(1)), pl.BlockSpec((bm, sc), col(2))]
        + _halo_specs(bm, sc, lambda: 0, m, SUBLANES) + _halo_specs(bm, sc, lambda: 2, m, SUBLANES)
        + [pl.BlockSpec((1, nv), const), pl.BlockSpec((3, sc), const)],
        out_specs=pl.BlockSpec((bm, nv + sc), col(0)),
        out_shape=jax.ShapeDtypeStruct((m, nv + sc), BF16),
        scratch_shapes=[pltpu.VMEM((bm + 2 * SUBLANES, sc), F32)],
        compiler_params=pltpu.CompilerParams(vmem_limit_bytes=VMEM_LIMIT),
        name="even_finish",
    )(o_f, o_b, pa, pb, pb, pb, pb, pb, pb, pb, gain, conv_w)


def _mlstm_prep_kernel(x_ref, xp_ref, xn_ref, cw_ref, scale_ref, o_ref, s_ref, *, n_ctx, n_seq,
                       tiles_per_batch):
    bm = x_ref.shape[0]
    _, pos = _chunk_rows(0, bm, tiles_per_batch, bm)
    before, after = _edge_rows(0, bm, bm, tiles_per_batch, n_ctx, n_seq, x_ref, xp_ref, xn_ref)
    split = pos if _splits_rows(0, bm, bm, tiles_per_batch, n_ctx) else None
    z = _dwconv3(s_ref, x_ref[...], before, after, cw_ref, split, n_ctx)
    o_ref[...] = (z * _sigmoid(z) * scale_ref[...]).astype(BF16)


def _mlstm_prep(p, conv_w, scale, *, col, n_ctx, n_seq, bm=ROW_CHUNK):
    m = p.shape[0]
    width = conv_w.shape[1]
    tpb = n_seq // bm
    const = lambda i: (0, 0)
    return pl.pallas_call(
        functools.partial(_mlstm_prep_kernel, n_ctx=n_ctx, n_seq=n_seq, tiles_per_batch=tpb),
        grid=(m // bm,),
        in_specs=[pl.BlockSpec((bm, width), lambda i: (i, col))]
        + _halo_specs(bm, width, lambda: col, m, SUBLANES)
        + [pl.BlockSpec((3, width), const), pl.BlockSpec((1, width), const)],
        out_specs=pl.BlockSpec((bm, width), lambda i: (i, 0)),
        out_shape=jax.ShapeDtypeStruct((m, width), BF16),
        scratch_shapes=[pltpu.VMEM((bm + 2 * SUBLANES, width), F32)],
        compiler_params=pltpu.CompilerParams(vmem_limit_bytes=VMEM_LIMIT),
        name="mlstm_prep",
    )(p, p, p, conv_w, scale)


def _mlstm_finish_kernel(hf_ref, hb_ref, mo_ref, gain_ref, o_ref):
    o_ref[...] = (_head_rmsnorm(hf_ref[...] + hb_ref[...], ML_DV) * gain_ref[...]
                  * _sigmoid(mo_ref[...])).astype(BF16)


def _mlstm_finish(h_f, h_b, p, gain, *, col, bm=ROW_TILE // 2):
    m, nv = h_f.shape
    return pl.pallas_call(
        _mlstm_finish_kernel,
        grid=(m // bm,),
        in_specs=[pl.BlockSpec((bm, nv), lambda i: (i, 0)), pl.BlockSpec((bm, nv), lambda i: (i, 0)),
                  pl.BlockSpec((bm, nv), lambda i: (i, col)), pl.BlockSpec((1, nv), lambda i: (0, 0))],
        out_specs=pl.BlockSpec((bm, nv), lambda i: (i, 0)),
        out_shape=jax.ShapeDtypeStruct((m, nv), BF16),
        compiler_params=pltpu.CompilerParams(vmem_limit_bytes=VMEM_LIMIT),
        name="mlstm_finish",
    )(h_f, h_b, p, gain)


def _qk_prep(x, cos, sin, gain):
    lane = lax.broadcasted_iota(jnp.int32, x.shape, 1)
    lo = lane < DA_DQK
    x2 = x * x
    ss_lo = jnp.sum(jnp.where(lo, x2, 0.0), axis=-1, keepdims=True)
    ss_hi = jnp.sum(jnp.where(lo, 0.0, x2), axis=-1, keepdims=True)
    y = x * lax.rsqrt(jnp.where(lo, ss_lo, ss_hi) * (1.0 / DA_DQK) + EPS) * gain
    swapped = jnp.where((lane & ROPE_PAIRS) == 0, pltpu.roll(y, LANES - ROPE_PAIRS, 1),
                        pltpu.roll(y, ROPE_PAIRS, 1))
    return y * cos + swapped * sin


def _attn_kernel(lam_ref, q_ref, k_ref, v_ref, cos_ref, sin_ref, qg_ref, kg_ref, og_ref, o_ref,
                 kb_ref, vb_ref, *, n_ctx, chunk, prep_rows, out_scale):
    qi = pl.program_id(2)
    lam = lam_ref[0]
    n = k_ref.shape[1]
    bq = q_ref.shape[1]

    @pl.when(qi == 0)
    def _():
        for r0 in range(0, n, prep_rows):
            rs = slice(r0, r0 + prep_rows)
            kb_ref[rs, :] = _qk_prep(k_ref[0, rs, :], cos_ref[rs, :], sin_ref[rs, :], kg_ref[...]).astype(BF16)
            vb_ref[rs, :LANES] = v_ref[0, rs, :].astype(BF16)
            vb_ref[rs, LANES:] = jnp.ones((prep_rows, LANES), BF16)

    def attend(nk):
        rows = pl.ds(pl.multiple_of(qi * bq, bq), bq)
        q = _qk_prep(q_ref[0], cos_ref[rows, :], sin_ref[rows, :], qg_ref[...]) * (DA_DQK ** -0.5 * LOG2E)
        q = q.astype(BF16)
        lane = lax.broadcasted_iota(jnp.int32, q.shape, 1)
        zero = jnp.zeros_like(q)
        q2 = jnp.concatenate([jnp.where(lane < DA_DQK, q, zero),
                              jnp.where(lane >= DA_DQK, q, zero)], axis=0)
        m = jnp.full((2 * bq, 1), -jnp.inf, F32)
        acc = jnp.zeros((2 * bq, vb_ref.shape[1]), F32)
        for c0 in range(0, nk, chunk):
            c1 = min(c0 + chunk, nk)
            s = lax.dot_general(q2, kb_ref[c0:c1, :], NT, preferred_element_type=F32)
            m_new = jnp.maximum(m, jnp.max(s, axis=-1, keepdims=True))
            p = jnp.exp2(s - m_new)
            acc = jnp.exp2(m - m_new) * acc + jnp.dot(
                p.astype(BF16), vb_ref[c0:c1, :], preferred_element_type=F32)
            m = m_new
        o = acc[:, :LANES] / acc[:, LANES:]
        o = o[:bq] - lam * o[bq:]
        o = o * lax.rsqrt(jnp.mean(o * o, axis=-1, keepdims=True) + EPS) * og_ref[...] * out_scale
        o_ref[0] = o.astype(BF16)

    n_ctx_blocks = n_ctx // bq

    @pl.when(qi < n_ctx_blocks)
    def _():
        attend(n_ctx)

    @pl.when(qi >= n_ctx_blocks)
    def _():
        attend(n)


def _diff_attention(p, lam, cos, sin, q_gain, k_gain, out_gain, *, n_ctx, out_scale, bq=ATTN_BQ,
                    chunk=ATTN_CHUNK, prep_rows=ATTN_PREP):
    b, n, _ = p.shape
    h = DA_HEADS
    hd = h * LANES
    const = lambda bi, hi, qi: (0, 0)
    return pl.pallas_call(
        functools.partial(_attn_kernel, n_ctx=n_ctx, chunk=chunk, prep_rows=prep_rows, out_scale=out_scale),
        grid=(b, h, n // bq),
        in_specs=[pl.BlockSpec(memory_space=pltpu.SMEM),
                  pl.BlockSpec((1, bq, LANES), lambda bi, hi, qi: (bi, qi, hi)),
                  pl.BlockSpec((1, n, LANES), lambda bi, hi, qi: (bi, 0, h + hi)),
                  pl.BlockSpec((1, n, LANES), lambda bi, hi, qi: (bi, 0, 2 * h + hi)),
                  pl.BlockSpec((n, LANES), const), pl.BlockSpec((n, LANES), const),
                  pl.BlockSpec((1, LANES), const), pl.BlockSpec((1, LANES), const),
                  pl.BlockSpec((1, LANES), const)],
        out_specs=pl.BlockSpec((1, bq, LANES), lambda bi, hi, qi: (bi, qi, hi)),
        out_shape=jax.ShapeDtypeStruct((b, n, hd), BF16),
        scratch_shapes=[pltpu.VMEM((n, LANES), BF16), pltpu.VMEM((n, 2 * LANES), BF16)],
        compiler_params=pltpu.CompilerParams(
            dimension_semantics=("arbitrary", "arbitrary", "arbitrary"), vmem_limit_bytes=VMEM_LIMIT),
        name="diff_attention",
    )(lam, p, p, p, cos, sin, q_gain, k_gain, out_gain)


def _scan_chunk(t, n_ctx_chunks, n_chunks, reverse):
    if not reverse:
        return t
    return jnp.where(t < n_ctx_chunks, n_ctx_chunks - 1 - t, n_chunks - 1 - (t - n_ctx_chunks))


def _gla_constants(L, reverse):
    nlev = int(math.log2(L))
    idx = np.arange(L)
    i, t = idx[:, None], idx[None, :]
    if reverse:
        i, t = L - 1 - i, L - 1 - t
    rs = [(t <= i), (t > i)]
    am = [(i == t)]
    for lev in range(nlev):
        m = L >> (lev + 1)
        blk_i, blk_t = i // (2 * m), t // (2 * m)
        mid = blk_i * 2 * m + m
        q_role = i >= mid
        rs.append(np.where(q_role, (t >= mid) & (t <= i), (t > i) & (t < mid)) & (blk_i == blk_t))
        am.append((blk_i == blk_t) & q_role & (t < mid))
    return (np.stack(rs).astype(np.float32).reshape((nlev + 2) * L, L),
            np.stack(am).astype(np.float32))


def _gla_kernel(q_ref, k_ref, v_ref, glr_ref, w2_ref, gb_ref, rsum_ref, amask_ref, o_ref, st_ref):
    L = q_ref.shape[1]
    nlev = amask_ref.shape[0] - 1

    @pl.when(pl.program_id(1) == 0)
    def _():
        st_ref[...] = jnp.zeros_like(st_ref)

    z = jnp.dot(glr_ref[0].astype(BF16), w2_ref[...], preferred_element_type=F32) + gb_ref[...]
    g = (jnp.minimum(z, 0.0) - jnp.log(1.0 + jnp.exp(-jnp.abs(z)))) * (1.0 / GLA_TAU)
    g1 = g.astype(BF16)
    g2 = (g - g1.astype(F32)).astype(BF16)
    g3 = (g - g1.astype(F32) - g2.astype(F32)).astype(BF16)
    rsum = rsum_ref[...]
    e_all = (jnp.dot(rsum, g1, preferred_element_type=F32) + jnp.dot(rsum, g2, preferred_element_type=F32)
             + jnp.dot(rsum, g3, preferred_element_type=F32))
    b_tot = jnp.sum(g, axis=0, keepdims=True)

    for h in range(GLA_HEADS):
        ck = slice(h * GLA_DK, (h + 1) * GLA_DK)
        cv = slice(h * GLA_DV, (h + 1) * GLA_DV)
        q = q_ref[0, :, ck] * (GLA_DK ** -0.5)
        k = k_ref[0, :, ck]
        v = v_ref[0, :, cv].astype(BF16)
        st = st_ref[h]
        a = amask_ref[0] * lax.dot_general(q.astype(BF16), k.astype(BF16), NT, preferred_element_type=F32)
        for lev in range(nlev):
            e = jnp.exp(e_all[(2 + lev) * L:(3 + lev) * L, ck])
            a = a + amask_ref[1 + lev] * lax.dot_general(
                (q * e).astype(BF16), (k * e).astype(BF16), NT, preferred_element_type=F32)
        qe = (q * jnp.exp(e_all[0:L, ck])).astype(BF16)
        o = lax.dot_general(qe, st.astype(BF16), NT, preferred_element_type=F32)
        o = o + jnp.dot(a.astype(BF16), v, preferred_element_type=F32)
        o_ref[0, :, cv] = o
        kd = (k * jnp.exp(e_all[L:2 * L, ck])).astype(BF16)
        st_ref[h] = jnp.exp(b_tot[:, ck]) * st + jnp.dot(
            v.T, kd, preferred_element_type=F32)


def _gla_scan(p, w2, gb, *, n_ctx, reverse, L=GLA_CHUNK):
    b, n, _ = p.shape
    nc, ncc = n // L, n_ctx // L
    rsum, amask = _gla_constants(L, reverse)
    chunk = functools.partial(_scan_chunk, n_ctx_chunks=ncc, n_chunks=nc, reverse=reverse)
    hk, hv = GLA_HEADS * GLA_DK, GLA_HEADS * GLA_DV
    cols = lambda width, start: pl.BlockSpec((1, L, width), lambda bi, t: (bi, chunk(t), start // width))
    row = lambda bi, t: (bi, chunk(t), 0)
    const2 = lambda bi, t: (0, 0)
    const3 = lambda bi, t: (0, 0, 0)
    return pl.pallas_call(
        _gla_kernel,
        grid=(b, nc),
        in_specs=[cols(hk, 0), cols(hk, hk), cols(hv, 2 * hk), cols(LANES, 2 * hk + 2 * hv),
                  pl.BlockSpec(w2.shape, const2), pl.BlockSpec(gb.shape, const2),
                  pl.BlockSpec(rsum.shape, const2), pl.BlockSpec(amask.shape, const3)],
        out_specs=pl.BlockSpec((1, L, hv), row),
        out_shape=jax.ShapeDtypeStruct((b, n, hv), F32),
        scratch_shapes=[pltpu.VMEM((GLA_HEADS, GLA_DV, GLA_DK), F32)],
        compiler_params=pltpu.CompilerParams(
            dimension_semantics=("arbitrary", "arbitrary"), vmem_limit_bytes=VMEM_LIMIT),
        name="gla_scan_bwd" if reverse else "gla_scan_fwd",
    )(p, p, p, p, w2, gb, jnp.asarray(rsum, BF16), jnp.asarray(amask))


def _log_sigmoid(x):
    return jnp.minimum(x, 0.0) - jnp.log(1.0 + jnp.exp(-jnp.abs(x)))


def _mlstm_kernel(q_ref, k_ref, v_ref, gc_ref, gr_ref, tri_ref, o_ref, c_ref, n_ref, m_ref, *, reverse):
    L = q_ref.shape[1]
    H = ML_HEADS

    @pl.when(pl.program_id(1) == 0)
    def _():
        c_ref[...] = jnp.zeros_like(c_ref)
        n_ref[...] = jnp.zeros_like(n_ref)
        m_ref[...] = jnp.zeros_like(m_ref)

    tri = tri_ref[...]
    gc = gc_ref[0]
    gr = gr_ref[0]
    ic_col, ic_row = gc[:, :H], gr[:H, :]
    b_col = jnp.dot(tri, _log_sigmoid(gc[:, H:]), precision=HI, preferred_element_type=F32)
    b_row = lax.dot_general(_log_sigmoid(gr[H:, :]), tri, NT, precision=HI, preferred_element_type=F32)
    last = 0 if reverse else L - 1
    causal = tri > 0.5

    for h in range(H):
        ck = slice(h * ML_DK, (h + 1) * ML_DK)
        cv = slice(h * ML_DV, (h + 1) * ML_DV)
        q = q_ref[0, :, ck]
        k = k_ref[0, :, ck]
        v = v_ref[0, :, cv].astype(BF16)
        c, n, m = c_ref[h], n_ref[h], m_ref[h]
        bc, br = b_col[:, h:h + 1], b_row[h:h + 1, :]
        icc, icr = ic_col[:, h:h + 1], ic_row[h:h + 1, :]
        b_last = bc[last:last + 1, :]

        a = bc + m
        dmat = jnp.where(causal, bc - br + icr, -jnp.inf)
        m_t = jnp.maximum(a, jnp.max(dmat, axis=-1, keepdims=True))
        w_inter = jnp.exp(a - m_t)
        s = lax.dot_general(q, k, NT, preferred_element_type=F32) * jnp.exp(dmat - m_t)
        num = w_inter * jnp.dot(q, c.astype(BF16), preferred_element_type=F32) + jnp.dot(
            s.astype(BF16), v, preferred_element_type=F32)
        den = w_inter * jnp.sum(q.astype(F32) * n, axis=-1, keepdims=True) + jnp.sum(
            s, axis=-1, keepdims=True)
        o_ref[0, :, cv] = num / jnp.maximum(jnp.abs(den), jnp.exp(-m_t))

        gs_col = b_last - bc + icc
        gs_row = b_last - br + icr
        m_new = jnp.maximum(b_last + m, jnp.max(gs_row, axis=-1, keepdims=True))
        decay = jnp.exp(b_last + m - m_new)
        wk = jnp.exp(gs_col - m_new) * k.astype(F32)
        c_ref[h] = decay * c + jnp.dot(wk.astype(BF16).T, v, preferred_element_type=F32)
        n_ref[h] = decay * n + jnp.sum(wk, axis=0, keepdims=True)
        m_ref[h] = m_new


def _mlstm_scan(qk, p, gates, *, v_col, n_ctx, reverse, L=ML_CHUNK):
    b, n, _ = qk.shape
    hk, hv = ML_HEADS * ML_DK, ML_HEADS * ML_DV
    nc, ncc = n // L, n_ctx // L
    idx = np.arange(L)
    tri = (idx[None, :] >= idx[:, None]) if reverse else (idx[None, :] <= idx[:, None])
    tri = jnp.asarray(tri.astype(np.float32))
    gates_t = jnp.swapaxes(gates, 1, 2)
    chunk = functools.partial(_scan_chunk, n_ctx_chunks=ncc, n_chunks=nc, reverse=reverse)
    col = lambda j: (lambda bi, t: (bi, chunk(t), j))
    return pl.pallas_call(
        functools.partial(_mlstm_kernel, reverse=reverse),
        grid=(b, nc),
        in_specs=[pl.BlockSpec((1, L, hk), col(0)), pl.BlockSpec((1, L, hk), col(1)),
                  pl.BlockSpec((1, L, hv), col(v_col)), pl.BlockSpec((1, L, 2 * ML_HEADS), col(0)),
                  pl.BlockSpec((1, 2 * ML_HEADS, L), lambda bi, t: (bi, 0, chunk(t))),
                  pl.BlockSpec((L, L), lambda bi, t: (0, 0))],
        out_specs=pl.BlockSpec((1, L, hv), col(0)),
        out_shape=jax.ShapeDtypeStruct((b, n, hv), F32),
        scratch_shapes=[pltpu.VMEM((ML_HEADS, ML_DK, ML_DV), F32),
                        pltpu.VMEM((ML_HEADS, 1, ML_DK), F32),
                        pltpu.VMEM((ML_HEADS, 1, 1), F32)],
        compiler_params=pltpu.CompilerParams(
            dimension_semantics=("arbitrary", "arbitrary"), vmem_limit_bytes=VMEM_LIMIT),
        name="mlstm_scan_bwd" if reverse else "mlstm_scan_fwd",
    )(qk, qk, p, gates, gates_t, tri)


def _silu(x):
    return x * jax.nn.sigmoid(x)


def _rope_tables(n_lat, n_ctx):
    rows = n_lat // GRID_W
    row = jnp.repeat(jnp.arange(rows, dtype=F32), GRID_W)
    col = jnp.tile(jnp.arange(GRID_W, dtype=F32), rows)
    inv = jnp.power(ROPE_BASE, -jnp.arange(ROPE_PAIRS, dtype=F32) / ROPE_PAIRS)
    ang_r, ang_c = row[:, None] * inv, col[:, None] * inv
    cos = jnp.concatenate([jnp.cos(ang_r)] * 2 + [jnp.cos(ang_c)] * 2, axis=-1)
    sin = jnp.concatenate([-jnp.sin(ang_r), jnp.sin(ang_r), -jnp.sin(ang_c), jnp.sin(ang_c)], axis=-1)
    pad = ((n_ctx, 0), (0, 0))
    cos, sin = jnp.pad(cos, pad, constant_values=1.0), jnp.pad(sin, pad)
    return jnp.tile(cos, (1, 2)), jnp.tile(sin, (1, 2))


def _even_mixer(hn, bsz, w_in, i, gate_w2, gate_b, gla_norm_g, sc_conv_w, seq):
    nq = GLA_HEADS * GLA_DK
    nv = GLA_HEADS * GLA_DV
    gate_col = 2 * nq + 2 * nv
    pa = _matmul(hn, w_in, i, bn=640, ncols=gate_col + LANES)
    pb = _matmul(hn, w_in, i, bn=SC_WIDTH, col0=gate_col, ncols=3 * SC_WIDTH,
                 lane_shift=2 * GLA_GATE_RANK)
    pa3 = pa.reshape(bsz, -1, pa.shape[1])
    outs = []
    for direction in range(2):
        w2 = jnp.zeros((LANES, nq), F32).at[
            direction * GLA_GATE_RANK:(direction + 1) * GLA_GATE_RANK].set(gate_w2[direction])
        outs.append(_gla_scan(pa3, w2.astype(BF16), gate_b[direction][None, :],
                              n_ctx=seq["n_ctx"], reverse=bool(direction)).reshape(-1, nv))
    gain = jnp.tile(gla_norm_g, GLA_HEADS)[None, :]
    return [_even_finish(outs[0], outs[1], pa, pb, gain, sc_conv_w, **seq)]


def _odd_mixer(hn, bsz, rope, layer, w_in, i, qn_g, kn_g, lam_p, subln_g, ml_conv_w, ml_gate_b,
               ml_norm_g, seq):
    lam_init = 0.8 - 0.6 * math.exp(-0.3 * layer)
    lam = (jnp.exp(jnp.sum(lam_p[0] * lam_p[1])) - jnp.exp(jnp.sum(lam_p[2] * lam_p[3])) + lam_init)
    na = DA_HEADS * 2 * DA_DQK
    nk = ML_HEADS * ML_DK
    nv = ML_HEADS * ML_DV
    n_main = 3 * na + 2 * nk + 2 * nv
    p = _matmul(hn, w_in, i, bn=1024, ncols=n_main)
    mg = _matmul(hn, w_in, i, bn=LANES, col0=n_main, ncols=LANES)[:, :4 * ML_HEADS]
    p3 = p.reshape(bsz, -1, n_main)
    n = p3.shape[1]

    cos, sin = rope
    tile2 = lambda g: jnp.tile(g, 2)[None, :]
    da = _diff_attention(p3, lam.reshape(1).astype(F32), cos, sin, tile2(qn_g), tile2(kn_g),
                         subln_g[None, :], n_ctx=seq["n_ctx"], out_scale=1.0 - lam_init)

    scale = jnp.concatenate([jnp.ones((nk,), F32), jnp.full((nk,), ML_DK ** -0.5, F32)])[None, :]
    hqk = _mlstm_prep(p, ml_conv_w, scale, col=3 * na // (2 * nk), **seq).reshape(bsz, n, 2 * nk)
    gates = (mg + ml_gate_b).reshape(bsz, n, 4 * ML_HEADS)
    outs = []
    for direction in range(2):
        g_dir = gates[..., direction * 2 * ML_HEADS:(direction + 1) * 2 * ML_HEADS]
        outs.append(_mlstm_scan(hqk, p3, g_dir, v_col=(3 * na + 2 * nk) // nv, n_ctx=seq["n_ctx"],
                                reverse=bool(direction)).reshape(-1, nv))
    m = _mlstm_finish(outs[0], outs[1], p, jnp.tile(ml_norm_g, ML_HEADS)[None, :],
                      col=(3 * na + 2 * nk + nv) // nv)
    return [da.reshape(-1, na), m]


def _modulation(c, c_ctx, ada_w, ada_b, layer):
    cc = jnp.concatenate([c, c_ctx[None, :]], axis=0)
    rows = cc.shape[0]
    act = jnp.pad(_silu(cc), ((0, BF16_ROWS - rows), (0, 0))).astype(BF16)
    mod = _matmul(act, ada_w, layer, bn=1024, bm=BF16_ROWS)[:rows] + ada_b[layer]
    return mod.reshape(rows, 6, -1)


def kernel(x, c, ctx, c_ctx, ada_w, ada_b, norm1_g, norm2_g, ev_w_in, ev_w_out, gla_gate_w2, gla_gate_b, gla_norm_g, sc_conv_w, od_w_in, od_w_out, da_qnorm_g, da_knorm_g, da_lambda, da_subln_g, ml_conv_w, ml_gate_b, ml_norm_g, ffn_w_up, ffn_conv_w, ffn_conv_b, ffn_w_down):
    bsz, n_lat, d = x.shape
    n_ctx = ctx.shape[1]
    n = n_ctx + n_lat
    depth = ada_w.shape[0]
    d_ff = ffn_w_down.shape[1]
    assert n % ROW_TILE == 0 and n_ctx <= ROW_CHUNK and d_ff % (4 * LANES) == 0
    rope = _rope_tables(n_lat, n_ctx)
    seq = dict(n_ctx=n_ctx, n_seq=n)

    mods = [_modulation(c, c_ctx, ada_w, ada_b, layer) for layer in range(depth)]
    xs = jnp.concatenate([ctx, x], axis=1).reshape(bsz * n, d)
    hn = _first_norm(xs, norm1_g[0][None, :], mods[0], **seq)
    for layer in range(depth):
        i = layer // 2
        if layer % 2 == 0:
            mix = _even_mixer(hn, bsz, ev_w_in, i, gla_gate_w2[i], gla_gate_b[i], gla_norm_g[i],
                              sc_conv_w[i], seq)
            w_out = ev_w_out
        else:
            mix = _odd_mixer(hn, bsz, rope, layer, od_w_in, i, da_qnorm_g[i], da_knorm_g[i],
                             da_lambda[i], da_subln_g[i], ml_conv_w[i], ml_gate_b[i], ml_norm_g[i], seq)
            w_out = od_w_out
        xs, hn = _out_proj(mix, _to_bf16(w_out, i, rows=512), xs, mods[layer], norm2_g[layer][None, :],
                           **seq)
        act = _ffn_up(hn, ffn_w_up, layer, ffn_conv_w[layer], ffn_conv_b[layer][None, :], **seq)
        nxt = min(layer + 1, depth - 1)
        xs, hn = _down_proj(act, _to_bf16(ffn_w_down, layer, rows=d_ff // 8), xs,
                            mods[layer], norm1_g[nxt][None, :], mods[nxt], bk=d_ff // 4, **seq)
    return xs.reshape(bsz, n, d)[:, n_ctx:, :]
```

```python
import functools
import math

import jax
import jax.numpy as jnp
import numpy as np
from jax import lax
from jax.experimental import pallas as pl
from jax.experimental.pallas import tpu as pltpu

GRID_W = 64
EPS = 1e-6
GLA_HEADS, GLA_DK, GLA_DV, GLA_GATE_RANK, GLA_TAU = 4, 128, 256, 16, 16.0
SC_WIDTH = 1024
DA_HEADS, DA_DQK, DA_DV = 8, 64, 128
ROPE_BASE = 10000.0
ROPE_PAIRS = DA_DQK // 4
ML_HEADS, ML_DK, ML_DV = 4, 128, 256

GLA_CHUNK = 128
ML_CHUNK = 256
ATTN_BQ = 256
ATTN_CHUNK = 1024
ATTN_PREP = 544
ATTN_HEADS = 2
ROW_TILE = 1088
ROW_CHUNK = 272
LANES = 128
SUBLANES = 8
BF16_ROWS = 16
LOG2E = 1.4426950408889634
VMEM_LIMIT = 56 * 1024 * 1024

F32 = jnp.float32
BF16 = jnp.bfloat16
HI = lax.Precision.HIGHEST
NT = (((1,), (1,)), ((), ()))


def _mm_kernel(x_ref, w_ref, *rest, lane_shift):
    o_ref, wb_ref = rest[-2:]

    @pl.when(pl.program_id(1) == 0)
    def _():
        if lane_shift:
            wide = jnp.concatenate([w_ref[...], rest[0][...]], axis=1)
            wide = pltpu.roll(wide, wide.shape[1] - lane_shift, 1)
            wb_ref[...] = wide[:, :wb_ref.shape[1]].astype(BF16)
        else:
            wb_ref[...] = w_ref[...].astype(BF16)

    o_ref[...] = jnp.dot(x_ref[...], wb_ref[...], preferred_element_type=F32).astype(o_ref.dtype)


def _matmul(x, w, layer, *, bn, col0=0, ncols=None, lane_shift=0, bm=ROW_TILE, out_dtype=F32):
    m, k = x.shape
    ncols = w.shape[2] - col0 if ncols is None else ncols
    j0 = col0 // bn
    in_specs = [pl.BlockSpec((bm, k), lambda j, i: (i, 0)),
                pl.BlockSpec((None, k, bn), lambda j, i: (layer, 0, j0 + j))]
    operands = [x, w]
    if lane_shift:
        per = bn // LANES
        in_specs.append(pl.BlockSpec((None, k, LANES), lambda j, i: (layer, 0, (j0 + j + 1) * per)))
        operands.append(w)
    return pl.pallas_call(
        functools.partial(_mm_kernel, lane_shift=lane_shift),
        grid=(pl.cdiv(ncols, bn), m // bm),
        in_specs=in_specs,
        out_specs=pl.BlockSpec((bm, bn), lambda j, i: (i, j)),
        out_shape=jax.ShapeDtypeStruct((m, ncols), out_dtype),
        scratch_shapes=[pltpu.VMEM((k, bn), BF16)],
        compiler_params=pltpu.CompilerParams(
            dimension_semantics=("arbitrary", "arbitrary"), vmem_limit_bytes=VMEM_LIMIT),
        name="matmul",
    )(*operands)


def _cast_kernel(x_ref, o_ref):
    o_ref[...] = x_ref[...].astype(o_ref.dtype)


def _to_bf16(w, layer, *, rows):
    _, r, c = w.shape
    return pl.pallas_call(
        _cast_kernel,
        grid=(r // rows,),
        in_specs=[pl.BlockSpec((None, rows, c), lambda i: (layer, i, 0))],
        out_specs=pl.BlockSpec((rows, c), lambda i: (i, 0)),
        out_shape=jax.ShapeDtypeStruct((r, c), BF16),
        compiler_params=pltpu.CompilerParams(vmem_limit_bytes=VMEM_LIMIT),
        name="to_bf16",
    )(w)


def _norm_mod(x, gain, scale, shift):
    return x * lax.rsqrt(jnp.mean(x * x, axis=-1, keepdims=True) + EPS) * gain * (1.0 + scale) + shift


def _head_rmsnorm(x, width):
    parts = []
    for c0 in range(0, x.shape[1], width):
        xh = x[:, c0:c0 + width]
        parts.append(xh * lax.rsqrt(jnp.mean(xh * xh, axis=-1, keepdims=True) + EPS))
    return parts[0] if len(parts) == 1 else jnp.concatenate(parts, axis=1)


def _sigmoid(x):
    return 1.0 / (1.0 + jnp.exp(-x))


def _mod_row(is_ctx, modb_ref, modc_ref, idx):
    if is_ctx is None:
        return modb_ref[0, idx:idx + 1, :]
    return jnp.where(is_ctx, modc_ref[0, idx:idx + 1, :], modb_ref[0, idx:idx + 1, :])


def _chunk_rows(r0, rc, tiles_per_batch, bm):
    rows = r0 + lax.broadcasted_iota(jnp.int32, (rc, 1), 0)
    return rows, (pl.program_id(0) % tiles_per_batch) * bm + rows


def _chunk_is_ctx(r0, pos, n_ctx):
    return pos < n_ctx if r0 < n_ctx else None


def _edge_rows(r0, rc, bm, tiles_per_batch, n_ctx, n_seq, gate_ref, hprev_ref, hnext_ref):
    hr = hprev_ref.shape[0]
    pos0 = (pl.program_id(0) % tiles_per_batch) * bm
    if r0 == 0:
        before = hprev_ref[...].astype(F32)[hr - 1:hr, :]
        before = jnp.where((pos0 == 0) | (pos0 == n_ctx), 0.0, before)
    else:
        before = gate_ref[r0 - hr:r0, :].astype(F32)[hr - 1:hr, :]
    if r0 + rc == bm:
        after = hnext_ref[...].astype(F32)[0:1, :]
        after = jnp.where((pos0 + bm == n_ctx) | (pos0 + bm == n_seq), 0.0, after)
    else:
        after = gate_ref[r0 + rc:r0 + rc + hr, :].astype(F32)[0:1, :]
    return before, after


def _splits_rows(r0, rc, bm, tiles_per_batch, n_ctx):
    return any(r0 <= n_ctx - k * bm <= r0 + rc for k in range(tiles_per_batch))


def _dwconv3(s_ref, g, before, after, cw_ref, pos, n_ctx):
    rc = g.shape[0]
    s_ref[SUBLANES:SUBLANES + rc, :] = g
    s_ref[SUBLANES - 1:SUBLANES, :] = before
    s_ref[SUBLANES + rc:SUBLANES + rc + 1, :] = after
    prev = s_ref[SUBLANES - 1:SUBLANES - 1 + rc, :]
    nxt = s_ref[SUBLANES + 1:SUBLANES + 1 + rc, :]
    if pos is not None:
        prev = jnp.where(pos == n_ctx, 0.0, prev)
        nxt = jnp.where(pos == n_ctx - 1, 0.0, nxt)
    return prev * cw_ref[0:1, :] + g * cw_ref[1:2, :] + nxt * cw_ref[2:3, :]


def _mod_specs(n_batch, tiles_per_batch, d):
    return [pl.BlockSpec((1, 6, d), lambda i, *_: (i // tiles_per_batch, 0, 0)),
            pl.BlockSpec((1, 6, d), lambda i, *_: (n_batch, 0, 0))]


def _halo_specs(bm, width, col, n_rows, hr):
    hb = bm // hr
    last = n_rows // hr - 1
    return [pl.BlockSpec((hr, width), lambda i, *k: (jnp.maximum(i * hb - 1, 0), col(*k))),
            pl.BlockSpec((hr, width), lambda i, *k: (jnp.minimum((i + 1) * hb, last), col(*k)))]


def _norm_kernel(x_ref, gain_ref, modb_ref, modc_ref, o_ref, *, n_ctx, tiles_per_batch):
    _, pos = _chunk_rows(0, x_ref.shape[0], tiles_per_batch, x_ref.shape[0])
    is_ctx = pos < n_ctx
    o_ref[...] = _norm_mod(x_ref[...], gain_ref[...], _mod_row(is_ctx, modb_ref, modc_ref, 1),
                           _mod_row(is_ctx, modb_ref, modc_ref, 0)).astype(BF16)


def _first_norm(x, gain, mod, *, n_ctx, n_seq, bm=ROW_TILE):
    m, d = x.shape
    tpb = n_seq // bm
    return pl.pallas_call(
        functools.partial(_norm_kernel, n_ctx=n_ctx, tiles_per_batch=tpb),
        grid=(m // bm,),
        in_specs=[pl.BlockSpec((bm, d), lambda i: (i, 0)), pl.BlockSpec((1, d), lambda i: (0, 0))]
        + _mod_specs(mod.shape[0] - 1, tpb, d),
        out_specs=pl.BlockSpec((bm, d), lambda i: (i, 0)),
        out_shape=jax.ShapeDtypeStruct((m, d), BF16),
        compiler_params=pltpu.CompilerParams(vmem_limit_bytes=VMEM_LIMIT),
        name="first_norm",
    )(x, gain, mod, mod)


def _residual_norm(rs, acc, xo_ref, hn_ref, x_ref, modb_ref, modc_ref, gain_ref, nmodb_ref, nmodc_ref,
                   is_ctx, gate_idx, shift_idx):
    x_new = x_ref[rs, :] + _mod_row(is_ctx, modb_ref, modc_ref, gate_idx) * acc
    xo_ref[rs, :] = x_new
    hn_ref[rs, :] = _norm_mod(
        x_new, gain_ref[...], _mod_row(is_ctx, nmodb_ref, nmodc_ref, shift_idx + 1),
        _mod_row(is_ctx, nmodb_ref, nmodc_ref, shift_idx)).astype(BF16)


def _out_kernel(*refs, n_in, n_ctx, tiles_per_batch):
    a_refs = refs[:n_in]
    w_ref, x_ref, modb_ref, modc_ref, gain_ref, xo_ref, hn_ref = refs[n_in:]
    bm = x_ref.shape[0]
    k0 = 0
    for n_done, a_ref in enumerate(a_refs):
        part = jnp.dot(a_ref[...], w_ref[k0:k0 + a_ref.shape[1], :], preferred_element_type=F32)
        xo_ref[...] = part if n_done == 0 else xo_ref[...] + part
        k0 += a_ref.shape[1]
    for r0 in range(0, bm, ROW_CHUNK):
        rs = slice(r0, r0 + ROW_CHUNK)
        _, pos = _chunk_rows(r0, ROW_CHUNK, tiles_per_batch, bm)
        _residual_norm(rs, xo_ref[rs, :], xo_ref, hn_ref, x_ref, modb_ref, modc_ref, gain_ref, modb_ref,
                       modc_ref, _chunk_is_ctx(r0, pos, n_ctx), 2, 3)


def _out_proj(acts, w, x, mod, gain, *, n_ctx, n_seq, bm=ROW_TILE // 2):
    m, d = x.shape
    tpb = n_seq // bm
    rows = lambda width: pl.BlockSpec((bm, width), lambda i: (i, 0))
    return pl.pallas_call(
        functools.partial(_out_kernel, n_in=len(acts), n_ctx=n_ctx, tiles_per_batch=tpb),
        grid=(m // bm,),
        in_specs=[rows(a.shape[1]) for a in acts]
        + [pl.BlockSpec(w.shape, lambda i: (0, 0), pipeline_mode=pl.Buffered(1)), rows(d)]
        + _mod_specs(mod.shape[0] - 1, tpb, d) + [pl.BlockSpec((1, d), lambda i: (0, 0))],
        out_specs=[rows(d), rows(d)],
        out_shape=[jax.ShapeDtypeStruct((m, d), F32), jax.ShapeDtypeStruct((m, d), BF16)],
        compiler_params=pltpu.CompilerParams(vmem_limit_bytes=VMEM_LIMIT),
        name="out_proj",
    )(*acts, w, x, mod, mod, gain)


def _up_kernel(x_ref, xp_ref, xn_ref, wg_ref, wv_ref, cw_ref, cb_ref, o_ref, wgb_ref, wvb_ref, xe_ref,
               s_ref, v_ref, *, n_ctx, n_seq, tiles_per_batch):
    i = pl.program_id(1)
    bm, rc, hr = x_ref.shape[0], ROW_CHUNK, xp_ref.shape[0]

    @pl.when(i == 0)
    def _():
        wgb_ref[...] = wg_ref[...].astype(BF16)
        wvb_ref[...] = wv_ref[...].astype(BF16)

    xe_ref[0:hr, :] = xp_ref[...]
    xe_ref[hr:hr + bm, :] = x_ref[...]
    xe_ref[hr + bm:, :] = xn_ref[...]
    s_ref[...] = jnp.dot(xe_ref[...], wgb_ref[...], preferred_element_type=F32)
    v_ref[...] = jnp.dot(x_ref[...], wvb_ref[...], preferred_element_type=F32)
    pos0 = (i % tiles_per_batch) * bm
    s_ref[hr - 1:hr, :] = jnp.where((pos0 == 0) | (pos0 == n_ctx), 0.0, s_ref[hr - 1:hr, :])
    s_ref[hr + bm:hr + bm + 1, :] = jnp.where(
        (pos0 + bm == n_ctx) | (pos0 + bm == n_seq), 0.0, s_ref[hr + bm:hr + bm + 1, :])
    for r0 in range(0, bm, rc):
        rs = slice(r0, r0 + rc)
        prev = s_ref[hr - 1 + r0:hr - 1 + r0 + rc, :]
        nxt = s_ref[hr + 1 + r0:hr + 1 + r0 + rc, :]
        if _splits_rows(r0, rc, bm, tiles_per_batch, n_ctx):
            pos = pos0 + r0 + lax.broadcasted_iota(jnp.int32, (rc, 1), 0)
            prev = jnp.where(pos == n_ctx, 0.0, prev)
            nxt = jnp.where(pos == n_ctx - 1, 0.0, nxt)
        z = (prev * cw_ref[0:1, :] + s_ref[hr + r0:hr + r0 + rc, :] * cw_ref[1:2, :]
             + nxt * cw_ref[2:3, :] + cb_ref[...])
        o_ref[rs, :] = (z * _sigmoid(z) * v_ref[rs, :]).astype(BF16)


def _ffn_up(x, w, layer, conv_w, conv_b, *, n_ctx, n_seq, bn=512, bm=ROW_TILE):
    m, k = x.shape
    f = w.shape[2] // 2
    nj = f // bn
    tpb = n_seq // bm
    hb = bm // BF16_ROWS
    last = m // BF16_ROWS - 1
    return pl.pallas_call(
        functools.partial(_up_kernel, n_ctx=n_ctx, n_seq=n_seq, tiles_per_batch=tpb),
        grid=(nj, m // bm),
        in_specs=[pl.BlockSpec((bm, k), lambda j, i: (i, 0)),
                  pl.BlockSpec((BF16_ROWS, k), lambda j, i: (jnp.maximum(i * hb - 1, 0), 0)),
                  pl.BlockSpec((BF16_ROWS, k), lambda j, i: (jnp.minimum((i + 1) * hb, last), 0)),
                  pl.BlockSpec((None, k, bn), lambda j, i: (layer, 0, j)),
                  pl.BlockSpec((None, k, bn), lambda j, i: (layer, 0, j + nj)),
                  pl.BlockSpec((3, bn), lambda j, i: (0, j)), pl.BlockSpec((1, bn), lambda j, i: (0, j))],
        out_specs=pl.BlockSpec((bm, bn), lambda j, i: (i, j)),
        out_shape=jax.ShapeDtypeStruct((m, f), BF16),
        scratch_shapes=[pltpu.VMEM((k, bn), BF16), pltpu.VMEM((k, bn), BF16),
                        pltpu.VMEM((bm + 2 * BF16_ROWS, k), BF16),
                        pltpu.VMEM((bm + 2 * BF16_ROWS, bn), F32), pltpu.VMEM((bm, bn), F32)],
        compiler_params=pltpu.CompilerParams(
            dimension_semantics=("arbitrary", "arbitrary"), vmem_limit_bytes=VMEM_LIMIT),
        name="ffn_up",
    )(x, x, x, w, w, conv_w, conv_b)


def _down_kernel(a_ref, w_ref, x_ref, modb_ref, modc_ref, gain_ref, nmodb_ref, nmodc_ref, xo_ref, hn_ref, *,
                 n_ctx, tiles_per_batch):
    bm = a_ref.shape[0]
    rc = ROW_CHUNK

    part = jnp.dot(a_ref[...], w_ref[...], preferred_element_type=F32)

    @pl.when(pl.program_id(1) == 0)
    def _():
        xo_ref[...] = part

    @pl.when(pl.program_id(1) > 0)
    def _():
        xo_ref[...] += part

    @pl.when(pl.program_id(1) == pl.num_programs(1) - 1)
    def _():
        for r0 in range(0, bm, rc):
            rs = slice(r0, r0 + rc)
            _, pos = _chunk_rows(r0, rc, tiles_per_batch, bm)
            _residual_norm(rs, xo_ref[rs, :], xo_ref, hn_ref, x_ref, modb_ref, modc_ref, gain_ref,
                           nmodb_ref, nmodc_ref, _chunk_is_ctx(r0, pos, n_ctx), 5, 0)


def _down_proj(a, w, x, mod, gain, next_mod, *, n_ctx, n_seq, bm=ROW_TILE // 2, bk=1408):
    m, f = a.shape
    d = w.shape[1]
    tpb = n_seq // bm
    row = lambda i, k: (i, 0)
    return pl.pallas_call(
        functools.partial(_down_kernel, n_ctx=n_ctx, tiles_per_batch=tpb),
        grid=(m // bm, f // bk),
        in_specs=[pl.BlockSpec((bm, bk), lambda i, k: (i, k)), pl.BlockSpec((bk, d), lambda i, k: (k, 0)),
                  pl.BlockSpec((bm, d), row)]
        + _mod_specs(mod.shape[0] - 1, tpb, d) + [pl.BlockSpec((1, d), lambda i, k: (0, 0))]
        + _mod_specs(mod.shape[0] - 1, tpb, d),
        out_specs=[pl.BlockSpec((bm, d), row), pl.BlockSpec((bm, d), row)],
        out_shape=[jax.ShapeDtypeStruct((m, d), F32), jax.ShapeDtypeStruct((m, d), BF16)],
        compiler_params=pltpu.CompilerParams(
            dimension_semantics=("arbitrary", "arbitrary"), vmem_limit_bytes=VMEM_LIMIT),
        name="down_proj",
    )(a, w, x, mod, mod, gain, next_mod, next_mod)


def _even_finish_kernel(of_ref, ob_ref, r_ref, sx_ref, sb_ref, sg_ref, sxp_ref, sxn_ref, sgp_ref, sgn_ref,
                        gain_ref, cw_ref, o_ref, s_ref, *, n_ctx, n_seq, tiles_per_batch):
    bm = of_ref.shape[0]
    nv = of_ref.shape[1]
    r = r_ref[...]
    o_ref[:, :nv] = (_head_rmsnorm(of_ref[...] + ob_ref[...], GLA_DV) * gain_ref[...]
                     * (r * _sigmoid(r))).astype(BF16)

    _, pos = _chunk_rows(0, bm, tiles_per_batch, bm)
    pos0 = (pl.program_id(0) % tiles_per_batch) * bm
    hr = sxp_ref.shape[0]
    before = (sxp_ref[...] * sgp_ref[...])[hr - 1:hr, :]
    before = jnp.where((pos0 == 0) | (pos0 == n_ctx), 0.0, before)
    after = (sxn_ref[...] * sgn_ref[...])[0:1, :]
    after = jnp.where((pos0 + bm == n_ctx) | (pos0 + bm == n_seq), 0.0, after)
    split = pos if _splits_rows(0, bm, bm, tiles_per_batch, n_ctx) else None
    conv = _dwconv3(s_ref, sg_ref[...] * sx_ref[...], before, after, cw_ref, split, n_ctx)
    o_ref[:, nv:] = (sb_ref[...] * conv).astype(BF16)


def _even_finish(o_f, o_b, pa, pb, gain, conv_w, *, n_ctx, n_seq, bm=ROW_CHUNK):
    m, nv = o_f.shape
    sc = conv_w.shape[1]
    tpb = n_seq // bm
    col = lambda j: (lambda i: (i, j))
    const = lambda i: (0, 0)
    return pl.pallas_call(
        functools.partial(_even_finish_kernel, n_ctx=n_ctx, n_seq=n_seq, tiles_per_batch=tpb),
        grid=(m // bm,),
        in_specs=[pl.BlockSpec((bm, nv), col(0)), pl.BlockSpec((bm, nv), col(0)),
                  pl.BlockSpec((bm, nv), col(2)),
                  pl.BlockSpec((bm, sc), col(0)), pl.BlockSpec((bm, sc), col(1)), pl.BlockSpec((bm, sc), col(2))]
        + _halo_specs(bm, sc, lambda: 0, m, SUBLANES) + _halo_specs(bm, sc, lambda: 2, m, SUBLANES)
        + [pl.BlockSpec((1, nv), const), pl.BlockSpec((3, sc), const)],
        out_specs=pl.BlockSpec((bm, nv + sc), col(0)),
        out_shape=jax.ShapeDtypeStruct((m, nv + sc), BF16),
        scratch_shapes=[pltpu.VMEM((bm + 2 * SUBLANES, sc), F32)],
        compiler_params=pltpu.CompilerParams(vmem_limit_bytes=VMEM_LIMIT),
        name="even_finish",
    )(o_f, o_b, pa, pb, pb, pb, pb, pb, pb, pb, gain, conv_w)


def _mlstm_prep_kernel(x_ref, xp_ref, xn_ref, cw_ref, scale_ref, o_ref, s_ref, *, n_ctx, n_seq,
                       tiles_per_batch):
    bm = x_ref.shape[0]
    _, pos = _chunk_rows(0, bm, tiles_per_batch, bm)
    before, after = _edge_rows(0, bm, bm, tiles_per_batch, n_ctx, n_seq, x_ref, xp_ref, xn_ref)
    split = pos if _splits_rows(0, bm, bm, tiles_per_batch, n_ctx) else None
    z = _dwconv3(s_ref, x_ref[...], before, after, cw_ref, split, n_ctx)
    o_ref[...] = (z * _sigmoid(z) * scale_ref[...]).astype(BF16)


def _mlstm_prep(p, conv_w, scale, *, col, n_ctx, n_seq, bm=ROW_CHUNK):
    m = p.shape[0]
    width = conv_w.shape[1]
    tpb = n_seq // bm
    const = lambda i: (0, 0)
    return pl.pallas_call(
        functools.partial(_mlstm_prep_kernel, n_ctx=n_ctx, n_seq=n_seq, tiles_per_batch=tpb),
        grid=(m // bm,),
        in_specs=[pl.BlockSpec((bm, width), lambda i: (i, col))]
        + _halo_specs(bm, width, lambda: col, m, SUBLANES)
        + [pl.BlockSpec((3, width), const), pl.BlockSpec((1, width), const)],
        out_specs=pl.BlockSpec((bm, width), lambda i: (i, 0)),
        out_shape=jax.ShapeDtypeStruct((m, width), BF16),
        scratch_shapes=[pltpu.VMEM((bm + 2 * SUBLANES, width), F32)],
        compiler_params=pltpu.CompilerParams(vmem_limit_bytes=VMEM_LIMIT),
        name="mlstm_prep",
    )(p, p, p, conv_w, scale)


def _mlstm_finish_kernel(hf_ref, hb_ref, mo_ref, gain_ref, o_ref):
    o_ref[...] = (_head_rmsnorm(hf_ref[...] + hb_ref[...], ML_DV) * gain_ref[...]
                  * _sigmoid(mo_ref[...])).astype(BF16)


def _mlstm_finish(h_f, h_b, p, gain, *, col, bm=ROW_TILE // 2):
    m, nv = h_f.shape
    return pl.pallas_call(
        _mlstm_finish_kernel,
        grid=(m // bm,),
        in_specs=[pl.BlockSpec((bm, nv), lambda i: (i, 0)), pl.BlockSpec((bm, nv), lambda i: (i, 0)),
                  pl.BlockSpec((bm, nv), lambda i: (i, col)), pl.BlockSpec((1, nv), lambda i: (0, 0))],
        out_specs=pl.BlockSpec((bm, nv), lambda i: (i, 0)),
        out_shape=jax.ShapeDtypeStruct((m, nv), BF16),
        compiler_params=pltpu.CompilerParams(vmem_limit_bytes=VMEM_LIMIT),
        name="mlstm_finish",
    )(h_f, h_b, p, gain)


def _qk_prep(x, cos, sin, gain):
    lane = lax.broadcasted_iota(jnp.int32, x.shape, 1)
    lo = lane < DA_DQK
    x2 = x * x
    ss_lo = jnp.sum(jnp.where(lo, x2, 0.0), axis=-1, keepdims=True)
    ss_hi = jnp.sum(jnp.where(lo, 0.0, x2), axis=-1, keepdims=True)
    y = x * lax.rsqrt(jnp.where(lo, ss_lo, ss_hi) * (1.0 / DA_DQK) + EPS) * gain
    swapped = jnp.where((lane & ROPE_PAIRS) == 0, pltpu.roll(y, LANES - ROPE_PAIRS, 1),
                        pltpu.roll(y, ROPE_PAIRS, 1))
    return y * cos + swapped * sin


def _attn_kernel(lam_ref, q_ref, k_ref, v_ref, cos_ref, sin_ref, qg_ref, kg_ref, og_ref, o_ref,
                 kb_ref, vb_ref, *, n_ctx, chunk, prep_rows, out_scale):
    qi = pl.program_id(2)
    lam = lam_ref[0]
    n = k_ref.shape[1]
    bq = q_ref.shape[1]
    heads = q_ref.shape[2] // LANES

    @pl.when(qi == 0)
    def _():
        for r0 in range(0, n, prep_rows):
            rs = slice(r0, r0 + prep_rows)
            for h in range(heads):
                hs = slice(h * LANES, (h + 1) * LANES)
                kb_ref[h, rs, :] = _qk_prep(k_ref[0, rs, hs], cos_ref[rs, :], sin_ref[rs, :],
                                            kg_ref[...]).astype(BF16)
                vb_ref[h, rs, :LANES] = v_ref[0, rs, hs].astype(BF16)
                vb_ref[h, rs, LANES:] = jnp.ones((prep_rows, LANES), BF16)

    def attend(nk):
        rows = pl.ds(pl.multiple_of(qi * bq, bq), bq)
        cos, sin = cos_ref[rows, :], sin_ref[rows, :]
        lane = lax.broadcasted_iota(jnp.int32, (bq, LANES), 1)
        q2, m, acc = [], [], []
        for h in range(heads):
            q = _qk_prep(q_ref[0, :, h * LANES:(h + 1) * LANES], cos, sin, qg_ref[...])
            q = (q * (DA_DQK ** -0.5 * LOG2E)).astype(BF16)
            zero = jnp.zeros_like(q)
            q2.append(jnp.concatenate([jnp.where(lane < DA_DQK, q, zero),
                                       jnp.where(lane >= DA_DQK, q, zero)], axis=0))
            m.append(jnp.full((2 * bq, 1), -jnp.inf, F32))
            acc.append(jnp.zeros((2 * bq, 2 * LANES), F32))
        for c0 in range(0, nk, chunk):
            c1 = min(c0 + chunk, nk)
            for h in range(heads):
                s = lax.dot_general(q2[h], kb_ref[h, c0:c1, :], NT, preferred_element_type=F32)
                m_new = jnp.maximum(m[h], jnp.max(s, axis=-1, keepdims=True))
                p = jnp.exp2(s - m_new)
                acc[h] = jnp.exp2(m[h] - m_new) * acc[h] + jnp.dot(
                    p.astype(BF16), vb_ref[h, c0:c1, :], preferred_element_type=F32)
                m[h] = m_new
        for h in range(heads):
            o = acc[h][:, :LANES] / acc[h][:, LANES:]
            o = o[:bq] - lam * o[bq:]
            o = o * lax.rsqrt(jnp.mean(o * o, axis=-1, keepdims=True) + EPS) * og_ref[...] * out_scale
            o_ref[0, :, h * LANES:(h + 1) * LANES] = o.astype(BF16)

    n_ctx_blocks = n_ctx // bq

    @pl.when(qi < n_ctx_blocks)
    def _():
        attend(n_ctx)

    @pl.when(qi >= n_ctx_blocks)
    def _():
        attend(n)


def _diff_attention(p, lam, cos, sin, q_gain, k_gain, out_gain, *, n_ctx, out_scale, bq=ATTN_BQ,
                    chunk=ATTN_CHUNK, prep_rows=ATTN_PREP, heads=ATTN_HEADS):
    b, n, _ = p.shape
    hg = DA_HEADS // heads
    hd = DA_HEADS * LANES
    hw = heads * LANES
    const = lambda bi, hi, qi: (0, 0)
    return pl.pallas_call(
        functools.partial(_attn_kernel, n_ctx=n_ctx, chunk=chunk, prep_rows=prep_rows, out_scale=out_scale),
        grid=(b, hg, n // bq),
        in_specs=[pl.BlockSpec(memory_space=pltpu.SMEM),
                  pl.BlockSpec((1, bq, hw), lambda bi, hi, qi: (bi, qi, hi)),
                  pl.BlockSpec((1, n, hw), lambda bi, hi, qi: (bi, 0, hg + hi)),
                  pl.BlockSpec((1, n, hw), lambda bi, hi, qi: (bi, 0, 2 * hg + hi)),
                  pl.BlockSpec((n, LANES), const), pl.BlockSpec((n, LANES), const),
                  pl.BlockSpec((1, LANES), const), pl.BlockSpec((1, LANES), const),
                  pl.BlockSpec((1, LANES), const)],
        out_specs=pl.BlockSpec((1, bq, hw), lambda bi, hi, qi: (bi, qi, hi)),
        out_shape=jax.ShapeDtypeStruct((b, n, hd), BF16),
        scratch_shapes=[pltpu.VMEM((heads, n, LANES), BF16), pltpu.VMEM((heads, n, 2 * LANES), BF16)],
        compiler_params=pltpu.CompilerParams(
            dimension_semantics=("arbitrary", "arbitrary", "arbitrary"), vmem_limit_bytes=VMEM_LIMIT),
        name="diff_attention",
    )(lam, p, p, p, cos, sin, q_gain, k_gain, out_gain)


def _scan_chunk(t, n_ctx_chunks, n_chunks, reverse):
    if not reverse:
        return t
    return jnp.where(t < n_ctx_chunks, n_ctx_chunks - 1 - t, n_chunks - 1 - (t - n_ctx_chunks))


def _gla_constants(L, reverse):
    nlev = int(math.log2(L))
    idx = np.arange(L)
    i, t = idx[:, None], idx[None, :]
    if reverse:
        i, t = L - 1 - i, L - 1 - t
    rs = [(t <= i), (t > i)]
    am = [(i == t)]
    for lev in range(nlev):
        m = L >> (lev + 1)
        blk_i, blk_t = i // (2 * m), t // (2 * m)
        mid = blk_i * 2 * m + m
        q_role = i >= mid
        rs.append(np.where(q_role, (t >= mid) & (t <= i), (t > i) & (t < mid)) & (blk_i == blk_t))
        am.append((blk_i == blk_t) & q_role & (t < mid))
    return (np.stack(rs).astype(np.float32).reshape((nlev + 2) * L, L),
            np.stack(am).astype(np.float32))


def _gla_kernel(q_ref, k_ref, v_ref, glr_ref, w2_ref, gb_ref, rsum_ref, amask_ref, o_ref, st_ref):
    L = q_ref.shape[1]
    nlev = amask_ref.shape[0] - 1

    @pl.when(pl.program_id(1) == 0)
    def _():
        st_ref[...] = jnp.zeros_like(st_ref)

    z = jnp.dot(glr_ref[0].astype(BF16), w2_ref[...], preferred_element_type=F32) + gb_ref[...]
    g = (jnp.minimum(z, 0.0) - jnp.log(1.0 + jnp.exp(-jnp.abs(z)))) * (1.0 / GLA_TAU)
    g1 = g.astype(BF16)
    g2 = (g - g1.astype(F32)).astype(BF16)
    g3 = (g - g1.astype(F32) - g2.astype(F32)).astype(BF16)
    rsum = rsum_ref[...]
    e_all = (jnp.dot(rsum, g1, preferred_element_type=F32) + jnp.dot(rsum, g2, preferred_element_type=F32)
             + jnp.dot(rsum, g3, preferred_element_type=F32))
    b_tot = jnp.sum(g, axis=0, keepdims=True)

    for h in range(GLA_HEADS):
        ck = slice(h * GLA_DK, (h + 1) * GLA_DK)
        cv = slice(h * GLA_DV, (h + 1) * GLA_DV)
        q = q_ref[0, :, ck] * (GLA_DK ** -0.5)
        k = k_ref[0, :, ck]
        v = v_ref[0, :, cv].astype(BF16)
        st = st_ref[h]
        a = amask_ref[0] * lax.dot_general(q.astype(BF16), k.astype(BF16), NT, preferred_element_type=F32)
        for lev in range(nlev):
            e = jnp.exp(e_all[(2 + lev) * L:(3 + lev) * L, ck])
            a = a + amask_ref[1 + lev] * lax.dot_general(
                (q * e).astype(BF16), (k * e).astype(BF16), NT, preferred_element_type=F32)
        qe = (q * jnp.exp(e_all[0:L, ck])).astype(BF16)
        o = lax.dot_general(qe, st.astype(BF16), NT, preferred_element_type=F32)
        o = o + jnp.dot(a.astype(BF16), v, preferred_element_type=F32)
        o_ref[0, :, cv] = o
        kd = (k * jnp.exp(e_all[L:2 * L, ck])).astype(BF16)
        st_ref[h] = jnp.exp(b_tot[:, ck]) * st + jnp.dot(
            v.T, kd, preferred_element_type=F32)


def _gla_scan(p, w2, gb, *, n_ctx, reverse, L=GLA_CHUNK):
    b, n, _ = p.shape
    nc, ncc = n // L, n_ctx // L
    rsum, amask = _gla_constants(L, reverse)
    chunk = functools.partial(_scan_chunk, n_ctx_chunks=ncc, n_chunks=nc, reverse=reverse)
    hk, hv = GLA_HEADS * GLA_DK, GLA_HEADS * GLA_DV
    cols = lambda width, start: pl.BlockSpec((1, L, width), lambda bi, t: (bi, chunk(t), start // width))
    row = lambda bi, t: (bi, chunk(t), 0)
    const2 = lambda bi, t: (0, 0)
    const3 = lambda bi, t: (0, 0, 0)
    return pl.pallas_call(
        _gla_kernel,
        grid=(b, nc),
        in_specs=[cols(hk, 0), cols(hk, hk), cols(hv, 2 * hk), cols(LANES, 2 * hk + 2 * hv),
                  pl.BlockSpec(w2.shape, const2), pl.BlockSpec(gb.shape, const2),
                  pl.BlockSpec(rsum.shape, const2), pl.BlockSpec(amask.shape, const3)],
        out_specs=pl.BlockSpec((1, L, hv), row),
        out_shape=jax.ShapeDtypeStruct((b, n, hv), F32),
        scratch_shapes=[pltpu.VMEM((GLA_HEADS, GLA_DV, GLA_DK), F32)],
        compiler_params=pltpu.CompilerParams(
            dimension_semantics=("arbitrary", "arbitrary"), vmem_limit_bytes=VMEM_LIMIT),
        name="gla_scan_bwd" if reverse else "gla_scan_fwd",
    )(p, p, p, p, w2, gb, jnp.asarray(rsum, BF16), jnp.asarray(amask))


def _log_sigmoid(x):
    return jnp.minimum(x, 0.0) - jnp.log(1.0 + jnp.exp(-jnp.abs(x)))


def _mlstm_kernel(q_ref, k_ref, v_ref, gc_ref, gr_ref, tri_ref, o_ref, c_ref, n_ref, m_ref, *, reverse):
    L = q_ref.shape[1]
    H = ML_HEADS

    @pl.when(pl.program_id(1) == 0)
    def _():
        c_ref[...] = jnp.zeros_like(c_ref)
        n_ref[...] = jnp.zeros_like(n_ref)
        m_ref[...] = jnp.zeros_like(m_ref)

    tri = tri_ref[...]
    gc = gc_ref[0]
    gr = gr_ref[0]
    ic_col, ic_row = gc[:, :H], gr[:H, :]
    b_col = jnp.dot(tri, _log_sigmoid(gc[:, H:]), precision=HI, preferred_element_type=F32)
    b_row = lax.dot_general(_log_sigmoid(gr[H:, :]), tri, NT, precision=HI, preferred_element_type=F32)
    last = 0 if reverse else L - 1
    causal = tri > 0.5

    for h in range(H):
        ck = slice(h * ML_DK, (h + 1) * ML_DK)
        cv = slice(h * ML_DV, (h + 1) * ML_DV)
        q = q_ref[0, :, ck]
        k = k_ref[0, :, ck]
        v = v_ref[0, :, cv].astype(BF16)
        c, n, m = c_ref[h], n_ref[h], m_ref[h]
        bc, br = b_col[:, h:h + 1], b_row[h:h + 1, :]
        icc, icr = ic_col[:, h:h + 1], ic_row[h:h + 1, :]
        b_last = bc[last:last + 1, :]

        a = bc + m
        dmat = jnp.where(causal, bc - br + icr, -jnp.inf)
        m_t = jnp.maximum(a, jnp.max(dmat, axis=-1, keepdims=True))
        w_inter = jnp.exp(a - m_t)
        s = lax.dot_general(q, k, NT, preferred_element_type=F32) * jnp.exp(dmat - m_t)
        num = w_inter * jnp.dot(q, c.astype(BF16), preferred_element_type=F32) + jnp.dot(
            s.astype(BF16), v, preferred_element_type=F32)
        den = w_inter * jnp.sum(q.astype(F32) * n, axis=-1, keepdims=True) + jnp.sum(
            s, axis=-1, keepdims=True)
        o_ref[0, :, cv] = num / jnp.maximum(jnp.abs(den), jnp.exp(-m_t))

        gs_col = b_last - bc + icc
        gs_row = b_last - br + icr
        m_new = jnp.maximum(b_last + m, jnp.max(gs_row, axis=-1, keepdims=True))
        decay = jnp.exp(b_last + m - m_new)
        wk = jnp.exp(gs_col - m_new) * k.astype(F32)
        c_ref[h] = decay * c + jnp.dot(wk.astype(BF16).T, v, preferred_element_type=F32)
        n_ref[h] = decay * n + jnp.sum(wk, axis=0, keepdims=True)
        m_ref[h] = m_new


def _mlstm_scan(qk, p, gates, *, v_col, n_ctx, reverse, L=ML_CHUNK):
    b, n, _ = qk.shape
    hk, hv = ML_HEADS * ML_DK, ML_HEADS * ML_DV
    nc, ncc = n // L, n_ctx // L
    idx = np.arange(L)
    tri = (idx[None, :] >= idx[:, None]) if reverse else (idx[None, :] <= idx[:, None])
    tri = jnp.asarray(tri.astype(np.float32))
    gates_t = jnp.swapaxes(gates, 1, 2)
    chunk = functools.partial(_scan_chunk, n_ctx_chunks=ncc, n_chunks=nc, reverse=reverse)
    col = lambda j: (lambda bi, t: (bi, chunk(t), j))
    return pl.pallas_call(
        functools.partial(_mlstm_kernel, reverse=reverse),
        grid=(b, nc),
        in_specs=[pl.BlockSpec((1, L, hk), col(0)), pl.BlockSpec((1, L, hk), col(1)),
                  pl.BlockSpec((1, L, hv), col(v_col)), pl.BlockSpec((1, L, 2 * ML_HEADS), col(0)),
                  pl.BlockSpec((1, 2 * ML_HEADS, L), lambda bi, t: (bi, 0, chunk(t))),
                  pl.BlockSpec((L, L), lambda bi, t: (0, 0))],
        out_specs=pl.BlockSpec((1, L, hv), col(0)),
        out_shape=jax.ShapeDtypeStruct((b, n, hv), F32),
        scratch_shapes=[pltpu.VMEM((ML_HEADS, ML_DK, ML_DV), F32),
                        pltpu.VMEM((ML_HEADS, 1, ML_DK), F32),
                        pltpu.VMEM((ML_HEADS, 1, 1), F32)],
        compiler_params=pltpu.CompilerParams(
            dimension_semantics=("arbitrary", "arbitrary"), vmem_limit_bytes=VMEM_LIMIT),
        name="mlstm_scan_bwd" if reverse else "mlstm_scan_fwd",
    )(qk, qk, p, gates, gates_t, tri)


def _silu(x):
    return x * jax.nn.sigmoid(x)


def _rope_tables(n_lat, n_ctx):
    rows = n_lat // GRID_W
    row = jnp.repeat(jnp.arange(rows, dtype=F32), GRID_W)
    col = jnp.tile(jnp.arange(GRID_W, dtype=F32), rows)
    inv = jnp.power(ROPE_BASE, -jnp.arange(ROPE_PAIRS, dtype=F32) / ROPE_PAIRS)
    ang_r, ang_c = row[:, None] * inv, col[:, None] * inv
    cos = jnp.concatenate([jnp.cos(ang_r)] * 2 + [jnp.cos(ang_c)] * 2, axis=-1)
    sin = jnp.concatenate([-jnp.sin(ang_r), jnp.sin(ang_r), -jnp.sin(ang_c), jnp.sin(ang_c)], axis=-1)
    pad = ((n_ctx, 0), (0, 0))
    cos, sin = jnp.pad(cos, pad, constant_values=1.0), jnp.pad(sin, pad)
    return jnp.tile(cos, (1, 2)), jnp.tile(sin, (1, 2))


def _even_mixer(hn, bsz, w_in, i, gate_w2, gate_b, gla_norm_g, sc_conv_w, seq):
    nq = GLA_HEADS * GLA_DK
    nv = GLA_HEADS * GLA_DV
    gate_col = 2 * nq + 2 * nv
    pa = _matmul(hn, w_in, i, bn=640, ncols=gate_col + LANES)
    pb = _matmul(hn, w_in, i, bn=SC_WIDTH, col0=gate_col, ncols=3 * SC_WIDTH,
                 lane_shift=2 * GLA_GATE_RANK)
    pa3 = pa.reshape(bsz, -1, pa.shape[1])
    outs = []
    for direction in range(2):
        w2 = jnp.zeros((LANES, nq), F32).at[
            direction * GLA_GATE_RANK:(direction + 1) * GLA_GATE_RANK].set(gate_w2[direction])
        outs.append(_gla_scan(pa3, w2.astype(BF16), gate_b[direction][None, :],
                              n_ctx=seq["n_ctx"], reverse=bool(direction)).reshape(-1, nv))
    gain = jnp.tile(gla_norm_g, GLA_HEADS)[None, :]
    return [_even_finish(outs[0], outs[1], pa, pb, gain, sc_conv_w, **seq)]


def _odd_mixer(hn, bsz, rope, layer, w_in, i, qn_g, kn_g, lam_p, subln_g, ml_conv_w, ml_gate_b,
               ml_norm_g, seq):
    lam_init = 0.8 - 0.6 * math.exp(-0.3 * layer)
    lam = (jnp.exp(jnp.sum(lam_p[0] * lam_p[1])) - jnp.exp(jnp.sum(lam_p[2] * lam_p[3])) + lam_init)
    na = DA_HEADS * 2 * DA_DQK
    nk = ML_HEADS * ML_DK
    nv = ML_HEADS * ML_DV
    n_main = 3 * na + 2 * nk + 2 * nv
    p = _matmul(hn, w_in, i, bn=1024, ncols=n_main)
    mg = _matmul(hn, w_in, i, bn=LANES, col0=n_main, ncols=LANES)[:, :4 * ML_HEADS]
    p3 = p.reshape(bsz, -1, n_main)
    n = p3.shape[1]

    cos, sin = rope
    tile2 = lambda g: jnp.tile(g, 2)[None, :]
    da = _diff_attention(p3, lam.reshape(1).astype(F32), cos, sin, tile2(qn_g), tile2(kn_g),
                         subln_g[None, :], n_ctx=seq["n_ctx"], out_scale=1.0 - lam_init)

    scale = jnp.concatenate([jnp.ones((nk,), F32), jnp.full((nk,), ML_DK ** -0.5, F32)])[None, :]
    hqk = _mlstm_prep(p, ml_conv_w, scale, col=3 * na // (2 * nk), **seq).reshape(bsz, n, 2 * nk)
    gates = (mg + ml_gate_b).reshape(bsz, n, 4 * ML_HEADS)
    outs = []
    for direction in range(2):
        g_dir = gates[..., direction * 2 * ML_HEADS:(direction + 1) * 2 * ML_HEADS]
        outs.append(_mlstm_scan(hqk, p3, g_dir, v_col=(3 * na + 2 * nk) // nv, n_ctx=seq["n_ctx"],
                                reverse=bool(direction)).reshape(-1, nv))
    m = _mlstm_finish(outs[0], outs[1], p, jnp.tile(ml_norm_g, ML_HEADS)[None, :],
                      col=(3 * na + 2 * nk + nv) // nv)
    return [da.reshape(-1, na), m]


def _modulation(c, c_ctx, ada_w, ada_b, layer):
    cc = jnp.concatenate([c, c_ctx[None, :]], axis=0)
    rows = cc.shape[0]
    act = jnp.pad(_silu(cc), ((0, BF16_ROWS - rows), (0, 0))).astype(BF16)
    mod = _matmul(act, ada_w, layer, bn=1024, bm=BF16_ROWS)[:rows] + ada_b[layer]
    return mod.reshape(rows, 6, -1)


def kernel(x, c, ctx, c_ctx, ada_w, ada_b, norm1_g, norm2_g, ev_w_in, ev_w_out, gla_gate_w2, gla_gate_b, gla_norm_g, sc_conv_w, od_w_in, od_w_out, da_qnorm_g, da_knorm_g, da_lambda, da_subln_g, ml_conv_w, ml_gate_b, ml_norm_g, ffn_w_up, ffn_conv_w, ffn_conv_b, ffn_w_down):
    bsz, n_lat, d = x.shape
    n_ctx = ctx.shape[1]
    n = n_ctx + n_lat
    depth = ada_w.shape[0]
    d_ff = ffn_w_down.shape[1]
    assert n % ROW_TILE == 0 and n_ctx <= ROW_CHUNK and d_ff % (4 * LANES) == 0
    rope = _rope_tables(n_lat, n_ctx)
    seq = dict(n_ctx=n_ctx, n_seq=n)

    mods = [_modulation(c, c_ctx, ada_w, ada_b, layer) for layer in range(depth)]
    xs = jnp.concatenate([ctx, x], axis=1).reshape(bsz * n, d)
    hn = _first_norm(xs, norm1_g[0][None, :], mods[0], **seq)
    for layer in range(depth):
        i = layer // 2
        if layer % 2 == 0:
            mix = _even_mixer(hn, bsz, ev_w_in, i, gla_gate_w2[i], gla_gate_b[i], gla_norm_g[i],
                              sc_conv_w[i], seq)
            w_out = ev_w_out
        else:
            mix = _odd_mixer(hn, bsz, rope, layer, od_w_in, i, da_qnorm_g[i], da_knorm_g[i],
                             da_lambda[i], da_subln_g[i], ml_conv_w[i], ml_gate_b[i], ml_norm_g[i], seq)
            w_out = od_w_out
        xs, hn = _out_proj(mix, _to_bf16(w_out, i, rows=512), xs, mods[layer], norm2_g[layer][None, :],
                           **seq)
        act = _ffn_up(hn, ffn_w_up, layer, ffn_conv_w[layer], ffn_conv_b[layer][None, :], **seq)
        nxt = min(layer + 1, depth - 1)
        xs, hn = _down_proj(act, _to_bf16(ffn_w_down, layer, rows=d_ff // 8), xs,
                            mods[layer], norm1_g[nxt][None, :], mods[nxt], bk=d_ff // 4, **seq)
    return xs.reshape(bsz, n, d)[:, n_ctx:, :]
```

```python
import functools
import math

import jax
import jax.numpy as jnp
import numpy as np
from jax import lax
from jax.experimental import pallas as pl
from jax.experimental.pallas import tpu as pltpu

GRID_W = 64
EPS = 1e-6
GLA_HEADS, GLA_DK, GLA_DV, GLA_GATE_RANK, GLA_TAU = 4, 128, 256, 16, 16.0
SC_WIDTH = 1024
DA_HEADS, DA_DQK, DA_DV = 8, 64, 128
ROPE_BASE = 10000.0
ROPE_PAIRS = DA_DQK // 4
ML_HEADS, ML_DK, ML_DV = 4, 128, 256

GLA_CHUNK = 128
ML_CHUNK = 256
ATTN_BQ = 256
ATTN_CHUNK = 1024
ATTN_PREP = 544
ATTN_HEADS = 2
ROW_TILE = 1088
ROW_CHUNK = 272
LANES = 128
SUBLANES = 8
BF16_ROWS = 16
LOG2E = 1.4426950408889634
VMEM_LIMIT = 56 * 1024 * 1024

F32 = jnp.float32
BF16 = jnp.bfloat16
HI = lax.Precision.HIGHEST
NT = (((1,), (1,)), ((), ()))


def _mm_kernel(x_ref, w_ref, *rest, row_shift, transposed):
    o_ref, wb_ref = rest[-2:]

    @pl.when(pl.program_id(1) == 0)
    def _():
        if not transposed:
            wb_ref[...] = w_ref[...].astype(BF16)
            return
        bn = wb_ref.shape[1]
        piece = 2 * LANES
        for c0 in range(0, bn, piece):
            c1 = min(c0 + piece, bn)
            if row_shift and c1 == bn:
                rows = jnp.concatenate([w_ref[c0 + row_shift:bn, :], rest[0][...]], axis=0)
            else:
                rows = w_ref[c0 + row_shift:c1 + row_shift, :]
            wb_ref[:, c0:c1] = rows.T.astype(BF16)

    o_ref[...] = jnp.dot(x_ref[...], wb_ref[...], preferred_element_type=F32).astype(o_ref.dtype)


def _matmul(x, w, layer, *, bn, col0=0, ncols=None, col_shift=0, transposed=False, bm=ROW_TILE,
            out_dtype=F32):
    m, k = x.shape
    n_all = w.shape[1] if transposed else w.shape[2]
    ncols = n_all - col0 if ncols is None else ncols
    j0 = col0 // bn
    if transposed:
        w_spec = pl.BlockSpec((None, bn, k), lambda j, i: (layer, j0 + j, 0))
    else:
        w_spec = pl.BlockSpec((None, k, bn), lambda j, i: (layer, 0, j0 + j))
    in_specs = [pl.BlockSpec((bm, k), lambda j, i: (i, 0)), w_spec]
    operands = [x, w]
    if col_shift:
        per = bn // col_shift
        in_specs.append(pl.BlockSpec((None, col_shift, k), lambda j, i: (layer, (j0 + j + 1) * per, 0)))
        operands.append(w)
    return pl.pallas_call(
        functools.partial(_mm_kernel, row_shift=col_shift, transposed=transposed),
        grid=(pl.cdiv(ncols, bn), m // bm),
        in_specs=in_specs,
        out_specs=pl.BlockSpec((bm, bn), lambda j, i: (i, j)),
        out_shape=jax.ShapeDtypeStruct((m, ncols), out_dtype),
        scratch_shapes=[pltpu.VMEM((k, bn), BF16)],
        compiler_params=pltpu.CompilerParams(
            dimension_semantics=("arbitrary", "arbitrary"), vmem_limit_bytes=VMEM_LIMIT),
        name="matmul",
    )(*operands)


def _cast_kernel(x_ref, o_ref):
    o_ref[...] = x_ref[...].astype(o_ref.dtype)


def _to_bf16(w, layer, *, rows):
    _, r, c = w.shape
    return pl.pallas_call(
        _cast_kernel,
        grid=(r // rows,),
        in_specs=[pl.BlockSpec((None, rows, c), lambda i: (layer, i, 0))],
        out_specs=pl.BlockSpec((rows, c), lambda i: (i, 0)),
        out_shape=jax.ShapeDtypeStruct((r, c), BF16),
        compiler_params=pltpu.CompilerParams(vmem_limit_bytes=VMEM_LIMIT),
        name="to_bf16",
    )(w)


def _norm_mod(x, gain, scale, shift):
    return x * lax.rsqrt(jnp.mean(x * x, axis=-1, keepdims=True) + EPS) * gain * (1.0 + scale) + shift


def _head_rmsnorm(x, width):
    parts = []
    for c0 in range(0, x.shape[1], width):
        xh = x[:, c0:c0 + width]
        parts.append(xh * lax.rsqrt(jnp.mean(xh * xh, axis=-1, keepdims=True) + EPS))
    return parts[0] if len(parts) == 1 else jnp.concatenate(parts, axis=1)


def _sigmoid(x):
    return 1.0 / (1.0 + jnp.exp(-x))


def _mod_row(is_ctx, modb_ref, modc_ref, idx):
    if is_ctx is None:
        return modb_ref[0, idx:idx + 1, :]
    return jnp.where(is_ctx, modc_ref[0, idx:idx + 1, :], modb_ref[0, idx:idx + 1, :])


def _chunk_rows(r0, rc, tiles_per_batch, bm):
    rows = r0 + lax.broadcasted_iota(jnp.int32, (rc, 1), 0)
    return rows, (pl.program_id(0) % tiles_per_batch) * bm + rows


def _chunk_is_ctx(r0, pos, n_ctx):
    return pos < n_ctx if r0 < n_ctx else None


def _edge_rows(r0, rc, bm, tiles_per_batch, n_ctx, n_seq, gate_ref, hprev_ref, hnext_ref):
    hr = hprev_ref.shape[0]
    pos0 = (pl.program_id(0) % tiles_per_batch) * bm
    if r0 == 0:
        before = hprev_ref[...].astype(F32)[hr - 1:hr, :]
        before = jnp.where((pos0 == 0) | (pos0 == n_ctx), 0.0, before)
    else:
        before = gate_ref[r0 - hr:r0, :].astype(F32)[hr - 1:hr, :]
    if r0 + rc == bm:
        after = hnext_ref[...].astype(F32)[0:1, :]
        after = jnp.where((pos0 + bm == n_ctx) | (pos0 + bm == n_seq), 0.0, after)
    else:
        after = gate_ref[r0 + rc:r0 + rc + hr, :].astype(F32)[0:1, :]
    return before, after


def _splits_rows(r0, rc, bm, tiles_per_batch, n_ctx):
    return any(r0 <= n_ctx - k * bm <= r0 + rc for k in range(tiles_per_batch))


def _dwconv3(s_ref, g, before, after, cw_ref, pos, n_ctx):
    rc = g.shape[0]
    s_ref[SUBLANES:SUBLANES + rc, :] = g
    s_ref[SUBLANES - 1:SUBLANES, :] = before
    s_ref[SUBLANES + rc:SUBLANES + rc + 1, :] = after
    prev = s_ref[SUBLANES - 1:SUBLANES - 1 + rc, :]
    nxt = s_ref[SUBLANES + 1:SUBLANES + 1 + rc, :]
    if pos is not None:
        prev = jnp.where(pos == n_ctx, 0.0, prev)
        nxt = jnp.where(pos == n_ctx - 1, 0.0, nxt)
    return prev * cw_ref[0:1, :] + g * cw_ref[1:2, :] + nxt * cw_ref[2:3, :]


def _mod_specs(n_batch, tiles_per_batch, d):
    return [pl.BlockSpec((1, 6, d), lambda i, *_: (i // tiles_per_batch, 0, 0)),
            pl.BlockSpec((1, 6, d), lambda i, *_: (n_batch, 0, 0))]


def _halo_specs(bm, width, col, n_rows, hr):
    hb = bm // hr
    last = n_rows // hr - 1
    return [pl.BlockSpec((hr, width), lambda i, *k: (jnp.maximum(i * hb - 1, 0), col(*k))),
            pl.BlockSpec((hr, width), lambda i, *k: (jnp.minimum((i + 1) * hb, last), col(*k)))]


def _norm_kernel(x_ref, gain_ref, modb_ref, modc_ref, o_ref, *, n_ctx, tiles_per_batch):
    _, pos = _chunk_rows(0, x_ref.shape[0], tiles_per_batch, x_ref.shape[0])
    is_ctx = pos < n_ctx
    o_ref[...] = _norm_mod(x_ref[...], gain_ref[...], _mod_row(is_ctx, modb_ref, modc_ref, 1),
                           _mod_row(is_ctx, modb_ref, modc_ref, 0)).astype(BF16)


def _first_norm(x, gain, mod, *, n_ctx, n_seq, bm=ROW_TILE):
    m, d = x.shape
    tpb = n_seq // bm
    return pl.pallas_call(
        functools.partial(_norm_kernel, n_ctx=n_ctx, tiles_per_batch=tpb),
        grid=(m // bm,),
        in_specs=[pl.BlockSpec((bm, d), lambda i: (i, 0)), pl.BlockSpec((1, d), lambda i: (0, 0))]
        + _mod_specs(mod.shape[0] - 1, tpb, d),
        out_specs=pl.BlockSpec((bm, d), lambda i: (i, 0)),
        out_shape=jax.ShapeDtypeStruct((m, d), BF16),
        compiler_params=pltpu.CompilerParams(vmem_limit_bytes=VMEM_LIMIT),
        name="first_norm",
    )(x, gain, mod, mod)


def _residual_norm(rs, acc, xo_ref, hn_ref, x_ref, modb_ref, modc_ref, gain_ref, nmodb_ref, nmodc_ref,
                   is_ctx, gate_idx, shift_idx):
    x_new = x_ref[rs, :] + _mod_row(is_ctx, modb_ref, modc_ref, gate_idx) * acc
    xo_ref[rs, :] = x_new
    hn_ref[rs, :] = _norm_mod(
        x_new, gain_ref[...], _mod_row(is_ctx, nmodb_ref, nmodc_ref, shift_idx + 1),
        _mod_row(is_ctx, nmodb_ref, nmodc_ref, shift_idx)).astype(BF16)


def _out_kernel(*refs, n_in, n_ctx, tiles_per_batch):
    a_refs = refs[:n_in]
    w_ref, x_ref, modb_ref, modc_ref, gain_ref, xo_ref, hn_ref = refs[n_in:]
    bm = x_ref.shape[0]
    k0 = 0
    for n_done, a_ref in enumerate(a_refs):
        part = jnp.dot(a_ref[...], w_ref[k0:k0 + a_ref.shape[1], :], preferred_element_type=F32)
        xo_ref[...] = part if n_done == 0 else xo_ref[...] + part
        k0 += a_ref.shape[1]
    for r0 in range(0, bm, ROW_CHUNK):
        rs = slice(r0, r0 + ROW_CHUNK)
        _, pos = _chunk_rows(r0, ROW_CHUNK, tiles_per_batch, bm)
        _residual_norm(rs, xo_ref[rs, :], xo_ref, hn_ref, x_ref, modb_ref, modc_ref, gain_ref, modb_ref,
                       modc_ref, _chunk_is_ctx(r0, pos, n_ctx), 2, 3)


def _out_proj(acts, w, x, mod, gain, *, n_ctx, n_seq, bm=ROW_TILE // 2):
    m, d = x.shape
    tpb = n_seq // bm
    rows = lambda width: pl.BlockSpec((bm, width), lambda i: (i, 0))
    return pl.pallas_call(
        functools.partial(_out_kernel, n_in=len(acts), n_ctx=n_ctx, tiles_per_batch=tpb),
        grid=(m // bm,),
        in_specs=[rows(a.shape[1]) for a in acts]
        + [pl.BlockSpec(w.shape, lambda i: (0, 0), pipeline_mode=pl.Buffered(1)), rows(d)]
        + _mod_specs(mod.shape[0] - 1, tpb, d) + [pl.BlockSpec((1, d), lambda i: (0, 0))],
        out_specs=[rows(d), rows(d)],
        out_shape=[jax.ShapeDtypeStruct((m, d), F32), jax.ShapeDtypeStruct((m, d), BF16)],
        compiler_params=pltpu.CompilerParams(vmem_limit_bytes=VMEM_LIMIT),
        name="out_proj",
    )(*acts, w, x, mod, mod, gain)


def _up_kernel(x_ref, xp_ref, xn_ref, wg_ref, wv_ref, cw_ref, cb_ref, o_ref, wgb_ref, wvb_ref, xe_ref,
               s_ref, v_ref, *, n_ctx, n_seq, tiles_per_batch):
    i = pl.program_id(1)
    bm, rc, hr = x_ref.shape[0], ROW_CHUNK, xp_ref.shape[0]

    @pl.when(i == 0)
    def _():
        wgb_ref[...] = wg_ref[...].astype(BF16)
        wvb_ref[...] = wv_ref[...].astype(BF16)

    xe_ref[0:hr, :] = xp_ref[...]
    xe_ref[hr:hr + bm, :] = x_ref[...]
    xe_ref[hr + bm:, :] = xn_ref[...]
    s_ref[...] = jnp.dot(xe_ref[...], wgb_ref[...], preferred_element_type=F32)
    v_ref[...] = jnp.dot(x_ref[...], wvb_ref[...], preferred_element_type=F32)
    pos0 = (i % tiles_per_batch) * bm
    s_ref[hr - 1:hr, :] = jnp.where((pos0 == 0) | (pos0 == n_ctx), 0.0, s_ref[hr - 1:hr, :])
    s_ref[hr + bm:hr + bm + 1, :] = jnp.where(
        (pos0 + bm == n_ctx) | (pos0 + bm == n_seq), 0.0, s_ref[hr + bm:hr + bm + 1, :])
    for r0 in range(0, bm, rc):
        rs = slice(r0, r0 + rc)
        prev = s_ref[hr - 1 + r0:hr - 1 + r0 + rc, :]
        nxt = s_ref[hr + 1 + r0:hr + 1 + r0 + rc, :]
        if _splits_rows(r0, rc, bm, tiles_per_batch, n_ctx):
            pos = pos0 + r0 + lax.broadcasted_iota(jnp.int32, (rc, 1), 0)
            prev = jnp.where(pos == n_ctx, 0.0, prev)
            nxt = jnp.where(pos == n_ctx - 1, 0.0, nxt)
        z = (prev * cw_ref[0:1, :] + s_ref[hr + r0:hr + r0 + rc, :] * cw_ref[1:2, :]
             + nxt * cw_ref[2:3, :] + cb_ref[...])
        o_ref[rs, :] = (z * _sigmoid(z) * v_ref[rs, :]).astype(BF16)


def _ffn_up(x, w, layer, conv_w, conv_b, *, n_ctx, n_seq, bn=512, bm=ROW_TILE):
    m, k = x.shape
    f = w.shape[2] // 2
    nj = f // bn
    tpb = n_seq // bm
    hb = bm // BF16_ROWS
    last = m // BF16_ROWS - 1
    return pl.pallas_call(
        functools.partial(_up_kernel, n_ctx=n_ctx, n_seq=n_seq, tiles_per_batch=tpb),
        grid=(nj, m // bm),
        in_specs=[pl.BlockSpec((bm, k), lambda j, i: (i, 0)),
                  pl.BlockSpec((BF16_ROWS, k), lambda j, i: (jnp.maximum(i * hb - 1, 0), 0)),
                  pl.BlockSpec((BF16_ROWS, k), lambda j, i: (jnp.minimum((i + 1) * hb, last), 0)),
                  pl.BlockSpec((None, k, bn), lambda j, i: (layer, 0, j)),
                  pl.BlockSpec((None, k, bn), lambda j, i: (layer, 0, j + nj)),
                  pl.BlockSpec((3, bn), lambda j, i: (0, j)), pl.BlockSpec((1, bn), lambda j, i: (0, j))],
        out_specs=pl.BlockSpec((bm, bn), lambda j, i: (i, j)),
        out_shape=jax.ShapeDtypeStruct((m, f), BF16),
        scratch_shapes=[pltpu.VMEM((k, bn), BF16), pltpu.VMEM((k, bn), BF16),
                        pltpu.VMEM((bm + 2 * BF16_ROWS, k), BF16),
                        pltpu.VMEM((bm + 2 * BF16_ROWS, bn), F32), pltpu.VMEM((bm, bn), F32)],
        compiler_params=pltpu.CompilerParams(
            dimension_semantics=("arbitrary", "arbitrary"), vmem_limit_bytes=VMEM_LIMIT),
        name="ffn_up",
    )(x, x, x, w, w, conv_w, conv_b)


def _down_kernel(a_ref, w_ref, x_ref, modb_ref, modc_ref, gain_ref, nmodb_ref, nmodc_ref, xo_ref, hn_ref, *,
                 n_ctx, tiles_per_batch):
    bm = a_ref.shape[0]
    rc = ROW_CHUNK

    part = jnp.dot(a_ref[...], w_ref[...], preferred_element_type=F32)

    @pl.when(pl.program_id(1) == 0)
    def _():
        xo_ref[...] = part

    @pl.when(pl.program_id(1) > 0)
    def _():
        xo_ref[...] += part

    @pl.when(pl.program_id(1) == pl.num_programs(1) - 1)
    def _():
        for r0 in range(0, bm, rc):
            rs = slice(r0, r0 + rc)
            _, pos = _chunk_rows(r0, rc, tiles_per_batch, bm)
            _residual_norm(rs, xo_ref[rs, :], xo_ref, hn_ref, x_ref, modb_ref, modc_ref, gain_ref,
                           nmodb_ref, nmodc_ref, _chunk_is_ctx(r0, pos, n_ctx), 5, 0)


def _down_proj(a, w, x, mod, gain, next_mod, *, n_ctx, n_seq, bm=ROW_TILE // 2, bk=1408):
    m, f = a.shape
    d = w.shape[1]
    tpb = n_seq // bm
    row = lambda i, k: (i, 0)
    return pl.pallas_call(
        functools.partial(_down_kernel, n_ctx=n_ctx, tiles_per_batch=tpb),
        grid=(m // bm, f // bk),
        in_specs=[pl.BlockSpec((bm, bk), lambda i, k: (i, k)), pl.BlockSpec((bk, d), lambda i, k: (k, 0)),
                  pl.BlockSpec((bm, d), row)]
        + _mod_specs(mod.shape[0] - 1, tpb, d) + [pl.BlockSpec((1, d), lambda i, k: (0, 0))]
        + _mod_specs(mod.shape[0] - 1, tpb, d),
        out_specs=[pl.BlockSpec((bm, d), row), pl.BlockSpec((bm, d), row)],
        out_shape=[jax.ShapeDtypeStruct((m, d), F32), jax.ShapeDtypeStruct((m, d), BF16)],
        compiler_params=pltpu.CompilerParams(
            dimension_semantics=("arbitrary", "arbitrary"), vmem_limit_bytes=VMEM_LIMIT),
        name="down_proj",
    )(a, w, x, mod, mod, gain, next_mod, next_mod)


def _even_finish_kernel(of_ref, ob_ref, r_ref, sx_ref, sb_ref, sg_ref, sxp_ref, sxn_ref, sgp_ref, sgn_ref,
                        gain_ref, cw_ref, o_ref, s_ref, *, n_ctx, n_seq, tiles_per_batch):
    bm = of_ref.shape[0]
    nv = of_ref.shape[1]
    r = r_ref[...]
    o_ref[:, :nv] = (_head_rmsnorm(of_ref[...] + ob_ref[...], GLA_DV) * gain_ref[...]
                     * (r * _sigmoid(r))).astype(BF16)

    _, pos = _chunk_rows(0, bm, tiles_per_batch, bm)
    pos0 = (pl.program_id(0) % tiles_per_batch) * bm
    hr = sxp_ref.shape[0]
    before = (sxp_ref[...] * sgp_ref[...])[hr - 1:hr, :]
    before = jnp.where((pos0 == 0) | (pos0 == n_ctx), 0.0, before)
    after = (sxn_ref[...] * sgn_ref[...])[0:1, :]
    after = jnp.where((pos0 + bm == n_ctx) | (pos0 + bm == n_seq), 0.0, after)
    split = pos if _splits_rows(0, bm, bm, tiles_per_batch, n_ctx) else None
    conv = _dwconv3(s_ref, sg_ref[...] * sx_ref[...], before, after, cw_ref, split, n_ctx)
    o_ref[:, nv:] = (sb_ref[...] * conv).astype(BF16)


def _even_finish(o_f, o_b, pa, pb, gain, conv_w, *, n_ctx, n_seq, bm=ROW_CHUNK):
    m, nv = o_f.shape
    sc = conv_w.shape[1]
    tpb = n_seq // bm
    col = lambda j: (lambda i: (i, j))
    const = lambda i: (0, 0)
    return pl.pallas_call(
        functools.partial(_even_finish_kernel, n_ctx=n_ctx, n_seq=n_seq, tiles_per_batch=tpb),
        grid=(m // bm,),
        in_specs=[pl.BlockSpec((bm, nv), col(0)), pl.BlockSpec((bm, nv), col(0)),
                  pl.BlockSpec((bm, nv), col(2)),
                  pl.BlockSpec((bm, sc), col(0)), pl.BlockSpec((bm, sc), col(1)), pl.BlockSpec((bm, sc), col(2))]
        + _halo_specs(bm, sc, lambda: 0, m, SUBLANES) + _halo_specs(bm, sc, lambda: 2, m, SUBLANES)
        + [pl.BlockSpec((1, nv), const), pl.BlockSpec((3, sc), const)],
        out_specs=pl.BlockSpec((bm, nv + sc), col(0)),
        out_shape=jax.ShapeDtypeStruct((m, nv + sc), BF16),
        scratch_shapes=[pltpu.VMEM((bm + 2 * SUBLANES, sc), F32)],
        compiler_params=pltpu.CompilerParams(vmem_limit_bytes=VMEM_LIMIT),
        name="even_finish",
    )(o_f, o_b, pa, pb, pb, pb, pb, pb, pb, pb, gain, conv_w)


def _mlstm_prep_kernel(x_ref, xp_ref, xn_ref, cw_ref, scale_ref, o_ref, s_ref, *, n_ctx, n_seq,
                       tiles_per_batch):
    bm = x_ref.shape[0]
    _, pos = _chunk_rows(0, bm, tiles_per_batch, bm)
    before, after = _edge_rows(0, bm, bm, tiles_per_batch, n_ctx, n_seq, x_ref, xp_ref, xn_ref)
    split = pos if _splits_rows(0, bm, bm, tiles_per_batch, n_ctx) else None
    z = _dwconv3(s_ref, x_ref[...], before, after, cw_ref, split, n_ctx)
    o_ref[...] = (z * _sigmoid(z) * scale_ref[...]).astype(BF16)


def _mlstm_prep(p, conv_w, scale, *, col, n_ctx, n_seq, bm=ROW_CHUNK):
    m = p.shape[0]
    width = conv_w.shape[1]
    tpb = n_seq // bm
    const = lambda i: (0, 0)
    return pl.pallas_call(
        functools.partial(_mlstm_prep_kernel, n_ctx=n_ctx, n_seq=n_seq, tiles_per_batch=tpb),
        grid=(m // bm,),
        in_specs=[pl.BlockSpec((bm, width), lambda i: (i, col))]
        + _halo_specs(bm, width, lambda: col, m, SUBLANES)
        + [pl.BlockSpec((3, width), const), pl.BlockSpec((1, width), const)],
        out_specs=pl.BlockSpec((bm, width), lambda i: (i, 0)),
        out_shape=jax.ShapeDtypeStruct((m, width), BF16),
        scratch_shapes=[pltpu.VMEM((bm + 2 * SUBLANES, width), F32)],
        compiler_params=pltpu.CompilerParams(vmem_limit_bytes=VMEM_LIMIT),
        name="mlstm_prep",
    )(p, p, p, conv_w, scale)


def _mlstm_finish_kernel(hf_ref, hb_ref, mo_ref, gain_ref, o_ref):
    o_ref[...] = (_head_rmsnorm(hf_ref[...] + hb_ref[...], ML_DV) * gain_ref[...]
                  * _sigmoid(mo_ref[...])).astype(BF16)


def _mlstm_finish(h_f, h_b, p, gain, *, col, bm=ROW_TILE // 2):
    m, nv = h_f.shape
    return pl.pallas_call(
        _mlstm_finish_kernel,
        grid=(m // bm,),
        in_specs=[pl.BlockSpec((bm, nv), lambda i: (i, 0)), pl.BlockSpec((bm, nv), lambda i: (i, 0)),
                  pl.BlockSpec((bm, nv), lambda i: (i, col)), pl.BlockSpec((1, nv), lambda i: (0, 0))],
        out_specs=pl.BlockSpec((bm, nv), lambda i: (i, 0)),
        out_shape=jax.ShapeDtypeStruct((m, nv), BF16),
        compiler_params=pltpu.CompilerParams(vmem_limit_bytes=VMEM_LIMIT),
        name="mlstm_finish",
    )(h_f, h_b, p, gain)


def _qk_prep(x, cos, sin, gain):
    lane = lax.broadcasted_iota(jnp.int32, x.shape, 1)
    lo = lane < DA_DQK
    x2 = x * x
    ss_lo = jnp.sum(jnp.where(lo, x2, 0.0), axis=-1, keepdims=True)
    ss_hi = jnp.sum(jnp.where(lo, 0.0, x2), axis=-1, keepdims=True)
    y = x * lax.rsqrt(jnp.where(lo, ss_lo, ss_hi) * (1.0 / DA_DQK) + EPS) * gain
    swapped = jnp.where((lane & ROPE_PAIRS) == 0, pltpu.roll(y, LANES - ROPE_PAIRS, 1),
                        pltpu.roll(y, ROPE_PAIRS, 1))
    return y * cos + swapped * sin


def _attn_kernel(lam_ref, q_ref, k_ref, v_ref, cos_ref, sin_ref, qg_ref, kg_ref, og_ref, o_ref,
                 kb_ref, vb_ref, *, n_ctx, chunk, prep_rows, out_scale):
    qi = pl.program_id(2)
    lam = lam_ref[0]
    n = k_ref.shape[1]
    bq = q_ref.shape[1]
    heads = q_ref.shape[2] // LANES

    @pl.when(qi == 0)
    def _():
        for r0 in range(0, n, prep_rows):
            rs = slice(r0, r0 + prep_rows)
            for h in range(heads):
                hs = slice(h * LANES, (h + 1) * LANES)
                kb_ref[h, rs, :] = _qk_prep(k_ref[0, rs, hs], cos_ref[rs, :], sin_ref[rs, :],
                                            kg_ref[...]).astype(BF16)
                vb_ref[h, rs, :LANES] = v_ref[0, rs, hs].astype(BF16)
                vb_ref[h, rs, LANES:] = jnp.ones((prep_rows, LANES), BF16)

    def attend(nk):
        rows = pl.ds(pl.multiple_of(qi * bq, bq), bq)
        cos, sin = cos_ref[rows, :], sin_ref[rows, :]
        lane = lax.broadcasted_iota(jnp.int32, (bq, LANES), 1)
        q2, m, acc = [], [], []
        for h in range(heads):
            q = _qk_prep(q_ref[0, :, h * LANES:(h + 1) * LANES], cos, sin, qg_ref[...])
            q = (q * (DA_DQK ** -0.5 * LOG2E)).astype(BF16)
            zero = jnp.zeros_like(q)
            q2.append(jnp.concatenate([jnp.where(lane < DA_DQK, q, zero),
                                       jnp.where(lane >= DA_DQK, q, zero)], axis=0))
            m.append(jnp.full((2 * bq, 1), -jnp.inf, F32))
            acc.append(jnp.zeros((2 * bq, 2 * LANES), F32))
        for c0 in range(0, nk, chunk):
            c1 = min(c0 + chunk, nk)
            for h in range(heads):
                s = lax.dot_general(q2[h], kb_ref[h, c0:c1, :], NT, preferred_element_type=F32)
                m_new = jnp.maximum(m[h], jnp.max(s, axis=-1, keepdims=True))
                p = jnp.exp2(s - m_new)
                acc[h] = jnp.exp2(m[h] - m_new) * acc[h] + jnp.dot(
                    p.astype(BF16), vb_ref[h, c0:c1, :], preferred_element_type=F32)
                m[h] = m_new
        for h in range(heads):
            o = acc[h][:, :LANES] / acc[h][:, LANES:]
            o = o[:bq] - lam * o[bq:]
            o = o * lax.rsqrt(jnp.mean(o * o, axis=-1, keepdims=True) + EPS) * og_ref[...] * out_scale
            o_ref[0, :, h * LANES:(h + 1) * LANES] = o.astype(BF16)

    n_ctx_blocks = n_ctx // bq

    @pl.when(qi < n_ctx_blocks)
    def _():
        attend(n_ctx)

    @pl.when(qi >= n_ctx_blocks)
    def _():
        attend(n)


def _diff_attention(p, lam, cos, sin, q_gain, k_gain, out_gain, *, n_ctx, out_scale, bq=ATTN_BQ,
                    chunk=ATTN_CHUNK, prep_rows=ATTN_PREP, heads=ATTN_HEADS):
    b, n, _ = p.shape
    hg = DA_HEADS // heads
    hd = DA_HEADS * LANES
    hw = heads * LANES
    const = lambda bi, hi, qi: (0, 0)
    return pl.pallas_call(
        functools.partial(_attn_kernel, n_ctx=n_ctx, chunk=chunk, prep_rows=prep_rows, out_scale=out_scale),
        grid=(b, hg, n // bq),
        in_specs=[pl.BlockSpec(memory_space=pltpu.SMEM),
                  pl.BlockSpec((1, bq, hw), lambda bi, hi, qi: (bi, qi, hi)),
                  pl.BlockSpec((1, n, hw), lambda bi, hi, qi: (bi, 0, hg + hi)),
                  pl.BlockSpec((1, n, hw), lambda bi, hi, qi: (bi, 0, 2 * hg + hi)),
                  pl.BlockSpec((n, LANES), const), pl.BlockSpec((n, LANES), const),
                  pl.BlockSpec((1, LANES), const), pl.BlockSpec((1, LANES), const),
                  pl.BlockSpec((1, LANES), const)],
        out_specs=pl.BlockSpec((1, bq, hw), lambda bi, hi, qi: (bi, qi, hi)),
        out_shape=jax.ShapeDtypeStruct((b, n, hd), BF16),
        scratch_shapes=[pltpu.VMEM((heads, n, LANES), BF16), pltpu.VMEM((heads, n, 2 * LANES), BF16)],
        compiler_params=pltpu.CompilerParams(
            dimension_semantics=("arbitrary", "arbitrary", "arbitrary"), vmem_limit_bytes=VMEM_LIMIT),
        name="diff_attention",
    )(lam, p, p, p, cos, sin, q_gain, k_gain, out_gain)


def _scan_chunk(t, n_ctx_chunks, n_chunks, reverse):
    if not reverse:
        return t
    return jnp.where(t < n_ctx_chunks, n_ctx_chunks - 1 - t, n_chunks - 1 - (t - n_ctx_chunks))


def _gla_constants(L, reverse):
    nlev = int(math.log2(L))
    idx = np.arange(L)
    i, t = idx[:, None], idx[None, :]
    if reverse:
        i, t = L - 1 - i, L - 1 - t
    rs = [(t <= i), (t > i)]
    am = [(i == t)]
    for lev in range(nlev):
        m = L >> (lev + 1)
        blk_i, blk_t = i // (2 * m), t // (2 * m)
        mid = blk_i * 2 * m + m
        q_role = i >= mid
        rs.append(np.where(q_role, (t >= mid) & (t <= i), (t > i) & (t < mid)) & (blk_i == blk_t))
        am.append((blk_i == blk_t) & q_role & (t < mid))
    return (np.stack(rs).astype(np.float32).reshape((nlev + 2) * L, L),
            np.stack(am).astype(np.float32))


def _gla_kernel(q_ref, k_ref, v_ref, glr_ref, w2_ref, gb_ref, rsum_ref, amask_ref, o_ref, st_ref):
    L = q_ref.shape[1]
    nlev = amask_ref.shape[0] - 1

    @pl.when(pl.program_id(1) == 0)
    def _():
        st_ref[...] = jnp.zeros_like(st_ref)

    z = jnp.dot(glr_ref[0].astype(BF16), w2_ref[...], preferred_element_type=F32) + gb_ref[...]
    g = (jnp.minimum(z, 0.0) - jnp.log(1.0 + jnp.exp(-jnp.abs(z)))) * (1.0 / GLA_TAU)
    g1 = g.astype(BF16)
    g2 = (g - g1.astype(F32)).astype(BF16)
    g3 = (g - g1.astype(F32) - g2.astype(F32)).astype(BF16)
    rsum = rsum_ref[...]
    e_all = (jnp.dot(rsum, g1, preferred_element_type=F32) + jnp.dot(rsum, g2, preferred_element_type=F32)
             + jnp.dot(rsum, g3, preferred_element_type=F32))
    b_tot = jnp.sum(g, axis=0, keepdims=True)

    for h in range(GLA_HEADS):
        ck = slice(h * GLA_DK, (h + 1) * GLA_DK)
        cv = slice(h * GLA_DV, (h + 1) * GLA_DV)
        q = q_ref[0, :, ck] * (GLA_DK ** -0.5)
        k = k_ref[0, :, ck]
        v = v_ref[0, :, cv].astype(BF16)
        st = st_ref[h]
        a = amask_ref[0] * lax.dot_general(q.astype(BF16), k.astype(BF16), NT, preferred_element_type=F32)
        for lev in range(nlev):
            e = jnp.exp(e_all[(2 + lev) * L:(3 + lev) * L, ck])
            a = a + amask_ref[1 + lev] * lax.dot_general(
                (q * e).astype(BF16), (k * e).astype(BF16), NT, preferred_element_type=F32)
        qe = (q * jnp.exp(e_all[0:L, ck])).astype(BF16)
        o = lax.dot_general(qe, st.astype(BF16), NT, preferred_element_type=F32)
        o = o + jnp.dot(a.astype(BF16), v, preferred_element_type=F32)
        o_ref[0, :, cv] = o
        kd = (k * jnp.exp(e_all[L:2 * L, ck])).astype(BF16)
        st_ref[h] = jnp.exp(b_tot[:, ck]) * st + jnp.dot(
            v.T, kd, preferred_element_type=F32)


def _gla_scan(p, glr, w2, gb, *, n_ctx, reverse, L=GLA_CHUNK):
    b, n, _ = p.shape
    nc, ncc = n // L, n_ctx // L
    rsum, amask = _gla_constants(L, reverse)
    chunk = functools.partial(_scan_chunk, n_ctx_chunks=ncc, n_chunks=nc, reverse=reverse)
    hk, hv = GLA_HEADS * GLA_DK, GLA_HEADS * GLA_DV
    cols = lambda width, start: pl.BlockSpec((1, L, width), lambda bi, t: (bi, chunk(t), start // width))
    row = lambda bi, t: (bi, chunk(t), 0)
    const2 = lambda bi, t: (0, 0)
    const3 = lambda bi, t: (0, 0, 0)
    return pl.pallas_call(
        _gla_kernel,
        grid=(b, nc),
        in_specs=[cols(hk, 0), cols(hk, hk), cols(hv, 2 * hk), cols(LANES, 0),
                  pl.BlockSpec(w2.shape, const2), pl.BlockSpec(gb.shape, const2),
                  pl.BlockSpec(rsum.shape, const2), pl.BlockSpec(amask.shape, const3)],
        out_specs=pl.BlockSpec((1, L, hv), row),
        out_shape=jax.ShapeDtypeStruct((b, n, hv), F32),
        scratch_shapes=[pltpu.VMEM((GLA_HEADS, GLA_DV, GLA_DK), F32)],
        compiler_params=pltpu.CompilerParams(
            dimension_semantics=("arbitrary", "arbitrary"), vmem_limit_bytes=VMEM_LIMIT),
        name="gla_scan_bwd" if reverse else "gla_scan_fwd",
    )(p, p, p, glr, w2, gb, jnp.asarray(rsum, BF16), jnp.asarray(amask))


def _log_sigmoid(x):
    return jnp.minimum(x, 0.0) - jnp.log(1.0 + jnp.exp(-jnp.abs(x)))


def _mlstm_kernel(*refs):
    tri_ref, of_ref, ob_ref, c_ref, n_ref, m_ref = refs[10:]

    @pl.when(pl.program_id(1) == 0)
    def _():
        c_ref[...] = jnp.zeros_like(c_ref)
        n_ref[...] = jnp.zeros_like(n_ref)
        m_ref[...] = jnp.zeros_like(m_ref)

    for d, o_ref in enumerate((of_ref, ob_ref)):
        _mlstm_direction(*refs[5 * d:5 * d + 5], tri_ref[d], o_ref, c_ref, n_ref, m_ref,
                         d * ML_HEADS, bool(d))


def _mlstm_direction(q_ref, k_ref, v_ref, gc_ref, gr_ref, tri, o_ref, c_ref, n_ref, m_ref, h0, reverse):
    L = q_ref.shape[1]
    H = ML_HEADS
    gc = gc_ref[0]
    gr = gr_ref[0]
    ic_col, ic_row = gc[:, :H], gr[:H, :]
    b_col = jnp.dot(tri, _log_sigmoid(gc[:, H:]), precision=HI, preferred_element_type=F32)
    b_row = lax.dot_general(_log_sigmoid(gr[H:, :]), tri, NT, precision=HI, preferred_element_type=F32)
    last = 0 if reverse else L - 1
    causal = tri > 0.5

    for h in range(H):
        ck = slice(h * ML_DK, (h + 1) * ML_DK)
        cv = slice(h * ML_DV, (h + 1) * ML_DV)
        q = q_ref[0, :, ck]
        k = k_ref[0, :, ck]
        v = v_ref[0, :, cv].astype(BF16)
        c, n, m = c_ref[h0 + h], n_ref[h0 + h], m_ref[h0 + h]
        bc, br = b_col[:, h:h + 1], b_row[h:h + 1, :]
        icc, icr = ic_col[:, h:h + 1], ic_row[h:h + 1, :]
        b_last = bc[last:last + 1, :]

        a = bc + m
        dmat = jnp.where(causal, bc - br + icr, -jnp.inf)
        m_t = jnp.maximum(a, jnp.max(dmat, axis=-1, keepdims=True))
        w_inter = jnp.exp(a - m_t)
        s = lax.dot_general(q, k, NT, preferred_element_type=F32) * jnp.exp(dmat - m_t)
        num = w_inter * jnp.dot(q, c.astype(BF16), preferred_element_type=F32) + jnp.dot(
            s.astype(BF16), v, preferred_element_type=F32)
        den = w_inter * jnp.sum(q.astype(F32) * n, axis=-1, keepdims=True) + jnp.sum(
            s, axis=-1, keepdims=True)
        o_ref[0, :, cv] = num / jnp.maximum(jnp.abs(den), jnp.exp(-m_t))

        gs_col = b_last - bc + icc
        gs_row = b_last - br + icr
        m_new = jnp.maximum(b_last + m, jnp.max(gs_row, axis=-1, keepdims=True))
        decay = jnp.exp(b_last + m - m_new)
        wk = jnp.exp(gs_col - m_new) * k.astype(F32)
        c_ref[h0 + h] = decay * c + jnp.dot(wk.astype(BF16).T, v, preferred_element_type=F32)
        n_ref[h0 + h] = decay * n + jnp.sum(wk, axis=0, keepdims=True)
        m_ref[h0 + h] = m_new


def _mlstm_scan(qk, p, gates, *, v_col, n_ctx, L=ML_CHUNK):
    b, n, _ = qk.shape
    hk, hv, ng = ML_HEADS * ML_DK, ML_HEADS * ML_DV, 2 * ML_HEADS
    nc, ncc = n // L, n_ctx // L
    idx = np.arange(L)
    lower = idx[None, :] <= idx[:, None]
    tri = jnp.asarray(np.stack([lower, lower.T]).astype(np.float32))
    in_specs, operands, out_specs = [], [], []
    for d in range(2):
        chunk = functools.partial(_scan_chunk, n_ctx_chunks=ncc, n_chunks=nc, reverse=bool(d))
        col = lambda j, chunk=chunk: (lambda bi, t: (bi, chunk(t), j))
        g_dir = gates[..., d * ng:(d + 1) * ng]
        in_specs += [pl.BlockSpec((1, L, hk), col(0)), pl.BlockSpec((1, L, hk), col(1)),
                     pl.BlockSpec((1, L, hv), col(v_col)), pl.BlockSpec((1, L, ng), col(0)),
                     pl.BlockSpec((1, ng, L), lambda bi, t, chunk=chunk: (bi, 0, chunk(t)))]
        operands += [qk, qk, p, g_dir, jnp.swapaxes(g_dir, 1, 2)]
        out_specs.append(pl.BlockSpec((1, L, hv), col(0)))
    return pl.pallas_call(
        _mlstm_kernel,
        grid=(b, nc),
        in_specs=in_specs + [pl.BlockSpec((2, L, L), lambda bi, t: (0, 0, 0))],
        out_specs=out_specs,
        out_shape=[jax.ShapeDtypeStruct((b, n, hv), F32)] * 2,
        scratch_shapes=[pltpu.VMEM((2 * ML_HEADS, ML_DK, ML_DV), F32),
                        pltpu.VMEM((2 * ML_HEADS, 1, ML_DK), F32),
                        pltpu.VMEM((2 * ML_HEADS, 1, 1), F32)],
        compiler_params=pltpu.CompilerParams(
            dimension_semantics=("arbitrary", "arbitrary"), vmem_limit_bytes=VMEM_LIMIT),
        name="mlstm_scan",
    )(*operands, tri)


def _silu(x):
    return x * jax.nn.sigmoid(x)


def _rope_tables(n_lat, n_ctx):
    rows = n_lat // GRID_W
    row = jnp.repeat(jnp.arange(rows, dtype=F32), GRID_W)
    col = jnp.tile(jnp.arange(GRID_W, dtype=F32), rows)
    inv = jnp.power(ROPE_BASE, -jnp.arange(ROPE_PAIRS, dtype=F32) / ROPE_PAIRS)
    ang_r, ang_c = row[:, None] * inv, col[:, None] * inv
    cos = jnp.concatenate([jnp.cos(ang_r)] * 2 + [jnp.cos(ang_c)] * 2, axis=-1)
    sin = jnp.concatenate([-jnp.sin(ang_r), jnp.sin(ang_r), -jnp.sin(ang_c), jnp.sin(ang_c)], axis=-1)
    pad = ((n_ctx, 0), (0, 0))
    cos, sin = jnp.pad(cos, pad, constant_values=1.0), jnp.pad(sin, pad)
    return jnp.tile(cos, (1, 2)), jnp.tile(sin, (1, 2))


def _even_mixer(hn, bsz, w_in, i, gate_w2, gate_b, gla_norm_g, sc_conv_w, seq):
    nq = GLA_HEADS * GLA_DK
    nv = GLA_HEADS * GLA_DV
    gate_col = 2 * nq + 2 * nv
    pa = _matmul(hn, w_in, i, bn=1024, ncols=gate_col, transposed=True)
    glr = _matmul(hn, w_in, i, bn=LANES, col0=gate_col, ncols=LANES, transposed=True)
    pb = _matmul(hn, w_in, i, bn=SC_WIDTH, col0=gate_col, ncols=3 * SC_WIDTH,
                 col_shift=2 * GLA_GATE_RANK, transposed=True)
    pa3 = pa.reshape(bsz, -1, pa.shape[1])
    glr3 = glr.reshape(bsz, -1, LANES)
    outs = []
    for direction in range(2):
        w2 = jnp.zeros((LANES, nq), F32).at[
            direction * GLA_GATE_RANK:(direction + 1) * GLA_GATE_RANK].set(gate_w2[direction])
        outs.append(_gla_scan(pa3, glr3, w2.astype(BF16), gate_b[direction][None, :],
                              n_ctx=seq["n_ctx"], reverse=bool(direction)).reshape(-1, nv))
    gain = jnp.tile(gla_norm_g, GLA_HEADS)[None, :]
    return [_even_finish(outs[0], outs[1], pa, pb, gain, sc_conv_w, **seq)]


def _odd_mixer(hn, bsz, rope, layer, w_in, i, qn_g, kn_g, lam_p, subln_g, ml_conv_w, ml_gate_b,
               ml_norm_g, seq):
    lam_init = 0.8 - 0.6 * math.exp(-0.3 * layer)
    lam = (jnp.exp(jnp.sum(lam_p[0] * lam_p[1])) - jnp.exp(jnp.sum(lam_p[2] * lam_p[3])) + lam_init)
    na = DA_HEADS * 2 * DA_DQK
    nk = ML_HEADS * ML_DK
    nv = ML_HEADS * ML_DV
    n_main = 3 * na + 2 * nk + 2 * nv
    p = _matmul(hn, w_in, i, bn=1024, ncols=n_main, transposed=True)
    mg = _matmul(hn, w_in, i, bn=LANES, col0=n_main, ncols=LANES, transposed=True)[:, :4 * ML_HEADS]
    p3 = p.reshape(bsz, -1, n_main)
    n = p3.shape[1]

    cos, sin = rope
    tile2 = lambda g: jnp.tile(g, 2)[None, :]
    da = _diff_attention(p3, lam.reshape(1).astype(F32), cos, sin, tile2(qn_g), tile2(kn_g),
                         subln_g[None, :], n_ctx=seq["n_ctx"], out_scale=1.0 - lam_init)

    scale = jnp.concatenate([jnp.ones((nk,), F32), jnp.full((nk,), ML_DK ** -0.5, F32)])[None, :]
    hqk = _mlstm_prep(p, ml_conv_w, scale, col=3 * na // (2 * nk), **seq).reshape(bsz, n, 2 * nk)
    gates = (mg + ml_gate_b).reshape(bsz, n, 4 * ML_HEADS)
    outs = [o.reshape(-1, nv) for o in
            _mlstm_scan(hqk, p3, gates, v_col=(3 * na + 2 * nk) // nv, n_ctx=seq["n_ctx"])]
    m = _mlstm_finish(outs[0], outs[1], p, jnp.tile(ml_norm_g, ML_HEADS)[None, :],
                      col=(3 * na + 2 * nk + nv) // nv)
    return [da.reshape(-1, na), m]


def _modulation(c, c_ctx, ada_w, ada_b, layer):
    cc = jnp.concatenate([c, c_ctx[None, :]], axis=0)
    rows = cc.shape[0]
    act = jnp.pad(_silu(cc), ((0, BF16_ROWS - rows), (0, 0))).astype(BF16)
    mod = _matmul(act, ada_w, layer, bn=1024, bm=BF16_ROWS)[:rows] + ada_b[layer]
    return mod.reshape(rows, 6, -1)


def kernel(x, c, ctx, c_ctx, ada_w, ada_b, norm1_g, norm2_g, ev_w_in, ev_w_out, gla_gate_w2, gla_gate_b, gla_norm_g, sc_conv_w, od_w_in, od_w_out, da_qnorm_g, da_knorm_g, da_lambda, da_subln_g, ml_conv_w, ml_gate_b, ml_norm_g, ffn_w_up, ffn_conv_w, ffn_conv_b, ffn_w_down):
    bsz, n_lat, d = x.shape
    n_ctx = ctx.shape[1]
    n = n_ctx + n_lat
    depth = ada_w.shape[0]
    d_ff = ffn_w_down.shape[1]
    assert n % ROW_TILE == 0 and n_ctx <= ROW_CHUNK and d_ff % (4 * LANES) == 0
    rope = _rope_tables(n_lat, n_ctx)
    seq = dict(n_ctx=n_ctx, n_seq=n)

    ev_w_in = jnp.swapaxes(ev_w_in, 1, 2)
    od_w_in = jnp.swapaxes(od_w_in, 1, 2)
    mods = [_modulation(c, c_ctx, ada_w, ada_b, layer) for layer in range(depth)]
    xs = jnp.concatenate([ctx, x], axis=1).reshape(bsz * n, d)
    hn = _first_norm(xs, norm1_g[0][None, :], mods[0], **seq)
    for layer in range(depth):
        i = layer // 2
        if layer % 2 == 0:
            mix = _even_mixer(hn, bsz, ev_w_in, i, gla_gate_w2[i], gla_gate_b[i], gla_norm_g[i],
                              sc_conv_w[i], seq)
            w_out = ev_w_out
        else:
            mix = _odd_mixer(hn, bsz, rope, layer, od_w_in, i, da_qnorm_g[i], da_knorm_g[i],
                             da_lambda[i], da_subln_g[i], ml_conv_w[i], ml_gate_b[i], ml_norm_g[i], seq)
            w_out = od_w_out
        xs, hn = _out_proj(mix, _to_bf16(w_out, i, rows=512), xs, mods[layer], norm2_g[layer][None, :],
                           **seq)
        act = _ffn_up(hn, ffn_w_up, layer, ffn_conv_w[layer], ffn_conv_b[layer][None, :], **seq)
        nxt = min(layer + 1, depth - 1)
        xs, hn = _down_proj(act, _to_bf16(ffn_w_down, layer, rows=d_ff // 8), xs,
                            mods[layer], norm1_g[nxt][None, :], mods[nxt], bk=d_ff // 4, **seq)
    return xs.reshape(bsz, n, d)[:, n_ctx:, :]
```

```python
import functools
import math

import jax
import jax.numpy as jnp
import numpy as np
from jax import lax
from jax.experimental import pallas as pl
from jax.experimental.pallas import tpu as pltpu

GRID_W = 64
EPS = 1e-6
GLA_HEADS, GLA_DK, GLA_DV, GLA_GATE_RANK, GLA_TAU = 4, 128, 256, 16, 16.0
SC_WIDTH = 1024
DA_HEADS, DA_DQK, DA_DV = 8, 64, 128
ROPE_BASE = 10000.0
ROPE_PAIRS = DA_DQK // 4
ML_HEADS, ML_DK, ML_DV = 4, 128, 256

GLA_CHUNK = 128
ML_CHUNK = 256
ATTN_BQ = 256
ATTN_CHUNK = 1024
ATTN_PREP = 544
ATTN_HEADS = 2
ROW_TILE = 1088
ROW_CHUNK = 272
LANES = 128
SUBLANES = 8
BF16_ROWS = 16
ROW_GROUP_UNROLL = 4
LOG2E = 1.4426950408889634
VMEM_LIMIT = 56 * 1024 * 1024

F32 = jnp.float32
BF16 = jnp.bfloat16
HI = lax.Precision.HIGHEST
NT = (((1,), (1,)), ((), ()))


def _mm_kernel(x_ref, w_ref, *rest, row_shift, transposed):
    o_ref, wb_ref = rest[-2:]

    @pl.when(pl.program_id(1) == 0)
    def _():
        if not transposed:
            wb_ref[...] = w_ref[...].astype(BF16)
            return
        bn = wb_ref.shape[1]
        piece = 2 * LANES
        for c0 in range(0, bn, piece):
            c1 = min(c0 + piece, bn)
            if row_shift and c1 == bn:
                rows = jnp.concatenate([w_ref[c0 + row_shift:bn, :], rest[0][...]], axis=0)
            else:
                rows = w_ref[c0 + row_shift:c1 + row_shift, :]
            wb_ref[:, c0:c1] = rows.T.astype(BF16)

    o_ref[...] = jnp.dot(x_ref[...], wb_ref[...], preferred_element_type=F32).astype(o_ref.dtype)


def _matmul(x, w, layer, *, bn, col0=0, ncols=None, col_shift=0, transposed=False, bm=ROW_TILE,
            out_dtype=F32):
    m, k = x.shape
    n_all = w.shape[1] if transposed else w.shape[2]
    ncols = n_all - col0 if ncols is None else ncols
    j0 = col0 // bn
    if transposed:
        w_spec = pl.BlockSpec((None, bn, k), lambda j, i: (layer, j0 + j, 0))
    else:
        w_spec = pl.BlockSpec((None, k, bn), lambda j, i: (layer, 0, j0 + j))
    in_specs = [pl.BlockSpec((bm, k), lambda j, i: (i, 0)), w_spec]
    operands = [x, w]
    if col_shift:
        per = bn // col_shift
        in_specs.append(pl.BlockSpec((None, col_shift, k), lambda j, i: (layer, (j0 + j + 1) * per, 0)))
        operands.append(w)
    return pl.pallas_call(
        functools.partial(_mm_kernel, row_shift=col_shift, transposed=transposed),
        grid=(pl.cdiv(ncols, bn), m // bm),
        in_specs=in_specs,
        out_specs=pl.BlockSpec((bm, bn), lambda j, i: (i, j)),
        out_shape=jax.ShapeDtypeStruct((m, ncols), out_dtype),
        scratch_shapes=[pltpu.VMEM((k, bn), BF16)],
        compiler_params=pltpu.CompilerParams(
            dimension_semantics=("arbitrary", "arbitrary"), vmem_limit_bytes=VMEM_LIMIT),
        name="matmul",
    )(*operands)


def _cast_kernel(x_ref, o_ref):
    o_ref[...] = x_ref[...].astype(o_ref.dtype)


def _to_bf16(w, layer, *, rows):
    _, r, c = w.shape
    return pl.pallas_call(
        _cast_kernel,
        grid=(r // rows,),
        in_specs=[pl.BlockSpec((None, rows, c), lambda i: (layer, i, 0))],
        out_specs=pl.BlockSpec((rows, c), lambda i: (i, 0)),
        out_shape=jax.ShapeDtypeStruct((r, c), BF16),
        compiler_params=pltpu.CompilerParams(vmem_limit_bytes=VMEM_LIMIT),
        name="to_bf16",
    )(w)


def _norm_mod(x, gain_scale, shift):
    return x * lax.rsqrt(jnp.mean(x * x, axis=-1, keepdims=True) + EPS) * gain_scale + shift


def _row_vectors(modb_ref, modc_ref, idx, gain_ref=None):
    def vec(ref):
        row = ref[0, idx:idx + 1, :]
        if gain_ref is not None:
            row = gain_ref[...] * (1.0 + row)
        return jnp.broadcast_to(row, (BF16_ROWS, row.shape[1]))

    lat, ctx = vec(modb_ref), vec(modc_ref)
    return lambda is_ctx: lat if is_ctx is None else jnp.where(is_ctx, ctx, lat)


def _head_rmsnorm(x, width):
    parts = []
    for c0 in range(0, x.shape[1], width):
        xh = x[:, c0:c0 + width]
        parts.append(xh * lax.rsqrt(jnp.mean(xh * xh, axis=-1, keepdims=True) + EPS))
    return parts[0] if len(parts) == 1 else jnp.concatenate(parts, axis=1)


def _sigmoid(x):
    return 1.0 / (1.0 + jnp.exp(-x))


def _mod_row(is_ctx, modb_ref, modc_ref, idx):
    if is_ctx is None:
        return modb_ref[0, idx:idx + 1, :]
    return jnp.where(is_ctx, modc_ref[0, idx:idx + 1, :], modb_ref[0, idx:idx + 1, :])


def _chunk_rows(r0, rc, tiles_per_batch, bm):
    rows = r0 + lax.broadcasted_iota(jnp.int32, (rc, 1), 0)
    return rows, (pl.program_id(0) % tiles_per_batch) * bm + rows


def _chunk_is_ctx(r0, pos, n_ctx):
    return pos < n_ctx if r0 < n_ctx else None


def _edge_rows(r0, rc, bm, tiles_per_batch, n_ctx, n_seq, gate_ref, hprev_ref, hnext_ref):
    hr = hprev_ref.shape[0]
    pos0 = (pl.program_id(0) % tiles_per_batch) * bm
    if r0 == 0:
        before = hprev_ref[...].astype(F32)[hr - 1:hr, :]
        before = jnp.where((pos0 == 0) | (pos0 == n_ctx), 0.0, before)
    else:
        before = gate_ref[r0 - hr:r0, :].astype(F32)[hr - 1:hr, :]
    if r0 + rc == bm:
        after = hnext_ref[...].astype(F32)[0:1, :]
        after = jnp.where((pos0 + bm == n_ctx) | (pos0 + bm == n_seq), 0.0, after)
    else:
        after = gate_ref[r0 + rc:r0 + rc + hr, :].astype(F32)[0:1, :]
    return before, after


def _splits_rows(r0, rc, bm, tiles_per_batch, n_ctx):
    return any(r0 <= n_ctx - k * bm <= r0 + rc for k in range(tiles_per_batch))


def _dwconv3(s_ref, g, before, after, cw_ref, pos, n_ctx):
    rc = g.shape[0]
    s_ref[SUBLANES:SUBLANES + rc, :] = g
    s_ref[SUBLANES - 1:SUBLANES, :] = before
    s_ref[SUBLANES + rc:SUBLANES + rc + 1, :] = after
    prev = s_ref[SUBLANES - 1:SUBLANES - 1 + rc, :]
    nxt = s_ref[SUBLANES + 1:SUBLANES + 1 + rc, :]
    if pos is not None:
        prev = jnp.where(pos == n_ctx, 0.0, prev)
        nxt = jnp.where(pos == n_ctx - 1, 0.0, nxt)
    return prev * cw_ref[0:1, :] + g * cw_ref[1:2, :] + nxt * cw_ref[2:3, :]


def _mod_specs(n_batch, tiles_per_batch, d):
    return [pl.BlockSpec((1, 6, d), lambda i, *_: (i // tiles_per_batch, 0, 0)),
            pl.BlockSpec((1, 6, d), lambda i, *_: (n_batch, 0, 0))]


def _halo_specs(bm, width, col, n_rows, hr):
    hb = bm // hr
    last = n_rows // hr - 1
    return [pl.BlockSpec((hr, width), lambda i, *k: (jnp.maximum(i * hb - 1, 0), col(*k))),
            pl.BlockSpec((hr, width), lambda i, *k: (jnp.minimum((i + 1) * hb, last), col(*k)))]


def _norm_kernel(x_ref, gain_ref, modb_ref, modc_ref, o_ref, *, n_ctx, tiles_per_batch):
    gain_scale = _row_vectors(modb_ref, modc_ref, 1, gain_ref)
    shift = _row_vectors(modb_ref, modc_ref, 0)

    def rows(rs, is_ctx):
        o_ref[rs, :] = _norm_mod(x_ref[rs, :], gain_scale(is_ctx), shift(is_ctx)).astype(BF16)

    _for_row_groups(x_ref.shape[0], n_ctx, tiles_per_batch, rows)


def _first_norm(x, gain, mod, *, n_ctx, n_seq, bm=ROW_TILE):
    m, d = x.shape
    tpb = n_seq // bm
    return pl.pallas_call(
        functools.partial(_norm_kernel, n_ctx=n_ctx, tiles_per_batch=tpb),
        grid=(m // bm,),
        in_specs=[pl.BlockSpec((bm, d), lambda i: (i, 0)), pl.BlockSpec((1, d), lambda i: (0, 0))]
        + _mod_specs(mod.shape[0] - 1, tpb, d),
        out_specs=pl.BlockSpec((bm, d), lambda i: (i, 0)),
        out_shape=jax.ShapeDtypeStruct((m, d), BF16),
        compiler_params=pltpu.CompilerParams(vmem_limit_bytes=VMEM_LIMIT),
        name="first_norm",
    )(x, gain, mod, mod)


def _residual_norm(xo_ref, hn_ref, x_ref, modb_ref, modc_ref, gain_ref, nmodb_ref, nmodc_ref, *,
                   n_ctx, tiles_per_batch, gate_idx, shift_idx):
    gate = _row_vectors(modb_ref, modc_ref, gate_idx)
    gain_scale = _row_vectors(nmodb_ref, nmodc_ref, shift_idx + 1, gain_ref)
    shift = _row_vectors(nmodb_ref, nmodc_ref, shift_idx)

    def rows(rs, is_ctx):
        x_new = x_ref[rs, :] + gate(is_ctx) * xo_ref[rs, :]
        xo_ref[rs, :] = x_new
        hn_ref[rs, :] = _norm_mod(x_new, gain_scale(is_ctx), shift(is_ctx)).astype(BF16)

    _for_row_groups(x_ref.shape[0], n_ctx, tiles_per_batch, rows)


def _for_row_groups(bm, n_ctx, tiles_per_batch, fn):
    rg = BF16_ROWS
    pos0 = (pl.program_id(0) % tiles_per_batch) * bm

    def group(with_ctx):
        def body(g, carry):
            r0 = pl.multiple_of(g * rg, rg)
            is_ctx = (pos0 + r0 + lax.broadcasted_iota(jnp.int32, (rg, 1), 0) < n_ctx) if with_ctx else None
            fn(pl.ds(r0, rg), is_ctx)
            return carry
        return body

    n_lead = -(-min(n_ctx, bm) // rg)
    lax.fori_loop(0, n_lead, group(True), 0, unroll=ROW_GROUP_UNROLL)
    lax.fori_loop(n_lead, bm // rg, group(False), 0, unroll=ROW_GROUP_UNROLL)


def _out_kernel(*refs, n_in, n_ctx, tiles_per_batch):
    a_refs = refs[:n_in]
    w_ref, x_ref, modb_ref, modc_ref, gain_ref, xo_ref, hn_ref = refs[n_in:]
    bm = x_ref.shape[0]
    k0 = 0
    for n_done, a_ref in enumerate(a_refs):
        part = jnp.dot(a_ref[...], w_ref[k0:k0 + a_ref.shape[1], :], preferred_element_type=F32)
        xo_ref[...] = part if n_done == 0 else xo_ref[...] + part
        k0 += a_ref.shape[1]
    _residual_norm(xo_ref, hn_ref, x_ref, modb_ref, modc_ref, gain_ref, modb_ref, modc_ref,
                   n_ctx=n_ctx, tiles_per_batch=tiles_per_batch, gate_idx=2, shift_idx=3)


def _out_proj(acts, w, x, mod, gain, *, n_ctx, n_seq, bm=ROW_TILE // 2):
    m, d = x.shape
    tpb = n_seq // bm
    rows = lambda width: pl.BlockSpec((bm, width), lambda i: (i, 0))
    return pl.pallas_call(
        functools.partial(_out_kernel, n_in=len(acts), n_ctx=n_ctx, tiles_per_batch=tpb),
        grid=(m // bm,),
        in_specs=[rows(a.shape[1]) for a in acts]
        + [pl.BlockSpec(w.shape, lambda i: (0, 0), pipeline_mode=pl.Buffered(1)), rows(d)]
        + _mod_specs(mod.shape[0] - 1, tpb, d) + [pl.BlockSpec((1, d), lambda i: (0, 0))],
        out_specs=[rows(d), rows(d)],
        out_shape=[jax.ShapeDtypeStruct((m, d), F32), jax.ShapeDtypeStruct((m, d), BF16)],
        compiler_params=pltpu.CompilerParams(vmem_limit_bytes=VMEM_LIMIT),
        name="out_proj",
    )(*acts, w, x, mod, mod, gain)


def _up_kernel(x_ref, xp_ref, xn_ref, wg_ref, wv_ref, cw_ref, cb_ref, o_ref, wgb_ref, wvb_ref, xe_ref,
               s_ref, v_ref, *, n_ctx, n_seq, tiles_per_batch):
    i = pl.program_id(1)
    bm, rc, hr = x_ref.shape[0], ROW_CHUNK, xp_ref.shape[0]

    @pl.when(i == 0)
    def _():
        wgb_ref[...] = wg_ref[...].astype(BF16)
        wvb_ref[...] = wv_ref[...].astype(BF16)

    xe_ref[0:hr, :] = xp_ref[...]
    xe_ref[hr:hr + bm, :] = x_ref[...]
    xe_ref[hr + bm:, :] = xn_ref[...]
    s_ref[...] = jnp.dot(xe_ref[...], wgb_ref[...], preferred_element_type=F32)
    v_ref[...] = jnp.dot(x_ref[...], wvb_ref[...], preferred_element_type=F32)
    pos0 = (i % tiles_per_batch) * bm
    s_ref[hr - 1:hr, :] = jnp.where((pos0 == 0) | (pos0 == n_ctx), 0.0, s_ref[hr - 1:hr, :])
    s_ref[hr + bm:hr + bm + 1, :] = jnp.where(
        (pos0 + bm == n_ctx) | (pos0 + bm == n_seq), 0.0, s_ref[hr + bm:hr + bm + 1, :])
    for r0 in range(0, bm, rc):
        rs = slice(r0, r0 + rc)
        prev = s_ref[hr - 1 + r0:hr - 1 + r0 + rc, :]
        nxt = s_ref[hr + 1 + r0:hr + 1 + r0 + rc, :]
        if _splits_rows(r0, rc, bm, tiles_per_batch, n_ctx):
            pos = pos0 + r0 + lax.broadcasted_iota(jnp.int32, (rc, 1), 0)
            prev = jnp.where(pos == n_ctx, 0.0, prev)
            nxt = jnp.where(pos == n_ctx - 1, 0.0, nxt)
        z = (prev * cw_ref[0:1, :] + s_ref[hr + r0:hr + r0 + rc, :] * cw_ref[1:2, :]
             + nxt * cw_ref[2:3, :] + cb_ref[...])
        o_ref[rs, :] = (z * _sigmoid(z) * v_ref[rs, :]).astype(BF16)


def _ffn_up(x, w, layer, conv_w, conv_b, *, n_ctx, n_seq, bn=512, bm=ROW_TILE):
    m, k = x.shape
    f = w.shape[2] // 2
    nj = f // bn
    tpb = n_seq // bm
    hb = bm // BF16_ROWS
    last = m // BF16_ROWS - 1
    return pl.pallas_call(
        functools.partial(_up_kernel, n_ctx=n_ctx, n_seq=n_seq, tiles_per_batch=tpb),
        grid=(nj, m // bm),
        in_specs=[pl.BlockSpec((bm, k), lambda j, i: (i, 0)),
                  pl.BlockSpec((BF16_ROWS, k), lambda j, i: (jnp.maximum(i * hb - 1, 0), 0)),
                  pl.BlockSpec((BF16_ROWS, k), lambda j, i: (jnp.minimum((i + 1) * hb, last), 0)),
                  pl.BlockSpec((None, k, bn), lambda j, i: (layer, 0, j)),
                  pl.BlockSpec((None, k, bn), lambda j, i: (layer, 0, j + nj)),
                  pl.BlockSpec((3, bn), lambda j, i: (0, j)), pl.BlockSpec((1, bn), lambda j, i: (0, j))],
        out_specs=pl.BlockSpec((bm, bn), lambda j, i: (i, j)),
        out_shape=jax.ShapeDtypeStruct((m, f), BF16),
        scratch_shapes=[pltpu.VMEM((k, bn), BF16), pltpu.VMEM((k, bn), BF16),
                        pltpu.VMEM((bm + 2 * BF16_ROWS, k), BF16),
                        pltpu.VMEM((bm + 2 * BF16_ROWS, bn), F32), pltpu.VMEM((bm, bn), F32)],
        compiler_params=pltpu.CompilerParams(
            dimension_semantics=("arbitrary", "arbitrary"), vmem_limit_bytes=VMEM_LIMIT),
        name="ffn_up",
    )(x, x, x, w, w, conv_w, conv_b)


def _down_kernel(a_ref, w_ref, x_ref, modb_ref, modc_ref, gain_ref, nmodb_ref, nmodc_ref, xo_ref, hn_ref, *,
                 n_ctx, tiles_per_batch):
    part = jnp.dot(a_ref[...], w_ref[...], preferred_element_type=F32)

    @pl.when(pl.program_id(1) == 0)
    def _():
        xo_ref[...] = part

    @pl.when(pl.program_id(1) > 0)
    def _():
        xo_ref[...] += part

    @pl.when(pl.program_id(1) == pl.num_programs(1) - 1)
    def _():
        _residual_norm(xo_ref, hn_ref, x_ref, modb_ref, modc_ref, gain_ref, nmodb_ref, nmodc_ref,
                       n_ctx=n_ctx, tiles_per_batch=tiles_per_batch, gate_idx=5, shift_idx=0)


def _down_proj(a, w, x, mod, gain, next_mod, *, n_ctx, n_seq, bm=ROW_TILE // 2, bk=1408):
    m, f = a.shape
    d = w.shape[1]
    tpb = n_seq // bm
    row = lambda i, k: (i, 0)
    return pl.pallas_call(
        functools.partial(_down_kernel, n_ctx=n_ctx, tiles_per_batch=tpb),
        grid=(m // bm, f // bk),
        in_specs=[pl.BlockSpec((bm, bk), lambda i, k: (i, k)), pl.BlockSpec((bk, d), lambda i, k: (k, 0)),
                  pl.BlockSpec((bm, d), row)]
        + _mod_specs(mod.shape[0] - 1, tpb, d) + [pl.BlockSpec((1, d), lambda i, k: (0, 0))]
        + _mod_specs(mod.shape[0] - 1, tpb, d),
        out_specs=[pl.BlockSpec((bm, d), row), pl.BlockSpec((bm, d), row)],
        out_shape=[jax.ShapeDtypeStruct((m, d), F32), jax.ShapeDtypeStruct((m, d), BF16)],
        compiler_params=pltpu.CompilerParams(
            dimension_semantics=("arbitrary", "arbitrary"), vmem_limit_bytes=VMEM_LIMIT),
        name="down_proj",
    )(a, w, x, mod, mod, gain, next_mod, next_mod)


def _even_finish_kernel(of_ref, ob_ref, r_ref, sx_ref, sb_ref, sg_ref, sxp_ref, sxn_ref, sgp_ref, sgn_ref,
                        gain_ref, cw_ref, o_ref, s_ref, *, n_ctx, n_seq, tiles_per_batch):
    bm = of_ref.shape[0]
    nv = of_ref.shape[1]
    r = r_ref[...]
    o_ref[:, :nv] = (_head_rmsnorm(of_ref[...] + ob_ref[...], GLA_DV) * gain_ref[...]
                     * (r * _sigmoid(r))).astype(BF16)

    _, pos = _chunk_rows(0, bm, tiles_per_batch, bm)
    pos0 = (pl.program_id(0) % tiles_per_batch) * bm
    hr = sxp_ref.shape[0]
    before = (sxp_ref[...] * sgp_ref[...])[hr - 1:hr, :]
    before = jnp.where((pos0 == 0) | (pos0 == n_ctx), 0.0, before)
    after = (sxn_ref[...] * sgn_ref[...])[0:1, :]
    after = jnp.where((pos0 + bm == n_ctx) | (pos0 + bm == n_seq), 0.0, after)
    split = pos if _splits_rows(0, bm, bm, tiles_per_batch, n_ctx) else None
    conv = _dwconv3(s_ref, sg_ref[...] * sx_ref[...], before, after, cw_ref, split, n_ctx)
    o_ref[:, nv:] = (sb_ref[...] * conv).astype(BF16)


def _even_finish(o_f, o_b, pa, pb, gain, conv_w, *, n_ctx, n_seq, bm=ROW_CHUNK):
    m, nv = o_f.shape
    sc = conv_w.shape[1]
    tpb = n_seq // bm
    col = lambda j: (lambda i: (i, j))
    const = lambda i: (0, 0)
    return pl.pallas_call(
        functools.partial(_even_finish_kernel, n_ctx=n_ctx, n_seq=n_seq, tiles_per_batch=tpb),
        grid=(m // bm,),
        in_specs=[pl.BlockSpec((bm, nv), col(0)), pl.BlockSpec((bm, nv), col(0)),
                  pl.BlockSpec((bm, nv), col(2)),
                  pl.BlockSpec((bm, sc), col(0)), pl.BlockSpec((bm, sc), col(1)), pl.BlockSpec((bm, sc), col(2))]
        + _halo_specs(bm, sc, lambda: 0, m, SUBLANES) + _halo_specs(bm, sc, lambda: 2, m, SUBLANES)
        + [pl.BlockSpec((1, nv), const), pl.BlockSpec((3, sc), const)],
        out_specs=pl.BlockSpec((bm, nv + sc), col(0)),
        out_shape=jax.ShapeDtypeStruct((m, nv + sc), BF16),
        scratch_shapes=[pltpu.VMEM((bm + 2 * SUBLANES, sc), F32)],
        compiler_params=pltpu.CompilerParams(vmem_limit_bytes=VMEM_LIMIT),
        name="even_finish",
    )(o_f, o_b, pa, pb, pb, pb, pb, pb, pb, pb, gain, conv_w)


def _mlstm_prep_kernel(x_ref, xp_ref, xn_ref, cw_ref, scale_ref, o_ref, s_ref, *, n_ctx, n_seq,
                       tiles_per_batch):
    bm = x_ref.shape[0]
    _, pos = _chunk_rows(0, bm, tiles_per_batch, bm)
    before, after = _edge_rows(0, bm, bm, tiles_per_batch, n_ctx, n_seq, x_ref, xp_ref, xn_ref)
    split = pos if _splits_rows(0, bm, bm, tiles_per_batch, n_ctx) else None
    z = _dwconv3(s_ref, x_ref[...], before, after, cw_ref, split, n_ctx)
    o_ref[...] = (z * _sigmoid(z) * scale_ref[...]).astype(BF16)


def _mlstm_prep(p, conv_w, scale, *, col, n_ctx, n_seq, bm=ROW_CHUNK):
    m = p.shape[0]
    width = conv_w.shape[1]
    tpb = n_seq // bm
    const = lambda i: (0, 0)
    return pl.pallas_call(
        functools.partial(_mlstm_prep_kernel, n_ctx=n_ctx, n_seq=n_seq, tiles_per_batch=tpb),
        grid=(m // bm,),
        in_specs=[pl.BlockSpec((bm, width), lambda i: (i, col))]
        + _halo_specs(bm, width, lambda: col, m, SUBLANES)
        + [pl.BlockSpec((3, width), const), pl.BlockSpec((1, width), const)],
        out_specs=pl.BlockSpec((bm, width), lambda i: (i, 0)),
        out_shape=jax.ShapeDtypeStruct((m, width), BF16),
        scratch_shapes=[pltpu.VMEM((bm + 2 * SUBLANES, width), F32)],
        compiler_params=pltpu.CompilerParams(vmem_limit_bytes=VMEM_LIMIT),
        name="mlstm_prep",
    )(p, p, p, conv_w, scale)


def _mlstm_finish_kernel(hf_ref, hb_ref, mo_ref, gain_ref, o_ref):
    o_ref[...] = (_head_rmsnorm(hf_ref[...] + hb_ref[...], ML_DV) * gain_ref[...]
                  * _sigmoid(mo_ref[...])).astype(BF16)


def _mlstm_finish(h_f, h_b, p, gain, *, col, bm=ROW_TILE // 2):
    m, nv = h_f.shape
    return pl.pallas_call(
        _mlstm_finish_kernel,
        grid=(m // bm,),
        in_specs=[pl.BlockSpec((bm, nv), lambda i: (i, 0)), pl.BlockSpec((bm, nv), lambda i: (i, 0)),
                  pl.BlockSpec((bm, nv), lambda i: (i, col)), pl.BlockSpec((1, nv), lambda i: (0, 0))],
        out_specs=pl.BlockSpec((bm, nv), lambda i: (i, 0)),
        out_shape=jax.ShapeDtypeStruct((m, nv), BF16),
        compiler_params=pltpu.CompilerParams(vmem_limit_bytes=VMEM_LIMIT),
        name="mlstm_finish",
    )(h_f, h_b, p, gain)


def _qk_prep(x, cos, sin, gain):
    lane = lax.broadcasted_iota(jnp.int32, x.shape, 1)
    lo = lane < DA_DQK
    x2 = x * x
    ss_lo = jnp.sum(jnp.where(lo, x2, 0.0), axis=-1, keepdims=True)
    ss_hi = jnp.sum(jnp.where(lo, 0.0, x2), axis=-1, keepdims=True)
    y = x * lax.rsqrt(jnp.where(lo, ss_lo, ss_hi) * (1.0 / DA_DQK) + EPS) * gain
    swapped = jnp.where((lane & ROPE_PAIRS) == 0, pltpu.roll(y, LANES - ROPE_PAIRS, 1),
                        pltpu.roll(y, ROPE_PAIRS, 1))
    return y * cos + swapped * sin


def _attn_kernel(lam_ref, q_ref, k_ref, v_ref, cos_ref, sin_ref, qg_ref, kg_ref, og_ref, o_ref,
                 kb_ref, vb_ref, *, n_ctx, chunk, prep_rows, out_scale):
    qi = pl.program_id(2)
    lam = lam_ref[0]
    n = k_ref.shape[1]
    bq = q_ref.shape[1]
    heads = q_ref.shape[2] // LANES

    @pl.when(qi == 0)
    def _():
        for r0 in range(0, n, prep_rows):
            rs = slice(r0, r0 + prep_rows)
            for h in range(heads):
                hs = slice(h * LANES, (h + 1) * LANES)
                kb_ref[h, rs, :] = _qk_prep(k_ref[0, rs, hs], cos_ref[rs, :], sin_ref[rs, :],
                                            kg_ref[...]).astype(BF16)
                vb_ref[h, rs, :LANES] = v_ref[0, rs, hs].astype(BF16)
                vb_ref[h, rs, LANES:] = jnp.ones((prep_rows, LANES), BF16)

    def attend(nk):
        rows = pl.ds(pl.multiple_of(qi * bq, bq), bq)
        cos, sin = cos_ref[rows, :], sin_ref[rows, :]
        lane = lax.broadcasted_iota(jnp.int32, (bq, LANES), 1)
        q2, m, acc = [], [], []
        for h in range(heads):
            q = _qk_prep(q_ref[0, :, h * LANES:(h + 1) * LANES], cos, sin, qg_ref[...])
            q = (q * (DA_DQK ** -0.5 * LOG2E)).astype(BF16)
            zero = jnp.zeros_like(q)
            q2.append(jnp.concatenate([jnp.where(lane < DA_DQK, q, zero),
                                       jnp.where(lane >= DA_DQK, q, zero)], axis=0))
            m.append(jnp.full((2 * bq, 1), -jnp.inf, F32))
            acc.append(jnp.zeros((2 * bq, 2 * LANES), F32))
        for c0 in range(0, nk, chunk):
            c1 = min(c0 + chunk, nk)
            for h in range(heads):
                s = lax.dot_general(q2[h], kb_ref[h, c0:c1, :], NT, preferred_element_type=F32)
                m_new = jnp.maximum(m[h], jnp.max(s, axis=-1, keepdims=True))
                p = jnp.exp2(s - m_new)
                acc[h] = jnp.exp2(m[h] - m_new) * acc[h] + jnp.dot(
                    p.astype(BF16), vb_ref[h, c0:c1, :], preferred_element_type=F32)
                m[h] = m_new
        for h in range(heads):
            o = acc[h][:, :LANES] / acc[h][:, LANES:]
            o = o[:bq] - lam * o[bq:]
            o = o * lax.rsqrt(jnp.mean(o * o, axis=-1, keepdims=True) + EPS) * og_ref[...] * out_scale
            o_ref[0, :, h * LANES:(h + 1) * LANES] = o.astype(BF16)

    n_ctx_blocks = n_ctx // bq

    @pl.when(qi < n_ctx_blocks)
    def _():
        attend(n_ctx)

    @pl.when(qi >= n_ctx_blocks)
    def _():
        attend(n)


def _diff_attention(p, lam, cos, sin, q_gain, k_gain, out_gain, *, n_ctx, out_scale, bq=ATTN_BQ,
                    chunk=ATTN_CHUNK, prep_rows=ATTN_PREP, heads=ATTN_HEADS):
    b, n, _ = p.shape
    hg = DA_HEADS // heads
    hd = DA_HEADS * LANES
    hw = heads * LANES
    const = lambda bi, hi, qi: (0, 0)
    return pl.pallas_call(
        functools.partial(_attn_kernel, n_ctx=n_ctx, chunk=chunk, prep_rows=prep_rows, out_scale=out_scale),
        grid=(b, hg, n // bq),
        in_specs=[pl.BlockSpec(memory_space=pltpu.SMEM),
                  pl.BlockSpec((1, bq, hw), lambda bi, hi, qi: (bi, qi, hi)),
                  pl.BlockSpec((1, n, hw), lambda bi, hi, qi: (bi, 0, hg + hi)),
                  pl.BlockSpec((1, n, hw), lambda bi, hi, qi: (bi, 0, 2 * hg + hi)),
                  pl.BlockSpec((n, LANES), const), pl.BlockSpec((n, LANES), const),
                  pl.BlockSpec((1, LANES), const), pl.BlockSpec((1, LANES), const),
                  pl.BlockSpec((1, LANES), const)],
        out_specs=pl.BlockSpec((1, bq, hw), lambda bi, hi, qi: (bi, qi, hi)),
        out_shape=jax.ShapeDtypeStruct((b, n, hd), BF16),
        scratch_shapes=[pltpu.VMEM((heads, n, LANES), BF16), pltpu.VMEM((heads, n, 2 * LANES), BF16)],
        compiler_params=pltpu.CompilerParams(
            dimension_semantics=("arbitrary", "arbitrary", "arbitrary"), vmem_limit_bytes=VMEM_LIMIT),
        name="diff_attention",
    )(lam, p, p, p, cos, sin, q_gain, k_gain, out_gain)


def _scan_chunk(t, n_ctx_chunks, n_chunks, reverse):
    if not reverse:
        return t
    return jnp.where(t < n_ctx_chunks, n_ctx_chunks - 1 - t, n_chunks - 1 - (t - n_ctx_chunks))


def _gla_constants(L, reverse):
    nlev = int(math.log2(L))
    idx = np.arange(L)
    i, t = idx[:, None], idx[None, :]
    if reverse:
        i, t = L - 1 - i, L - 1 - t
    rs = [(t <= i), (t > i)]
    am = [(i == t)]
    for lev in range(nlev):
        m = L >> (lev + 1)
        blk_i, blk_t = i // (2 * m), t // (2 * m)
        mid = blk_i * 2 * m + m
        q_role = i >= mid
        rs.append(np.where(q_role, (t >= mid) & (t <= i), (t > i) & (t < mid)) & (blk_i == blk_t))
        am.append((blk_i == blk_t) & q_role & (t < mid))
    return (np.stack(rs).astype(np.float32).reshape((nlev + 2) * L, L),
            np.stack(am).astype(np.float32))


def _gla_kernel(q_ref, k_ref, v_ref, glr_ref, w2_ref, gb_ref, rsum_ref, amask_ref, o_ref, st_ref):
    L = q_ref.shape[1]
    nlev = amask_ref.shape[0] - 1

    @pl.when(pl.program_id(1) == 0)
    def _():
        st_ref[...] = jnp.zeros_like(st_ref)

    z = jnp.dot(glr_ref[0].astype(BF16), w2_ref[...], preferred_element_type=F32) + gb_ref[...]
    g = (jnp.minimum(z, 0.0) - jnp.log(1.0 + jnp.exp(-jnp.abs(z)))) * (1.0 / GLA_TAU)
    g1 = g.astype(BF16)
    g2 = (g - g1.astype(F32)).astype(BF16)
    g3 = (g - g1.astype(F32) - g2.astype(F32)).astype(BF16)
    rsum = rsum_ref[...]
    e_all = (jnp.dot(rsum, g1, preferred_element_type=F32) + jnp.dot(rsum, g2, preferred_element_type=F32)
             + jnp.dot(rsum, g3, preferred_element_type=F32))
    b_tot = jnp.sum(g, axis=0, keepdims=True)

    for h in range(GLA_HEADS):
        ck = slice(h * GLA_DK, (h + 1) * GLA_DK)
        cv = slice(h * GLA_DV, (h + 1) * GLA_DV)
        q = q_ref[0, :, ck] * (GLA_DK ** -0.5)
        k = k_ref[0, :, ck]
        v = v_ref[0, :, cv].astype(BF16)
        st = st_ref[h]
        a = amask_ref[0] * lax.dot_general(q.astype(BF16), k.astype(BF16), NT, preferred_element_type=F32)
        for lev in range(nlev):
            e = jnp.exp(e_all[(2 + lev) * L:(3 + lev) * L, ck])
            a = a + amask_ref[1 + lev] * lax.dot_general(
                (q * e).astype(BF16), (k * e).astype(BF16), NT, preferred_element_type=F32)
        qe = (q * jnp.exp(e_all[0:L, ck])).astype(BF16)
        o = lax.dot_general(qe, st.astype(BF16), NT, preferred_element_type=F32)
        o = o + jnp.dot(a.astype(BF16), v, preferred_element_type=F32)
        o_ref[0, :, cv] = o
        kd = (k * jnp.exp(e_all[L:2 * L, ck])).astype(BF16)
        st_ref[h] = jnp.exp(b_tot[:, ck]) * st + jnp.dot(
            v.T, kd, preferred_element_type=F32)


def _gla_scan(p, glr, w2, gb, *, n_ctx, reverse, L=GLA_CHUNK):
    b, n, _ = p.shape
    nc, ncc = n // L, n_ctx // L
    rsum, amask = _gla_constants(L, reverse)
    chunk = functools.partial(_scan_chunk, n_ctx_chunks=ncc, n_chunks=nc, reverse=reverse)
    hk, hv = GLA_HEADS * GLA_DK, GLA_HEADS * GLA_DV
    cols = lambda width, start: pl.BlockSpec((1, L, width), lambda bi, t: (bi, chunk(t), start // width))
    row = lambda bi, t: (bi, chunk(t), 0)
    const2 = lambda bi, t: (0, 0)
    const3 = lambda bi, t: (0, 0, 0)
    return pl.pallas_call(
        _gla_kernel,
        grid=(b, nc),
        in_specs=[cols(hk, 0), cols(hk, hk), cols(hv, 2 * hk), cols(LANES, 0),
                  pl.BlockSpec(w2.shape, const2), pl.BlockSpec(gb.shape, const2),
                  pl.BlockSpec(rsum.shape, const2), pl.BlockSpec(amask.shape, const3)],
        out_specs=pl.BlockSpec((1, L, hv), row),
        out_shape=jax.ShapeDtypeStruct((b, n, hv), F32),
        scratch_shapes=[pltpu.VMEM((GLA_HEADS, GLA_DV, GLA_DK), F32)],
        compiler_params=pltpu.CompilerParams(
            dimension_semantics=("arbitrary", "arbitrary"), vmem_limit_bytes=VMEM_LIMIT),
        name="gla_scan_bwd" if reverse else "gla_scan_fwd",
    )(p, p, p, glr, w2, gb, jnp.asarray(rsum, BF16), jnp.asarray(amask))


def _log_sigmoid(x):
    return jnp.minimum(x, 0.0) - jnp.log(1.0 + jnp.exp(-jnp.abs(x)))


def _mlstm_kernel(*refs):
    tri_ref, of_ref, ob_ref, c_ref, n_ref, m_ref = refs[10:]

    @pl.when(pl.program_id(1) == 0)
    def _():
        c_ref[...] = jnp.zeros_like(c_ref)
        n_ref[...] = jnp.zeros_like(n_ref)
        m_ref[...] = jnp.zeros_like(m_ref)

    for d, o_ref in enumerate((of_ref, ob_ref)):
        _mlstm_direction(*refs[5 * d:5 * d + 5], tri_ref[d], o_ref, c_ref, n_ref, m_ref,
                         d * ML_HEADS, bool(d))


def _mlstm_direction(q_ref, k_ref, v_ref, gc_ref, gr_ref, tri, o_ref, c_ref, n_ref, m_ref, h0, reverse):
    L = q_ref.shape[1]
    H = ML_HEADS
    gc = gc_ref[0]
    gr = gr_ref[0]
    ic_col, ic_row = gc[:, :H], gr[:H, :]
    b_col = jnp.dot(tri, _log_sigmoid(gc[:, H:]), precision=HI, preferred_element_type=F32)
    b_row = lax.dot_general(_log_sigmoid(gr[H:, :]), tri, NT, precision=HI, preferred_element_type=F32)
    last = 0 if reverse else L - 1
    causal = tri > 0.5

    for h in range(H):
        ck = slice(h * ML_DK, (h + 1) * ML_DK)
        cv = slice(h * ML_DV, (h + 1) * ML_DV)
        q = q_ref[0, :, ck]
        k = k_ref[0, :, ck]
        v = v_ref[0, :, cv].astype(BF16)
        c, n, m = c_ref[h0 + h], n_ref[h0 + h], m_ref[h0 + h]
        bc, br = b_col[:, h:h + 1], b_row[h:h + 1, :]
        icc, icr = ic_col[:, h:h + 1], ic_row[h:h + 1, :]
        b_last = bc[last:last + 1, :]

        a = bc + m
        dmat = jnp.where(causal, bc - br + icr, -jnp.inf)
        m_t = jnp.maximum(a, jnp.max(dmat, axis=-1, keepdims=True))
        w_inter = jnp.exp(a - m_t)
        s = lax.dot_general(q, k, NT, preferred_element_type=F32) * jnp.exp(dmat - m_t)
        num = w_inter * jnp.dot(q, c.astype(BF16), preferred_element_type=F32) + jnp.dot(
            s.astype(BF16), v, preferred_element_type=F32)
        den = w_inter * jnp.sum(q.astype(F32) * n, axis=-1, keepdims=True) + jnp.sum(
            s, axis=-1, keepdims=True)
        o_ref[0, :, cv] = num / jnp.maximum(jnp.abs(den), jnp.exp(-m_t))

        gs_col = b_last - bc + icc
        gs_row = b_last - br + icr
        m_new = jnp.maximum(b_last + m, jnp.max(gs_row, axis=-1, keepdims=True))
        decay = jnp.exp(b_last + m - m_new)
        wk = jnp.exp(gs_col - m_new) * k.astype(F32)
        c_ref[h0 + h] = decay * c + jnp.dot(wk.astype(BF16).T, v, preferred_element_type=F32)
        n_ref[h0 + h] = decay * n + jnp.sum(wk, axis=0, keepdims=True)
        m_ref[h0 + h] = m_new


def _mlstm_scan(qk, p, gates, *, v_col, n_ctx, L=ML_CHUNK):
    b, n, _ = qk.shape
    hk, hv, ng = ML_HEADS * ML_DK, ML_HEADS * ML_DV, 2 * ML_HEADS
    nc, ncc = n // L, n_ctx // L
    idx = np.arange(L)
    lower = idx[None, :] <= idx[:, None]
    tri = jnp.asarray(np.stack([lower, lower.T]).astype(np.float32))
    in_specs, operands, out_specs = [], [], []
    for d in range(2):
        chunk = functools.partial(_scan_chunk, n_ctx_chunks=ncc, n_chunks=nc, reverse=bool(d))
        col = lambda j, chunk=chunk: (lambda bi, t: (bi, chunk(t), j))
        g_dir = gates[..., d * ng:(d + 1) * ng]
        in_specs += [pl.BlockSpec((1, L, hk), col(0)), pl.BlockSpec((1, L, hk), col(1)),
                     pl.BlockSpec((1, L, hv), col(v_col)), pl.BlockSpec((1, L, ng), col(0)),
                     pl.BlockSpec((1, ng, L), lambda bi, t, chunk=chunk: (bi, 0, chunk(t)))]
        operands += [qk, qk, p, g_dir, jnp.swapaxes(g_dir, 1, 2)]
        out_specs.append(pl.BlockSpec((1, L, hv), col(0)))
    return pl.pallas_call(
        _mlstm_kernel,
        grid=(b, nc),
        in_specs=in_specs + [pl.BlockSpec((2, L, L), lambda bi, t: (0, 0, 0))],
        out_specs=out_specs,
        out_shape=[jax.ShapeDtypeStruct((b, n, hv), F32)] * 2,
        scratch_shapes=[pltpu.VMEM((2 * ML_HEADS, ML_DK, ML_DV), F32),
                        pltpu.VMEM((2 * ML_HEADS, 1, ML_DK), F32),
                        pltpu.VMEM((2 * ML_HEADS, 1, 1), F32)],
        compiler_params=pltpu.CompilerParams(
            dimension_semantics=("arbitrary", "arbitrary"), vmem_limit_bytes=VMEM_LIMIT),
        name="mlstm_scan",
    )(*operands, tri)


def _silu(x):
    return x * jax.nn.sigmoid(x)


def _rope_tables(n_lat, n_ctx):
    rows = n_lat // GRID_W
    row = jnp.repeat(jnp.arange(rows, dtype=F32), GRID_W)
    col = jnp.tile(jnp.arange(GRID_W, dtype=F32), rows)
    inv = jnp.power(ROPE_BASE, -jnp.arange(ROPE_PAIRS, dtype=F32) / ROPE_PAIRS)
    ang_r, ang_c = row[:, None] * inv, col[:, None] * inv
    cos = jnp.concatenate([jnp.cos(ang_r)] * 2 + [jnp.cos(ang_c)] * 2, axis=-1)
    sin = jnp.concatenate([-jnp.sin(ang_r), jnp.sin(ang_r), -jnp.sin(ang_c), jnp.sin(ang_c)], axis=-1)
    pad = ((n_ctx, 0), (0, 0))
    cos, sin = jnp.pad(cos, pad, constant_values=1.0), jnp.pad(sin, pad)
    return jnp.tile(cos, (1, 2)), jnp.tile(sin, (1, 2))


def _even_mixer(hn, bsz, w_in, i, gate_w2, gate_b, gla_norm_g, sc_conv_w, seq):
    nq = GLA_HEADS * GLA_DK
    nv = GLA_HEADS * GLA_DV
    gate_col = 2 * nq + 2 * nv
    pa = _matmul(hn, w_in, i, bn=1024, ncols=gate_col, transposed=True)
    glr = _matmul(hn, w_in, i, bn=LANES, col0=gate_col, ncols=LANES, transposed=True)
    pb = _matmul(hn, w_in, i, bn=SC_WIDTH, col0=gate_col, ncols=3 * SC_WIDTH,
                 col_shift=2 * GLA_GATE_RANK, transposed=True)
    pa3 = pa.reshape(bsz, -1, pa.shape[1])
    glr3 = glr.reshape(bsz, -1, LANES)
    outs = []
    for direction in range(2):
        w2 = jnp.zeros((LANES, nq), F32).at[
            direction * GLA_GATE_RANK:(direction + 1) * GLA_GATE_RANK].set(gate_w2[direction])
        outs.append(_gla_scan(pa3, glr3, w2.astype(BF16), gate_b[direction][None, :],
                              n_ctx=seq["n_ctx"], reverse=bool(direction)).reshape(-1, nv))
    gain = jnp.tile(gla_norm_g, GLA_HEADS)[None, :]
    return [_even_finish(outs[0], outs[1], pa, pb, gain, sc_conv_w, **seq)]


def _odd_mixer(hn, bsz, rope, layer, w_in, i, qn_g, kn_g, lam_p, subln_g, ml_conv_w, ml_gate_b,
               ml_norm_g, seq):
    lam_init = 0.8 - 0.6 * math.exp(-0.3 * layer)
    lam = (jnp.exp(jnp.sum(lam_p[0] * lam_p[1])) - jnp.exp(jnp.sum(lam_p[2] * lam_p[3])) + lam_init)
    na = DA_HEADS * 2 * DA_DQK
    nk = ML_HEADS * ML_DK
    nv = ML_HEADS * ML_DV
    n_main = 3 * na + 2 * nk + 2 * nv
    p = _matmul(hn, w_in, i, bn=1024, ncols=n_main, transposed=True)
    mg = _matmul(hn, w_in, i, bn=LANES, col0=n_main, ncols=LANES, transposed=True)[:, :4 * ML_HEADS]
    p3 = p.reshape(bsz, -1, n_main)
    n = p3.shape[1]

    cos, sin = rope
    tile2 = lambda g: jnp.tile(g, 2)[None, :]
    da = _diff_attention(p3, lam.reshape(1).astype(F32), cos, sin, tile2(qn_g), tile2(kn_g),
                         subln_g[None, :], n_ctx=seq["n_ctx"], out_scale=1.0 - lam_init)

    scale = jnp.concatenate([jnp.ones((nk,), F32), jnp.full((nk,), ML_DK ** -0.5, F32)])[None, :]
    hqk = _mlstm_prep(p, ml_conv_w, scale, col=3 * na // (2 * nk), **seq).reshape(bsz, n, 2 * nk)
    gates = (mg + ml_gate_b).reshape(bsz, n, 4 * ML_HEADS)
    outs = [o.reshape(-1, nv) for o in
            _mlstm_scan(hqk, p3, gates, v_col=(3 * na + 2 * nk) // nv, n_ctx=seq["n_ctx"])]
    m = _mlstm_finish(outs[0], outs[1], p, jnp.tile(ml_norm_g, ML_HEADS)[None, :],
                      col=(3 * na + 2 * nk + nv) // nv)
    return [da.reshape(-1, na), m]


def _modulation(c, c_ctx, ada_w, ada_b, layer):
    cc = jnp.concatenate([c, c_ctx[None, :]], axis=0)
    rows = cc.shape[0]
    act = jnp.pad(_silu(cc), ((0, BF16_ROWS - rows), (0, 0))).astype(BF16)
    mod = _matmul(act, ada_w, layer, bn=1024, bm=BF16_ROWS)[:rows] + ada_b[layer]
    return mod.reshape(rows, 6, -1)


def kernel(x, c, ctx, c_ctx, ada_w, ada_b, norm1_g, norm2_g, ev_w_in, ev_w_out, gla_gate_w2, gla_gate_b, gla_norm_g, sc_conv_w, od_w_in, od_w_out, da_qnorm_g, da_knorm_g, da_lambda, da_subln_g, ml_conv_w, ml_gate_b, ml_norm_g, ffn_w_up, ffn_conv_w, ffn_conv_b, ffn_w_down):
    bsz, n_lat, d = x.shape
    n_ctx = ctx.shape[1]
    n = n_ctx + n_lat
    depth = ada_w.shape[0]
    d_ff = ffn_w_down.shape[1]
    assert n % ROW_TILE == 0 and n_ctx <= ROW_CHUNK and d_ff % (4 * LANES) == 0
    rope = _rope_tables(n_lat, n_ctx)
    seq = dict(n_ctx=n_ctx, n_seq=n)

    ev_w_in = jnp.swapaxes(ev_w_in, 1, 2)
    od_w_in = jnp.swapaxes(od_w_in, 1, 2)
    mods = [_modulation(c, c_ctx, ada_w, ada_b, layer) for layer in range(depth)]
    xs = jnp.concatenate([ctx, x], axis=1).reshape(bsz * n, d)
    hn = _first_norm(xs, norm1_g[0][None, :], mods[0], **seq)
    for layer in range(depth):
        i = layer // 2
        if layer % 2 == 0:
            mix = _even_mixer(hn, bsz, ev_w_in, i, gla_gate_w2[i], gla_gate_b[i], gla_norm_g[i],
                              sc_conv_w[i], seq)
            w_out = ev_w_out
        else:
            mix = _odd_mixer(hn, bsz, rope, layer, od_w_in, i, da_qnorm_g[i], da_knorm_g[i],
                             da_lambda[i], da_subln_g[i], ml_conv_w[i], ml_gate_b[i], ml_norm_g[i], seq)
            w_out = od_w_out
        xs, hn = _out_proj(mix, _to_bf16(w_out, i, rows=512), xs, mods[layer], norm2_g[layer][None, :],
                           **seq)
        act = _ffn_up(hn, ffn_w_up, layer, ffn_conv_w[layer], ffn_conv_b[layer][None, :], **seq)
        nxt = min(layer + 1, depth - 1)
        xs, hn = _down_proj(act, _to_bf16(ffn_w_down, layer, rows=d_ff // 8), xs,
                            mods[layer], norm1_g[nxt][None, :], mods[nxt], bk=d_ff // 4, **seq)
    return xs.reshape(bsz, n, d)[:, n_ctx:, :]
```

```python
import functools
import math

import jax
import jax.numpy as jnp
import numpy as np
from jax import lax
from jax.experimental import pallas as pl
from jax.experimental.pallas import tpu as pltpu

GRID_W = 64
EPS = 1e-6
GLA_HEADS, GLA_DK, GLA_DV, GLA_GATE_RANK, GLA_TAU = 4, 128, 256, 16, 16.0
SC_WIDTH = 1024
DA_HEADS, DA_DQK, DA_DV = 8, 64, 128
ROPE_BASE = 10000.0
ROPE_PAIRS = DA_DQK // 4
ML_HEADS, ML_DK, ML_DV = 4, 128, 256

GLA_CHUNK = 128
ML_CHUNK = 256
ATTN_BQ = 256
ATTN_CHUNK = 1024
ATTN_PREP = 544
ATTN_HEADS = 2
ROW_TILE = 1088
ROW_CHUNK = 272
LANES = 128
SUBLANES = 8
BF16_ROWS = 16
ROW_GROUP_UNROLL = 4
LOG2E = 1.4426950408889634
VMEM_LIMIT = 56 * 1024 * 1024

F32 = jnp.float32
BF16 = jnp.bfloat16
HI = lax.Precision.HIGHEST
NT = (((1,), (1,)), ((), ()))


def _mm_kernel(x_ref, w_ref, *rest, row_shift, transposed):
    o_ref, wb_ref = rest[-2:]

    @pl.when(pl.program_id(1) == 0)
    def _():
        if not transposed:
            wb_ref[...] = w_ref[...].astype(BF16)
            return
        bn = wb_ref.shape[1]
        piece = 2 * LANES
        for c0 in range(0, bn, piece):
            c1 = min(c0 + piece, bn)
            if row_shift and c1 == bn:
                rows = jnp.concatenate([w_ref[c0 + row_shift:bn, :], rest[0][...]], axis=0)
            else:
                rows = w_ref[c0 + row_shift:c1 + row_shift, :]
            wb_ref[:, c0:c1] = rows.T.astype(BF16)

    o_ref[...] = jnp.dot(x_ref[...], wb_ref[...], preferred_element_type=F32).astype(o_ref.dtype)


def _matmul(x, w, layer, *, bn, col0=0, ncols=None, col_shift=0, transposed=False, bm=ROW_TILE,
            out_dtype=F32):
    m, k = x.shape
    n_all = w.shape[1] if transposed else w.shape[2]
    ncols = n_all - col0 if ncols is None else ncols
    j0 = col0 // bn
    if transposed:
        w_spec = pl.BlockSpec((None, bn, k), lambda j, i: (layer, j0 + j, 0))
    else:
        w_spec = pl.BlockSpec((None, k, bn), lambda j, i: (layer, 0, j0 + j))
    in_specs = [pl.BlockSpec((bm, k), lambda j, i: (i, 0)), w_spec]
    operands = [x, w]
    if col_shift:
        per = bn // col_shift
        in_specs.append(pl.BlockSpec((None, col_shift, k), lambda j, i: (layer, (j0 + j + 1) * per, 0)))
        operands.append(w)
    return pl.pallas_call(
        functools.partial(_mm_kernel, row_shift=col_shift, transposed=transposed),
        grid=(pl.cdiv(ncols, bn), m // bm),
        in_specs=in_specs,
        out_specs=pl.BlockSpec((bm, bn), lambda j, i: (i, j)),
        out_shape=jax.ShapeDtypeStruct((m, ncols), out_dtype),
        scratch_shapes=[pltpu.VMEM((k, bn), BF16)],
        compiler_params=pltpu.CompilerParams(
            dimension_semantics=("arbitrary", "arbitrary"), vmem_limit_bytes=VMEM_LIMIT),
        name="matmul",
    )(*operands)


def _cast_kernel(x_ref, o_ref):
    o_ref[...] = x_ref[...].astype(o_ref.dtype)


def _to_bf16(w, layer, *, rows):
    _, r, c = w.shape
    return pl.pallas_call(
        _cast_kernel,
        grid=(r // rows,),
        in_specs=[pl.BlockSpec((None, rows, c), lambda i: (layer, i, 0))],
        out_specs=pl.BlockSpec((rows, c), lambda i: (i, 0)),
        out_shape=jax.ShapeDtypeStruct((r, c), BF16),
        compiler_params=pltpu.CompilerParams(vmem_limit_bytes=VMEM_LIMIT),
        name="to_bf16",
    )(w)


def _norm_mod(x, gain_scale, shift):
    return x * lax.rsqrt(jnp.mean(x * x, axis=-1, keepdims=True) + EPS) * gain_scale + shift


def _row_vectors(modb_ref, modc_ref, idx, gain_ref=None):
    def vec(ref):
        row = ref[0, idx:idx + 1, :]
        if gain_ref is not None:
            row = gain_ref[...] * (1.0 + row)
        return jnp.broadcast_to(row, (BF16_ROWS, row.shape[1]))

    lat, ctx = vec(modb_ref), vec(modc_ref)
    return lambda is_ctx: lat if is_ctx is None else jnp.where(is_ctx, ctx, lat)


def _head_rmsnorm(x, width):
    parts = []
    for c0 in range(0, x.shape[1], width):
        xh = x[:, c0:c0 + width]
        parts.append(xh * lax.rsqrt(jnp.mean(xh * xh, axis=-1, keepdims=True) + EPS))
    return parts[0] if len(parts) == 1 else jnp.concatenate(parts, axis=1)


def _sigmoid(x):
    return 1.0 / (1.0 + jnp.exp(-x))


def _mod_row(is_ctx, modb_ref, modc_ref, idx):
    if is_ctx is None:
        return modb_ref[0, idx:idx + 1, :]
    return jnp.where(is_ctx, modc_ref[0, idx:idx + 1, :], modb_ref[0, idx:idx + 1, :])


def _chunk_rows(r0, rc, tiles_per_batch, bm):
    rows = r0 + lax.broadcasted_iota(jnp.int32, (rc, 1), 0)
    return rows, (pl.program_id(0) % tiles_per_batch) * bm + rows


def _chunk_is_ctx(r0, pos, n_ctx):
    return pos < n_ctx if r0 < n_ctx else None


def _edge_rows(r0, rc, bm, tiles_per_batch, n_ctx, n_seq, gate_ref, hprev_ref, hnext_ref):
    hr = hprev_ref.shape[0]
    pos0 = (pl.program_id(0) % tiles_per_batch) * bm
    if r0 == 0:
        before = hprev_ref[...].astype(F32)[hr - 1:hr, :]
        before = jnp.where((pos0 == 0) | (pos0 == n_ctx), 0.0, before)
    else:
        before = gate_ref[r0 - hr:r0, :].astype(F32)[hr - 1:hr, :]
    if r0 + rc == bm:
        after = hnext_ref[...].astype(F32)[0:1, :]
        after = jnp.where((pos0 + bm == n_ctx) | (pos0 + bm == n_seq), 0.0, after)
    else:
        after = gate_ref[r0 + rc:r0 + rc + hr, :].astype(F32)[0:1, :]
    return before, after


def _splits_rows(r0, rc, bm, tiles_per_batch, n_ctx):
    return any(r0 <= n_ctx - k * bm <= r0 + rc for k in range(tiles_per_batch))


def _dwconv3(s_ref, g, before, after, cw_ref, pos, n_ctx):
    rc = g.shape[0]
    s_ref[SUBLANES:SUBLANES + rc, :] = g
    s_ref[SUBLANES - 1:SUBLANES, :] = before
    s_ref[SUBLANES + rc:SUBLANES + rc + 1, :] = after
    prev = s_ref[SUBLANES - 1:SUBLANES - 1 + rc, :]
    nxt = s_ref[SUBLANES + 1:SUBLANES + 1 + rc, :]
    if pos is not None:
        prev = jnp.where(pos == n_ctx, 0.0, prev)
        nxt = jnp.where(pos == n_ctx - 1, 0.0, nxt)
    return prev * cw_ref[0:1, :] + g * cw_ref[1:2, :] + nxt * cw_ref[2:3, :]


def _mod_specs(n_batch, tiles_per_batch, d):
    return [pl.BlockSpec((1, 6, d), lambda i, *_: (i // tiles_per_batch, 0, 0)),
            pl.BlockSpec((1, 6, d), lambda i, *_: (n_batch, 0, 0))]


def _halo_specs(bm, width, col, n_rows, hr):
    hb = bm // hr
    last = n_rows // hr - 1
    return [pl.BlockSpec((hr, width), lambda i, *k: (jnp.maximum(i * hb - 1, 0), col(*k))),
            pl.BlockSpec((hr, width), lambda i, *k: (jnp.minimum((i + 1) * hb, last), col(*k)))]


def _norm_kernel(x_ref, gain_ref, modb_ref, modc_ref, o_ref, *, n_ctx, tiles_per_batch):
    gain_scale = _row_vectors(modb_ref, modc_ref, 1, gain_ref)
    shift = _row_vectors(modb_ref, modc_ref, 0)

    def rows(rs, is_ctx):
        o_ref[rs, :] = _norm_mod(x_ref[rs, :], gain_scale(is_ctx), shift(is_ctx)).astype(BF16)

    _for_row_groups(x_ref.shape[0], n_ctx, tiles_per_batch, rows)


def _first_norm(x, gain, mod, *, n_ctx, n_seq, bm=ROW_TILE):
    m, d = x.shape
    tpb = n_seq // bm
    return pl.pallas_call(
        functools.partial(_norm_kernel, n_ctx=n_ctx, tiles_per_batch=tpb),
        grid=(m // bm,),
        in_specs=[pl.BlockSpec((bm, d), lambda i: (i, 0)), pl.BlockSpec((1, d), lambda i: (0, 0))]
        + _mod_specs(mod.shape[0] - 1, tpb, d),
        out_specs=pl.BlockSpec((bm, d), lambda i: (i, 0)),
        out_shape=jax.ShapeDtypeStruct((m, d), BF16),
        compiler_params=pltpu.CompilerParams(vmem_limit_bytes=VMEM_LIMIT),
        name="first_norm",
    )(x, gain, mod, mod)


def _residual_norm(xo_ref, hn_ref, x_ref, modb_ref, modc_ref, gain_ref, nmodb_ref, nmodc_ref, *,
                   n_ctx, tiles_per_batch, gate_idx, shift_idx, inline=False):
    gate = _row_vectors(modb_ref, modc_ref, gate_idx)
    gain_scale = _row_vectors(nmodb_ref, nmodc_ref, shift_idx + 1, gain_ref)
    shift = _row_vectors(nmodb_ref, nmodc_ref, shift_idx)

    def rows(rs, is_ctx):
        x_new = x_ref[rs, :] + gate(is_ctx) * xo_ref[rs, :]
        xo_ref[rs, :] = x_new
        hn_ref[rs, :] = _norm_mod(x_new, gain_scale(is_ctx), shift(is_ctx)).astype(BF16)

    _for_row_groups(x_ref.shape[0], n_ctx, tiles_per_batch, rows, inline)


def _for_row_groups(bm, n_ctx, tiles_per_batch, fn, inline=False):
    rg = BF16_ROWS
    pos0 = (pl.program_id(0) % tiles_per_batch) * bm
    n_lead = -(-min(n_ctx, bm) // rg)
    if inline:
        for g in range(bm // rg):
            is_ctx = (pos0 + g * rg + lax.broadcasted_iota(jnp.int32, (rg, 1), 0) < n_ctx) if g < n_lead else None
            fn(slice(g * rg, (g + 1) * rg), is_ctx)
        return

    def group(with_ctx):
        def body(g, carry):
            r0 = pl.multiple_of(g * rg, rg)
            is_ctx = (pos0 + r0 + lax.broadcasted_iota(jnp.int32, (rg, 1), 0) < n_ctx) if with_ctx else None
            fn(pl.ds(r0, rg), is_ctx)
            return carry
        return body

    lax.fori_loop(0, n_lead, group(True), 0, unroll=ROW_GROUP_UNROLL)
    lax.fori_loop(n_lead, bm // rg, group(False), 0, unroll=ROW_GROUP_UNROLL)


def _out_kernel(*refs, n_in, n_ctx, tiles_per_batch):
    a_refs = refs[:n_in]
    w_ref, x_ref, modb_ref, modc_ref, gain_ref, xo_ref, hn_ref = refs[n_in:]
    bm = x_ref.shape[0]
    k0 = 0
    for n_done, a_ref in enumerate(a_refs):
        part = jnp.dot(a_ref[...], w_ref[k0:k0 + a_ref.shape[1], :], preferred_element_type=F32)
        xo_ref[...] = part if n_done == 0 else xo_ref[...] + part
        k0 += a_ref.shape[1]
    _residual_norm(xo_ref, hn_ref, x_ref, modb_ref, modc_ref, gain_ref, modb_ref, modc_ref,
                   n_ctx=n_ctx, tiles_per_batch=tiles_per_batch, gate_idx=2, shift_idx=3, inline=True)


def _out_proj(acts, w, x, mod, gain, *, n_ctx, n_seq, bm=ROW_TILE // 2):
    m, d = x.shape
    tpb = n_seq // bm
    rows = lambda width: pl.BlockSpec((bm, width), lambda i: (i, 0))
    return pl.pallas_call(
        functools.partial(_out_kernel, n_in=len(acts), n_ctx=n_ctx, tiles_per_batch=tpb),
        grid=(m // bm,),
        in_specs=[rows(a.shape[1]) for a in acts]
        + [pl.BlockSpec(w.shape, lambda i: (0, 0), pipeline_mode=pl.Buffered(1)), rows(d)]
        + _mod_specs(mod.shape[0] - 1, tpb, d) + [pl.BlockSpec((1, d), lambda i: (0, 0))],
        out_specs=[rows(d), rows(d)],
        out_shape=[jax.ShapeDtypeStruct((m, d), F32), jax.ShapeDtypeStruct((m, d), BF16)],
        compiler_params=pltpu.CompilerParams(vmem_limit_bytes=VMEM_LIMIT),
        name="out_proj",
    )(*acts, w, x, mod, mod, gain)


def _up_kernel(x_ref, xp_ref, xn_ref, wg_ref, wv_ref, cw_ref, cb_ref, o_ref, wgb_ref, wvb_ref, xe_ref,
               s_ref, v_ref, *, n_ctx, n_seq, tiles_per_batch):
    i = pl.program_id(1)
    bm, rc, hr = x_ref.shape[0], ROW_CHUNK, xp_ref.shape[0]

    @pl.when(i == 0)
    def _():
        wgb_ref[...] = wg_ref[...].astype(BF16)
        wvb_ref[...] = wv_ref[...].astype(BF16)

    xe_ref[0:hr, :] = xp_ref[...]
    xe_ref[hr:hr + bm, :] = x_ref[...]
    xe_ref[hr + bm:, :] = xn_ref[...]
    s_ref[...] = jnp.dot(xe_ref[...], wgb_ref[...], preferred_element_type=F32)
    v_ref[...] = jnp.dot(x_ref[...], wvb_ref[...], preferred_element_type=F32)
    pos0 = (i % tiles_per_batch) * bm
    s_ref[hr - 1:hr, :] = jnp.where((pos0 == 0) | (pos0 == n_ctx), 0.0, s_ref[hr - 1:hr, :])
    s_ref[hr + bm:hr + bm + 1, :] = jnp.where(
        (pos0 + bm == n_ctx) | (pos0 + bm == n_seq), 0.0, s_ref[hr + bm:hr + bm + 1, :])
    for r0 in range(0, bm, rc):
        rs = slice(r0, r0 + rc)
        prev = s_ref[hr - 1 + r0:hr - 1 + r0 + rc, :]
        nxt = s_ref[hr + 1 + r0:hr + 1 + r0 + rc, :]
        if _splits_rows(r0, rc, bm, tiles_per_batch, n_ctx):
            pos = pos0 + r0 + lax.broadcasted_iota(jnp.int32, (rc, 1), 0)
            prev = jnp.where(pos == n_ctx, 0.0, prev)
            nxt = jnp.where(pos == n_ctx - 1, 0.0, nxt)
        z = (prev * cw_ref[0:1, :] + s_ref[hr + r0:hr + r0 + rc, :] * cw_ref[1:2, :]
             + nxt * cw_ref[2:3, :] + cb_ref[...])
        o_ref[rs, :] = (z * _sigmoid(z) * v_ref[rs, :]).astype(BF16)


def _ffn_up(x, w, layer, conv_w, conv_b, *, n_ctx, n_seq, bn=512, bm=ROW_TILE):
    m, k = x.shape
    f = w.shape[2] // 2
    nj = f // bn
    tpb = n_seq // bm
    hb = bm // BF16_ROWS
    last = m // BF16_ROWS - 1
    return pl.pallas_call(
        functools.partial(_up_kernel, n_ctx=n_ctx, n_seq=n_seq, tiles_per_batch=tpb),
        grid=(nj, m // bm),
        in_specs=[pl.BlockSpec((bm, k), lambda j, i: (i, 0)),
                  pl.BlockSpec((BF16_ROWS, k), lambda j, i: (jnp.maximum(i * hb - 1, 0), 0)),
                  pl.BlockSpec((BF16_ROWS, k), lambda j, i: (jnp.minimum((i + 1) * hb, last), 0)),
                  pl.BlockSpec((None, k, bn), lambda j, i: (layer, 0, j)),
                  pl.BlockSpec((None, k, bn), lambda j, i: (layer, 0, j + nj)),
                  pl.BlockSpec((3, bn), lambda j, i: (0, j)), pl.BlockSpec((1, bn), lambda j, i: (0, j))],
        out_specs=pl.BlockSpec((bm, bn), lambda j, i: (i, j)),
        out_shape=jax.ShapeDtypeStruct((m, f), BF16),
        scratch_shapes=[pltpu.VMEM((k, bn), BF16), pltpu.VMEM((k, bn), BF16),
                        pltpu.VMEM((bm + 2 * BF16_ROWS, k), BF16),
                        pltpu.VMEM((bm + 2 * BF16_ROWS, bn), F32), pltpu.VMEM((bm, bn), F32)],
        compiler_params=pltpu.CompilerParams(
            dimension_semantics=("arbitrary", "arbitrary"), vmem_limit_bytes=VMEM_LIMIT),
        name="ffn_up",
    )(x, x, x, w, w, conv_w, conv_b)


def _down_kernel(a_ref, w_ref, x_ref, modb_ref, modc_ref, gain_ref, nmodb_ref, nmodc_ref, xo_ref, hn_ref, *,
                 n_ctx, tiles_per_batch):
    part = jnp.dot(a_ref[...], w_ref[...], preferred_element_type=F32)

    @pl.when(pl.program_id(1) == 0)
    def _():
        xo_ref[...] = part

    @pl.when(pl.program_id(1) > 0)
    def _():
        xo_ref[...] += part

    @pl.when(pl.program_id(1) == pl.num_programs(1) - 1)
    def _():
        _residual_norm(xo_ref, hn_ref, x_ref, modb_ref, modc_ref, gain_ref, nmodb_ref, nmodc_ref,
                       n_ctx=n_ctx, tiles_per_batch=tiles_per_batch, gate_idx=5, shift_idx=0)


def _down_proj(a, w, x, mod, gain, next_mod, *, n_ctx, n_seq, bm=ROW_TILE // 2, bk=1408):
    m, f = a.shape
    d = w.shape[1]
    tpb = n_seq // bm
    row = lambda i, k: (i, 0)
    return pl.pallas_call(
        functools.partial(_down_kernel, n_ctx=n_ctx, tiles_per_batch=tpb),
        grid=(m // bm, f // bk),
        in_specs=[pl.BlockSpec((bm, bk), lambda i, k: (i, k)), pl.BlockSpec((bk, d), lambda i, k: (k, 0)),
                  pl.BlockSpec((bm, d), row)]
        + _mod_specs(mod.shape[0] - 1, tpb, d) + [pl.BlockSpec((1, d), lambda i, k: (0, 0))]
        + _mod_specs(mod.shape[0] - 1, tpb, d),
        out_specs=[pl.BlockSpec((bm, d), row), pl.BlockSpec((bm, d), row)],
        out_shape=[jax.ShapeDtypeStruct((m, d), F32), jax.ShapeDtypeStruct((m, d), BF16)],
        compiler_params=pltpu.CompilerParams(
            dimension_semantics=("arbitrary", "arbitrary"), vmem_limit_bytes=VMEM_LIMIT),
        name="down_proj",
    )(a, w, x, mod, mod, gain, next_mod, next_mod)


def _even_finish_kernel(of_ref, ob_ref, r_ref, sx_ref, sb_ref, sg_ref, sxp_ref, sxn_ref, sgp_ref, sgn_ref,
                        gain_ref, cw_ref, o_ref, s_ref, *, n_ctx, n_seq, tiles_per_batch):
    bm = of_ref.shape[0]
    nv = of_ref.shape[1]
    r = r_ref[...]
    o_ref[:, :nv] = (_head_rmsnorm(of_ref[...] + ob_ref[...], GLA_DV) * gain_ref[...]
                     * (r * _sigmoid(r))).astype(BF16)

    _, pos = _chunk_rows(0, bm, tiles_per_batch, bm)
    pos0 = (pl.program_id(0) % tiles_per_batch) * bm
    hr = sxp_ref.shape[0]
    before = (sxp_ref[...] * sgp_ref[...])[hr - 1:hr, :]
    before = jnp.where((pos0 == 0) | (pos0 == n_ctx), 0.0, before)
    after = (sxn_ref[...] * sgn_ref[...])[0:1, :]
    after = jnp.where((pos0 + bm == n_ctx) | (pos0 + bm == n_seq), 0.0, after)
    split = pos if _splits_rows(0, bm, bm, tiles_per_batch, n_ctx) else None
    conv = _dwconv3(s_ref, sg_ref[...] * sx_ref[...], before, after, cw_ref, split, n_ctx)
    o_ref[:, nv:] = (sb_ref[...] * conv).astype(BF16)


def _even_finish(o_f, o_b, pa, pb, gain, conv_w, *, n_ctx, n_seq, bm=ROW_CHUNK):
    m, nv = o_f.shape
    sc = conv_w.shape[1]
    tpb = n_seq // bm
    col = lambda j: (lambda i: (i, j))
    const = lambda i: (0, 0)
    return pl.pallas_call(
        functools.partial(_even_finish_kernel, n_ctx=n_ctx, n_seq=n_seq, tiles_per_batch=tpb),
        grid=(m // bm,),
        in_specs=[pl.BlockSpec((bm, nv), col(0)), pl.BlockSpec((bm, nv), col(0)),
                  pl.BlockSpec((bm, nv), col(2)),
                  pl.BlockSpec((bm, sc), col(0)), pl.BlockSpec((bm, sc), col(1)), pl.BlockSpec((bm, sc), col(2))]
        + _halo_specs(bm, sc, lambda: 0, m, SUBLANES) + _halo_specs(bm, sc, lambda: 2, m, SUBLANES)
        + [pl.BlockSpec((1, nv), const), pl.BlockSpec((3, sc), const)],
        out_specs=pl.BlockSpec((bm, nv + sc), col(0)),
        out_shape=jax.ShapeDtypeStruct((m, nv + sc), BF16),
        scratch_shapes=[pltpu.VMEM((bm + 2 * SUBLANES, sc), F32)],
        compiler_params=pltpu.CompilerParams(vmem_limit_bytes=VMEM_LIMIT),
        name="even_finish",
    )(o_f, o_b, pa, pb, pb, pb, pb, pb, pb, pb, gain, conv_w)


def _mlstm_prep_kernel(x_ref, xp_ref, xn_ref, cw_ref, scale_ref, o_ref, s_ref, *, n_ctx, n_seq,
                       tiles_per_batch):
    bm = x_ref.shape[0]
    _, pos = _chunk_rows(0, bm, tiles_per_batch, bm)
    before, after = _edge_rows(0, bm, bm, tiles_per_batch, n_ctx, n_seq, x_ref, xp_ref, xn_ref)
    split = pos if _splits_rows(0, bm, bm, tiles_per_batch, n_ctx) else None
    z = _dwconv3(s_ref, x_ref[...], before, after, cw_ref, split, n_ctx)
    o_ref[...] = (z * _sigmoid(z) * scale_ref[...]).astype(BF16)


def _mlstm_prep(p, conv_w, scale, *, col, n_ctx, n_seq, bm=ROW_CHUNK):
    m = p.shape[0]
    width = conv_w.shape[1]
    tpb = n_seq // bm
    const = lambda i: (0, 0)
    return pl.pallas_call(
        functools.partial(_mlstm_prep_kernel, n_ctx=n_ctx, n_seq=n_seq, tiles_per_batch=tpb),
        grid=(m // bm,),
        in_specs=[pl.BlockSpec((bm, width), lambda i: (i, col))]
        + _halo_specs(bm, width, lambda: col, m, SUBLANES)
        + [pl.BlockSpec((3, width), const), pl.BlockSpec((1, width), const)],
        out_specs=pl.BlockSpec((bm, width), lambda i: (i, 0)),
        out_shape=jax.ShapeDtypeStruct((m, width), BF16),
        scratch_shapes=[pltpu.VMEM((bm + 2 * SUBLANES, width), F32)],
        compiler_params=pltpu.CompilerParams(vmem_limit_bytes=VMEM_LIMIT),
        name="mlstm_prep",
    )(p, p, p, conv_w, scale)


def _mlstm_finish_kernel(hf_ref, hb_ref, mo_ref, gain_ref, o_ref):
    o_ref[...] = (_head_rmsnorm(hf_ref[...] + hb_ref[...], ML_DV) * gain_ref[...]
                  * _sigmoid(mo_ref[...])).astype(BF16)


def _mlstm_finish(h_f, h_b, p, gain, *, col, bm=ROW_TILE // 2):
    m, nv = h_f.shape
    return pl.pallas_call(
        _mlstm_finish_kernel,
        grid=(m // bm,),
        in_specs=[pl.BlockSpec((bm, nv), lambda i: (i, 0)), pl.BlockSpec((bm, nv), lambda i: (i, 0)),
                  pl.BlockSpec((bm, nv), lambda i: (i, col)), pl.BlockSpec((1, nv), lambda i: (0, 0))],
        out_specs=pl.BlockSpec((bm, nv), lambda i: (i, 0)),
        out_shape=jax.ShapeDtypeStruct((m, nv), BF16),
        compiler_params=pltpu.CompilerParams(vmem_limit_bytes=VMEM_LIMIT),
        name="mlstm_finish",
    )(h_f, h_b, p, gain)


def _qk_prep(x, cos, sin, gain):
    lane = lax.broadcasted_iota(jnp.int32, x.shape, 1)
    lo = lane < DA_DQK
    x2 = x * x
    ss_lo = jnp.sum(jnp.where(lo, x2, 0.0), axis=-1, keepdims=True)
    ss_hi = jnp.sum(jnp.where(lo, 0.0, x2), axis=-1, keepdims=True)
    y = x * lax.rsqrt(jnp.where(lo, ss_lo, ss_hi) * (1.0 / DA_DQK) + EPS) * gain
    swapped = jnp.where((lane & ROPE_PAIRS) == 0, pltpu.roll(y, LANES - ROPE_PAIRS, 1),
                        pltpu.roll(y, ROPE_PAIRS, 1))
    return y * cos + swapped * sin


def _attn_kernel(lam_ref, q_ref, k_ref, v_ref, cos_ref, sin_ref, qg_ref, kg_ref, og_ref, o_ref,
                 kb_ref, vb_ref, *, n_ctx, chunk, prep_rows, out_scale):
    qi = pl.program_id(2)
    lam = lam_ref[0]
    n = k_ref.shape[1]
    bq = q_ref.shape[1]
    heads = q_ref.shape[2] // LANES

    @pl.when(qi == 0)
    def _():
        for r0 in range(0, n, prep_rows):
            rs = slice(r0, r0 + prep_rows)
            for h in range(heads):
                hs = slice(h * LANES, (h + 1) * LANES)
                kb_ref[h, rs, :] = _qk_prep(k_ref[0, rs, hs], cos_ref[rs, :], sin_ref[rs, :],
                                            kg_ref[...]).astype(BF16)
                vb_ref[h, rs, :LANES] = v_ref[0, rs, hs].astype(BF16)
                vb_ref[h, rs, LANES:] = jnp.ones((prep_rows, LANES), BF16)

    def attend(nk):
        rows = pl.ds(pl.multiple_of(qi * bq, bq), bq)
        cos, sin = cos_ref[rows, :], sin_ref[rows, :]
        lane = lax.broadcasted_iota(jnp.int32, (bq, LANES), 1)
        q2, m, acc = [], [], []
        for h in range(heads):
            q = _qk_prep(q_ref[0, :, h * LANES:(h + 1) * LANES], cos, sin, qg_ref[...])
            q = (q * (DA_DQK ** -0.5 * LOG2E)).astype(BF16)
            zero = jnp.zeros_like(q)
            q2.append(jnp.concatenate([jnp.where(lane < DA_DQK, q, zero),
                                       jnp.where(lane >= DA_DQK, q, zero)], axis=0))
            m.append(jnp.full((2 * bq, 1), -jnp.inf, F32))
            acc.append(jnp.zeros((2 * bq, 2 * LANES), F32))
        for c0 in range(0, nk, chunk):
            c1 = min(c0 + chunk, nk)
            for h in range(heads):
                s = lax.dot_general(q2[h], kb_ref[h, c0:c1, :], NT, preferred_element_type=F32)
                m_new = jnp.maximum(m[h], jnp.max(s, axis=-1, keepdims=True))
                p = jnp.exp2(s - m_new)
                acc[h] = jnp.exp2(m[h] - m_new) * acc[h] + jnp.dot(
                    p.astype(BF16), vb_ref[h, c0:c1, :], preferred_element_type=F32)
                m[h] = m_new
        for h in range(heads):
            o = acc[h][:, :LANES] / acc[h][:, LANES:]
            o = o[:bq] - lam * o[bq:]
            o = o * lax.rsqrt(jnp.mean(o * o, axis=-1, keepdims=True) + EPS) * og_ref[...] * out_scale
            o_ref[0, :, h * LANES:(h + 1) * LANES] = o.astype(BF16)

    n_ctx_blocks = n_ctx // bq

    @pl.when(qi < n_ctx_blocks)
    def _():
        attend(n_ctx)

    @pl.when(qi >= n_ctx_blocks)
    def _():
        attend(n)


def _diff_attention(p, lam, cos, sin, q_gain, k_gain, out_gain, *, n_ctx, out_scale, bq=ATTN_BQ,
                    chunk=ATTN_CHUNK, prep_rows=ATTN_PREP, heads=ATTN_HEADS):
    b, n, _ = p.shape
    hg = DA_HEADS // heads
    hd = DA_HEADS * LANES
    hw = heads * LANES
    const = lambda bi, hi, qi: (0, 0)
    return pl.pallas_call(
        functools.partial(_attn_kernel, n_ctx=n_ctx, chunk=chunk, prep_rows=prep_rows, out_scale=out_scale),
        grid=(b, hg, n // bq),
        in_specs=[pl.BlockSpec(memory_space=pltpu.SMEM),
                  pl.BlockSpec((1, bq, hw), lambda bi, hi, qi: (bi, qi, hi)),
                  pl.BlockSpec((1, n, hw), lambda bi, hi, qi: (bi, 0, hg + hi)),
                  pl.BlockSpec((1, n, hw), lambda bi, hi, qi: (bi, 0, 2 * hg + hi)),
                  pl.BlockSpec((n, LANES), const), pl.BlockSpec((n, LANES), const),
                  pl.BlockSpec((1, LANES), const), pl.BlockSpec((1, LANES), const),
                  pl.BlockSpec((1, LANES), const)],
        out_specs=pl.BlockSpec((1, bq, hw), lambda bi, hi, qi: (bi, qi, hi)),
        out_shape=jax.ShapeDtypeStruct((b, n, hd), BF16),
        scratch_shapes=[pltpu.VMEM((heads, n, LANES), BF16), pltpu.VMEM((heads, n, 2 * LANES), BF16)],
        compiler_params=pltpu.CompilerParams(
            dimension_semantics=("arbitrary", "arbitrary", "arbitrary"), vmem_limit_bytes=VMEM_LIMIT),
        name="diff_attention",
    )(lam, p, p, p, cos, sin, q_gain, k_gain, out_gain)


def _scan_chunk(t, n_ctx_chunks, n_chunks, reverse):
    if not reverse:
        return t
    return jnp.where(t < n_ctx_chunks, n_ctx_chunks - 1 - t, n_chunks - 1 - (t - n_ctx_chunks))


def _gla_constants(L, reverse):
    nlev = int(math.log2(L))
    idx = np.arange(L)
    i, t = idx[:, None], idx[None, :]
    if reverse:
        i, t = L - 1 - i, L - 1 - t
    rs = [(t <= i), (t > i)]
    am = [(i == t)]
    for lev in range(nlev):
        m = L >> (lev + 1)
        blk_i, blk_t = i // (2 * m), t // (2 * m)
        mid = blk_i * 2 * m + m
        q_role = i >= mid
        rs.append(np.where(q_role, (t >= mid) & (t <= i), (t > i) & (t < mid)) & (blk_i == blk_t))
        am.append((blk_i == blk_t) & q_role & (t < mid))
    return (np.stack(rs).astype(np.float32).reshape((nlev + 2) * L, L),
            np.stack(am).astype(np.float32))


def _gla_kernel(q_ref, k_ref, v_ref, glr_ref, w2_ref, gb_ref, rsum_ref, amask_ref, o_ref, st_ref, *,
                reverse):
    L = q_ref.shape[1]
    nlev = amask_ref.shape[0] - 1
    n_coarse = nlev + 2 - rsum_ref.shape[0] // L

    @pl.when(pl.program_id(1) == 0)
    def _():
        st_ref[...] = jnp.zeros_like(st_ref)

    z = jnp.dot(glr_ref[0].astype(BF16), w2_ref[...], preferred_element_type=F32) + gb_ref[...]
    g = (jnp.minimum(z, 0.0) - jnp.log(1.0 + jnp.exp(-jnp.abs(z)))) * (1.0 / GLA_TAU)
    g1 = g.astype(BF16)
    g2 = (g - g1.astype(F32)).astype(BF16)
    g3 = (g - g1.astype(F32) - g2.astype(F32)).astype(BF16)
    rsum = rsum_ref[...]
    e_all = (jnp.dot(rsum, g1, preferred_element_type=F32) + jnp.dot(rsum, g2, preferred_element_type=F32)
             + jnp.dot(rsum, g3, preferred_element_type=F32))
    b_tot = jnp.sum(g, axis=0, keepdims=True)

    def level_decay(lev, ck):
        if lev >= n_coarse:
            return jnp.exp(e_all[(2 + lev - n_coarse) * L:(3 + lev - n_coarse) * L, ck])
        m = L >> (lev + 1)
        b3 = e_all[0:L, ck].reshape(L // (2 * m), 2 * m, GLA_DK)
        pivot = m if reverse else m - 1
        return jnp.exp(-jnp.abs(b3 - b3[:, pivot:pivot + 1, :])).reshape(L, GLA_DK)

    for h in range(GLA_HEADS):
        ck = slice(h * GLA_DK, (h + 1) * GLA_DK)
        cv = slice(h * GLA_DV, (h + 1) * GLA_DV)
        q = q_ref[0, :, ck] * (GLA_DK ** -0.5)
        k = k_ref[0, :, ck]
        v = v_ref[0, :, cv].astype(BF16)
        st = st_ref[h]
        a = amask_ref[0] * lax.dot_general(q.astype(BF16), k.astype(BF16), NT, preferred_element_type=F32)
        for lev in range(nlev):
            e = level_decay(lev, ck)
            a = a + amask_ref[1 + lev] * lax.dot_general(
                (q * e).astype(BF16), (k * e).astype(BF16), NT, preferred_element_type=F32)
        qe = (q * jnp.exp(e_all[0:L, ck])).astype(BF16)
        o = lax.dot_general(qe, st.astype(BF16), NT, preferred_element_type=F32)
        o = o + jnp.dot(a.astype(BF16), v, preferred_element_type=F32)
        o_ref[0, :, cv] = o
        kd = (k * jnp.exp(e_all[L:2 * L, ck])).astype(BF16)
        st_ref[h] = jnp.exp(b_tot[:, ck]) * st + jnp.dot(
            v.T, kd, preferred_element_type=F32)


def _gla_scan(p, glr, w2, gb, *, n_ctx, reverse, L=GLA_CHUNK):
    b, n, _ = p.shape
    nc, ncc = n // L, n_ctx // L
    rsum, amask = _gla_constants(L, reverse)
    n_coarse = sum(2 * (L >> (lev + 1)) >= SUBLANES for lev in range(amask.shape[0] - 1))
    rsum = rsum.reshape(-1, L, L)
    rsum = np.concatenate([rsum[:2], rsum[2 + n_coarse:]]).reshape(-1, L)
    chunk = functools.partial(_scan_chunk, n_ctx_chunks=ncc, n_chunks=nc, reverse=reverse)
    hk, hv = GLA_HEADS * GLA_DK, GLA_HEADS * GLA_DV
    cols = lambda width, start: pl.BlockSpec((1, L, width), lambda bi, t: (bi, chunk(t), start // width))
    row = lambda bi, t: (bi, chunk(t), 0)
    const2 = lambda bi, t: (0, 0)
    const3 = lambda bi, t: (0, 0, 0)
    return pl.pallas_call(
        functools.partial(_gla_kernel, reverse=reverse),
        grid=(b, nc),
        in_specs=[cols(hk, 0), cols(hk, hk), cols(hv, 2 * hk), cols(LANES, 0),
                  pl.BlockSpec(w2.shape, const2), pl.BlockSpec(gb.shape, const2),
                  pl.BlockSpec(rsum.shape, const2), pl.BlockSpec(amask.shape, const3)],
        out_specs=pl.BlockSpec((1, L, hv), row),
        out_shape=jax.ShapeDtypeStruct((b, n, hv), F32),
        scratch_shapes=[pltpu.VMEM((GLA_HEADS, GLA_DV, GLA_DK), F32)],
        compiler_params=pltpu.CompilerParams(
            dimension_semantics=("arbitrary", "arbitrary"), vmem_limit_bytes=VMEM_LIMIT),
        name="gla_scan_bwd" if reverse else "gla_scan_fwd",
    )(p, p, p, glr, w2, gb, jnp.asarray(rsum, BF16), jnp.asarray(amask))


def _log_sigmoid(x):
    return jnp.minimum(x, 0.0) - jnp.log(1.0 + jnp.exp(-jnp.abs(x)))


def _mlstm_kernel(*refs):
    tri_ref, of_ref, ob_ref, c_ref, m_ref = refs[10:]

    @pl.when(pl.program_id(1) == 0)
    def _():
        c_ref[...] = jnp.zeros_like(c_ref)
        m_ref[...] = jnp.zeros_like(m_ref)

    for d, o_ref in enumerate((of_ref, ob_ref)):
        _mlstm_direction(*refs[5 * d:5 * d + 5], tri_ref[d], o_ref, c_ref, m_ref, d * ML_HEADS, bool(d))


def _mlstm_direction(q_ref, k_ref, v_ref, gc_ref, gr_ref, tri, o_ref, c_ref, m_ref, h0, reverse):
    L = q_ref.shape[1]
    H = ML_HEADS
    ones = jnp.ones((L, LANES), BF16)
    gc = gc_ref[0]
    gr = gr_ref[0]
    ic_col, ic_row = gc[:, :H], gr[:H, :]
    b_col = jnp.dot(tri, _log_sigmoid(gc[:, H:]), precision=HI, preferred_element_type=F32)
    b_row = lax.dot_general(_log_sigmoid(gr[H:, :]), tri, NT, precision=HI, preferred_element_type=F32)
    last = 0 if reverse else L - 1
    causal = tri > 0.5

    for h in range(H):
        ck = slice(h * ML_DK, (h + 1) * ML_DK)
        cv = slice(h * ML_DV, (h + 1) * ML_DV)
        q = q_ref[0, :, ck]
        k = k_ref[0, :, ck]
        v = jnp.concatenate([v_ref[0, :, cv].astype(BF16), ones], axis=1)
        c, m = c_ref[h0 + h], m_ref[h0 + h]
        bc, br = b_col[:, h:h + 1], b_row[h:h + 1, :]
        icc, icr = ic_col[:, h:h + 1], ic_row[h:h + 1, :]
        b_last = bc[last:last + 1, :]

        a = bc + m
        dmat = jnp.where(causal, bc - br + icr, -jnp.inf)
        m_t = jnp.maximum(a, jnp.max(dmat, axis=-1, keepdims=True))
        w_inter = jnp.exp(a - m_t)
        s = lax.dot_general(q, k, NT, preferred_element_type=F32) * jnp.exp(dmat - m_t)
        both = w_inter * jnp.dot(q, c.astype(BF16), preferred_element_type=F32) + jnp.dot(
            s.astype(BF16), v, preferred_element_type=F32)
        den = jnp.maximum(jnp.abs(both[:, ML_DV:]), jnp.exp(-m_t))
        o_ref[0, :, cv] = both[:, :ML_DV] / jnp.concatenate([den] * (ML_DV // LANES), axis=1)

        gs_col = b_last - bc + icc
        gs_row = b_last - br + icr
        m_new = jnp.maximum(b_last + m, jnp.max(gs_row, axis=-1, keepdims=True))
        decay = jnp.exp(b_last + m - m_new)
        wk = jnp.exp(gs_col - m_new) * k.astype(F32)
        c_ref[h0 + h] = decay * c + jnp.dot(wk.astype(BF16).T, v, preferred_element_type=F32)
        m_ref[h0 + h] = m_new


def _mlstm_scan(qk, p, gates, *, v_col, n_ctx, L=ML_CHUNK):
    b, n, _ = qk.shape
    hk, hv, ng = ML_HEADS * ML_DK, ML_HEADS * ML_DV, 2 * ML_HEADS
    nc, ncc = n // L, n_ctx // L
    idx = np.arange(L)
    lower = idx[None, :] <= idx[:, None]
    tri = jnp.asarray(np.stack([lower, lower.T]).astype(np.float32))
    in_specs, operands, out_specs = [], [], []
    for d in range(2):
        chunk = functools.partial(_scan_chunk, n_ctx_chunks=ncc, n_chunks=nc, reverse=bool(d))
        col = lambda j, chunk=chunk: (lambda bi, t: (bi, chunk(t), j))
        g_dir = gates[..., d * ng:(d + 1) * ng]
        in_specs += [pl.BlockSpec((1, L, hk), col(0)), pl.BlockSpec((1, L, hk), col(1)),
                     pl.BlockSpec((1, L, hv), col(v_col)), pl.BlockSpec((1, L, ng), col(0)),
                     pl.BlockSpec((1, ng, L), lambda bi, t, chunk=chunk: (bi, 0, chunk(t)))]
        operands += [qk, qk, p, g_dir, jnp.swapaxes(g_dir, 1, 2)]
        out_specs.append(pl.BlockSpec((1, L, hv), col(0)))
    return pl.pallas_call(
        _mlstm_kernel,
        grid=(b, nc),
        in_specs=in_specs + [pl.BlockSpec((2, L, L), lambda bi, t: (0, 0, 0))],
        out_specs=out_specs,
        out_shape=[jax.ShapeDtypeStruct((b, n, hv), F32)] * 2,
        scratch_shapes=[pltpu.VMEM((2 * ML_HEADS, ML_DK, ML_DV + LANES), F32),
                        pltpu.VMEM((2 * ML_HEADS, 1, 1), F32)],
        compiler_params=pltpu.CompilerParams(
            dimension_semantics=("arbitrary", "arbitrary"), vmem_limit_bytes=VMEM_LIMIT),
        name="mlstm_scan",
    )(*operands, tri)


def _silu(x):
    return x * jax.nn.sigmoid(x)


def _rope_tables(n_lat, n_ctx):
    rows = n_lat // GRID_W
    row = jnp.repeat(jnp.arange(rows, dtype=F32), GRID_W)
    col = jnp.tile(jnp.arange(GRID_W, dtype=F32), rows)
    inv = jnp.power(ROPE_BASE, -jnp.arange(ROPE_PAIRS, dtype=F32) / ROPE_PAIRS)
    ang_r, ang_c = row[:, None] * inv, col[:, None] * inv
    cos = jnp.concatenate([jnp.cos(ang_r)] * 2 + [jnp.cos(ang_c)] * 2, axis=-1)
    sin = jnp.concatenate([-jnp.sin(ang_r), jnp.sin(ang_r), -jnp.sin(ang_c), jnp.sin(ang_c)], axis=-1)
    pad = ((n_ctx, 0), (0, 0))
    cos, sin = jnp.pad(cos, pad, constant_values=1.0), jnp.pad(sin, pad)
    return jnp.tile(cos, (1, 2)), jnp.tile(sin, (1, 2))


def _even_mixer(hn, bsz, w_in, i, gate_w2, gate_b, gla_norm_g, sc_conv_w, seq):
    nq = GLA_HEADS * GLA_DK
    nv = GLA_HEADS * GLA_DV
    gate_col = 2 * nq + 2 * nv
    pa = _matmul(hn, w_in, i, bn=1024, ncols=gate_col, transposed=True)
    glr = _matmul(hn, w_in, i, bn=LANES, col0=gate_col, ncols=LANES, transposed=True)
    pb = _matmul(hn, w_in, i, bn=SC_WIDTH, col0=gate_col, ncols=3 * SC_WIDTH,
                 col_shift=2 * GLA_GATE_RANK, transposed=True)
    pa3 = pa.reshape(bsz, -1, pa.shape[1])
    glr3 = glr.reshape(bsz, -1, LANES)
    outs = []
    for direction in range(2):
        w2 = jnp.zeros((LANES, nq), F32).at[
            direction * GLA_GATE_RANK:(direction + 1) * GLA_GATE_RANK].set(gate_w2[direction])
        outs.append(_gla_scan(pa3, glr3, w2.astype(BF16), gate_b[direction][None, :],
                              n_ctx=seq["n_ctx"], reverse=bool(direction)).reshape(-1, nv))
    gain = jnp.tile(gla_norm_g, GLA_HEADS)[None, :]
    return [_even_finish(outs[0], outs[1], pa, pb, gain, sc_conv_w, **seq)]


def _odd_mixer(hn, bsz, rope, layer, w_in, i, qn_g, kn_g, lam_p, subln_g, ml_conv_w, ml_gate_b,
               ml_norm_g, seq):
    lam_init = 0.8 - 0.6 * math.exp(-0.3 * layer)
    lam = (jnp.exp(jnp.sum(lam_p[0] * lam_p[1])) - jnp.exp(jnp.sum(lam_p[2] * lam_p[3])) + lam_init)
    na = DA_HEADS * 2 * DA_DQK
    nk = ML_HEADS * ML_DK
    nv = ML_HEADS * ML_DV
    n_main = 3 * na + 2 * nk + 2 * nv
    p = _matmul(hn, w_in, i, bn=1024, ncols=n_main, transposed=True)
    mg = _matmul(hn, w_in, i, bn=LANES, col0=n_main, ncols=LANES, transposed=True)[:, :4 * ML_HEADS]
    p3 = p.reshape(bsz, -1, n_main)
    n = p3.shape[1]

    cos, sin = rope
    tile2 = lambda g: jnp.tile(g, 2)[None, :]
    da = _diff_attention(p3, lam.reshape(1).astype(F32), cos, sin, tile2(qn_g), tile2(kn_g),
                         subln_g[None, :], n_ctx=seq["n_ctx"], out_scale=1.0 - lam_init)

    scale = jnp.concatenate([jnp.ones((nk,), F32), jnp.full((nk,), ML_DK ** -0.5, F32)])[None, :]
    hqk = _mlstm_prep(p, ml_conv_w, scale, col=3 * na // (2 * nk), **seq).reshape(bsz, n, 2 * nk)
    gates = (mg + ml_gate_b).reshape(bsz, n, 4 * ML_HEADS)
    outs = [o.reshape(-1, nv) for o in
            _mlstm_scan(hqk, p3, gates, v_col=(3 * na + 2 * nk) // nv, n_ctx=seq["n_ctx"])]
    m = _mlstm_finish(outs[0], outs[1], p, jnp.tile(ml_norm_g, ML_HEADS)[None, :],
                      col=(3 * na + 2 * nk + nv) // nv)
    return [da.reshape(-1, na), m]


def _modulation(c, c_ctx, ada_w, ada_b, layer):
    cc = jnp.concatenate([c, c_ctx[None, :]], axis=0)
    rows = cc.shape[0]
    act = jnp.pad(_silu(cc), ((0, BF16_ROWS - rows), (0, 0))).astype(BF16)
    mod = _matmul(act, ada_w, layer, bn=1024, bm=BF16_ROWS)[:rows] + ada_b[layer]
    return mod.reshape(rows, 6, -1)


def kernel(x, c, ctx, c_ctx, ada_w, ada_b, norm1_g, norm2_g, ev_w_in, ev_w_out, gla_gate_w2, gla_gate_b, gla_norm_g, sc_conv_w, od_w_in, od_w_out, da_qnorm_g, da_knorm_g, da_lambda, da_subln_g, ml_conv_w, ml_gate_b, ml_norm_g, ffn_w_up, ffn_conv_w, ffn_conv_b, ffn_w_down):
    bsz, n_lat, d = x.shape
    n_ctx = ctx.shape[1]
    n = n_ctx + n_lat
    depth = ada_w.shape[0]
    d_ff = ffn_w_down.shape[1]
    assert n % ROW_TILE == 0 and n_ctx <= ROW_CHUNK and d_ff % (4 * LANES) == 0
    rope = _rope_tables(n_lat, n_ctx)
    seq = dict(n_ctx=n_ctx, n_seq=n)

    ev_w_in = jnp.swapaxes(ev_w_in, 1, 2)
    od_w_in = jnp.swapaxes(od_w_in, 1, 2)
    mods = [_modulation(c, c_ctx, ada_w, ada_b, layer) for layer in range(depth)]
    xs = jnp.concatenate([ctx, x], axis=1).reshape(bsz * n, d)
    hn = _first_norm(xs, norm1_g[0][None, :], mods[0], **seq)
    for layer in range(depth):
        i = layer // 2
        if layer % 2 == 0:
            mix = _even_mixer(hn, bsz, ev_w_in, i, gla_gate_w2[i], gla_gate_b[i], gla_norm_g[i],
                              sc_conv_w[i], seq)
            w_out = ev_w_out
        else:
            mix = _odd_mixer(hn, bsz, rope, layer, od_w_in, i, da_qnorm_g[i], da_knorm_g[i],
                             da_lambda[i], da_subln_g[i], ml_conv_w[i], ml_gate_b[i], ml_norm_g[i], seq)
            w_out = od_w_out
        xs, hn = _out_proj(mix, _to_bf16(w_out, i, rows=512), xs, mods[layer], norm2_g[layer][None, :],
                           **seq)
        act = _ffn_up(hn, ffn_w_up, layer, ffn_conv_w[layer], ffn_conv_b[layer][None, :], **seq)
        nxt = min(layer + 1, depth - 1)
        xs, hn = _down_proj(act, _to_bf16(ffn_w_down, layer, rows=d_ff // 8), xs,
                            mods[layer], norm1_g[nxt][None, :], mods[nxt], bk=d_ff // 4, **seq)
    return xs.reshape(bsz, n, d)[:, n_ctx:, :]
```

```python
import functools
import math

import jax
import jax.numpy as jnp
import numpy as np
from jax import lax
from jax.experimental import pallas as pl
from jax.experimental.pallas import tpu as pltpu

GRID_W = 64
EPS = 1e-6
GLA_HEADS, GLA_DK, GLA_DV, GLA_GATE_RANK, GLA_TAU = 4, 128, 256, 16, 16.0
SC_WIDTH = 1024
DA_HEADS, DA_DQK, DA_DV = 8, 64, 128
ROPE_BASE = 10000.0
ROPE_PAIRS = DA_DQK // 4
ML_HEADS, ML_DK, ML_DV = 4, 128, 256

GLA_CHUNK = 128
ML_CHUNK = 256
ATTN_BQ = 256
ATTN_CHUNK = 1024
ATTN_PREP = 544
ATTN_HEADS = 2
ROW_TILE = 1088
ROW_CHUNK = 272
LANES = 128
SUBLANES = 8
BF16_ROWS = 16
ROW_GROUP_UNROLL = 4
LOG2E = 1.4426950408889634
VMEM_LIMIT = 56 * 1024 * 1024

F32 = jnp.float32
BF16 = jnp.bfloat16
HI = lax.Precision.HIGHEST
NT = (((1,), (1,)), ((), ()))


def _mm_kernel(x_ref, w_ref, *rest, row_shift, transposed):
    o_ref, wb_ref = rest[-2:]

    @pl.when(pl.program_id(1) == 0)
    def _():
        if not transposed:
            wb_ref[...] = w_ref[...].astype(BF16)
            return
        bn = wb_ref.shape[1]
        piece = 2 * LANES
        for c0 in range(0, bn, piece):
            c1 = min(c0 + piece, bn)
            if row_shift and c1 == bn:
                rows = jnp.concatenate([w_ref[c0 + row_shift:bn, :], rest[0][...]], axis=0)
            else:
                rows = w_ref[c0 + row_shift:c1 + row_shift, :]
            wb_ref[:, c0:c1] = rows.T.astype(BF16)

    o_ref[...] = jnp.dot(x_ref[...], wb_ref[...], preferred_element_type=F32).astype(o_ref.dtype)


def _matmul(x, w, layer, *, bn, col0=0, ncols=None, col_shift=0, transposed=False, bm=ROW_TILE,
            out_dtype=F32):
    m, k = x.shape
    n_all = w.shape[1] if transposed else w.shape[2]
    ncols = n_all - col0 if ncols is None else ncols
    j0 = col0 // bn
    if transposed:
        w_spec = pl.BlockSpec((None, bn, k), lambda j, i: (layer, j0 + j, 0))
    else:
        w_spec = pl.BlockSpec((None, k, bn), lambda j, i: (layer, 0, j0 + j))
    in_specs = [pl.BlockSpec((bm, k), lambda j, i: (i, 0)), w_spec]
    operands = [x, w]
    if col_shift:
        per = bn // col_shift
        in_specs.append(pl.BlockSpec((None, col_shift, k), lambda j, i: (layer, (j0 + j + 1) * per, 0)))
        operands.append(w)
    return pl.pallas_call(
        functools.partial(_mm_kernel, row_shift=col_shift, transposed=transposed),
        grid=(pl.cdiv(ncols, bn), m // bm),
        in_specs=in_specs,
        out_specs=pl.BlockSpec((bm, bn), lambda j, i: (i, j)),
        out_shape=jax.ShapeDtypeStruct((m, ncols), out_dtype),
        scratch_shapes=[pltpu.VMEM((k, bn), BF16)],
        compiler_params=pltpu.CompilerParams(
            dimension_semantics=("arbitrary", "arbitrary"), vmem_limit_bytes=VMEM_LIMIT),
        name="matmul",
    )(*operands)


def _cast_kernel(x_ref, o_ref):
    o_ref[...] = x_ref[...].astype(o_ref.dtype)


def _to_bf16(w, layer, *, rows):
    _, r, c = w.shape
    return pl.pallas_call(
        _cast_kernel,
        grid=(r // rows,),
        in_specs=[pl.BlockSpec((None, rows, c), lambda i: (layer, i, 0))],
        out_specs=pl.BlockSpec((rows, c), lambda i: (i, 0)),
        out_shape=jax.ShapeDtypeStruct((r, c), BF16),
        compiler_params=pltpu.CompilerParams(vmem_limit_bytes=VMEM_LIMIT),
        name="to_bf16",
    )(w)


def _norm_mod(x, gain_scale, shift):
    return x * lax.rsqrt(jnp.mean(x * x, axis=-1, keepdims=True) + EPS) * gain_scale + shift


def _row_vectors(modb_ref, modc_ref, idx, gain_ref=None):
    def vec(ref):
        row = ref[0, idx:idx + 1, :]
        if gain_ref is not None:
            row = gain_ref[...] * (1.0 + row)
        return jnp.broadcast_to(row, (BF16_ROWS, row.shape[1]))

    lat, ctx = vec(modb_ref), vec(modc_ref)
    return lambda is_ctx: lat if is_ctx is None else jnp.where(is_ctx, ctx, lat)


def _head_rmsnorm(x, width):
    parts = []
    for c0 in range(0, x.shape[1], width):
        xh = x[:, c0:c0 + width]
        parts.append(xh * lax.rsqrt(jnp.mean(xh * xh, axis=-1, keepdims=True) + EPS))
    return parts[0] if len(parts) == 1 else jnp.concatenate(parts, axis=1)


def _sigmoid(x):
    return 1.0 / (1.0 + jnp.exp(-x))


def _mod_row(is_ctx, modb_ref, modc_ref, idx):
    if is_ctx is None:
        return modb_ref[0, idx:idx + 1, :]
    return jnp.where(is_ctx, modc_ref[0, idx:idx + 1, :], modb_ref[0, idx:idx + 1, :])


def _chunk_rows(r0, rc, tiles_per_batch, bm):
    rows = r0 + lax.broadcasted_iota(jnp.int32, (rc, 1), 0)
    return rows, (pl.program_id(0) % tiles_per_batch) * bm + rows


def _chunk_is_ctx(r0, pos, n_ctx):
    return pos < n_ctx if r0 < n_ctx else None


def _edge_rows(r0, rc, bm, tiles_per_batch, n_ctx, n_seq, gate_ref, hprev_ref, hnext_ref):
    hr = hprev_ref.shape[0]
    pos0 = (pl.program_id(0) % tiles_per_batch) * bm
    if r0 == 0:
        before = hprev_ref[...].astype(F32)[hr - 1:hr, :]
        before = jnp.where((pos0 == 0) | (pos0 == n_ctx), 0.0, before)
    else:
        before = gate_ref[r0 - hr:r0, :].astype(F32)[hr - 1:hr, :]
    if r0 + rc == bm:
        after = hnext_ref[...].astype(F32)[0:1, :]
        after = jnp.where((pos0 + bm == n_ctx) | (pos0 + bm == n_seq), 0.0, after)
    else:
        after = gate_ref[r0 + rc:r0 + rc + hr, :].astype(F32)[0:1, :]
    return before, after


def _splits_rows(r0, rc, bm, tiles_per_batch, n_ctx):
    return any(r0 <= n_ctx - k * bm <= r0 + rc for k in range(tiles_per_batch))


def _dwconv3(s_ref, g, before, after, cw_ref, pos, n_ctx):
    rc = g.shape[0]
    s_ref[SUBLANES:SUBLANES + rc, :] = g
    s_ref[SUBLANES - 1:SUBLANES, :] = before
    s_ref[SUBLANES + rc:SUBLANES + rc + 1, :] = after
    prev = s_ref[SUBLANES - 1:SUBLANES - 1 + rc, :]
    nxt = s_ref[SUBLANES + 1:SUBLANES + 1 + rc, :]
    if pos is not None:
        prev = jnp.where(pos == n_ctx, 0.0, prev)
        nxt = jnp.where(pos == n_ctx - 1, 0.0, nxt)
    return prev * cw_ref[0:1, :] + g * cw_ref[1:2, :] + nxt * cw_ref[2:3, :]


def _mod_specs(n_batch, tiles_per_batch, d):
    return [pl.BlockSpec((1, 6, d), lambda i, *_: (i // tiles_per_batch, 0, 0)),
            pl.BlockSpec((1, 6, d), lambda i, *_: (n_batch, 0, 0))]


def _halo_specs(bm, width, col, n_rows, hr):
    hb = bm // hr
    last = n_rows // hr - 1
    return [pl.BlockSpec((hr, width), lambda i, *k: (jnp.maximum(i * hb - 1, 0), col(*k))),
            pl.BlockSpec((hr, width), lambda i, *k: (jnp.minimum((i + 1) * hb, last), col(*k)))]


def _norm_kernel(x_ref, gain_ref, modb_ref, modc_ref, o_ref, *, n_ctx, tiles_per_batch):
    gain_scale = _row_vectors(modb_ref, modc_ref, 1, gain_ref)
    shift = _row_vectors(modb_ref, modc_ref, 0)

    def rows(rs, is_ctx):
        o_ref[rs, :] = _norm_mod(x_ref[rs, :], gain_scale(is_ctx), shift(is_ctx)).astype(BF16)

    _for_row_groups(x_ref.shape[0], n_ctx, tiles_per_batch, rows)


def _first_norm(x, gain, mod, *, n_ctx, n_seq, bm=ROW_TILE):
    m, d = x.shape
    tpb = n_seq // bm
    return pl.pallas_call(
        functools.partial(_norm_kernel, n_ctx=n_ctx, tiles_per_batch=tpb),
        grid=(m // bm,),
        in_specs=[pl.BlockSpec((bm, d), lambda i: (i, 0)), pl.BlockSpec((1, d), lambda i: (0, 0))]
        + _mod_specs(mod.shape[0] - 1, tpb, d),
        out_specs=pl.BlockSpec((bm, d), lambda i: (i, 0)),
        out_shape=jax.ShapeDtypeStruct((m, d), BF16),
        compiler_params=pltpu.CompilerParams(vmem_limit_bytes=VMEM_LIMIT),
        name="first_norm",
    )(x, gain, mod, mod)


def _residual_norm(xo_ref, hn_ref, x_ref, modb_ref, modc_ref, gain_ref, nmodb_ref, nmodc_ref, *,
                   n_ctx, tiles_per_batch, gate_idx, shift_idx, inline=False):
    gate = _row_vectors(modb_ref, modc_ref, gate_idx)
    gain_scale = _row_vectors(nmodb_ref, nmodc_ref, shift_idx + 1, gain_ref)
    shift = _row_vectors(nmodb_ref, nmodc_ref, shift_idx)

    def rows(rs, is_ctx):
        x_new = x_ref[rs, :] + gate(is_ctx) * xo_ref[rs, :]
        xo_ref[rs, :] = x_new
        hn_ref[rs, :] = _norm_mod(x_new, gain_scale(is_ctx), shift(is_ctx)).astype(BF16)

    _for_row_groups(x_ref.shape[0], n_ctx, tiles_per_batch, rows, inline)


def _for_row_groups(bm, n_ctx, tiles_per_batch, fn, inline=False):
    rg = BF16_ROWS
    pos0 = (pl.program_id(0) % tiles_per_batch) * bm
    n_lead = -(-min(n_ctx, bm) // rg)
    if inline:
        for g in range(bm // rg):
            is_ctx = (pos0 + g * rg + lax.broadcasted_iota(jnp.int32, (rg, 1), 0) < n_ctx) if g < n_lead else None
            fn(slice(g * rg, (g + 1) * rg), is_ctx)
        return

    def group(with_ctx):
        def body(g, carry):
            r0 = pl.multiple_of(g * rg, rg)
            is_ctx = (pos0 + r0 + lax.broadcasted_iota(jnp.int32, (rg, 1), 0) < n_ctx) if with_ctx else None
            fn(pl.ds(r0, rg), is_ctx)
            return carry
        return body

    lax.fori_loop(0, n_lead, group(True), 0, unroll=ROW_GROUP_UNROLL)
    lax.fori_loop(n_lead, bm // rg, group(False), 0, unroll=ROW_GROUP_UNROLL)


def _out_kernel(*refs, n_in, n_ctx, tiles_per_batch):
    a_refs = refs[:n_in]
    w_ref, x_ref, modb_ref, modc_ref, gain_ref, xo_ref, hn_ref = refs[n_in:]
    bm = x_ref.shape[0]
    k0 = 0
    for n_done, a_ref in enumerate(a_refs):
        part = jnp.dot(a_ref[...], w_ref[k0:k0 + a_ref.shape[1], :], preferred_element_type=F32)
        xo_ref[...] = part if n_done == 0 else xo_ref[...] + part
        k0 += a_ref.shape[1]
    _residual_norm(xo_ref, hn_ref, x_ref, modb_ref, modc_ref, gain_ref, modb_ref, modc_ref,
                   n_ctx=n_ctx, tiles_per_batch=tiles_per_batch, gate_idx=2, shift_idx=3, inline=True)


def _out_proj(acts, w, x, mod, gain, *, n_ctx, n_seq, bm=ROW_TILE // 2):
    m, d = x.shape
    tpb = n_seq // bm
    rows = lambda width: pl.BlockSpec((bm, width), lambda i: (i, 0))
    return pl.pallas_call(
        functools.partial(_out_kernel, n_in=len(acts), n_ctx=n_ctx, tiles_per_batch=tpb),
        grid=(m // bm,),
        in_specs=[rows(a.shape[1]) for a in acts]
        + [pl.BlockSpec(w.shape, lambda i: (0, 0), pipeline_mode=pl.Buffered(1)), rows(d)]
        + _mod_specs(mod.shape[0] - 1, tpb, d) + [pl.BlockSpec((1, d), lambda i: (0, 0))],
        out_specs=[rows(d), rows(d)],
        out_shape=[jax.ShapeDtypeStruct((m, d), F32), jax.ShapeDtypeStruct((m, d), BF16)],
        compiler_params=pltpu.CompilerParams(vmem_limit_bytes=VMEM_LIMIT),
        name="out_proj",
    )(*acts, w, x, mod, mod, gain)


def _up_kernel(x_ref, xp_ref, xn_ref, wg_ref, wv_ref, cw_ref, cb_ref, wd_ref, o_ref, wdb_ref, wgb_ref,
               wvb_ref, xe_ref, s_ref, v_ref, *, n_ctx, n_seq, tiles_per_batch):
    i = pl.program_id(1)
    bm, rc, hr = x_ref.shape[0], ROW_CHUNK, xp_ref.shape[0]
    wdb_ref[...] = wd_ref[...].astype(BF16)

    @pl.when(i == 0)
    def _():
        wgb_ref[...] = wg_ref[...].astype(BF16)
        wvb_ref[...] = wv_ref[...].astype(BF16)

    xe_ref[0:hr, :] = xp_ref[...]
    xe_ref[hr:hr + bm, :] = x_ref[...]
    xe_ref[hr + bm:, :] = xn_ref[...]
    s_ref[...] = jnp.dot(xe_ref[...], wgb_ref[...], preferred_element_type=F32)
    v_ref[...] = jnp.dot(x_ref[...], wvb_ref[...], preferred_element_type=F32)
    pos0 = (i % tiles_per_batch) * bm
    s_ref[hr - 1:hr, :] = jnp.where((pos0 == 0) | (pos0 == n_ctx), 0.0, s_ref[hr - 1:hr, :])
    s_ref[hr + bm:hr + bm + 1, :] = jnp.where(
        (pos0 + bm == n_ctx) | (pos0 + bm == n_seq), 0.0, s_ref[hr + bm:hr + bm + 1, :])
    for r0 in range(0, bm, rc):
        rs = slice(r0, r0 + rc)
        prev = s_ref[hr - 1 + r0:hr - 1 + r0 + rc, :]
        nxt = s_ref[hr + 1 + r0:hr + 1 + r0 + rc, :]
        if _splits_rows(r0, rc, bm, tiles_per_batch, n_ctx):
            pos = pos0 + r0 + lax.broadcasted_iota(jnp.int32, (rc, 1), 0)
            prev = jnp.where(pos == n_ctx, 0.0, prev)
            nxt = jnp.where(pos == n_ctx - 1, 0.0, nxt)
        z = (prev * cw_ref[0:1, :] + s_ref[hr + r0:hr + r0 + rc, :] * cw_ref[1:2, :]
             + nxt * cw_ref[2:3, :] + cb_ref[...])
        o_ref[rs, :] = (z * _sigmoid(z) * v_ref[rs, :]).astype(BF16)


def _ffn_up(x, w, w_down, layer, conv_w, conv_b, *, n_ctx, n_seq, bn=512, bm=ROW_TILE):
    m, k = x.shape
    f = w.shape[2] // 2
    nj, ni = f // bn, m // bm
    slab = f // (nj * ni)
    assert slab * nj * ni == f and slab % BF16_ROWS == 0
    d_out = w_down.shape[2]
    tpb = n_seq // bm
    hb = bm // BF16_ROWS
    last = m // BF16_ROWS - 1
    return pl.pallas_call(
        functools.partial(_up_kernel, n_ctx=n_ctx, n_seq=n_seq, tiles_per_batch=tpb),
        grid=(nj, ni),
        in_specs=[pl.BlockSpec((bm, k), lambda j, i: (i, 0)),
                  pl.BlockSpec((BF16_ROWS, k), lambda j, i: (jnp.maximum(i * hb - 1, 0), 0)),
                  pl.BlockSpec((BF16_ROWS, k), lambda j, i: (jnp.minimum((i + 1) * hb, last), 0)),
                  pl.BlockSpec((None, k, bn), lambda j, i: (layer, 0, j)),
                  pl.BlockSpec((None, k, bn), lambda j, i: (layer, 0, j + nj)),
                  pl.BlockSpec((3, bn), lambda j, i: (0, j)), pl.BlockSpec((1, bn), lambda j, i: (0, j)),
                  pl.BlockSpec((None, slab, d_out), lambda j, i: (layer, j * ni + i, 0))],
        out_specs=[pl.BlockSpec((bm, bn), lambda j, i: (i, j)),
                   pl.BlockSpec((slab, d_out), lambda j, i: (j * ni + i, 0))],
        out_shape=[jax.ShapeDtypeStruct((m, f), BF16), jax.ShapeDtypeStruct((f, d_out), BF16)],
        scratch_shapes=[pltpu.VMEM((k, bn), BF16), pltpu.VMEM((k, bn), BF16),
                        pltpu.VMEM((bm + 2 * BF16_ROWS, k), BF16),
                        pltpu.VMEM((bm + 2 * BF16_ROWS, bn), F32), pltpu.VMEM((bm, bn), F32)],
        compiler_params=pltpu.CompilerParams(
            dimension_semantics=("arbitrary", "arbitrary"), vmem_limit_bytes=VMEM_LIMIT),
        name="ffn_up",
    )(x, x, x, w, w, conv_w, conv_b, w_down)


def _down_kernel(a_ref, w_ref, x_ref, modb_ref, modc_ref, gain_ref, nmodb_ref, nmodc_ref, xo_ref, hn_ref, *,
                 n_ctx, tiles_per_batch):
    part = jnp.dot(a_ref[...], w_ref[...], preferred_element_type=F32)

    @pl.when(pl.program_id(1) == 0)
    def _():
        xo_ref[...] = part

    @pl.when(pl.program_id(1) > 0)
    def _():
        xo_ref[...] += part

    @pl.when(pl.program_id(1) == pl.num_programs(1) - 1)
    def _():
        _residual_norm(xo_ref, hn_ref, x_ref, modb_ref, modc_ref, gain_ref, nmodb_ref, nmodc_ref,
                       n_ctx=n_ctx, tiles_per_batch=tiles_per_batch, gate_idx=5, shift_idx=0)


def _down_proj(a, w, x, mod, gain, next_mod, *, n_ctx, n_seq, bm=ROW_TILE // 2, bk=1408):
    m, f = a.shape
    d = w.shape[1]
    tpb = n_seq // bm
    row = lambda i, k: (i, 0)
    return pl.pallas_call(
        functools.partial(_down_kernel, n_ctx=n_ctx, tiles_per_batch=tpb),
        grid=(m // bm, f // bk),
        in_specs=[pl.BlockSpec((bm, bk), lambda i, k: (i, k)), pl.BlockSpec((bk, d), lambda i, k: (k, 0)),
                  pl.BlockSpec((bm, d), row)]
        + _mod_specs(mod.shape[0] - 1, tpb, d) + [pl.BlockSpec((1, d), lambda i, k: (0, 0))]
        + _mod_specs(mod.shape[0] - 1, tpb, d),
        out_specs=[pl.BlockSpec((bm, d), row), pl.BlockSpec((bm, d), row)],
        out_shape=[jax.ShapeDtypeStruct((m, d), F32), jax.ShapeDtypeStruct((m, d), BF16)],
        compiler_params=pltpu.CompilerParams(
            dimension_semantics=("arbitrary", "arbitrary"), vmem_limit_bytes=VMEM_LIMIT),
        name="down_proj",
    )(a, w, x, mod, mod, gain, next_mod, next_mod)


def _even_finish_kernel(of_ref, ob_ref, r_ref, sx_ref, sb_ref, sg_ref, sxp_ref, sxn_ref, sgp_ref, sgn_ref,
                        gain_ref, cw_ref, o_ref, s_ref, *, n_ctx, n_seq, tiles_per_batch):
    bm = of_ref.shape[0]
    nv = of_ref.shape[1]
    r = r_ref[...]
    o_ref[:, :nv] = (_head_rmsnorm(of_ref[...] + ob_ref[...], GLA_DV) * gain_ref[...]
                     * (r * _sigmoid(r))).astype(BF16)

    _, pos = _chunk_rows(0, bm, tiles_per_batch, bm)
    pos0 = (pl.program_id(0) % tiles_per_batch) * bm
    hr = sxp_ref.shape[0]
    before = (sxp_ref[...] * sgp_ref[...])[hr - 1:hr, :]
    before = jnp.where((pos0 == 0) | (pos0 == n_ctx), 0.0, before)
    after = (sxn_ref[...] * sgn_ref[...])[0:1, :]
    after = jnp.where((pos0 + bm == n_ctx) | (pos0 + bm == n_seq), 0.0, after)
    split = pos if _splits_rows(0, bm, bm, tiles_per_batch, n_ctx) else None
    conv = _dwconv3(s_ref, sg_ref[...] * sx_ref[...], before, after, cw_ref, split, n_ctx)
    o_ref[:, nv:] = (sb_ref[...] * conv).astype(BF16)


def _even_finish(o_f, o_b, pa, pb, gain, conv_w, *, n_ctx, n_seq, bm=ROW_CHUNK):
    m, nv = o_f.shape
    sc = conv_w.shape[1]
    tpb = n_seq // bm
    col = lambda j: (lambda i: (i, j))
    const = lambda i: (0, 0)
    return pl.pallas_call(
        functools.partial(_even_finish_kernel, n_ctx=n_ctx, n_seq=n_seq, tiles_per_batch=tpb),
        grid=(m // bm,),
        in_specs=[pl.BlockSpec((bm, nv), col(0)), pl.BlockSpec((bm, nv), col(0)),
                  pl.BlockSpec((bm, nv), col(2)),
                  pl.BlockSpec((bm, sc), col(0)), pl.BlockSpec((bm, sc), col(1)), pl.BlockSpec((bm, sc), col(2))]
        + _halo_specs(bm, sc, lambda: 0, m, SUBLANES) + _halo_specs(bm, sc, lambda: 2, m, SUBLANES)
        + [pl.BlockSpec((1, nv), const), pl.BlockSpec((3, sc), const)],
        out_specs=pl.BlockSpec((bm, nv + sc), col(0)),
        out_shape=jax.ShapeDtypeStruct((m, nv + sc), BF16),
        scratch_shapes=[pltpu.VMEM((bm + 2 * SUBLANES, sc), F32)],
        compiler_params=pltpu.CompilerParams(vmem_limit_bytes=VMEM_LIMIT),
        name="even_finish",
    )(o_f, o_b, pa, pb, pb, pb, pb, pb, pb, pb, gain, conv_w)


def _mlstm_prep_kernel(x_ref, xp_ref, xn_ref, cw_ref, scale_ref, o_ref, s_ref, *, n_ctx, n_seq,
                       tiles_per_batch):
    bm = x_ref.shape[0]
    _, pos = _chunk_rows(0, bm, tiles_per_batch, bm)
    before, after = _edge_rows(0, bm, bm, tiles_per_batch, n_ctx, n_seq, x_ref, xp_ref, xn_ref)
    split = pos if _splits_rows(0, bm, bm, tiles_per_batch, n_ctx) else None
    z = _dwconv3(s_ref, x_ref[...], before, after, cw_ref, split, n_ctx)
    o_ref[...] = (z * _sigmoid(z) * scale_ref[...]).astype(BF16)


def _mlstm_prep(p, conv_w, scale, *, col, n_ctx, n_seq, bm=ROW_CHUNK):
    m = p.shape[0]
    width = conv_w.shape[1]
    tpb = n_seq // bm
    const = lambda i: (0, 0)
    return pl.pallas_call(
        functools.partial(_mlstm_prep_kernel, n_ctx=n_ctx, n_seq=n_seq, tiles_per_batch=tpb),
        grid=(m // bm,),
        in_specs=[pl.BlockSpec((bm, width), lambda i: (i, col))]
        + _halo_specs(bm, width, lambda: col, m, SUBLANES)
        + [pl.BlockSpec((3, width), const), pl.BlockSpec((1, width), const)],
        out_specs=pl.BlockSpec((bm, width), lambda i: (i, 0)),
        out_shape=jax.ShapeDtypeStruct((m, width), BF16),
        scratch_shapes=[pltpu.VMEM((bm + 2 * SUBLANES, width), F32)],
        compiler_params=pltpu.CompilerParams(vmem_limit_bytes=VMEM_LIMIT),
        name="mlstm_prep",
    )(p, p, p, conv_w, scale)


def _mlstm_finish_kernel(hf_ref, hb_ref, mo_ref, gain_ref, o_ref):
    o_ref[...] = (_head_rmsnorm(hf_ref[...] + hb_ref[...], ML_DV) * gain_ref[...]
                  * _sigmoid(mo_ref[...])).astype(BF16)


def _mlstm_finish(h_f, h_b, p, gain, *, col, bm=ROW_TILE // 2):
    m, nv = h_f.shape
    return pl.pallas_call(
        _mlstm_finish_kernel,
        grid=(m // bm,),
        in_specs=[pl.BlockSpec((bm, nv), lambda i: (i, 0)), pl.BlockSpec((bm, nv), lambda i: (i, 0)),
                  pl.BlockSpec((bm, nv), lambda i: (i, col)), pl.BlockSpec((1, nv), lambda i: (0, 0))],
        out_specs=pl.BlockSpec((bm, nv), lambda i: (i, 0)),
        out_shape=jax.ShapeDtypeStruct((m, nv), BF16),
        compiler_params=pltpu.CompilerParams(vmem_limit_bytes=VMEM_LIMIT),
        name="mlstm_finish",
    )(h_f, h_b, p, gain)


def _qk_prep(x, cos, sin, gain):
    lane = lax.broadcasted_iota(jnp.int32, x.shape, 1)
    lo = lane < DA_DQK
    x2 = x * x
    ss_lo = jnp.sum(jnp.where(lo, x2, 0.0), axis=-1, keepdims=True)
    ss_hi = jnp.sum(jnp.where(lo, 0.0, x2), axis=-1, keepdims=True)
    y = x * lax.rsqrt(jnp.where(lo, ss_lo, ss_hi) * (1.0 / DA_DQK) + EPS) * gain
    swapped = jnp.where((lane & ROPE_PAIRS) == 0, pltpu.roll(y, LANES - ROPE_PAIRS, 1),
                        pltpu.roll(y, ROPE_PAIRS, 1))
    return y * cos + swapped * sin


def _attn_kernel(lam_ref, q_ref, k_ref, v_ref, cos_ref, sin_ref, qg_ref, kg_ref, og_ref, o_ref,
                 kb_ref, vb_ref, *, n_ctx, chunk, prep_rows, out_scale):
    qi = pl.program_id(2)
    lam = lam_ref[0]
    n = k_ref.shape[1]
    bq = q_ref.shape[1]
    heads = q_ref.shape[2] // LANES

    @pl.when(qi == 0)
    def _():
        for r0 in range(0, n, prep_rows):
            rs = slice(r0, r0 + prep_rows)
            for h in range(heads):
                hs = slice(h * LANES, (h + 1) * LANES)
                kb_ref[h, rs, :] = _qk_prep(k_ref[0, rs, hs], cos_ref[rs, :], sin_ref[rs, :],
                                            kg_ref[...]).astype(BF16)
                vb_ref[h, rs, :LANES] = v_ref[0, rs, hs].astype(BF16)
                vb_ref[h, rs, LANES:] = jnp.ones((prep_rows, LANES), BF16)

    def attend(r0, r1, nk):
        nr = r1 - r0
        rows = pl.ds(pl.multiple_of(qi * bq, BF16_ROWS) + r0, nr)
        cos, sin = cos_ref[rows, :], sin_ref[rows, :]
        lane = lax.broadcasted_iota(jnp.int32, (nr, LANES), 1)
        q2, m, acc = [], [], []
        for h in range(heads):
            q = _qk_prep(q_ref[0, r0:r1, h * LANES:(h + 1) * LANES], cos, sin, qg_ref[...])
            q = (q * (DA_DQK ** -0.5 * LOG2E)).astype(BF16)
            zero = jnp.zeros_like(q)
            q2.append(jnp.concatenate([jnp.where(lane < DA_DQK, q, zero),
                                       jnp.where(lane >= DA_DQK, q, zero)], axis=0))
            m.append(jnp.full((2 * nr, 1), -jnp.inf, F32))
            acc.append(jnp.zeros((2 * nr, 2 * LANES), F32))
        for c0 in range(0, nk, chunk):
            c1 = min(c0 + chunk, nk)
            for h in range(heads):
                s = lax.dot_general(q2[h], kb_ref[h, c0:c1, :], NT, preferred_element_type=F32)
                m_new = jnp.maximum(m[h], jnp.max(s, axis=-1, keepdims=True))
                p = jnp.exp2(s - m_new)
                acc[h] = jnp.exp2(m[h] - m_new) * acc[h] + jnp.dot(
                    p.astype(BF16), vb_ref[h, c0:c1, :], preferred_element_type=F32)
                m[h] = m_new
        for h in range(heads):
            o = acc[h][:, :LANES] / acc[h][:, LANES:]
            o = o[:nr] - lam * o[nr:]
            o = o * lax.rsqrt(jnp.mean(o * o, axis=-1, keepdims=True) + EPS) * og_ref[...] * out_scale
            o_ref[0, r0:r1, h * LANES:(h + 1) * LANES] = o.astype(BF16)

    @pl.when(qi == 0)
    def _():
        attend(0, n_ctx, n_ctx)
        if n_ctx < bq:
            attend(n_ctx, bq, n)

    @pl.when(qi > 0)
    def _():
        attend(0, bq, n)


def _diff_attention(p, lam, cos, sin, q_gain, k_gain, out_gain, *, n_ctx, out_scale, bq=ATTN_BQ,
                    chunk=ATTN_CHUNK, prep_rows=ATTN_PREP, heads=ATTN_HEADS):
    b, n, _ = p.shape
    hg = DA_HEADS // heads
    hd = DA_HEADS * LANES
    hw = heads * LANES
    const = lambda bi, hi, qi: (0, 0)
    return pl.pallas_call(
        functools.partial(_attn_kernel, n_ctx=n_ctx, chunk=chunk, prep_rows=prep_rows, out_scale=out_scale),
        grid=(b, hg, n // bq),
        in_specs=[pl.BlockSpec(memory_space=pltpu.SMEM),
                  pl.BlockSpec((1, bq, hw), lambda bi, hi, qi: (bi, qi, hi)),
                  pl.BlockSpec((1, n, hw), lambda bi, hi, qi: (bi, 0, hg + hi)),
                  pl.BlockSpec((1, n, hw), lambda bi, hi, qi: (bi, 0, 2 * hg + hi)),
                  pl.BlockSpec((n, LANES), const), pl.BlockSpec((n, LANES), const),
                  pl.BlockSpec((1, LANES), const), pl.BlockSpec((1, LANES), const),
                  pl.BlockSpec((1, LANES), const)],
        out_specs=pl.BlockSpec((1, bq, hw), lambda bi, hi, qi: (bi, qi, hi)),
        out_shape=jax.ShapeDtypeStruct((b, n, hd), BF16),
        scratch_shapes=[pltpu.VMEM((heads, n, LANES), BF16), pltpu.VMEM((heads, n, 2 * LANES), BF16)],
        compiler_params=pltpu.CompilerParams(
            dimension_semantics=("arbitrary", "arbitrary", "arbitrary"), vmem_limit_bytes=VMEM_LIMIT),
        name="diff_attention",
    )(lam, p, p, p, cos, sin, q_gain, k_gain, out_gain)


def _scan_chunk(t, n_ctx_chunks, n_chunks, reverse):
    if not reverse:
        return t
    return jnp.where(t < n_ctx_chunks, n_ctx_chunks - 1 - t, n_chunks - 1 - (t - n_ctx_chunks))


def _gla_constants(L, reverse):
    nlev = int(math.log2(L))
    idx = np.arange(L)
    i, t = idx[:, None], idx[None, :]
    if reverse:
        i, t = L - 1 - i, L - 1 - t
    rs = [(t <= i), (t > i)]
    am = [(i == t)]
    for lev in range(nlev):
        m = L >> (lev + 1)
        blk_i, blk_t = i // (2 * m), t // (2 * m)
        mid = blk_i * 2 * m + m
        q_role = i >= mid
        rs.append(np.where(q_role, (t >= mid) & (t <= i), (t > i) & (t < mid)) & (blk_i == blk_t))
        am.append((blk_i == blk_t) & q_role & (t < mid))
    return (np.stack(rs).astype(np.float32).reshape((nlev + 2) * L, L),
            np.stack(am).astype(np.float32))


def _gla_kernel(q_ref, k_ref, v_ref, glr_ref, w2_ref, gb_ref, rsum_ref, amask_ref, o_ref, st_ref, *,
                reverse):
    L = q_ref.shape[1]
    nlev = amask_ref.shape[0] - 1
    n_coarse = nlev + 2 - rsum_ref.shape[0] // L

    @pl.when(pl.program_id(1) == 0)
    def _():
        st_ref[...] = jnp.zeros_like(st_ref)

    z = jnp.dot(glr_ref[0].astype(BF16), w2_ref[...], preferred_element_type=F32) + gb_ref[...]
    g = (jnp.minimum(z, 0.0) - jnp.log(1.0 + jnp.exp(-jnp.abs(z)))) * (1.0 / GLA_TAU)
    g1 = g.astype(BF16)
    g2 = (g - g1.astype(F32)).astype(BF16)
    g3 = (g - g1.astype(F32) - g2.astype(F32)).astype(BF16)
    rsum = rsum_ref[...]
    e_all = (jnp.dot(rsum, g1, preferred_element_type=F32) + jnp.dot(rsum, g2, preferred_element_type=F32)
             + jnp.dot(rsum, g3, preferred_element_type=F32))
    b_tot = jnp.sum(g, axis=0, keepdims=True)

    def level_decay(lev, ck):
        if lev >= n_coarse:
            return jnp.exp(e_all[(2 + lev - n_coarse) * L:(3 + lev - n_coarse) * L, ck])
        m = L >> (lev + 1)
        b3 = e_all[0:L, ck].reshape(L // (2 * m), 2 * m, GLA_DK)
        pivot = m if reverse else m - 1
        return jnp.exp(-jnp.abs(b3 - b3[:, pivot:pivot + 1, :])).reshape(L, GLA_DK)

    for h in range(GLA_HEADS):
        ck = slice(h * GLA_DK, (h + 1) * GLA_DK)
        cv = slice(h * GLA_DV, (h + 1) * GLA_DV)
        q = q_ref[0, :, ck] * (GLA_DK ** -0.5)
        k = k_ref[0, :, ck]
        v = v_ref[0, :, cv].astype(BF16)
        st = st_ref[h]
        a = amask_ref[0] * lax.dot_general(q.astype(BF16), k.astype(BF16), NT, preferred_element_type=F32)
        for lev in range(nlev):
            e = level_decay(lev, ck)
            a = a + amask_ref[1 + lev] * lax.dot_general(
                (q * e).astype(BF16), (k * e).astype(BF16), NT, preferred_element_type=F32)
        qe = (q * jnp.exp(e_all[0:L, ck])).astype(BF16)
        o = lax.dot_general(qe, st.astype(BF16), NT, preferred_element_type=F32)
        o = o + jnp.dot(a.astype(BF16), v, preferred_element_type=F32)
        o_ref[0, :, cv] = o
        kd = (k * jnp.exp(e_all[L:2 * L, ck])).astype(BF16)
        st_ref[h] = jnp.exp(b_tot[:, ck]) * st + jnp.dot(
            v.T, kd, preferred_element_type=F32)


def _gla_scan(p, glr, w2, gb, *, n_ctx, reverse, L=GLA_CHUNK):
    b, n, _ = p.shape
    nc, ncc = n // L, n_ctx // L
    rsum, amask = _gla_constants(L, reverse)
    n_coarse = sum(2 * (L >> (lev + 1)) >= SUBLANES for lev in range(amask.shape[0] - 1))
    rsum = rsum.reshape(-1, L, L)
    rsum = np.concatenate([rsum[:2], rsum[2 + n_coarse:]]).reshape(-1, L)
    chunk = functools.partial(_scan_chunk, n_ctx_chunks=ncc, n_chunks=nc, reverse=reverse)
    hk, hv = GLA_HEADS * GLA_DK, GLA_HEADS * GLA_DV
    cols = lambda width, start: pl.BlockSpec((1, L, width), lambda bi, t: (bi, chunk(t), start // width))
    row = lambda bi, t: (bi, chunk(t), 0)
    const2 = lambda bi, t: (0, 0)
    const3 = lambda bi, t: (0, 0, 0)
    return pl.pallas_call(
        functools.partial(_gla_kernel, reverse=reverse),
        grid=(b, nc),
        in_specs=[cols(hk, 0), cols(hk, hk), cols(hv, 2 * hk), cols(LANES, 0),
                  pl.BlockSpec(w2.shape, const2), pl.BlockSpec(gb.shape, const2),
                  pl.BlockSpec(rsum.shape, const2), pl.BlockSpec(amask.shape, const3)],
        out_specs=pl.BlockSpec((1, L, hv), row),
        out_shape=jax.ShapeDtypeStruct((b, n, hv), F32),
        scratch_shapes=[pltpu.VMEM((GLA_HEADS, GLA_DV, GLA_DK), F32)],
        compiler_params=pltpu.CompilerParams(
            dimension_semantics=("arbitrary", "arbitrary"), vmem_limit_bytes=VMEM_LIMIT),
        name="gla_scan_bwd" if reverse else "gla_scan_fwd",
    )(p, p, p, glr, w2, gb, jnp.asarray(rsum, BF16), jnp.asarray(amask))


def _log_sigmoid(x):
    return jnp.minimum(x, 0.0) - jnp.log(1.0 + jnp.exp(-jnp.abs(x)))


def _mlstm_kernel(*refs):
    tri_ref, of_ref, ob_ref, c_ref, m_ref = refs[10:]

    @pl.when(pl.program_id(1) == 0)
    def _():
        c_ref[...] = jnp.zeros_like(c_ref)
        m_ref[...] = jnp.zeros_like(m_ref)

    for d, o_ref in enumerate((of_ref, ob_ref)):
        _mlstm_direction(*refs[5 * d:5 * d + 5], tri_ref[d], o_ref, c_ref, m_ref, d * ML_HEADS, bool(d))


def _mlstm_direction(q_ref, k_ref, v_ref, gc_ref, gr_ref, tri, o_ref, c_ref, m_ref, h0, reverse):
    L = q_ref.shape[1]
    H = ML_HEADS
    ones = jnp.ones((L, LANES), BF16)
    gc = gc_ref[0]
    gr = gr_ref[0]
    ic_col, ic_row = gc[:, :H], gr[:H, :]
    b_col = jnp.dot(tri, _log_sigmoid(gc[:, H:]), precision=HI, preferred_element_type=F32)
    b_row = lax.dot_general(_log_sigmoid(gr[H:, :]), tri, NT, precision=HI, preferred_element_type=F32)
    last = 0 if reverse else L - 1
    causal = tri > 0.5

    for h in range(H):
        ck = slice(h * ML_DK, (h + 1) * ML_DK)
        cv = slice(h * ML_DV, (h + 1) * ML_DV)
        q = q_ref[0, :, ck]
        k = k_ref[0, :, ck]
        v = jnp.concatenate([v_ref[0, :, cv].astype(BF16), ones], axis=1)
        c, m = c_ref[h0 + h], m_ref[h0 + h]
        bc, br = b_col[:, h:h + 1], b_row[h:h + 1, :]
        icc, icr = ic_col[:, h:h + 1], ic_row[h:h + 1, :]
        b_last = bc[last:last + 1, :]

        a = bc + m
        dmat = jnp.where(causal, bc - br + icr, -jnp.inf)
        m_t = jnp.maximum(a, jnp.max(dmat, axis=-1, keepdims=True))
        w_inter = jnp.exp(a - m_t)
        s = lax.dot_general(q, k, NT, preferred_element_type=F32) * jnp.exp(dmat - m_t)
        both = w_inter * jnp.dot(q, c.astype(BF16), preferred_element_type=F32) + jnp.dot(
            s.astype(BF16), v, preferred_element_type=F32)
        den = jnp.maximum(jnp.abs(both[:, ML_DV:]), jnp.exp(-m_t))
        o_ref[0, :, cv] = both[:, :ML_DV] / jnp.concatenate([den] * (ML_DV // LANES), axis=1)

        gs_col = b_last - bc + icc
        gs_row = b_last - br + icr
        m_new = jnp.maximum(b_last + m, jnp.max(gs_row, axis=-1, keepdims=True))
        decay = jnp.exp(b_last + m - m_new)
        wk = jnp.exp(gs_col - m_new) * k.astype(F32)
        c_ref[h0 + h] = decay * c + jnp.dot(wk.astype(BF16).T, v, preferred_element_type=F32)
        m_ref[h0 + h] = m_new


def _mlstm_scan(qk, p, gates, *, v_col, n_ctx, L=ML_CHUNK):
    b, n, _ = qk.shape
    hk, hv, ng = ML_HEADS * ML_DK, ML_HEADS * ML_DV, 2 * ML_HEADS
    nc, ncc = n // L, n_ctx // L
    idx = np.arange(L)
    lower = idx[None, :] <= idx[:, None]
    tri = jnp.asarray(np.stack([lower, lower.T]).astype(np.float32))
    in_specs, operands, out_specs = [], [], []
    for d in range(2):
        chunk = functools.partial(_scan_chunk, n_ctx_chunks=ncc, n_chunks=nc, reverse=bool(d))
        col = lambda j, chunk=chunk: (lambda bi, t: (bi, chunk(t), j))
        g_dir = gates[..., d * ng:(d + 1) * ng]
        in_specs += [pl.BlockSpec((1, L, hk), col(0)), pl.BlockSpec((1, L, hk), col(1)),
                     pl.BlockSpec((1, L, hv), col(v_col)), pl.BlockSpec((1, L, ng), col(0)),
                     pl.BlockSpec((1, ng, L), lambda bi, t, chunk=chunk: (bi, 0, chunk(t)))]
        operands += [qk, qk, p, g_dir, jnp.swapaxes(g_dir, 1, 2)]
        out_specs.append(pl.BlockSpec((1, L, hv), col(0)))
    return pl.pallas_call(
        _mlstm_kernel,
        grid=(b, nc),
        in_specs=in_specs + [pl.BlockSpec((2, L, L), lambda bi, t: (0, 0, 0))],
        out_specs=out_specs,
        out_shape=[jax.ShapeDtypeStruct((b, n, hv), F32)] * 2,
        scratch_shapes=[pltpu.VMEM((2 * ML_HEADS, ML_DK, ML_DV + LANES), F32),
                        pltpu.VMEM((2 * ML_HEADS, 1, 1), F32)],
        compiler_params=pltpu.CompilerParams(
            dimension_semantics=("arbitrary", "arbitrary"), vmem_limit_bytes=VMEM_LIMIT),
        name="mlstm_scan",
    )(*operands, tri)


def _silu(x):
    return x * jax.nn.sigmoid(x)


def _rope_tables(n_lat, n_ctx):
    rows = n_lat // GRID_W
    row = jnp.repeat(jnp.arange(rows, dtype=F32), GRID_W)
    col = jnp.tile(jnp.arange(GRID_W, dtype=F32), rows)
    inv = jnp.power(ROPE_BASE, -jnp.arange(ROPE_PAIRS, dtype=F32) / ROPE_PAIRS)
    ang_r, ang_c = row[:, None] * inv, col[:, None] * inv
    cos = jnp.concatenate([jnp.cos(ang_r)] * 2 + [jnp.cos(ang_c)] * 2, axis=-1)
    sin = jnp.concatenate([-jnp.sin(ang_r), jnp.sin(ang_r), -jnp.sin(ang_c), jnp.sin(ang_c)], axis=-1)
    pad = ((n_ctx, 0), (0, 0))
    cos, sin = jnp.pad(cos, pad, constant_values=1.0), jnp.pad(sin, pad)
    return jnp.tile(cos, (1, 2)), jnp.tile(sin, (1, 2))


def _even_mixer(hn, bsz, w_in, i, gate_w2, gate_b, gla_norm_g, sc_conv_w, seq):
    nq = GLA_HEADS * GLA_DK
    nv = GLA_HEADS * GLA_DV
    gate_col = 2 * nq + 2 * nv
    pa = _matmul(hn, w_in, i, bn=1024, ncols=gate_col, transposed=True)
    glr = _matmul(hn, w_in, i, bn=LANES, col0=gate_col, ncols=LANES, transposed=True)
    pb = _matmul(hn, w_in, i, bn=SC_WIDTH, col0=gate_col, ncols=3 * SC_WIDTH,
                 col_shift=2 * GLA_GATE_RANK, transposed=True)
    pa3 = pa.reshape(bsz, -1, pa.shape[1])
    glr3 = glr.reshape(bsz, -1, LANES)
    outs = []
    for direction in range(2):
        w2 = jnp.zeros((LANES, nq), F32).at[
            direction * GLA_GATE_RANK:(direction + 1) * GLA_GATE_RANK].set(gate_w2[direction])
        outs.append(_gla_scan(pa3, glr3, w2.astype(BF16), gate_b[direction][None, :],
                              n_ctx=seq["n_ctx"], reverse=bool(direction)).reshape(-1, nv))
    gain = jnp.tile(gla_norm_g, GLA_HEADS)[None, :]
    return [_even_finish(outs[0], outs[1], pa, pb, gain, sc_conv_w, **seq)]


def _odd_mixer(hn, bsz, rope, layer, w_in, i, qn_g, kn_g, lam_p, subln_g, ml_conv_w, ml_gate_b,
               ml_norm_g, seq):
    lam_init = 0.8 - 0.6 * math.exp(-0.3 * layer)
    lam = (jnp.exp(jnp.sum(lam_p[0] * lam_p[1])) - jnp.exp(jnp.sum(lam_p[2] * lam_p[3])) + lam_init)
    na = DA_HEADS * 2 * DA_DQK
    nk = ML_HEADS * ML_DK
    nv = ML_HEADS * ML_DV
    n_main = 3 * na + 2 * nk + 2 * nv
    p = _matmul(hn, w_in, i, bn=1024, ncols=n_main, transposed=True)
    mg = _matmul(hn, w_in, i, bn=LANES, col0=n_main, ncols=LANES, transposed=True)[:, :4 * ML_HEADS]
    p3 = p.reshape(bsz, -1, n_main)
    n = p3.shape[1]

    cos, sin = rope
    tile2 = lambda g: jnp.tile(g, 2)[None, :]
    da = _diff_attention(p3, lam.reshape(1).astype(F32), cos, sin, tile2(qn_g), tile2(kn_g),
                         subln_g[None, :], n_ctx=seq["n_ctx"], out_scale=1.0 - lam_init)

    scale = jnp.concatenate([jnp.ones((nk,), F32), jnp.full((nk,), ML_DK ** -0.5, F32)])[None, :]
    hqk = _mlstm_prep(p, ml_conv_w, scale, col=3 * na // (2 * nk), **seq).reshape(bsz, n, 2 * nk)
    gates = (mg + ml_gate_b).reshape(bsz, n, 4 * ML_HEADS)
    outs = [o.reshape(-1, nv) for o in
            _mlstm_scan(hqk, p3, gates, v_col=(3 * na + 2 * nk) // nv, n_ctx=seq["n_ctx"])]
    m = _mlstm_finish(outs[0], outs[1], p, jnp.tile(ml_norm_g, ML_HEADS)[None, :],
                      col=(3 * na + 2 * nk + nv) // nv)
    return [da.reshape(-1, na), m]


def _modulation(c, c_ctx, ada_w, ada_b, layer):
    cc = jnp.concatenate([c, c_ctx[None, :]], axis=0)
    rows = cc.shape[0]
    act = jnp.pad(_silu(cc), ((0, BF16_ROWS - rows), (0, 0))).astype(BF16)
    mod = _matmul(act, ada_w, layer, bn=1024, bm=BF16_ROWS)[:rows] + ada_b[layer]
    return mod.reshape(rows, 6, -1)


def kernel(x, c, ctx, c_ctx, ada_w, ada_b, norm1_g, norm2_g, ev_w_in, ev_w_out, gla_gate_w2, gla_gate_b, gla_norm_g, sc_conv_w, od_w_in, od_w_out, da_qnorm_g, da_knorm_g, da_lambda, da_subln_g, ml_conv_w, ml_gate_b, ml_norm_g, ffn_w_up, ffn_conv_w, ffn_conv_b, ffn_w_down):
    bsz, n_lat, d = x.shape
    n_ctx = ctx.shape[1]
    n = n_ctx + n_lat
    depth = ada_w.shape[0]
    d_ff = ffn_w_down.shape[1]
    assert n % ROW_TILE == 0 and n_ctx <= ROW_CHUNK and d_ff % (4 * LANES) == 0
    rope = _rope_tables(n_lat, n_ctx)
    seq = dict(n_ctx=n_ctx, n_seq=n)

    ev_w_in = jnp.swapaxes(ev_w_in, 1, 2)
    od_w_in = jnp.swapaxes(od_w_in, 1, 2)
    mods = [_modulation(c, c_ctx, ada_w, ada_b, layer) for layer in range(depth)]
    xs = jnp.concatenate([ctx, x], axis=1).reshape(bsz * n, d)
    hn = _first_norm(xs, norm1_g[0][None, :], mods[0], **seq)
    for layer in range(depth):
        i = layer // 2
        if layer % 2 == 0:
            mix = _even_mixer(hn, bsz, ev_w_in, i, gla_gate_w2[i], gla_gate_b[i], gla_norm_g[i],
                              sc_conv_w[i], seq)
            w_out = ev_w_out
        else:
            mix = _odd_mixer(hn, bsz, rope, layer, od_w_in, i, da_qnorm_g[i], da_knorm_g[i],
                             da_lambda[i], da_subln_g[i], ml_conv_w[i], ml_gate_b[i], ml_norm_g[i], seq)
            w_out = od_w_out
        xs, hn = _out_proj(mix, _to_bf16(w_out, i, rows=512), xs, mods[layer], norm2_g[layer][None, :],
                           **seq)
        act, w_down = _ffn_up(hn, ffn_w_up, ffn_w_down, layer, ffn_conv_w[layer],
                              ffn_conv_b[layer][None, :], **seq)
        nxt = min(layer + 1, depth - 1)
        xs, hn = _down_proj(act, w_down, xs,
                            mods[layer], norm1_g[nxt][None, :], mods[nxt], bk=d_ff // 2, **seq)
    return xs.reshape(bsz, n, d)[:, n_ctx:, :]
```

```python
import functools
import math

import jax
import jax.numpy as jnp
import numpy as np
from jax import lax
from jax.experimental import pallas as pl
from jax.experimental.pallas import tpu as pltpu

GRID_W = 64
EPS = 1e-6
GLA_HEADS, GLA_DK, GLA_DV, GLA_GATE_RANK, GLA_TAU = 4, 128, 256, 16, 16.0
SC_WIDTH = 1024
DA_HEADS, DA_DQK, DA_DV = 8, 64, 128
ROPE_BASE = 10000.0
ROPE_PAIRS = DA_DQK // 4
ML_HEADS, ML_DK, ML_DV = 4, 128, 256

GLA_CHUNK = 128
ML_CHUNK = 256
ATTN_BQ = 256
ATTN_CHUNK = 1024
ATTN_PREP = 544
ATTN_HEADS = 2
ROW_TILE = 1088
ROW_CHUNK = 272
LANES = 128
SUBLANES = 8
BF16_ROWS = 16
ROW_GROUP_UNROLL = 8
LOG2E = 1.4426950408889634
VMEM_LIMIT = 56 * 1024 * 1024

F32 = jnp.float32
BF16 = jnp.bfloat16
HI = lax.Precision.HIGHEST
NT = (((1,), (1,)), ((), ()))


def _mm_kernel(x_ref, w_ref, *rest, row_shift, transposed):
    o_ref, wb_ref = rest[-2:]

    @pl.when(pl.program_id(1) == 0)
    def _():
        if not transposed:
            wb_ref[...] = w_ref[...].astype(BF16)
            return
        bn = wb_ref.shape[1]
        piece = 2 * LANES
        for c0 in range(0, bn, piece):
            c1 = min(c0 + piece, bn)
            if row_shift and c1 == bn:
                rows = jnp.concatenate([w_ref[c0 + row_shift:bn, :], rest[0][...]], axis=0)
            else:
                rows = w_ref[c0 + row_shift:c1 + row_shift, :]
            wb_ref[:, c0:c1] = rows.T.astype(BF16)

    o_ref[...] = jnp.dot(x_ref[...], wb_ref[...], preferred_element_type=F32).astype(o_ref.dtype)


def _matmul(x, w, layer, *, bn, col0=0, ncols=None, col_shift=0, transposed=False, bm=ROW_TILE,
            out_dtype=F32):
    m, k = x.shape
    n_all = w.shape[1] if transposed else w.shape[2]
    ncols = n_all - col0 if ncols is None else ncols
    j0 = col0 // bn
    if transposed:
        w_spec = pl.BlockSpec((None, bn, k), lambda j, i: (layer, j0 + j, 0))
    else:
        w_spec = pl.BlockSpec((None, k, bn), lambda j, i: (layer, 0, j0 + j))
    in_specs = [pl.BlockSpec((bm, k), lambda j, i: (i, 0)), w_spec]
    operands = [x, w]
    if col_shift:
        per = bn // col_shift
        in_specs.append(pl.BlockSpec((None, col_shift, k), lambda j, i: (layer, (j0 + j + 1) * per, 0)))
        operands.append(w)
    return pl.pallas_call(
        functools.partial(_mm_kernel, row_shift=col_shift, transposed=transposed),
        grid=(pl.cdiv(ncols, bn), m // bm),
        in_specs=in_specs,
        out_specs=pl.BlockSpec((bm, bn), lambda j, i: (i, j)),
        out_shape=jax.ShapeDtypeStruct((m, ncols), out_dtype),
        scratch_shapes=[pltpu.VMEM((k, bn), BF16)],
        compiler_params=pltpu.CompilerParams(
            dimension_semantics=("arbitrary", "arbitrary"), vmem_limit_bytes=VMEM_LIMIT),
        name="matmul",
    )(*operands)


def _norm_mod(x, gain_scale, shift):
    return x * lax.rsqrt(jnp.mean(x * x, axis=-1, keepdims=True) + EPS) * gain_scale + shift


def _row_vectors(modb_ref, modc_ref, idx, gain_ref=None):
    def vec(ref):
        row = ref[0, idx:idx + 1, :]
        if gain_ref is not None:
            row = gain_ref[...] * (1.0 + row)
        return jnp.broadcast_to(row, (BF16_ROWS, row.shape[1]))

    lat, ctx = vec(modb_ref), vec(modc_ref)
    return lambda is_ctx: lat if is_ctx is None else jnp.where(is_ctx, ctx, lat)


def _head_rmsnorm(x, width):
    parts = []
    for c0 in range(0, x.shape[1], width):
        xh = x[:, c0:c0 + width]
        parts.append(xh * lax.rsqrt(jnp.mean(xh * xh, axis=-1, keepdims=True) + EPS))
    return parts[0] if len(parts) == 1 else jnp.concatenate(parts, axis=1)


def _sigmoid(x):
    return 1.0 / (1.0 + jnp.exp(-x))


def _chunk_rows(r0, rc, tiles_per_batch, bm):
    rows = r0 + lax.broadcasted_iota(jnp.int32, (rc, 1), 0)
    return rows, (pl.program_id(0) % tiles_per_batch) * bm + rows


def _edge_rows(r0, rc, bm, tiles_per_batch, n_ctx, n_seq, gate_ref, hprev_ref, hnext_ref):
    hr = hprev_ref.shape[0]
    pos0 = (pl.program_id(0) % tiles_per_batch) * bm
    if r0 == 0:
        before = hprev_ref[...].astype(F32)[hr - 1:hr, :]
        before = jnp.where((pos0 == 0) | (pos0 == n_ctx), 0.0, before)
    else:
        before = gate_ref[r0 - hr:r0, :].astype(F32)[hr - 1:hr, :]
    if r0 + rc == bm:
        after = hnext_ref[...].astype(F32)[0:1, :]
        after = jnp.where((pos0 + bm == n_ctx) | (pos0 + bm == n_seq), 0.0, after)
    else:
        after = gate_ref[r0 + rc:r0 + rc + hr, :].astype(F32)[0:1, :]
    return before, after


def _splits_rows(r0, rc, bm, tiles_per_batch, n_ctx):
    return any(r0 <= n_ctx - k * bm <= r0 + rc for k in range(tiles_per_batch))


def _dwconv3(s_ref, g, before, after, cw_ref, pos, n_ctx):
    rc = g.shape[0]
    s_ref[SUBLANES:SUBLANES + rc, :] = g
    s_ref[SUBLANES - 1:SUBLANES, :] = before
    s_ref[SUBLANES + rc:SUBLANES + rc + 1, :] = after
    prev = s_ref[SUBLANES - 1:SUBLANES - 1 + rc, :]
    nxt = s_ref[SUBLANES + 1:SUBLANES + 1 + rc, :]
    if pos is not None:
        prev = jnp.where(pos == n_ctx, 0.0, prev)
        nxt = jnp.where(pos == n_ctx - 1, 0.0, nxt)
    return prev * cw_ref[0:1, :] + g * cw_ref[1:2, :] + nxt * cw_ref[2:3, :]


def _mod_specs(n_batch, tiles_per_batch, d):
    return [pl.BlockSpec((1, 6, d), lambda i, *_: (i // tiles_per_batch, 0, 0)),
            pl.BlockSpec((1, 6, d), lambda i, *_: (n_batch, 0, 0))]


def _halo_specs(bm, width, col, n_rows, hr):
    hb = bm // hr
    last = n_rows // hr - 1
    return [pl.BlockSpec((hr, width), lambda i, *k: (jnp.maximum(i * hb - 1, 0), col(*k))),
            pl.BlockSpec((hr, width), lambda i, *k: (jnp.minimum((i + 1) * hb, last), col(*k)))]


def _norm_kernel(x_ref, gain_ref, modb_ref, modc_ref, o_ref, *, n_ctx, tiles_per_batch):
    gain_scale = _row_vectors(modb_ref, modc_ref, 1, gain_ref)
    shift = _row_vectors(modb_ref, modc_ref, 0)

    def rows(rs, is_ctx):
        o_ref[rs, :] = _norm_mod(x_ref[rs, :], gain_scale(is_ctx), shift(is_ctx)).astype(BF16)

    _for_row_groups(x_ref.shape[0], n_ctx, tiles_per_batch, rows)


def _first_norm(x, gain, mod, *, n_ctx, n_seq, bm=ROW_TILE):
    m, d = x.shape
    tpb = n_seq // bm
    return pl.pallas_call(
        functools.partial(_norm_kernel, n_ctx=n_ctx, tiles_per_batch=tpb),
        grid=(m // bm,),
        in_specs=[pl.BlockSpec((bm, d), lambda i: (i, 0)), pl.BlockSpec((1, d), lambda i: (0, 0))]
        + _mod_specs(mod.shape[0] - 1, tpb, d),
        out_specs=pl.BlockSpec((bm, d), lambda i: (i, 0)),
        out_shape=jax.ShapeDtypeStruct((m, d), BF16),
        compiler_params=pltpu.CompilerParams(vmem_limit_bytes=VMEM_LIMIT),
        name="first_norm",
    )(x, gain, mod, mod)


def _residual_norm(xo_ref, hn_ref, x_ref, modb_ref, modc_ref, gain_ref, nmodb_ref, nmodc_ref, *,
                   n_ctx, tiles_per_batch, gate_idx, shift_idx, inline=False):
    gate = _row_vectors(modb_ref, modc_ref, gate_idx)
    gain_scale = _row_vectors(nmodb_ref, nmodc_ref, shift_idx + 1, gain_ref)
    shift = _row_vectors(nmodb_ref, nmodc_ref, shift_idx)

    def rows(rs, is_ctx):
        x_new = x_ref[rs, :] + gate(is_ctx) * xo_ref[rs, :]
        xo_ref[rs, :] = x_new
        hn_ref[rs, :] = _norm_mod(x_new, gain_scale(is_ctx), shift(is_ctx)).astype(BF16)

    _for_row_groups(x_ref.shape[0], n_ctx, tiles_per_batch, rows, inline)


def _for_row_groups(bm, n_ctx, tiles_per_batch, fn, inline=False):
    rg = BF16_ROWS
    pos0 = (pl.program_id(0) % tiles_per_batch) * bm
    n_lead = -(-min(n_ctx, bm) // rg)
    if inline:
        for g in range(bm // rg):
            is_ctx = (pos0 + g * rg + lax.broadcasted_iota(jnp.int32, (rg, 1), 0) < n_ctx) if g < n_lead else None
            fn(slice(g * rg, (g + 1) * rg), is_ctx)
        return

    def group(with_ctx):
        def body(g, carry):
            r0 = pl.multiple_of(g * rg, rg)
            is_ctx = (pos0 + r0 + lax.broadcasted_iota(jnp.int32, (rg, 1), 0) < n_ctx) if with_ctx else None
            fn(pl.ds(r0, rg), is_ctx)
            return carry
        return body

    lax.fori_loop(0, n_lead, group(True), 0, unroll=ROW_GROUP_UNROLL)
    lax.fori_loop(n_lead, bm // rg, group(False), 0, unroll=ROW_GROUP_UNROLL)


def _out_kernel(*refs, n_in, n_ctx, tiles_per_batch):
    a_refs = refs[:n_in]
    w_ref, x_ref, modb_ref, modc_ref, gain_ref, xo_ref, hn_ref, wb_ref = refs[n_in:]

    @pl.when(pl.program_id(0) == 0)
    def _():
        for r0 in range(0, w_ref.shape[0], ROW_CHUNK):
            r1 = min(r0 + ROW_CHUNK, w_ref.shape[0])
            wb_ref[r0:r1, :] = w_ref[r0:r1, :].astype(BF16)

    k0 = 0
    for n_done, a_ref in enumerate(a_refs):
        part = jnp.dot(a_ref[...], wb_ref[k0:k0 + a_ref.shape[1], :], preferred_element_type=F32)
        xo_ref[...] = part if n_done == 0 else xo_ref[...] + part
        k0 += a_ref.shape[1]
    _residual_norm(xo_ref, hn_ref, x_ref, modb_ref, modc_ref, gain_ref, modb_ref, modc_ref,
                   n_ctx=n_ctx, tiles_per_batch=tiles_per_batch, gate_idx=2, shift_idx=3, inline=True)


def _out_proj(acts, w, layer, x, mod, gain, *, n_ctx, n_seq, bm=ROW_TILE // 2):
    m, d = x.shape
    kk = w.shape[1]
    tpb = n_seq // bm
    rows = lambda width: pl.BlockSpec((bm, width), lambda i: (i, 0))
    return pl.pallas_call(
        functools.partial(_out_kernel, n_in=len(acts), n_ctx=n_ctx, tiles_per_batch=tpb),
        grid=(m // bm,),
        in_specs=[rows(a.shape[1]) for a in acts]
        + [pl.BlockSpec((None, kk, d), lambda i: (layer, 0, 0), pipeline_mode=pl.Buffered(1)), rows(d)]
        + _mod_specs(mod.shape[0] - 1, tpb, d) + [pl.BlockSpec((1, d), lambda i: (0, 0))],
        out_specs=[rows(d), rows(d)],
        out_shape=[jax.ShapeDtypeStruct((m, d), F32), jax.ShapeDtypeStruct((m, d), BF16)],
        scratch_shapes=[pltpu.VMEM((kk, d), BF16)],
        compiler_params=pltpu.CompilerParams(
            dimension_semantics=("arbitrary",), vmem_limit_bytes=VMEM_LIMIT),
        name="out_proj",
    )(*acts, w, x, mod, mod, gain)


def _up_kernel(x_ref, xp_ref, xn_ref, wg_ref, wv_ref, cw_ref, cb_ref, wd_ref, o_ref, wdb_ref, wgb_ref,
               wvb_ref, xe_ref, s_ref, v_ref, *, n_ctx, n_seq, tiles_per_batch):
    i = pl.program_id(1)
    bm, rc, hr = x_ref.shape[0], ROW_CHUNK, xp_ref.shape[0]
    wdb_ref[...] = wd_ref[...].astype(BF16)

    @pl.when(i == 0)
    def _():
        wgb_ref[...] = wg_ref[...].astype(BF16)
        wvb_ref[...] = wv_ref[...].astype(BF16)

    xe_ref[0:hr, :] = xp_ref[...]
    xe_ref[hr:hr + bm, :] = x_ref[...]
    xe_ref[hr + bm:, :] = xn_ref[...]
    s_ref[...] = jnp.dot(xe_ref[...], wgb_ref[...], preferred_element_type=F32)
    v_ref[...] = jnp.dot(x_ref[...], wvb_ref[...], preferred_element_type=F32)
    pos0 = (i % tiles_per_batch) * bm
    s_ref[hr - 1:hr, :] = jnp.where((pos0 == 0) | (pos0 == n_ctx), 0.0, s_ref[hr - 1:hr, :])
    s_ref[hr + bm:hr + bm + 1, :] = jnp.where(
        (pos0 + bm == n_ctx) | (pos0 + bm == n_seq), 0.0, s_ref[hr + bm:hr + bm + 1, :])
    for r0 in range(0, bm, rc):
        rs = slice(r0, r0 + rc)
        prev = s_ref[hr - 1 + r0:hr - 1 + r0 + rc, :]
        nxt = s_ref[hr + 1 + r0:hr + 1 + r0 + rc, :]
        if _splits_rows(r0, rc, bm, tiles_per_batch, n_ctx):
            pos = pos0 + r0 + lax.broadcasted_iota(jnp.int32, (rc, 1), 0)
            prev = jnp.where(pos == n_ctx, 0.0, prev)
            nxt = jnp.where(pos == n_ctx - 1, 0.0, nxt)
        z = (prev * cw_ref[0:1, :] + s_ref[hr + r0:hr + r0 + rc, :] * cw_ref[1:2, :]
             + nxt * cw_ref[2:3, :] + cb_ref[...])
        o_ref[rs, :] = (z * _sigmoid(z) * v_ref[rs, :]).astype(BF16)


def _ffn_up(x, w, w_down, layer, conv_w, conv_b, *, n_ctx, n_seq, bn=512, bm=ROW_TILE):
    m, k = x.shape
    f = w.shape[2] // 2
    nj, ni = f // bn, m // bm
    slab = f // (nj * ni)
    assert slab * nj * ni == f and slab % BF16_ROWS == 0
    d_out = w_down.shape[2]
    tpb = n_seq // bm
    hb = bm // BF16_ROWS
    last = m // BF16_ROWS - 1
    return pl.pallas_call(
        functools.partial(_up_kernel, n_ctx=n_ctx, n_seq=n_seq, tiles_per_batch=tpb),
        grid=(nj, ni),
        in_specs=[pl.BlockSpec((bm, k), lambda j, i: (i, 0)),
                  pl.BlockSpec((BF16_ROWS, k), lambda j, i: (jnp.maximum(i * hb - 1, 0), 0)),
                  pl.BlockSpec((BF16_ROWS, k), lambda j, i: (jnp.minimum((i + 1) * hb, last), 0)),
                  pl.BlockSpec((None, k, bn), lambda j, i: (layer, 0, j)),
                  pl.BlockSpec((None, k, bn), lambda j, i: (layer, 0, j + nj)),
                  pl.BlockSpec((3, bn), lambda j, i: (0, j)), pl.BlockSpec((1, bn), lambda j, i: (0, j)),
                  pl.BlockSpec((None, slab, d_out), lambda j, i: (layer, j * ni + i, 0))],
        out_specs=[pl.BlockSpec((bm, bn), lambda j, i: (i, j)),
                   pl.BlockSpec((slab, d_out), lambda j, i: (j * ni + i, 0))],
        out_shape=[jax.ShapeDtypeStruct((m, f), BF16), jax.ShapeDtypeStruct((f, d_out), BF16)],
        scratch_shapes=[pltpu.VMEM((k, bn), BF16), pltpu.VMEM((k, bn), BF16),
                        pltpu.VMEM((bm + 2 * BF16_ROWS, k), BF16),
                        pltpu.VMEM((bm + 2 * BF16_ROWS, bn), F32), pltpu.VMEM((bm, bn), F32)],
        compiler_params=pltpu.CompilerParams(
            dimension_semantics=("arbitrary", "arbitrary"), vmem_limit_bytes=VMEM_LIMIT),
        name="ffn_up",
    )(x, x, x, w, w, conv_w, conv_b, w_down)


def _down_kernel(a_ref, w_ref, x_ref, modb_ref, modc_ref, gain_ref, nmodb_ref, nmodc_ref, xo_ref, hn_ref, *,
                 n_ctx, tiles_per_batch):
    part = jnp.dot(a_ref[...], w_ref[...], preferred_element_type=F32)

    @pl.when(pl.program_id(1) == 0)
    def _():
        xo_ref[...] = part

    @pl.when(pl.program_id(1) > 0)
    def _():
        xo_ref[...] += part

    @pl.when(pl.program_id(1) == pl.num_programs(1) - 1)
    def _():
        _residual_norm(xo_ref, hn_ref, x_ref, modb_ref, modc_ref, gain_ref, nmodb_ref, nmodc_ref,
                       n_ctx=n_ctx, tiles_per_batch=tiles_per_batch, gate_idx=5, shift_idx=0)


def _down_proj(a, w, x, mod, gain, next_mod, *, n_ctx, n_seq, bm=ROW_TILE // 2, bk=1408):
    m, f = a.shape
    d = w.shape[1]
    tpb = n_seq // bm
    row = lambda i, k: (i, 0)
    return pl.pallas_call(
        functools.partial(_down_kernel, n_ctx=n_ctx, tiles_per_batch=tpb),
        grid=(m // bm, f // bk),
        in_specs=[pl.BlockSpec((bm, bk), lambda i, k: (i, k)), pl.BlockSpec((bk, d), lambda i, k: (k, 0)),
                  pl.BlockSpec((bm, d), row)]
        + _mod_specs(mod.shape[0] - 1, tpb, d) + [pl.BlockSpec((1, d), lambda i, k: (0, 0))]
        + _mod_specs(mod.shape[0] - 1, tpb, d),
        out_specs=[pl.BlockSpec((bm, d), row), pl.BlockSpec((bm, d), row)],
        out_shape=[jax.ShapeDtypeStruct((m, d), F32), jax.ShapeDtypeStruct((m, d), BF16)],
        compiler_params=pltpu.CompilerParams(
            dimension_semantics=("arbitrary", "arbitrary"), vmem_limit_bytes=VMEM_LIMIT),
        name="down_proj",
    )(a, w, x, mod, mod, gain, next_mod, next_mod)


def _even_finish_kernel(of_ref, ob_ref, r_ref, sx_ref, sb_ref, sg_ref, sxp_ref, sxn_ref, sgp_ref, sgn_ref,
                        gain_ref, cw_ref, o_ref, s_ref, *, n_ctx, n_seq, tiles_per_batch):
    bm = of_ref.shape[0]
    nv = of_ref.shape[1]
    r = r_ref[...]
    o_ref[:, :nv] = (_head_rmsnorm(of_ref[...] + ob_ref[...], GLA_DV) * gain_ref[...]
                     * (r * _sigmoid(r))).astype(BF16)

    _, pos = _chunk_rows(0, bm, tiles_per_batch, bm)
    pos0 = (pl.program_id(0) % tiles_per_batch) * bm
    hr = sxp_ref.shape[0]
    before = (sxp_ref[...] * sgp_ref[...])[hr - 1:hr, :]
    before = jnp.where((pos0 == 0) | (pos0 == n_ctx), 0.0, before)
    after = (sxn_ref[...] * sgn_ref[...])[0:1, :]
    after = jnp.where((pos0 + bm == n_ctx) | (pos0 + bm == n_seq), 0.0, after)
    split = pos if _splits_rows(0, bm, bm, tiles_per_batch, n_ctx) else None
    conv = _dwconv3(s_ref, sg_ref[...] * sx_ref[...], before, after, cw_ref, split, n_ctx)
    o_ref[:, nv:] = (sb_ref[...] * conv).astype(BF16)


def _even_finish(o_f, o_b, pa, pb, gain, conv_w, *, n_ctx, n_seq, bm=ROW_CHUNK):
    m, nv = o_f.shape
    sc = conv_w.shape[1]
    tpb = n_seq // bm
    col = lambda j: (lambda i: (i, j))
    const = lambda i: (0, 0)
    return pl.pallas_call(
        functools.partial(_even_finish_kernel, n_ctx=n_ctx, n_seq=n_seq, tiles_per_batch=tpb),
        grid=(m // bm,),
        in_specs=[pl.BlockSpec((bm, nv), col(0)), pl.BlockSpec((bm, nv), col(0)),
                  pl.BlockSpec((bm, nv), col(2)),
                  pl.BlockSpec((bm, sc), col(0)), pl.BlockSpec((bm, sc), col(1)), pl.BlockSpec((bm, sc), col(2))]
        + _halo_specs(bm, sc, lambda: 0, m, SUBLANES) + _halo_specs(bm, sc, lambda: 2, m, SUBLANES)
        + [pl.BlockSpec((1, nv), const), pl.BlockSpec((3, sc), const)],
        out_specs=pl.BlockSpec((bm, nv + sc), col(0)),
        out_shape=jax.ShapeDtypeStruct((m, nv + sc), BF16),
        scratch_shapes=[pltpu.VMEM((bm + 2 * SUBLANES, sc), F32)],
        compiler_params=pltpu.CompilerParams(vmem_limit_bytes=VMEM_LIMIT),
        name="even_finish",
    )(o_f, o_b, pa, pb, pb, pb, pb, pb, pb, pb, gain, conv_w)


def _mlstm_prep_kernel(x_ref, xp_ref, xn_ref, cw_ref, scale_ref, o_ref, s_ref, *, n_ctx, n_seq,
                       tiles_per_batch):
    bm = x_ref.shape[0]
    _, pos = _chunk_rows(0, bm, tiles_per_batch, bm)
    before, after = _edge_rows(0, bm, bm, tiles_per_batch, n_ctx, n_seq, x_ref, xp_ref, xn_ref)
    split = pos if _splits_rows(0, bm, bm, tiles_per_batch, n_ctx) else None
    z = _dwconv3(s_ref, x_ref[...], before, after, cw_ref, split, n_ctx)
    o_ref[...] = (z * _sigmoid(z) * scale_ref[...]).astype(BF16)


def _mlstm_prep(p, conv_w, scale, *, col, n_ctx, n_seq, bm=ROW_CHUNK):
    m = p.shape[0]
    width = conv_w.shape[1]
    tpb = n_seq // bm
    const = lambda i: (0, 0)
    return pl.pallas_call(
        functools.partial(_mlstm_prep_kernel, n_ctx=n_ctx, n_seq=n_seq, tiles_per_batch=tpb),
        grid=(m // bm,),
        in_specs=[pl.BlockSpec((bm, width), lambda i: (i, col))]
        + _halo_specs(bm, width, lambda: col, m, SUBLANES)
        + [pl.BlockSpec((3, width), const), pl.BlockSpec((1, width), const)],
        out_specs=pl.BlockSpec((bm, width), lambda i: (i, 0)),
        out_shape=jax.ShapeDtypeStruct((m, width), BF16),
        scratch_shapes=[pltpu.VMEM((bm + 2 * SUBLANES, width), F32)],
        compiler_params=pltpu.CompilerParams(vmem_limit_bytes=VMEM_LIMIT),
        name="mlstm_prep",
    )(p, p, p, conv_w, scale)


def _mlstm_finish_kernel(hf_ref, hb_ref, mo_ref, gain_ref, o_ref):
    o_ref[...] = (_head_rmsnorm(hf_ref[...] + hb_ref[...], ML_DV) * gain_ref[...]
                  * _sigmoid(mo_ref[...])).astype(BF16)


def _mlstm_finish(h_f, h_b, p, gain, *, col, bm=ROW_TILE // 2):
    m, nv = h_f.shape
    return pl.pallas_call(
        _mlstm_finish_kernel,
        grid=(m // bm,),
        in_specs=[pl.BlockSpec((bm, nv), lambda i: (i, 0)), pl.BlockSpec((bm, nv), lambda i: (i, 0)),
                  pl.BlockSpec((bm, nv), lambda i: (i, col)), pl.BlockSpec((1, nv), lambda i: (0, 0))],
        out_specs=pl.BlockSpec((bm, nv), lambda i: (i, 0)),
        out_shape=jax.ShapeDtypeStruct((m, nv), BF16),
        compiler_params=pltpu.CompilerParams(vmem_limit_bytes=VMEM_LIMIT),
        name="mlstm_finish",
    )(h_f, h_b, p, gain)


def _qk_prep(x, cos, sin, gain):
    lane = lax.broadcasted_iota(jnp.int32, x.shape, 1)
    lo = lane < DA_DQK
    x2 = x * x
    ss_lo = jnp.sum(jnp.where(lo, x2, 0.0), axis=-1, keepdims=True)
    ss_hi = jnp.sum(jnp.where(lo, 0.0, x2), axis=-1, keepdims=True)
    y = x * lax.rsqrt(jnp.where(lo, ss_lo, ss_hi) * (1.0 / DA_DQK) + EPS) * gain
    swapped = jnp.where((lane & ROPE_PAIRS) == 0, pltpu.roll(y, LANES - ROPE_PAIRS, 1),
                        pltpu.roll(y, ROPE_PAIRS, 1))
    return y * cos + swapped * sin


def _attn_kernel(lam_ref, q_ref, k_ref, v_ref, cos_ref, sin_ref, qg_ref, kg_ref, og_ref, o_ref,
                 kb_ref, vb_ref, *, n_ctx, chunk, prep_rows, out_scale):
    qi = pl.program_id(2)
    lam = lam_ref[0]
    n = k_ref.shape[1]
    bq = q_ref.shape[1]
    heads = q_ref.shape[2] // LANES

    @pl.when(qi == 0)
    def _():
        for r0 in range(0, n, prep_rows):
            rs = slice(r0, r0 + prep_rows)
            for h in range(heads):
                hs = slice(h * LANES, (h + 1) * LANES)
                kb_ref[h, rs, :] = _qk_prep(k_ref[0, rs, hs], cos_ref[rs, :], sin_ref[rs, :],
                                            kg_ref[...]).astype(BF16)
                vb_ref[h, rs, :LANES] = v_ref[0, rs, hs].astype(BF16)
                vb_ref[h, rs, LANES:] = jnp.ones((prep_rows, LANES), BF16)

    def attend(r0, r1, nk):
        nr = r1 - r0
        rows = pl.ds(pl.multiple_of(qi * bq, BF16_ROWS) + r0, nr)
        cos, sin = cos_ref[rows, :], sin_ref[rows, :]
        lane = lax.broadcasted_iota(jnp.int32, (nr, LANES), 1)
        q2, m, acc = [], [], []
        for h in range(heads):
            q = _qk_prep(q_ref[0, r0:r1, h * LANES:(h + 1) * LANES], cos, sin, qg_ref[...])
            q = (q * (DA_DQK ** -0.5 * LOG2E)).astype(BF16)
            zero = jnp.zeros_like(q)
            q2.append(jnp.concatenate([jnp.where(lane < DA_DQK, q, zero),
                                       jnp.where(lane >= DA_DQK, q, zero)], axis=0))
            m.append(jnp.full((2 * nr, 1), -jnp.inf, F32))
            acc.append(jnp.zeros((2 * nr, 2 * LANES), F32))
        for c0 in range(0, nk, chunk):
            c1 = min(c0 + chunk, nk)
            for h in range(heads):
                s = lax.dot_general(q2[h], kb_ref[h, c0:c1, :], NT, preferred_element_type=F32)
                m_new = jnp.maximum(m[h], jnp.max(s, axis=-1, keepdims=True))
                p = jnp.exp2(s - m_new)
                acc[h] = jnp.exp2(m[h] - m_new) * acc[h] + jnp.dot(
                    p.astype(BF16), vb_ref[h, c0:c1, :], preferred_element_type=F32)
                m[h] = m_new
        for h in range(heads):
            o = acc[h][:, :LANES] / acc[h][:, LANES:]
            o = o[:nr] - lam * o[nr:]
            o = o * lax.rsqrt(jnp.mean(o * o, axis=-1, keepdims=True) + EPS) * og_ref[...] * out_scale
            o_ref[0, r0:r1, h * LANES:(h + 1) * LANES] = o.astype(BF16)

    @pl.when(qi == 0)
    def _():
        attend(0, n_ctx, n_ctx)
        if n_ctx < bq:
            attend(n_ctx, bq, n)

    @pl.when(qi > 0)
    def _():
        attend(0, bq, n)


def _diff_attention(p, lam, cos, sin, q_gain, k_gain, out_gain, *, n_ctx, out_scale, bq=ATTN_BQ,
                    chunk=ATTN_CHUNK, prep_rows=ATTN_PREP, heads=ATTN_HEADS):
    b, n, _ = p.shape
    hg = DA_HEADS // heads
    hd = DA_HEADS * LANES
    hw = heads * LANES
    const = lambda bi, hi, qi: (0, 0)
    return pl.pallas_call(
        functools.partial(_attn_kernel, n_ctx=n_ctx, chunk=chunk, prep_rows=prep_rows, out_scale=out_scale),
        grid=(b, hg, n // bq),
        in_specs=[pl.BlockSpec(memory_space=pltpu.SMEM),
                  pl.BlockSpec((1, bq, hw), lambda bi, hi, qi: (bi, qi, hi)),
                  pl.BlockSpec((1, n, hw), lambda bi, hi, qi: (bi, 0, hg + hi)),
                  pl.BlockSpec((1, n, hw), lambda bi, hi, qi: (bi, 0, 2 * hg + hi)),
                  pl.BlockSpec((n, LANES), const), pl.BlockSpec((n, LANES), const),
                  pl.BlockSpec((1, LANES), const), pl.BlockSpec((1, LANES), const),
                  pl.BlockSpec((1, LANES), const)],
        out_specs=pl.BlockSpec((1, bq, hw), lambda bi, hi, qi: (bi, qi, hi)),
        out_shape=jax.ShapeDtypeStruct((b, n, hd), BF16),
        scratch_shapes=[pltpu.VMEM((heads, n, LANES), BF16), pltpu.VMEM((heads, n, 2 * LANES), BF16)],
        compiler_params=pltpu.CompilerParams(
            dimension_semantics=("arbitrary", "arbitrary", "arbitrary"), vmem_limit_bytes=VMEM_LIMIT),
        name="diff_attention",
    )(lam, p, p, p, cos, sin, q_gain, k_gain, out_gain)


def _scan_chunk(t, n_ctx_chunks, n_chunks, reverse):
    if not reverse:
        return t
    return jnp.where(t < n_ctx_chunks, n_ctx_chunks - 1 - t, n_chunks - 1 - (t - n_ctx_chunks))


def _gla_constants(L, reverse):
    nlev = int(math.log2(L))
    idx = np.arange(L)
    i, t = idx[:, None], idx[None, :]
    if reverse:
        i, t = L - 1 - i, L - 1 - t
    rs = [(t <= i), (t > i)]
    am = [(i == t)]
    for lev in range(nlev):
        m = L >> (lev + 1)
        blk_i, blk_t = i // (2 * m), t // (2 * m)
        mid = blk_i * 2 * m + m
        q_role = i >= mid
        rs.append(np.where(q_role, (t >= mid) & (t <= i), (t > i) & (t < mid)) & (blk_i == blk_t))
        am.append((blk_i == blk_t) & q_role & (t < mid))
    return (np.stack(rs).astype(np.float32).reshape((nlev + 2) * L, L),
            np.stack(am).astype(np.float32))


def _gla_kernel(q_ref, k_ref, v_ref, glr_ref, w2_ref, gb_ref, rsum_ref, amask_ref, o_ref, st_ref, *,
                reverse):
    L = q_ref.shape[1]
    nlev = amask_ref.shape[0] - 1
    n_coarse = nlev + 2 - rsum_ref.shape[0] // L

    @pl.when(pl.program_id(1) == 0)
    def _():
        st_ref[...] = jnp.zeros_like(st_ref)

    z = jnp.dot(glr_ref[0].astype(BF16), w2_ref[...], preferred_element_type=F32) + gb_ref[...]
    g = (jnp.minimum(z, 0.0) - jnp.log(1.0 + jnp.exp(-jnp.abs(z)))) * (1.0 / GLA_TAU)
    g1 = g.astype(BF16)
    g2 = (g - g1.astype(F32)).astype(BF16)
    g3 = (g - g1.astype(F32) - g2.astype(F32)).astype(BF16)
    rsum = rsum_ref[...]
    e_all = (jnp.dot(rsum, g1, preferred_element_type=F32) + jnp.dot(rsum, g2, preferred_element_type=F32)
             + jnp.dot(rsum, g3, preferred_element_type=F32))
    b_tot = jnp.sum(g, axis=0, keepdims=True)

    def level_decay(lev, ck):
        if lev >= n_coarse:
            return jnp.exp(e_all[(2 + lev - n_coarse) * L:(3 + lev - n_coarse) * L, ck])
        m = L >> (lev + 1)
        b3 = e_all[0:L, ck].reshape(L // (2 * m), 2 * m, GLA_DK)
        pivot = m if reverse else m - 1
        return jnp.exp(-jnp.abs(b3 - b3[:, pivot:pivot + 1, :])).reshape(L, GLA_DK)

    for h in range(GLA_HEADS):
        ck = slice(h * GLA_DK, (h + 1) * GLA_DK)
        cv = slice(h * GLA_DV, (h + 1) * GLA_DV)
        q = q_ref[0, :, ck] * (GLA_DK ** -0.5)
        k = k_ref[0, :, ck]
        v = v_ref[0, :, cv].astype(BF16)
        st = st_ref[h]
        a = amask_ref[0] * lax.dot_general(q.astype(BF16), k.astype(BF16), NT, preferred_element_type=F32)
        for lev in range(nlev):
            e = level_decay(lev, ck)
            a = a + amask_ref[1 + lev] * lax.dot_general(
                (q * e).astype(BF16), (k * e).astype(BF16), NT, preferred_element_type=F32)
        qe = (q * jnp.exp(e_all[0:L, ck])).astype(BF16)
        o = lax.dot_general(qe, st.astype(BF16), NT, preferred_element_type=F32)
        o = o + jnp.dot(a.astype(BF16), v, preferred_element_type=F32)
        o_ref[0, :, cv] = o
        kd = (k * jnp.exp(e_all[L:2 * L, ck])).astype(BF16)
        st_ref[h] = jnp.exp(b_tot[:, ck]) * st + jnp.dot(
            v.T, kd, preferred_element_type=F32)


def _gla_scan(p, glr, w2, gb, *, n_ctx, reverse, L=GLA_CHUNK):
    b, n, _ = p.shape
    nc, ncc = n // L, n_ctx // L
    rsum, amask = _gla_constants(L, reverse)
    n_coarse = sum(2 * (L >> (lev + 1)) >= SUBLANES for lev in range(amask.shape[0] - 1))
    rsum = rsum.reshape(-1, L, L)
    rsum = np.concatenate([rsum[:2], rsum[2 + n_coarse:]]).reshape(-1, L)
    chunk = functools.partial(_scan_chunk, n_ctx_chunks=ncc, n_chunks=nc, reverse=reverse)
    hk, hv = GLA_HEADS * GLA_DK, GLA_HEADS * GLA_DV
    cols = lambda width, start: pl.BlockSpec((1, L, width), lambda bi, t: (bi, chunk(t), start // width))
    row = lambda bi, t: (bi, chunk(t), 0)
    const2 = lambda bi, t: (0, 0)
    const3 = lambda bi, t: (0, 0, 0)
    return pl.pallas_call(
        functools.partial(_gla_kernel, reverse=reverse),
        grid=(b, nc),
        in_specs=[cols(hk, 0), cols(hk, hk), cols(hv, 2 * hk), cols(LANES, 0),
                  pl.BlockSpec(w2.shape, const2), pl.BlockSpec(gb.shape, const2),
                  pl.BlockSpec(rsum.shape, const2), pl.BlockSpec(amask.shape, const3)],
        out_specs=pl.BlockSpec((1, L, hv), row),
        out_shape=jax.ShapeDtypeStruct((b, n, hv), F32),
        scratch_shapes=[pltpu.VMEM((GLA_HEADS, GLA_DV, GLA_DK), F32)],
        compiler_params=pltpu.CompilerParams(
            dimension_semantics=("arbitrary", "arbitrary"), vmem_limit_bytes=VMEM_LIMIT),
        name="gla_scan_bwd" if reverse else "gla_scan_fwd",
    )(p, p, p, glr, w2, gb, jnp.asarray(rsum, BF16), jnp.asarray(amask))


def _log_sigmoid(x):
    return jnp.minimum(x, 0.0) - jnp.log(1.0 + jnp.exp(-jnp.abs(x)))


def _mlstm_kernel(*refs):
    tri_ref, of_ref, ob_ref, c_ref, m_ref = refs[10:]

    @pl.when(pl.program_id(1) == 0)
    def _():
        c_ref[...] = jnp.zeros_like(c_ref)
        m_ref[...] = jnp.zeros_like(m_ref)

    for d, o_ref in enumerate((of_ref, ob_ref)):
        _mlstm_direction(*refs[5 * d:5 * d + 5], tri_ref[d], o_ref, c_ref, m_ref, d * ML_HEADS, bool(d))


def _mlstm_direction(q_ref, k_ref, v_ref, gc_ref, gr_ref, tri, o_ref, c_ref, m_ref, h0, reverse):
    L = q_ref.shape[1]
    H = ML_HEADS
    ones = jnp.ones((L, LANES), BF16)
    gc = gc_ref[0]
    gr = gr_ref[0]
    ic_col, ic_row = gc[:, :H], gr[:H, :]
    b_col = jnp.dot(tri, _log_sigmoid(gc[:, H:]), precision=HI, preferred_element_type=F32)
    b_row = lax.dot_general(_log_sigmoid(gr[H:, :]), tri, NT, precision=HI, preferred_element_type=F32)
    last = 0 if reverse else L - 1
    causal = tri > 0.5

    for h in range(H):
        ck = slice(h * ML_DK, (h + 1) * ML_DK)
        cv = slice(h * ML_DV, (h + 1) * ML_DV)
        q = q_ref[0, :, ck]
        k = k_ref[0, :, ck]
        v = jnp.concatenate([v_ref[0, :, cv].astype(BF16), ones], axis=1)
        c, m = c_ref[h0 + h], m_ref[h0 + h]
        bc, br = b_col[:, h:h + 1], b_row[h:h + 1, :]
        icc, icr = ic_col[:, h:h + 1], ic_row[h:h + 1, :]
        b_last = bc[last:last + 1, :]

        a = bc + m
        dmat = jnp.where(causal, bc - br + icr, -jnp.inf)
        m_t = jnp.maximum(a, jnp.max(dmat, axis=-1, keepdims=True))
        w_inter = jnp.exp(a - m_t)
        s = lax.dot_general(q, k, NT, preferred_element_type=F32) * jnp.exp(dmat - m_t)
        both = w_inter * jnp.dot(q, c.astype(BF16), preferred_element_type=F32) + jnp.dot(
            s.astype(BF16), v, preferred_element_type=F32)
        den = jnp.maximum(jnp.abs(both[:, ML_DV:]), jnp.exp(-m_t))
        o_ref[0, :, cv] = both[:, :ML_DV] / jnp.concatenate([den] * (ML_DV // LANES), axis=1)

        gs_col = b_last - bc + icc
        gs_row = b_last - br + icr
        m_new = jnp.maximum(b_last + m, jnp.max(gs_row, axis=-1, keepdims=True))
        decay = jnp.exp(b_last + m - m_new)
        wk = jnp.exp(gs_col - m_new) * k.astype(F32)
        c_ref[h0 + h] = decay * c + jnp.dot(wk.astype(BF16).T, v, preferred_element_type=F32)
        m_ref[h0 + h] = m_new


def _mlstm_scan(qk, p, gates, *, v_col, n_ctx, L=ML_CHUNK):
    b, n, _ = qk.shape
    hk, hv, ng = ML_HEADS * ML_DK, ML_HEADS * ML_DV, 2 * ML_HEADS
    nc, ncc = n // L, n_ctx // L
    idx = np.arange(L)
    lower = idx[None, :] <= idx[:, None]
    tri = jnp.asarray(np.stack([lower, lower.T]).astype(np.float32))
    in_specs, operands, out_specs = [], [], []
    for d in range(2):
        chunk = functools.partial(_scan_chunk, n_ctx_chunks=ncc, n_chunks=nc, reverse=bool(d))
        col = lambda j, chunk=chunk: (lambda bi, t: (bi, chunk(t), j))
        g_dir = gates[..., d * ng:(d + 1) * ng]
        in_specs += [pl.BlockSpec((1, L, hk), col(0)), pl.BlockSpec((1, L, hk), col(1)),
                     pl.BlockSpec((1, L, hv), col(v_col)), pl.BlockSpec((1, L, ng), col(0)),
                     pl.BlockSpec((1, ng, L), lambda bi, t, chunk=chunk: (bi, 0, chunk(t)))]
        operands += [qk, qk, p, g_dir, jnp.swapaxes(g_dir, 1, 2)]
        out_specs.append(pl.BlockSpec((1, L, hv), col(0)))
    return pl.pallas_call(
        _mlstm_kernel,
        grid=(b, nc),
        in_specs=in_specs + [pl.BlockSpec((2, L, L), lambda bi, t: (0, 0, 0))],
        out_specs=out_specs,
        out_shape=[jax.ShapeDtypeStruct((b, n, hv), F32)] * 2,
        scratch_shapes=[pltpu.VMEM((2 * ML_HEADS, ML_DK, ML_DV + LANES), F32),
                        pltpu.VMEM((2 * ML_HEADS, 1, 1), F32)],
        compiler_params=pltpu.CompilerParams(
            dimension_semantics=("arbitrary", "arbitrary"), vmem_limit_bytes=VMEM_LIMIT),
        name="mlstm_scan",
    )(*operands, tri)


def _silu(x):
    return x * jax.nn.sigmoid(x)


def _rope_tables(n_lat, n_ctx):
    rows = n_lat // GRID_W
    row = jnp.repeat(jnp.arange(rows, dtype=F32), GRID_W)
    col = jnp.tile(jnp.arange(GRID_W, dtype=F32), rows)
    inv = jnp.power(ROPE_BASE, -jnp.arange(ROPE_PAIRS, dtype=F32) / ROPE_PAIRS)
    ang_r, ang_c = row[:, None] * inv, col[:, None] * inv
    cos = jnp.concatenate([jnp.cos(ang_r)] * 2 + [jnp.cos(ang_c)] * 2, axis=-1)
    sin = jnp.concatenate([-jnp.sin(ang_r), jnp.sin(ang_r), -jnp.sin(ang_c), jnp.sin(ang_c)], axis=-1)
    pad = ((n_ctx, 0), (0, 0))
    cos, sin = jnp.pad(cos, pad, constant_values=1.0), jnp.pad(sin, pad)
    return jnp.tile(cos, (1, 2)), jnp.tile(sin, (1, 2))


def _even_mixer(hn, bsz, w_in, i, gate_w2, gate_b, gla_norm_g, sc_conv_w, seq):
    nq = GLA_HEADS * GLA_DK
    nv = GLA_HEADS * GLA_DV
    gate_col = 2 * nq + 2 * nv
    pa = _matmul(hn, w_in, i, bn=1024, ncols=gate_col, transposed=True)
    glr = _matmul(hn, w_in, i, bn=LANES, col0=gate_col, ncols=LANES, transposed=True)
    pb = _matmul(hn, w_in, i, bn=SC_WIDTH, col0=gate_col, ncols=3 * SC_WIDTH,
                 col_shift=2 * GLA_GATE_RANK, transposed=True)
    pa3 = pa.reshape(bsz, -1, pa.shape[1])
    glr3 = glr.reshape(bsz, -1, LANES)
    outs = []
    for direction in range(2):
        w2 = jnp.zeros((LANES, nq), F32).at[
            direction * GLA_GATE_RANK:(direction + 1) * GLA_GATE_RANK].set(gate_w2[direction])
        outs.append(_gla_scan(pa3, glr3, w2.astype(BF16), gate_b[direction][None, :],
                              n_ctx=seq["n_ctx"], reverse=bool(direction)).reshape(-1, nv))
    gain = jnp.tile(gla_norm_g, GLA_HEADS)[None, :]
    return [_even_finish(outs[0], outs[1], pa, pb, gain, sc_conv_w, **seq)]


def _odd_mixer(hn, bsz, rope, layer, w_in, i, qn_g, kn_g, lam_p, subln_g, ml_conv_w, ml_gate_b,
               ml_norm_g, seq):
    lam_init = 0.8 - 0.6 * math.exp(-0.3 * layer)
    lam = (jnp.exp(jnp.sum(lam_p[0] * lam_p[1])) - jnp.exp(jnp.sum(lam_p[2] * lam_p[3])) + lam_init)
    na = DA_HEADS * 2 * DA_DQK
    nk = ML_HEADS * ML_DK
    nv = ML_HEADS * ML_DV
    n_main = 3 * na + 2 * nk + 2 * nv
    p = _matmul(hn, w_in, i, bn=1024, ncols=n_main, transposed=True)
    mg = _matmul(hn, w_in, i, bn=LANES, col0=n_main, ncols=LANES, transposed=True)[:, :4 * ML_HEADS]
    p3 = p.reshape(bsz, -1, n_main)
    n = p3.shape[1]

    cos, sin = rope
    tile2 = lambda g: jnp.tile(g, 2)[None, :]
    da = _diff_attention(p3, lam.reshape(1).astype(F32), cos, sin, tile2(qn_g), tile2(kn_g),
                         subln_g[None, :], n_ctx=seq["n_ctx"], out_scale=1.0 - lam_init)

    scale = jnp.concatenate([jnp.ones((nk,), F32), jnp.full((nk,), ML_DK ** -0.5, F32)])[None, :]
    hqk = _mlstm_prep(p, ml_conv_w, scale, col=3 * na // (2 * nk), **seq).reshape(bsz, n, 2 * nk)
    gates = (mg + ml_gate_b).reshape(bsz, n, 4 * ML_HEADS)
    outs = [o.reshape(-1, nv) for o in
            _mlstm_scan(hqk, p3, gates, v_col=(3 * na + 2 * nk) // nv, n_ctx=seq["n_ctx"])]
    m = _mlstm_finish(outs[0], outs[1], p, jnp.tile(ml_norm_g, ML_HEADS)[None, :],
                      col=(3 * na + 2 * nk + nv) // nv)
    return [da.reshape(-1, na), m]


def _modulation(c, c_ctx, ada_w, ada_b, layer):
    cc = jnp.concatenate([c, c_ctx[None, :]], axis=0)
    rows = cc.shape[0]
    act = jnp.pad(_silu(cc), ((0, BF16_ROWS - rows), (0, 0))).astype(BF16)
    mod = _matmul(act, ada_w, layer, bn=1024, bm=BF16_ROWS)[:rows] + ada_b[layer]
    return mod.reshape(rows, 6, -1)


def kernel(x, c, ctx, c_ctx, ada_w, ada_b, norm1_g, norm2_g, ev_w_in, ev_w_out, gla_gate_w2, gla_gate_b, gla_norm_g, sc_conv_w, od_w_in, od_w_out, da_qnorm_g, da_knorm_g, da_lambda, da_subln_g, ml_conv_w, ml_gate_b, ml_norm_g, ffn_w_up, ffn_conv_w, ffn_conv_b, ffn_w_down):
    bsz, n_lat, d = x.shape
    n_ctx = ctx.shape[1]
    n = n_ctx + n_lat
    depth = ada_w.shape[0]
    d_ff = ffn_w_down.shape[1]
    assert n % ROW_TILE == 0 and n_ctx <= ROW_CHUNK and d_ff % (4 * LANES) == 0
    rope = _rope_tables(n_lat, n_ctx)
    seq = dict(n_ctx=n_ctx, n_seq=n)

    ev_w_in = jnp.swapaxes(ev_w_in, 1, 2)
    od_w_in = jnp.swapaxes(od_w_in, 1, 2)
    mods = [_modulation(c, c_ctx, ada_w, ada_b, layer) for layer in range(depth)]
    xs = jnp.concatenate([ctx, x], axis=1).reshape(bsz * n, d)
    hn = _first_norm(xs, norm1_g[0][None, :], mods[0], **seq)
    for layer in range(depth):
        i = layer // 2
        if layer % 2 == 0:
            mix = _even_mixer(hn, bsz, ev_w_in, i, gla_gate_w2[i], gla_gate_b[i], gla_norm_g[i],
                              sc_conv_w[i], seq)
            w_out = ev_w_out
        else:
            mix = _odd_mixer(hn, bsz, rope, layer, od_w_in, i, da_qnorm_g[i], da_knorm_g[i],
                             da_lambda[i], da_subln_g[i], ml_conv_w[i], ml_gate_b[i], ml_norm_g[i], seq)
            w_out = od_w_out
        xs, hn = _out_proj(mix, w_out, i, xs, mods[layer], norm2_g[layer][None, :], **seq)
        act, w_down = _ffn_up(hn, ffn_w_up, ffn_w_down, layer, ffn_conv_w[layer],
                              ffn_conv_b[layer][None, :], **seq)
        nxt = min(layer + 1, depth - 1)
        xs, hn = _down_proj(act, w_down, xs,
                            mods[layer], norm1_g[nxt][None, :], mods[nxt], bk=d_ff // 2, **seq)
    return xs.reshape(bsz, n, d)[:, n_ctx:, :]
```

```python
import functools
import math

import jax
import jax.numpy as jnp
import numpy as np
from jax import lax
from jax.experimental import pallas as pl
from jax.experimental.pallas import tpu as pltpu

GRID_W = 64
EPS = 1e-6
GLA_HEADS, GLA_DK, GLA_DV, GLA_GATE_RANK, GLA_TAU = 4, 128, 256, 16, 16.0
SC_WIDTH = 1024
DA_HEADS, DA_DQK, DA_DV = 8, 64, 128
ROPE_BASE = 10000.0
ROPE_PAIRS = DA_DQK // 4
ML_HEADS, ML_DK, ML_DV = 4, 128, 256

GLA_CHUNK = 128
ML_CHUNK = 256
ATTN_BQ = 512
ATTN_SUB = 256
ATTN_CHUNK = 1024
ATTN_PREP = 544
ATTN_HEADS = 2
ROW_TILE = 1088
ROW_CHUNK = 272
LANES = 128
SUBLANES = 8
BF16_ROWS = 16
ROW_GROUP_UNROLL = 8
LOG2E = 1.4426950408889634
VMEM_LIMIT = 56 * 1024 * 1024

F32 = jnp.float32
BF16 = jnp.bfloat16
HI = lax.Precision.HIGHEST
NT = (((1,), (1,)), ((), ()))


def _mm_kernel(x_ref, w_ref, *rest, row_shift, transposed):
    o_ref, wb_ref = rest[-2:]

    @pl.when(pl.program_id(1) == 0)
    def _():
        if not transposed:
            wb_ref[...] = w_ref[...].astype(BF16)
            return
        bn = wb_ref.shape[1]
        piece = 2 * LANES
        for c0 in range(0, bn, piece):
            c1 = min(c0 + piece, bn)
            if row_shift and c1 == bn:
                rows = jnp.concatenate([w_ref[c0 + row_shift:bn, :], rest[0][...]], axis=0)
            else:
                rows = w_ref[c0 + row_shift:c1 + row_shift, :]
            wb_ref[:, c0:c1] = rows.T.astype(BF16)

    o_ref[...] = jnp.dot(x_ref[...], wb_ref[...], preferred_element_type=F32).astype(o_ref.dtype)


def _matmul(x, w, layer, *, bn, col0=0, ncols=None, col_shift=0, transposed=False, bm=ROW_TILE,
            out_dtype=F32):
    m, k = x.shape
    n_all = w.shape[1] if transposed else w.shape[2]
    ncols = n_all - col0 if ncols is None else ncols
    j0 = col0 // bn
    if transposed:
        w_spec = pl.BlockSpec((None, bn, k), lambda j, i: (layer, j0 + j, 0))
    else:
        w_spec = pl.BlockSpec((None, k, bn), lambda j, i: (layer, 0, j0 + j))
    in_specs = [pl.BlockSpec((bm, k), lambda j, i: (i, 0)), w_spec]
    operands = [x, w]
    if col_shift:
        per = bn // col_shift
        in_specs.append(pl.BlockSpec((None, col_shift, k), lambda j, i: (layer, (j0 + j + 1) * per, 0)))
        operands.append(w)
    return pl.pallas_call(
        functools.partial(_mm_kernel, row_shift=col_shift, transposed=transposed),
        grid=(pl.cdiv(ncols, bn), m // bm),
        in_specs=in_specs,
        out_specs=pl.BlockSpec((bm, bn), lambda j, i: (i, j)),
        out_shape=jax.ShapeDtypeStruct((m, ncols), out_dtype),
        scratch_shapes=[pltpu.VMEM((k, bn), BF16)],
        compiler_params=pltpu.CompilerParams(
            dimension_semantics=("arbitrary", "arbitrary"), vmem_limit_bytes=VMEM_LIMIT),
        name="matmul",
    )(*operands)


def _norm_mod(x, gain_scale, shift):
    return x * lax.rsqrt(jnp.mean(x * x, axis=-1, keepdims=True) + EPS) * gain_scale + shift


def _row_vectors(modb_ref, modc_ref, idx, gain_ref=None):
    def vec(ref):
        row = ref[0, idx:idx + 1, :]
        if gain_ref is not None:
            row = gain_ref[...] * (1.0 + row)
        return jnp.broadcast_to(row, (BF16_ROWS, row.shape[1]))

    lat, ctx = vec(modb_ref), vec(modc_ref)
    return lambda is_ctx: lat if is_ctx is None else jnp.where(is_ctx, ctx, lat)


def _head_rmsnorm(x, width):
    parts = []
    for c0 in range(0, x.shape[1], width):
        xh = x[:, c0:c0 + width]
        parts.append(xh * lax.rsqrt(jnp.mean(xh * xh, axis=-1, keepdims=True) + EPS))
    return parts[0] if len(parts) == 1 else jnp.concatenate(parts, axis=1)


def _sigmoid(x):
    return 1.0 / (1.0 + jnp.exp(-x))


def _chunk_rows(r0, rc, tiles_per_batch, bm):
    rows = r0 + lax.broadcasted_iota(jnp.int32, (rc, 1), 0)
    return rows, (pl.program_id(0) % tiles_per_batch) * bm + rows


def _edge_rows(r0, rc, bm, tiles_per_batch, n_ctx, n_seq, gate_ref, hprev_ref, hnext_ref):
    hr = hprev_ref.shape[0]
    pos0 = (pl.program_id(0) % tiles_per_batch) * bm
    if r0 == 0:
        before = hprev_ref[...].astype(F32)[hr - 1:hr, :]
        before = jnp.where((pos0 == 0) | (pos0 == n_ctx), 0.0, before)
    else:
        before = gate_ref[r0 - hr:r0, :].astype(F32)[hr - 1:hr, :]
    if r0 + rc == bm:
        after = hnext_ref[...].astype(F32)[0:1, :]
        after = jnp.where((pos0 + bm == n_ctx) | (pos0 + bm == n_seq), 0.0, after)
    else:
        after = gate_ref[r0 + rc:r0 + rc + hr, :].astype(F32)[0:1, :]
    return before, after


def _splits_rows(r0, rc, bm, tiles_per_batch, n_ctx):
    return any(r0 <= n_ctx - k * bm <= r0 + rc for k in range(tiles_per_batch))


def _dwconv3(s_ref, g, before, after, cw_ref, pos, n_ctx):
    rc = g.shape[0]
    s_ref[SUBLANES:SUBLANES + rc, :] = g
    s_ref[SUBLANES - 1:SUBLANES, :] = before
    s_ref[SUBLANES + rc:SUBLANES + rc + 1, :] = after
    prev = s_ref[SUBLANES - 1:SUBLANES - 1 + rc, :]
    nxt = s_ref[SUBLANES + 1:SUBLANES + 1 + rc, :]
    if pos is not None:
        prev = jnp.where(pos == n_ctx, 0.0, prev)
        nxt = jnp.where(pos == n_ctx - 1, 0.0, nxt)
    return prev * cw_ref[0:1, :] + g * cw_ref[1:2, :] + nxt * cw_ref[2:3, :]


def _mod_specs(n_batch, tiles_per_batch, d):
    return [pl.BlockSpec((1, 6, d), lambda i, *_: (i // tiles_per_batch, 0, 0)),
            pl.BlockSpec((1, 6, d), lambda i, *_: (n_batch, 0, 0))]


def _halo_specs(bm, width, col, n_rows, hr):
    hb = bm // hr
    last = n_rows // hr - 1
    return [pl.BlockSpec((hr, width), lambda i, *k: (jnp.maximum(i * hb - 1, 0), col(*k))),
            pl.BlockSpec((hr, width), lambda i, *k: (jnp.minimum((i + 1) * hb, last), col(*k)))]


def _norm_kernel(x_ref, gain_ref, modb_ref, modc_ref, o_ref, *, n_ctx, tiles_per_batch):
    gain_scale = _row_vectors(modb_ref, modc_ref, 1, gain_ref)
    shift = _row_vectors(modb_ref, modc_ref, 0)

    def rows(rs, is_ctx):
        o_ref[rs, :] = _norm_mod(x_ref[rs, :], gain_scale(is_ctx), shift(is_ctx)).astype(BF16)

    _for_row_groups(x_ref.shape[0], n_ctx, tiles_per_batch, rows)


def _first_norm(x, gain, mod, *, n_ctx, n_seq, bm=ROW_TILE):
    m, d = x.shape
    tpb = n_seq // bm
    return pl.pallas_call(
        functools.partial(_norm_kernel, n_ctx=n_ctx, tiles_per_batch=tpb),
        grid=(m // bm,),
        in_specs=[pl.BlockSpec((bm, d), lambda i: (i, 0)), pl.BlockSpec((1, d), lambda i: (0, 0))]
        + _mod_specs(mod.shape[0] - 1, tpb, d),
        out_specs=pl.BlockSpec((bm, d), lambda i: (i, 0)),
        out_shape=jax.ShapeDtypeStruct((m, d), BF16),
        compiler_params=pltpu.CompilerParams(vmem_limit_bytes=VMEM_LIMIT),
        name="first_norm",
    )(x, gain, mod, mod)


def _residual_norm(xo_ref, hn_ref, x_ref, modb_ref, modc_ref, gain_ref, nmodb_ref, nmodc_ref, *,
                   n_ctx, tiles_per_batch, gate_idx, shift_idx, inline=False):
    gate = _row_vectors(modb_ref, modc_ref, gate_idx)
    gain_scale = _row_vectors(nmodb_ref, nmodc_ref, shift_idx + 1, gain_ref)
    shift = _row_vectors(nmodb_ref, nmodc_ref, shift_idx)

    def rows(rs, is_ctx):
        x_new = x_ref[rs, :] + gate(is_ctx) * xo_ref[rs, :]
        xo_ref[rs, :] = x_new
        hn_ref[rs, :] = _norm_mod(x_new, gain_scale(is_ctx), shift(is_ctx)).astype(BF16)

    _for_row_groups(x_ref.shape[0], n_ctx, tiles_per_batch, rows, inline)


def _for_row_groups(bm, n_ctx, tiles_per_batch, fn, inline=False):
    rg = BF16_ROWS
    pos0 = (pl.program_id(0) % tiles_per_batch) * bm
    n_lead = -(-min(n_ctx, bm) // rg)
    if inline:
        for g in range(bm // rg):
            is_ctx = (pos0 + g * rg + lax.broadcasted_iota(jnp.int32, (rg, 1), 0) < n_ctx) if g < n_lead else None
            fn(slice(g * rg, (g + 1) * rg), is_ctx)
        return

    def group(with_ctx):
        def body(g, carry):
            r0 = pl.multiple_of(g * rg, rg)
            is_ctx = (pos0 + r0 + lax.broadcasted_iota(jnp.int32, (rg, 1), 0) < n_ctx) if with_ctx else None
            fn(pl.ds(r0, rg), is_ctx)
            return carry
        return body

    lax.fori_loop(0, n_lead, group(True), 0, unroll=ROW_GROUP_UNROLL)
    lax.fori_loop(n_lead, bm // rg, group(False), 0, unroll=ROW_GROUP_UNROLL)


def _out_kernel(*refs, n_in, n_ctx, tiles_per_batch):
    a_refs = refs[:n_in]
    w_ref, x_ref, modb_ref, modc_ref, gain_ref, xo_ref, hn_ref, wb_ref = refs[n_in:]

    @pl.when(pl.program_id(0) == 0)
    def _():
        for r0 in range(0, w_ref.shape[0], ROW_CHUNK):
            r1 = min(r0 + ROW_CHUNK, w_ref.shape[0])
            wb_ref[r0:r1, :] = w_ref[r0:r1, :].astype(BF16)

    k0 = 0
    for n_done, a_ref in enumerate(a_refs):
        part = jnp.dot(a_ref[...], wb_ref[k0:k0 + a_ref.shape[1], :], preferred_element_type=F32)
        xo_ref[...] = part if n_done == 0 else xo_ref[...] + part
        k0 += a_ref.shape[1]
    _residual_norm(xo_ref, hn_ref, x_ref, modb_ref, modc_ref, gain_ref, modb_ref, modc_ref,
                   n_ctx=n_ctx, tiles_per_batch=tiles_per_batch, gate_idx=2, shift_idx=3, inline=True)


def _out_proj(acts, w, layer, x, mod, gain, *, n_ctx, n_seq, bm=ROW_TILE // 2):
    m, d = x.shape
    kk = w.shape[1]
    tpb = n_seq // bm
    rows = lambda width: pl.BlockSpec((bm, width), lambda i: (i, 0))
    return pl.pallas_call(
        functools.partial(_out_kernel, n_in=len(acts), n_ctx=n_ctx, tiles_per_batch=tpb),
        grid=(m // bm,),
        in_specs=[rows(a.shape[1]) for a in acts]
        + [pl.BlockSpec((None, kk, d), lambda i: (layer, 0, 0), pipeline_mode=pl.Buffered(1)), rows(d)]
        + _mod_specs(mod.shape[0] - 1, tpb, d) + [pl.BlockSpec((1, d), lambda i: (0, 0))],
        out_specs=[rows(d), rows(d)],
        out_shape=[jax.ShapeDtypeStruct((m, d), F32), jax.ShapeDtypeStruct((m, d), BF16)],
        scratch_shapes=[pltpu.VMEM((kk, d), BF16)],
        compiler_params=pltpu.CompilerParams(
            dimension_semantics=("arbitrary",), vmem_limit_bytes=VMEM_LIMIT),
        name="out_proj",
    )(*acts, w, x, mod, mod, gain)


def _up_kernel(x_ref, xp_ref, xn_ref, wg_ref, wv_ref, cw_ref, cb_ref, wd_ref, o_ref, wdb_ref, wgb_ref,
               wvb_ref, xe_ref, s_ref, v_ref, *, n_ctx, n_seq, tiles_per_batch):
    i = pl.program_id(1)
    bm, rc, hr = x_ref.shape[0], ROW_CHUNK, xp_ref.shape[0]
    wdb_ref[...] = wd_ref[...].astype(BF16)

    @pl.when(i == 0)
    def _():
        wgb_ref[...] = wg_ref[...].astype(BF16)
        wvb_ref[...] = wv_ref[...].astype(BF16)

    xe_ref[0:hr, :] = xp_ref[...]
    xe_ref[hr:hr + bm, :] = x_ref[...]
    xe_ref[hr + bm:, :] = xn_ref[...]
    s_ref[...] = jnp.dot(xe_ref[...], wgb_ref[...], preferred_element_type=F32)
    v_ref[...] = jnp.dot(x_ref[...], wvb_ref[...], preferred_element_type=F32)
    pos0 = (i % tiles_per_batch) * bm
    s_ref[hr - 1:hr, :] = jnp.where((pos0 == 0) | (pos0 == n_ctx), 0.0, s_ref[hr - 1:hr, :])
    s_ref[hr + bm:hr + bm + 1, :] = jnp.where(
        (pos0 + bm == n_ctx) | (pos0 + bm == n_seq), 0.0, s_ref[hr + bm:hr + bm + 1, :])
    for r0 in range(0, bm, rc):
        rs = slice(r0, r0 + rc)
        prev = s_ref[hr - 1 + r0:hr - 1 + r0 + rc, :]
        nxt = s_ref[hr + 1 + r0:hr + 1 + r0 + rc, :]
        if _splits_rows(r0, rc, bm, tiles_per_batch, n_ctx):
            pos = pos0 + r0 + lax.broadcasted_iota(jnp.int32, (rc, 1), 0)
            prev = jnp.where(pos == n_ctx, 0.0, prev)
            nxt = jnp.where(pos == n_ctx - 1, 0.0, nxt)
        z = (prev * cw_ref[0:1, :] + s_ref[hr + r0:hr + r0 + rc, :] * cw_ref[1:2, :]
             + nxt * cw_ref[2:3, :] + cb_ref[...])
        o_ref[rs, :] = (z * _sigmoid(z) * v_ref[rs, :]).astype(BF16)


def _ffn_up(x, w, w_down, layer, conv_w, conv_b, *, n_ctx, n_seq, bn=512, bm=ROW_TILE):
    m, k = x.shape
    f = w.shape[2] // 2
    nj, ni = f // bn, m // bm
    slab = f // (nj * ni)
    assert slab * nj * ni == f and slab % BF16_ROWS == 0
    d_out = w_down.shape[2]
    tpb = n_seq // bm
    hb = bm // BF16_ROWS
    last = m // BF16_ROWS - 1
    return pl.pallas_call(
        functools.partial(_up_kernel, n_ctx=n_ctx, n_seq=n_seq, tiles_per_batch=tpb),
        grid=(nj, ni),
        in_specs=[pl.BlockSpec((bm, k), lambda j, i: (i, 0)),
                  pl.BlockSpec((BF16_ROWS, k), lambda j, i: (jnp.maximum(i * hb - 1, 0), 0)),
                  pl.BlockSpec((BF16_ROWS, k), lambda j, i: (jnp.minimum((i + 1) * hb, last), 0)),
                  pl.BlockSpec((None, k, bn), lambda j, i: (layer, 0, j)),
                  pl.BlockSpec((None, k, bn), lambda j, i: (layer, 0, j + nj)),
                  pl.BlockSpec((3, bn), lambda j, i: (0, j)), pl.BlockSpec((1, bn), lambda j, i: (0, j)),
                  pl.BlockSpec((None, slab, d_out), lambda j, i: (layer, j * ni + i, 0))],
        out_specs=[pl.BlockSpec((bm, bn), lambda j, i: (i, j)),
                   pl.BlockSpec((slab, d_out), lambda j, i: (j * ni + i, 0))],
        out_shape=[jax.ShapeDtypeStruct((m, f), BF16), jax.ShapeDtypeStruct((f, d_out), BF16)],
        scratch_shapes=[pltpu.VMEM((k, bn), BF16), pltpu.VMEM((k, bn), BF16),
                        pltpu.VMEM((bm + 2 * BF16_ROWS, k), BF16),
                        pltpu.VMEM((bm + 2 * BF16_ROWS, bn), F32), pltpu.VMEM((bm, bn), F32)],
        compiler_params=pltpu.CompilerParams(
            dimension_semantics=("arbitrary", "arbitrary"), vmem_limit_bytes=VMEM_LIMIT),
        name="ffn_up",
    )(x, x, x, w, w, conv_w, conv_b, w_down)


def _down_kernel(a_ref, w_ref, x_ref, modb_ref, modc_ref, gain_ref, nmodb_ref, nmodc_ref, xo_ref, hn_ref, *,
                 n_ctx, tiles_per_batch):
    part = jnp.dot(a_ref[...], w_ref[...], preferred_element_type=F32)

    @pl.when(pl.program_id(1) == 0)
    def _():
        xo_ref[...] = part

    @pl.when(pl.program_id(1) > 0)
    def _():
        xo_ref[...] += part

    @pl.when(pl.program_id(1) == pl.num_programs(1) - 1)
    def _():
        _residual_norm(xo_ref, hn_ref, x_ref, modb_ref, modc_ref, gain_ref, nmodb_ref, nmodc_ref,
                       n_ctx=n_ctx, tiles_per_batch=tiles_per_batch, gate_idx=5, shift_idx=0)


def _down_proj(a, w, x, mod, gain, next_mod, *, n_ctx, n_seq, bm=ROW_TILE // 2, bk=1408):
    m, f = a.shape
    d = w.shape[1]
    tpb = n_seq // bm
    row = lambda i, k: (i, 0)
    return pl.pallas_call(
        functools.partial(_down_kernel, n_ctx=n_ctx, tiles_per_batch=tpb),
        grid=(m // bm, f // bk),
        in_specs=[pl.BlockSpec((bm, bk), lambda i, k: (i, k)), pl.BlockSpec((bk, d), lambda i, k: (k, 0)),
                  pl.BlockSpec((bm, d), row)]
        + _mod_specs(mod.shape[0] - 1, tpb, d) + [pl.BlockSpec((1, d), lambda i, k: (0, 0))]
        + _mod_specs(mod.shape[0] - 1, tpb, d),
        out_specs=[pl.BlockSpec((bm, d), row), pl.BlockSpec((bm, d), row)],
        out_shape=[jax.ShapeDtypeStruct((m, d), F32), jax.ShapeDtypeStruct((m, d), BF16)],
        compiler_params=pltpu.CompilerParams(
            dimension_semantics=("arbitrary", "arbitrary"), vmem_limit_bytes=VMEM_LIMIT),
        name="down_proj",
    )(a, w, x, mod, mod, gain, next_mod, next_mod)


def _even_finish_kernel(of_ref, ob_ref, r_ref, sx_ref, sb_ref, sg_ref, sxp_ref, sxn_ref, sgp_ref, sgn_ref,
                        gain_ref, cw_ref, o_ref, s_ref, *, n_ctx, n_seq, tiles_per_batch):
    bm = of_ref.shape[0]
    nv = of_ref.shape[1]
    r = r_ref[...]
    o_ref[:, :nv] = (_head_rmsnorm(of_ref[...] + ob_ref[...], GLA_DV) * gain_ref[...]
                     * (r * _sigmoid(r))).astype(BF16)

    _, pos = _chunk_rows(0, bm, tiles_per_batch, bm)
    pos0 = (pl.program_id(0) % tiles_per_batch) * bm
    hr = sxp_ref.shape[0]
    before = (sxp_ref[...] * sgp_ref[...])[hr - 1:hr, :]
    before = jnp.where((pos0 == 0) | (pos0 == n_ctx), 0.0, before)
    after = (sxn_ref[...] * sgn_ref[...])[0:1, :]
    after = jnp.where((pos0 + bm == n_ctx) | (pos0 + bm == n_seq), 0.0, after)
    split = pos if _splits_rows(0, bm, bm, tiles_per_batch, n_ctx) else None
    conv = _dwconv3(s_ref, sg_ref[...] * sx_ref[...], before, after, cw_ref, split, n_ctx)
    o_ref[:, nv:] = (sb_ref[...] * conv).astype(BF16)


def _even_finish(o_f, o_b, pa, pb, gain, conv_w, *, n_ctx, n_seq, bm=ROW_CHUNK):
    m, nv = o_f.shape
    sc = conv_w.shape[1]
    tpb = n_seq // bm
    col = lambda j: (lambda i: (i, j))
    const = lambda i: (0, 0)
    return pl.pallas_call(
        functools.partial(_even_finish_kernel, n_ctx=n_ctx, n_seq=n_seq, tiles_per_batch=tpb),
        grid=(m // bm,),
        in_specs=[pl.BlockSpec((bm, nv), col(0)), pl.BlockSpec((bm, nv), col(0)),
                  pl.BlockSpec((bm, nv), col(2)),
                  pl.BlockSpec((bm, sc), col(0)), pl.BlockSpec((bm, sc), col(1)), pl.BlockSpec((bm, sc), col(2))]
        + _halo_specs(bm, sc, lambda: 0, m, SUBLANES) + _halo_specs(bm, sc, lambda: 2, m, SUBLANES)
        + [pl.BlockSpec((1, nv), const), pl.BlockSpec((3, sc), const)],
        out_specs=pl.BlockSpec((bm, nv + sc), col(0)),
        out_shape=jax.ShapeDtypeStruct((m, nv + sc), BF16),
        scratch_shapes=[pltpu.VMEM((bm + 2 * SUBLANES, sc), F32)],
        compiler_params=pltpu.CompilerParams(vmem_limit_bytes=VMEM_LIMIT),
        name="even_finish",
    )(o_f, o_b, pa, pb, pb, pb, pb, pb, pb, pb, gain, conv_w)


def _mlstm_prep_kernel(x_ref, xp_ref, xn_ref, cw_ref, scale_ref, o_ref, s_ref, *, n_ctx, n_seq,
                       tiles_per_batch):
    bm = x_ref.shape[0]
    _, pos = _chunk_rows(0, bm, tiles_per_batch, bm)
    before, after = _edge_rows(0, bm, bm, tiles_per_batch, n_ctx, n_seq, x_ref, xp_ref, xn_ref)
    split = pos if _splits_rows(0, bm, bm, tiles_per_batch, n_ctx) else None
    z = _dwconv3(s_ref, x_ref[...], before, after, cw_ref, split, n_ctx)
    o_ref[...] = (z * _sigmoid(z) * scale_ref[...]).astype(BF16)


def _mlstm_prep(p, conv_w, scale, *, col, n_ctx, n_seq, bm=ROW_CHUNK):
    m = p.shape[0]
    width = conv_w.shape[1]
    tpb = n_seq // bm
    const = lambda i: (0, 0)
    return pl.pallas_call(
        functools.partial(_mlstm_prep_kernel, n_ctx=n_ctx, n_seq=n_seq, tiles_per_batch=tpb),
        grid=(m // bm,),
        in_specs=[pl.BlockSpec((bm, width), lambda i: (i, col))]
        + _halo_specs(bm, width, lambda: col, m, SUBLANES)
        + [pl.BlockSpec((3, width), const), pl.BlockSpec((1, width), const)],
        out_specs=pl.BlockSpec((bm, width), lambda i: (i, 0)),
        out_shape=jax.ShapeDtypeStruct((m, width), BF16),
        scratch_shapes=[pltpu.VMEM((bm + 2 * SUBLANES, width), F32)],
        compiler_params=pltpu.CompilerParams(vmem_limit_bytes=VMEM_LIMIT),
        name="mlstm_prep",
    )(p, p, p, conv_w, scale)


def _mlstm_finish_kernel(hf_ref, hb_ref, mo_ref, gain_ref, o_ref):
    o_ref[...] = (_head_rmsnorm(hf_ref[...] + hb_ref[...], ML_DV) * gain_ref[...]
                  * _sigmoid(mo_ref[...])).astype(BF16)


def _mlstm_finish(h_f, h_b, p, gain, *, col, bm=ROW_TILE // 2):
    m, nv = h_f.shape
    return pl.pallas_call(
        _mlstm_finish_kernel,
        grid=(m // bm,),
        in_specs=[pl.BlockSpec((bm, nv), lambda i: (i, 0)), pl.BlockSpec((bm, nv), lambda i: (i, 0)),
                  pl.BlockSpec((bm, nv), lambda i: (i, col)), pl.BlockSpec((1, nv), lambda i: (0, 0))],
        out_specs=pl.BlockSpec((bm, nv), lambda i: (i, 0)),
        out_shape=jax.ShapeDtypeStruct((m, nv), BF16),
        compiler_params=pltpu.CompilerParams(vmem_limit_bytes=VMEM_LIMIT),
        name="mlstm_finish",
    )(h_f, h_b, p, gain)


def _qk_prep(x, cos, sin, gain):
    lane = lax.broadcasted_iota(jnp.int32, x.shape, 1)
    lo = lane < DA_DQK
    x2 = x * x
    ss_lo = jnp.sum(jnp.where(lo, x2, 0.0), axis=-1, keepdims=True)
    ss_hi = jnp.sum(jnp.where(lo, 0.0, x2), axis=-1, keepdims=True)
    y = x * lax.rsqrt(jnp.where(lo, ss_lo, ss_hi) * (1.0 / DA_DQK) + EPS) * gain
    swapped = jnp.where((lane & ROPE_PAIRS) == 0, pltpu.roll(y, LANES - ROPE_PAIRS, 1),
                        pltpu.roll(y, ROPE_PAIRS, 1))
    return y * cos + swapped * sin


def _attn_kernel(lam_ref, q_ref, k_ref, v_ref, cos_ref, sin_ref, qg_ref, kg_ref, og_ref, o_ref,
                 kb_ref, vb_ref, *, n_ctx, chunk, prep_rows, out_scale):
    qi = pl.program_id(2)
    lam = lam_ref[0]
    n = k_ref.shape[1]
    bq = q_ref.shape[1]
    heads = q_ref.shape[2] // LANES

    @pl.when(qi == 0)
    def _():
        for r0 in range(0, n, prep_rows):
            rs = slice(r0, r0 + prep_rows)
            for h in range(heads):
                hs = slice(h * LANES, (h + 1) * LANES)
                kb_ref[h, rs, :] = _qk_prep(k_ref[0, rs, hs], cos_ref[rs, :], sin_ref[rs, :],
                                            kg_ref[...]).astype(BF16)
                vb_ref[h, rs, :LANES] = v_ref[0, rs, hs].astype(BF16)
                vb_ref[h, rs, LANES:] = jnp.ones((prep_rows, LANES), BF16)

    def attend(r0, r1, nk):
        nr = r1 - r0
        rows = pl.ds(pl.multiple_of(qi * bq, BF16_ROWS) + r0, nr)
        cos, sin = cos_ref[rows, :], sin_ref[rows, :]
        lane = lax.broadcasted_iota(jnp.int32, (nr, LANES), 1)
        q2, m, acc = [], [], []
        for h in range(heads):
            q = _qk_prep(q_ref[0, r0:r1, h * LANES:(h + 1) * LANES], cos, sin, qg_ref[...])
            q = (q * (DA_DQK ** -0.5 * LOG2E)).astype(BF16)
            zero = jnp.zeros_like(q)
            q2.append(jnp.concatenate([jnp.where(lane < DA_DQK, q, zero),
                                       jnp.where(lane >= DA_DQK, q, zero)], axis=0))
            m.append(jnp.full((2 * nr, 1), -jnp.inf, F32))
            acc.append(jnp.zeros((2 * nr, 2 * LANES), F32))
        for c0 in range(0, nk, chunk):
            c1 = min(c0 + chunk, nk)
            for h in range(heads):
                s = lax.dot_general(q2[h], kb_ref[h, c0:c1, :], NT, preferred_element_type=F32)
                m_new = jnp.maximum(m[h], jnp.max(s, axis=-1, keepdims=True))
                p = jnp.exp2(s - m_new)
                acc[h] = jnp.exp2(m[h] - m_new) * acc[h] + jnp.dot(
                    p.astype(BF16), vb_ref[h, c0:c1, :], preferred_element_type=F32)
                m[h] = m_new
        for h in range(heads):
            o = acc[h][:, :LANES] / acc[h][:, LANES:]
            o = o[:nr] - lam * o[nr:]
            o = o * lax.rsqrt(jnp.mean(o * o, axis=-1, keepdims=True) + EPS) * og_ref[...] * out_scale
            o_ref[0, r0:r1, h * LANES:(h + 1) * LANES] = o.astype(BF16)

    last = pl.num_programs(2) - 1
    last_rows = n - (n // bq) * bq if n % bq else bq

    @pl.when(qi == 0)
    def _():
        attend(0, n_ctx, n_ctx)
        for r0 in range(n_ctx, bq, ATTN_SUB):
            attend(r0, r0 + ATTN_SUB, n)

    @pl.when((qi > 0) & (qi < last))
    def _():
        for r0 in range(0, bq, ATTN_SUB):
            attend(r0, r0 + ATTN_SUB, n)

    @pl.when((qi > 0) & (qi == last))
    def _():
        for r0 in range(0, last_rows, ATTN_SUB):
            attend(r0, r0 + ATTN_SUB, n)


def _diff_attention(p, lam, cos, sin, q_gain, k_gain, out_gain, *, n_ctx, out_scale, bq=ATTN_BQ,
                    chunk=ATTN_CHUNK, prep_rows=ATTN_PREP, heads=ATTN_HEADS):
    b, n, _ = p.shape
    assert n_ctx % ATTN_SUB == 0 and bq % ATTN_SUB == 0 and n % ATTN_SUB == 0 and n_ctx <= bq
    hg = DA_HEADS // heads
    hd = DA_HEADS * LANES
    hw = heads * LANES
    const = lambda bi, hi, qi: (0, 0)
    return pl.pallas_call(
        functools.partial(_attn_kernel, n_ctx=n_ctx, chunk=chunk, prep_rows=prep_rows, out_scale=out_scale),
        grid=(b, hg, pl.cdiv(n, bq)),
        in_specs=[pl.BlockSpec(memory_space=pltpu.SMEM),
                  pl.BlockSpec((1, bq, hw), lambda bi, hi, qi: (bi, qi, hi)),
                  pl.BlockSpec((1, n, hw), lambda bi, hi, qi: (bi, 0, hg + hi)),
                  pl.BlockSpec((1, n, hw), lambda bi, hi, qi: (bi, 0, 2 * hg + hi)),
                  pl.BlockSpec((n, LANES), const), pl.BlockSpec((n, LANES), const),
                  pl.BlockSpec((1, LANES), const), pl.BlockSpec((1, LANES), const),
                  pl.BlockSpec((1, LANES), const)],
        out_specs=pl.BlockSpec((1, bq, hw), lambda bi, hi, qi: (bi, qi, hi)),
        out_shape=jax.ShapeDtypeStruct((b, n, hd), BF16),
        scratch_shapes=[pltpu.VMEM((heads, n, LANES), BF16), pltpu.VMEM((heads, n, 2 * LANES), BF16)],
        compiler_params=pltpu.CompilerParams(
            dimension_semantics=("arbitrary", "arbitrary", "arbitrary"), vmem_limit_bytes=VMEM_LIMIT),
        name="diff_attention",
    )(lam, p, p, p, cos, sin, q_gain, k_gain, out_gain)


def _scan_chunk(t, n_ctx_chunks, n_chunks, reverse):
    if not reverse:
        return t
    return jnp.where(t < n_ctx_chunks, n_ctx_chunks - 1 - t, n_chunks - 1 - (t - n_ctx_chunks))


def _gla_constants(L, reverse):
    nlev = int(math.log2(L))
    idx = np.arange(L)
    i, t = idx[:, None], idx[None, :]
    if reverse:
        i, t = L - 1 - i, L - 1 - t
    rs = [(t <= i), (t > i)]
    am = [(i == t)]
    for lev in range(nlev):
        m = L >> (lev + 1)
        blk_i, blk_t = i // (2 * m), t // (2 * m)
        mid = blk_i * 2 * m + m
        q_role = i >= mid
        rs.append(np.where(q_role, (t >= mid) & (t <= i), (t > i) & (t < mid)) & (blk_i == blk_t))
        am.append((blk_i == blk_t) & q_role & (t < mid))
    return (np.stack(rs).astype(np.float32).reshape((nlev + 2) * L, L),
            np.stack(am).astype(np.float32))


def _gla_kernel(q_ref, k_ref, v_ref, glr_ref, w2_ref, gb_ref, rsum_ref, amask_ref, o_ref, st_ref, *,
                reverse):
    L = q_ref.shape[1]
    nlev = amask_ref.shape[0] - 1
    n_coarse = nlev + 2 - rsum_ref.shape[0] // L

    @pl.when(pl.program_id(1) == 0)
    def _():
        st_ref[...] = jnp.zeros_like(st_ref)

    z = jnp.dot(glr_ref[0].astype(BF16), w2_ref[...], preferred_element_type=F32) + gb_ref[...]
    g = (jnp.minimum(z, 0.0) - jnp.log(1.0 + jnp.exp(-jnp.abs(z)))) * (1.0 / GLA_TAU)
    g1 = g.astype(BF16)
    g2 = (g - g1.astype(F32)).astype(BF16)
    g3 = (g - g1.astype(F32) - g2.astype(F32)).astype(BF16)
    rsum = rsum_ref[...]
    e_all = (jnp.dot(rsum, g1, preferred_element_type=F32) + jnp.dot(rsum, g2, preferred_element_type=F32)
             + jnp.dot(rsum, g3, preferred_element_type=F32))
    b_tot = jnp.sum(g, axis=0, keepdims=True)

    def level_decay(lev, ck):
        if lev >= n_coarse:
            return jnp.exp(e_all[(2 + lev - n_coarse) * L:(3 + lev - n_coarse) * L, ck])
        m = L >> (lev + 1)
        b3 = e_all[0:L, ck].reshape(L // (2 * m), 2 * m, GLA_DK)
        pivot = m if reverse else m - 1
        return jnp.exp(-jnp.abs(b3 - b3[:, pivot:pivot + 1, :])).reshape(L, GLA_DK)

    for h in range(GLA_HEADS):
        ck = slice(h * GLA_DK, (h + 1) * GLA_DK)
        cv = slice(h * GLA_DV, (h + 1) * GLA_DV)
        q = q_ref[0, :, ck] * (GLA_DK ** -0.5)
        k = k_ref[0, :, ck]
        v = v_ref[0, :, cv].astype(BF16)
        st = st_ref[h]
        a = amask_ref[0] * lax.dot_general(q.astype(BF16), k.astype(BF16), NT, preferred_element_type=F32)
        for lev in range(nlev):
            e = level_decay(lev, ck)
            a = a + amask_ref[1 + lev] * lax.dot_general(
                (q * e).astype(BF16), (k * e).astype(BF16), NT, preferred_element_type=F32)
        qe = (q * jnp.exp(e_all[0:L, ck])).astype(BF16)
        o = lax.dot_general(qe, st.astype(BF16), NT, preferred_element_type=F32)
        o = o + jnp.dot(a.astype(BF16), v, preferred_element_type=F32)
        o_ref[0, :, cv] = o
        kd = (k * jnp.exp(e_all[L:2 * L, ck])).astype(BF16)
        st_ref[h] = jnp.exp(b_tot[:, ck]) * st + jnp.dot(
            v.T, kd, preferred_element_type=F32)


def _gla_scan(p, glr, w2, gb, *, n_ctx, reverse, L=GLA_CHUNK):
    b, n, _ = p.shape
    nc, ncc = n // L, n_ctx // L
    rsum, amask = _gla_constants(L, reverse)
    n_coarse = sum(2 * (L >> (lev + 1)) >= SUBLANES for lev in range(amask.shape[0] - 1))
    rsum = rsum.reshape(-1, L, L)
    rsum = np.concatenate([rsum[:2], rsum[2 + n_coarse:]]).reshape(-1, L)
    chunk = functools.partial(_scan_chunk, n_ctx_chunks=ncc, n_chunks=nc, reverse=reverse)
    hk, hv = GLA_HEADS * GLA_DK, GLA_HEADS * GLA_DV
    cols = lambda width, start: pl.BlockSpec((1, L, width), lambda bi, t: (bi, chunk(t), start // width))
    row = lambda bi, t: (bi, chunk(t), 0)
    const2 = lambda bi, t: (0, 0)
    const3 = lambda bi, t: (0, 0, 0)
    return pl.pallas_call(
        functools.partial(_gla_kernel, reverse=reverse),
        grid=(b, nc),
        in_specs=[cols(hk, 0), cols(hk, hk), cols(hv, 2 * hk), cols(LANES, 0),
                  pl.BlockSpec(w2.shape, const2), pl.BlockSpec(gb.shape, const2),
                  pl.BlockSpec(rsum.shape, const2), pl.BlockSpec(amask.shape, const3)],
        out_specs=pl.BlockSpec((1, L, hv), row),
        out_shape=jax.ShapeDtypeStruct((b, n, hv), F32),
        scratch_shapes=[pltpu.VMEM((GLA_HEADS, GLA_DV, GLA_DK), F32)],
        compiler_params=pltpu.CompilerParams(
            dimension_semantics=("arbitrary", "arbitrary"), vmem_limit_bytes=VMEM_LIMIT),
        name="gla_scan_bwd" if reverse else "gla_scan_fwd",
    )(p, p, p, glr, w2, gb, jnp.asarray(rsum, BF16), jnp.asarray(amask))


def _log_sigmoid(x):
    return jnp.minimum(x, 0.0) - jnp.log(1.0 + jnp.exp(-jnp.abs(x)))


def _mlstm_kernel(*refs):
    tri_ref, of_ref, ob_ref, c_ref, m_ref = refs[10:]

    @pl.when(pl.program_id(1) == 0)
    def _():
        c_ref[...] = jnp.zeros_like(c_ref)
        m_ref[...] = jnp.zeros_like(m_ref)

    for d, o_ref in enumerate((of_ref, ob_ref)):
        _mlstm_direction(*refs[5 * d:5 * d + 5], tri_ref[d], o_ref, c_ref, m_ref, d * ML_HEADS, bool(d))


def _mlstm_direction(q_ref, k_ref, v_ref, gc_ref, gr_ref, tri, o_ref, c_ref, m_ref, h0, reverse):
    L = q_ref.shape[1]
    H = ML_HEADS
    ones = jnp.ones((L, LANES), BF16)
    gc = gc_ref[0]
    gr = gr_ref[0]
    ic_col, ic_row = gc[:, :H], gr[:H, :]
    b_col = jnp.dot(tri, _log_sigmoid(gc[:, H:]), precision=HI, preferred_element_type=F32)
    b_row = lax.dot_general(_log_sigmoid(gr[H:, :]), tri, NT, precision=HI, preferred_element_type=F32)
    last = 0 if reverse else L - 1
    causal = tri > 0.5

    for h in range(H):
        ck = slice(h * ML_DK, (h + 1) * ML_DK)
        cv = slice(h * ML_DV, (h + 1) * ML_DV)
        q = q_ref[0, :, ck]
        k = k_ref[0, :, ck]
        v = jnp.concatenate([v_ref[0, :, cv].astype(BF16), ones], axis=1)
        c, m = c_ref[h0 + h], m_ref[h0 + h]
        bc, br = b_col[:, h:h + 1], b_row[h:h + 1, :]
        icc, icr = ic_col[:, h:h + 1], ic_row[h:h + 1, :]
        b_last = bc[last:last + 1, :]

        a = bc + m
        dmat = jnp.where(causal, bc - br + icr, -jnp.inf)
        m_t = jnp.maximum(a, jnp.max(dmat, axis=-1, keepdims=True))
        w_inter = jnp.exp(a - m_t)
        s = lax.dot_general(q, k, NT, preferred_element_type=F32) * jnp.exp(dmat - m_t)
        both = w_inter * jnp.dot(q, c.astype(BF16), preferred_element_type=F32) + jnp.dot(
            s.astype(BF16), v, preferred_element_type=F32)
        den = jnp.maximum(jnp.abs(both[:, ML_DV:]), jnp.exp(-m_t))
        o_ref[0, :, cv] = both[:, :ML_DV] / jnp.concatenate([den] * (ML_DV // LANES), axis=1)

        gs_col = b_last - bc + icc
        gs_row = b_last - br + icr
        m_new = jnp.maximum(b_last + m, jnp.max(gs_row, axis=-1, keepdims=True))
        decay = jnp.exp(b_last + m - m_new)
        wk = jnp.exp(gs_col - m_new) * k.astype(F32)
        c_ref[h0 + h] = decay * c + jnp.dot(wk.astype(BF16).T, v, preferred_element_type=F32)
        m_ref[h0 + h] = m_new


def _mlstm_scan(qk, p, gates, *, v_col, n_ctx, L=ML_CHUNK):
    b, n, _ = qk.shape
    hk, hv, ng = ML_HEADS * ML_DK, ML_HEADS * ML_DV, 2 * ML_HEADS
    nc, ncc = n // L, n_ctx // L
    idx = np.arange(L)
    lower = idx[None, :] <= idx[:, None]
    tri = jnp.asarray(np.stack([lower, lower.T]).astype(np.float32))
    in_specs, operands, out_specs = [], [], []
    for d in range(2):
        chunk = functools.partial(_scan_chunk, n_ctx_chunks=ncc, n_chunks=nc, reverse=bool(d))
        col = lambda j, chunk=chunk: (lambda bi, t: (bi, chunk(t), j))
        g_dir = gates[..., d * ng:(d + 1) * ng]
        in_specs += [pl.BlockSpec((1, L, hk), col(0)), pl.BlockSpec((1, L, hk), col(1)),
                     pl.BlockSpec((1, L, hv), col(v_col)), pl.BlockSpec((1, L, ng), col(0)),
                     pl.BlockSpec((1, ng, L), lambda bi, t, chunk=chunk: (bi, 0, chunk(t)))]
        operands += [qk, qk, p, g_dir, jnp.swapaxes(g_dir, 1, 2)]
        out_specs.append(pl.BlockSpec((1, L, hv), col(0)))
    return pl.pallas_call(
        _mlstm_kernel,
        grid=(b, nc),
        in_specs=in_specs + [pl.BlockSpec((2, L, L), lambda bi, t: (0, 0, 0))],
        out_specs=out_specs,
        out_shape=[jax.ShapeDtypeStruct((b, n, hv), F32)] * 2,
        scratch_shapes=[pltpu.VMEM((2 * ML_HEADS, ML_DK, ML_DV + LANES), F32),
                        pltpu.VMEM((2 * ML_HEADS, 1, 1), F32)],
        compiler_params=pltpu.CompilerParams(
            dimension_semantics=("arbitrary", "arbitrary"), vmem_limit_bytes=VMEM_LIMIT),
        name="mlstm_scan",
    )(*operands, tri)


def _silu(x):
    return x * jax.nn.sigmoid(x)


def _rope_tables(n_lat, n_ctx):
    rows = n_lat // GRID_W
    row = jnp.repeat(jnp.arange(rows, dtype=F32), GRID_W)
    col = jnp.tile(jnp.arange(GRID_W, dtype=F32), rows)
    inv = jnp.power(ROPE_BASE, -jnp.arange(ROPE_PAIRS, dtype=F32) / ROPE_PAIRS)
    ang_r, ang_c = row[:, None] * inv, col[:, None] * inv
    cos = jnp.concatenate([jnp.cos(ang_r)] * 2 + [jnp.cos(ang_c)] * 2, axis=-1)
    sin = jnp.concatenate([-jnp.sin(ang_r), jnp.sin(ang_r), -jnp.sin(ang_c), jnp.sin(ang_c)], axis=-1)
    pad = ((n_ctx, 0), (0, 0))
    cos, sin = jnp.pad(cos, pad, constant_values=1.0), jnp.pad(sin, pad)
    return jnp.tile(cos, (1, 2)), jnp.tile(sin, (1, 2))


def _even_mixer(hn, bsz, w_in, i, gate_w2, gate_b, gla_norm_g, sc_conv_w, seq):
    nq = GLA_HEADS * GLA_DK
    nv = GLA_HEADS * GLA_DV
    gate_col = 2 * nq + 2 * nv
    pa = _matmul(hn, w_in, i, bn=1024, ncols=gate_col, transposed=True)
    glr = _matmul(hn, w_in, i, bn=LANES, col0=gate_col, ncols=LANES, transposed=True)
    pb = _matmul(hn, w_in, i, bn=SC_WIDTH, col0=gate_col, ncols=3 * SC_WIDTH,
                 col_shift=2 * GLA_GATE_RANK, transposed=True)
    pa3 = pa.reshape(bsz, -1, pa.shape[1])
    glr3 = glr.reshape(bsz, -1, LANES)
    outs = []
    for direction in range(2):
        w2 = jnp.zeros((LANES, nq), F32).at[
            direction * GLA_GATE_RANK:(direction + 1) * GLA_GATE_RANK].set(gate_w2[direction])
        outs.append(_gla_scan(pa3, glr3, w2.astype(BF16), gate_b[direction][None, :],
                              n_ctx=seq["n_ctx"], reverse=bool(direction)).reshape(-1, nv))
    gain = jnp.tile(gla_norm_g, GLA_HEADS)[None, :]
    return [_even_finish(outs[0], outs[1], pa, pb, gain, sc_conv_w, **seq)]


def _odd_mixer(hn, bsz, rope, layer, w_in, i, qn_g, kn_g, lam_p, subln_g, ml_conv_w, ml_gate_b,
               ml_norm_g, seq):
    lam_init = 0.8 - 0.6 * math.exp(-0.3 * layer)
    lam = (jnp.exp(jnp.sum(lam_p[0] * lam_p[1])) - jnp.exp(jnp.sum(lam_p[2] * lam_p[3])) + lam_init)
    na = DA_HEADS * 2 * DA_DQK
    nk = ML_HEADS * ML_DK
    nv = ML_HEADS * ML_DV
    n_main = 3 * na + 2 * nk + 2 * nv
    p = _matmul(hn, w_in, i, bn=1024, ncols=n_main, transposed=True)
    mg = _matmul(hn, w_in, i, bn=LANES, col0=n_main, ncols=LANES, transposed=True)[:, :4 * ML_HEADS]
    p3 = p.reshape(bsz, -1, n_main)
    n = p3.shape[1]

    cos, sin = rope
    tile2 = lambda g: jnp.tile(g, 2)[None, :]
    da = _diff_attention(p3, lam.reshape(1).astype(F32), cos, sin, tile2(qn_g), tile2(kn_g),
                         subln_g[None, :], n_ctx=seq["n_ctx"], out_scale=1.0 - lam_init)

    scale = jnp.concatenate([jnp.ones((nk,), F32), jnp.full((nk,), ML_DK ** -0.5, F32)])[None, :]
    hqk = _mlstm_prep(p, ml_conv_w, scale, col=3 * na // (2 * nk), **seq).reshape(bsz, n, 2 * nk)
    gates = (mg + ml_gate_b).reshape(bsz, n, 4 * ML_HEADS)
    outs = [o.reshape(-1, nv) for o in
            _mlstm_scan(hqk, p3, gates, v_col=(3 * na + 2 * nk) // nv, n_ctx=seq["n_ctx"])]
    m = _mlstm_finish(outs[0], outs[1], p, jnp.tile(ml_norm_g, ML_HEADS)[None, :],
                      col=(3 * na + 2 * nk + nv) // nv)
    return [da.reshape(-1, na), m]


def _modulation(c, c_ctx, ada_w, ada_b, layer):
    cc = jnp.concatenate([c, c_ctx[None, :]], axis=0)
    rows = cc.shape[0]
    act = jnp.pad(_silu(cc), ((0, BF16_ROWS - rows), (0, 0))).astype(BF16)
    mod = _matmul(act, ada_w, layer, bn=1024, bm=BF16_ROWS)[:rows] + ada_b[layer]
    return mod.reshape(rows, 6, -1)


def kernel(x, c, ctx, c_ctx, ada_w, ada_b, norm1_g, norm2_g, ev_w_in, ev_w_out, gla_gate_w2, gla_gate_b, gla_norm_g, sc_conv_w, od_w_in, od_w_out, da_qnorm_g, da_knorm_g, da_lambda, da_subln_g, ml_conv_w, ml_gate_b, ml_norm_g, ffn_w_up, ffn_conv_w, ffn_conv_b, ffn_w_down):
    bsz, n_lat, d = x.shape
    n_ctx = ctx.shape[1]
    n = n_ctx + n_lat
    depth = ada_w.shape[0]
    d_ff = ffn_w_down.shape[1]
    assert n % ROW_TILE == 0 and n_ctx <= ROW_CHUNK and d_ff % (4 * LANES) == 0
    rope = _rope_tables(n_lat, n_ctx)
    seq = dict(n_ctx=n_ctx, n_seq=n)

    ev_w_in = jnp.swapaxes(ev_w_in, 1, 2)
    od_w_in = jnp.swapaxes(od_w_in, 1, 2)
    mods = [_modulation(c, c_ctx, ada_w, ada_b, layer) for layer in range(depth)]
    xs = jnp.concatenate([ctx, x], axis=1).reshape(bsz * n, d)
    hn = _first_norm(xs, norm1_g[0][None, :], mods[0], **seq)
    for layer in range(depth):
        i = layer // 2
        if layer % 2 == 0:
            mix = _even_mixer(hn, bsz, ev_w_in, i, gla_gate_w2[i], gla_gate_b[i], gla_norm_g[i],
                              sc_conv_w[i], seq)
            w_out = ev_w_out
        else:
            mix = _odd_mixer(hn, bsz, rope, layer, od_w_in, i, da_qnorm_g[i], da_knorm_g[i],
                             da_lambda[i], da_subln_g[i], ml_conv_w[i], ml_gate_b[i], ml_norm_g[i], seq)
            w_out = od_w_out
        xs, hn = _out_proj(mix, w_out, i, xs, mods[layer], norm2_g[layer][None, :], **seq)
        act, w_down = _ffn_up(hn, ffn_w_up, ffn_w_down, layer, ffn_conv_w[layer],
                              ffn_conv_b[layer][None, :], **seq)
        nxt = min(layer + 1, depth - 1)
        xs, hn = _down_proj(act, w_down, xs,
                            mods[layer], norm1_g[nxt][None, :], mods[nxt], bk=d_ff // 2, **seq)
    return xs.reshape(bsz, n, d)[:, n_ctx:, :]
```

```python
import functools
import math

import jax
import jax.numpy as jnp
import numpy as np
from jax import lax
from jax.experimental import pallas as pl
from jax.experimental.pallas import tpu as pltpu

GRID_W = 64
EPS = 1e-6
GLA_HEADS, GLA_DK, GLA_DV, GLA_GATE_RANK, GLA_TAU = 4, 128, 256, 16, 16.0
SC_WIDTH = 1024
DA_HEADS, DA_DQK, DA_DV = 8, 64, 128
ROPE_BASE = 10000.0
ROPE_PAIRS = DA_DQK // 4
ML_HEADS, ML_DK, ML_DV = 4, 128, 256

GLA_CHUNK = 128
ML_CHUNK = 256
ATTN_BQ = 512
ATTN_SUB = 256
ATTN_CHUNK = 1024
ATTN_PREP = 544
ATTN_HEADS = 2
ROW_TILE = 1088
ROW_CHUNK = 272
LANES = 128
SUBLANES = 8
BF16_ROWS = 16
ROW_GROUP_UNROLL = 8
LOG2E = 1.4426950408889634
VMEM_LIMIT = 56 * 1024 * 1024

F32 = jnp.float32
BF16 = jnp.bfloat16
HI = lax.Precision.HIGHEST
NT = (((1,), (1,)), ((), ()))


def _mm_kernel(x_ref, w_ref, *rest, row_shift, transposed):
    o_ref, wb_ref = rest[-2:]

    @pl.when(pl.program_id(1) == 0)
    def _():
        if not transposed:
            wb_ref[...] = w_ref[...].astype(BF16)
            return
        bn = wb_ref.shape[1]
        piece = 2 * LANES
        for c0 in range(0, bn, piece):
            c1 = min(c0 + piece, bn)
            if row_shift and c1 == bn:
                rows = jnp.concatenate([w_ref[c0 + row_shift:bn, :], rest[0][...]], axis=0)
            else:
                rows = w_ref[c0 + row_shift:c1 + row_shift, :]
            wb_ref[:, c0:c1] = rows.T.astype(BF16)

    o_ref[...] = jnp.dot(x_ref[...], wb_ref[...], preferred_element_type=F32).astype(o_ref.dtype)


def _matmul(x, w, layer, *, bn, col0=0, ncols=None, col_shift=0, transposed=False, bm=ROW_TILE,
            out_dtype=F32):
    m, k = x.shape
    n_all = w.shape[1] if transposed else w.shape[2]
    ncols = n_all - col0 if ncols is None else ncols
    j0 = col0 // bn
    if transposed:
        w_spec = pl.BlockSpec((None, bn, k), lambda j, i: (layer, j0 + j, 0))
    else:
        w_spec = pl.BlockSpec((None, k, bn), lambda j, i: (layer, 0, j0 + j))
    in_specs = [pl.BlockSpec((bm, k), lambda j, i: (i, 0)), w_spec]
    operands = [x, w]
    if col_shift:
        per = bn // col_shift
        in_specs.append(pl.BlockSpec((None, col_shift, k), lambda j, i: (layer, (j0 + j + 1) * per, 0)))
        operands.append(w)
    return pl.pallas_call(
        functools.partial(_mm_kernel, row_shift=col_shift, transposed=transposed),
        grid=(pl.cdiv(ncols, bn), m // bm),
        in_specs=in_specs,
        out_specs=pl.BlockSpec((bm, bn), lambda j, i: (i, j)),
        out_shape=jax.ShapeDtypeStruct((m, ncols), out_dtype),
        scratch_shapes=[pltpu.VMEM((k, bn), BF16)],
        compiler_params=pltpu.CompilerParams(
            dimension_semantics=("arbitrary", "arbitrary"), vmem_limit_bytes=VMEM_LIMIT),
        name="matmul",
    )(*operands)


def _norm_mod(x, gain_scale, shift):
    return x * lax.rsqrt(jnp.mean(x * x, axis=-1, keepdims=True) + EPS) * gain_scale + shift


def _row_vectors(modb_ref, modc_ref, idx, gain_ref=None):
    def vec(ref):
        row = ref[0, idx:idx + 1, :]
        if gain_ref is not None:
            row = gain_ref[...] * (1.0 + row)
        return jnp.broadcast_to(row, (BF16_ROWS, row.shape[1]))

    lat, ctx = vec(modb_ref), vec(modc_ref)
    return lambda is_ctx: lat if is_ctx is None else jnp.where(is_ctx, ctx, lat)


def _head_rmsnorm(x, width):
    parts = []
    for c0 in range(0, x.shape[1], width):
        xh = x[:, c0:c0 + width]
        parts.append(xh * lax.rsqrt(jnp.mean(xh * xh, axis=-1, keepdims=True) + EPS))
    return parts[0] if len(parts) == 1 else jnp.concatenate(parts, axis=1)


def _sigmoid(x):
    return 1.0 / (1.0 + jnp.exp(-x))


def _chunk_rows(r0, rc, tiles_per_batch, bm):
    rows = r0 + lax.broadcasted_iota(jnp.int32, (rc, 1), 0)
    return rows, (pl.program_id(0) % tiles_per_batch) * bm + rows


def _edge_rows(r0, rc, bm, tiles_per_batch, n_ctx, n_seq, gate_ref, hprev_ref, hnext_ref):
    hr = hprev_ref.shape[0]
    pos0 = (pl.program_id(0) % tiles_per_batch) * bm
    if r0 == 0:
        before = hprev_ref[...].astype(F32)[hr - 1:hr, :]
        before = jnp.where((pos0 == 0) | (pos0 == n_ctx), 0.0, before)
    else:
        before = gate_ref[r0 - hr:r0, :].astype(F32)[hr - 1:hr, :]
    if r0 + rc == bm:
        after = hnext_ref[...].astype(F32)[0:1, :]
        after = jnp.where((pos0 + bm == n_ctx) | (pos0 + bm == n_seq), 0.0, after)
    else:
        after = gate_ref[r0 + rc:r0 + rc + hr, :].astype(F32)[0:1, :]
    return before, after


def _splits_rows(r0, rc, bm, tiles_per_batch, n_ctx):
    return any(r0 <= n_ctx - k * bm <= r0 + rc for k in range(tiles_per_batch))


def _dwconv3(s_ref, g, before, after, cw_ref, pos, n_ctx):
    rc = g.shape[0]
    s_ref[SUBLANES:SUBLANES + rc, :] = g
    s_ref[SUBLANES - 1:SUBLANES, :] = before
    s_ref[SUBLANES + rc:SUBLANES + rc + 1, :] = after
    prev = s_ref[SUBLANES - 1:SUBLANES - 1 + rc, :]
    nxt = s_ref[SUBLANES + 1:SUBLANES + 1 + rc, :]
    if pos is not None:
        prev = jnp.where(pos == n_ctx, 0.0, prev)
        nxt = jnp.where(pos == n_ctx - 1, 0.0, nxt)
    return prev * cw_ref[0:1, :] + g * cw_ref[1:2, :] + nxt * cw_ref[2:3, :]


def _mod_specs(n_batch, tiles_per_batch, d):
    return [pl.BlockSpec((1, 6, d), lambda i, *_: (i // tiles_per_batch, 0, 0)),
            pl.BlockSpec((1, 6, d), lambda i, *_: (n_batch, 0, 0))]


def _halo_specs(bm, width, col, n_rows, hr):
    hb = bm // hr
    last = n_rows // hr - 1
    return [pl.BlockSpec((hr, width), lambda i, *k: (jnp.maximum(i * hb - 1, 0), col(*k))),
            pl.BlockSpec((hr, width), lambda i, *k: (jnp.minimum((i + 1) * hb, last), col(*k)))]


def _norm_kernel(x_ref, gain_ref, modb_ref, modc_ref, o_ref, *, n_ctx, tiles_per_batch):
    gain_scale = _row_vectors(modb_ref, modc_ref, 1, gain_ref)
    shift = _row_vectors(modb_ref, modc_ref, 0)

    def rows(rs, is_ctx):
        o_ref[rs, :] = _norm_mod(x_ref[rs, :], gain_scale(is_ctx), shift(is_ctx)).astype(BF16)

    _for_row_groups(x_ref.shape[0], n_ctx, tiles_per_batch, rows)


def _first_norm(x, gain, mod, *, n_ctx, n_seq, bm=ROW_TILE):
    m, d = x.shape
    tpb = n_seq // bm
    return pl.pallas_call(
        functools.partial(_norm_kernel, n_ctx=n_ctx, tiles_per_batch=tpb),
        grid=(m // bm,),
        in_specs=[pl.BlockSpec((bm, d), lambda i: (i, 0)), pl.BlockSpec((1, d), lambda i: (0, 0))]
        + _mod_specs(mod.shape[0] - 1, tpb, d),
        out_specs=pl.BlockSpec((bm, d), lambda i: (i, 0)),
        out_shape=jax.ShapeDtypeStruct((m, d), BF16),
        compiler_params=pltpu.CompilerParams(vmem_limit_bytes=VMEM_LIMIT),
        name="first_norm",
    )(x, gain, mod, mod)


def _residual_norm(xo_ref, hn_ref, x_ref, modb_ref, modc_ref, gain_ref, nmodb_ref, nmodc_ref, *,
                   n_ctx, tiles_per_batch, gate_idx, shift_idx, inline=False):
    gate = _row_vectors(modb_ref, modc_ref, gate_idx)
    gain_scale = _row_vectors(nmodb_ref, nmodc_ref, shift_idx + 1, gain_ref)
    shift = _row_vectors(nmodb_ref, nmodc_ref, shift_idx)

    def rows(rs, is_ctx):
        x_new = x_ref[rs, :] + gate(is_ctx) * xo_ref[rs, :]
        xo_ref[rs, :] = x_new
        hn_ref[rs, :] = _norm_mod(x_new, gain_scale(is_ctx), shift(is_ctx)).astype(BF16)

    _for_row_groups(x_ref.shape[0], n_ctx, tiles_per_batch, rows, inline)


def _for_row_groups(bm, n_ctx, tiles_per_batch, fn, inline=False):
    rg = BF16_ROWS
    pos0 = (pl.program_id(0) % tiles_per_batch) * bm
    n_lead = -(-min(n_ctx, bm) // rg)
    if inline:
        for g in range(bm // rg):
            is_ctx = (pos0 + g * rg + lax.broadcasted_iota(jnp.int32, (rg, 1), 0) < n_ctx) if g < n_lead else None
            fn(slice(g * rg, (g + 1) * rg), is_ctx)
        return

    def group(with_ctx):
        def body(g, carry):
            r0 = pl.multiple_of(g * rg, rg)
            is_ctx = (pos0 + r0 + lax.broadcasted_iota(jnp.int32, (rg, 1), 0) < n_ctx) if with_ctx else None
            fn(pl.ds(r0, rg), is_ctx)
            return carry
        return body

    lax.fori_loop(0, n_lead, group(True), 0, unroll=ROW_GROUP_UNROLL)
    lax.fori_loop(n_lead, bm // rg, group(False), 0, unroll=ROW_GROUP_UNROLL)


def _out_kernel(*refs, n_in, n_ctx, tiles_per_batch):
    a_refs = refs[:n_in]
    w_ref, x_ref, modb_ref, modc_ref, gain_ref, xo_ref, hn_ref, wb_ref = refs[n_in:]

    @pl.when(pl.program_id(0) == 0)
    def _():
        for r0 in range(0, w_ref.shape[0], ROW_CHUNK):
            r1 = min(r0 + ROW_CHUNK, w_ref.shape[0])
            wb_ref[r0:r1, :] = w_ref[r0:r1, :].astype(BF16)

    k0 = 0
    for n_done, a_ref in enumerate(a_refs):
        part = jnp.dot(a_ref[...], wb_ref[k0:k0 + a_ref.shape[1], :], preferred_element_type=F32)
        xo_ref[...] = part if n_done == 0 else xo_ref[...] + part
        k0 += a_ref.shape[1]
    _residual_norm(xo_ref, hn_ref, x_ref, modb_ref, modc_ref, gain_ref, modb_ref, modc_ref,
                   n_ctx=n_ctx, tiles_per_batch=tiles_per_batch, gate_idx=2, shift_idx=3, inline=True)


def _out_proj(acts, w, layer, x, mod, gain, *, n_ctx, n_seq, bm=ROW_TILE // 2):
    m, d = x.shape
    kk = w.shape[1]
    tpb = n_seq // bm
    rows = lambda width: pl.BlockSpec((bm, width), lambda i: (i, 0))
    return pl.pallas_call(
        functools.partial(_out_kernel, n_in=len(acts), n_ctx=n_ctx, tiles_per_batch=tpb),
        grid=(m // bm,),
        in_specs=[rows(a.shape[1]) for a in acts]
        + [pl.BlockSpec((None, kk, d), lambda i: (layer, 0, 0), pipeline_mode=pl.Buffered(1)), rows(d)]
        + _mod_specs(mod.shape[0] - 1, tpb, d) + [pl.BlockSpec((1, d), lambda i: (0, 0))],
        out_specs=[rows(d), rows(d)],
        out_shape=[jax.ShapeDtypeStruct((m, d), F32), jax.ShapeDtypeStruct((m, d), BF16)],
        scratch_shapes=[pltpu.VMEM((kk, d), BF16)],
        compiler_params=pltpu.CompilerParams(
            dimension_semantics=("arbitrary",), vmem_limit_bytes=VMEM_LIMIT),
        name="out_proj",
    )(*acts, w, x, mod, mod, gain)


def _up_kernel(x_ref, xp_ref, xn_ref, wg_ref, wv_ref, cw_ref, cb_ref, wd_ref, o_ref, wdb_ref, wgb_ref,
               wvb_ref, xe_ref, s_ref, v_ref, *, n_ctx, n_seq, tiles_per_batch):
    i = pl.program_id(1)
    bm, rc, hr = x_ref.shape[0], ROW_CHUNK, xp_ref.shape[0]
    wdb_ref[...] = wd_ref[...].astype(BF16)

    @pl.when(i == 0)
    def _():
        wgb_ref[...] = wg_ref[...].astype(BF16)
        wvb_ref[...] = wv_ref[...].astype(BF16)

    xe_ref[0:hr, :] = xp_ref[...]
    xe_ref[hr:hr + bm, :] = x_ref[...]
    xe_ref[hr + bm:, :] = xn_ref[...]
    s_ref[...] = jnp.dot(xe_ref[...], wgb_ref[...], preferred_element_type=F32)
    v_ref[...] = jnp.dot(x_ref[...], wvb_ref[...], preferred_element_type=F32)
    pos0 = (i % tiles_per_batch) * bm
    s_ref[hr - 1:hr, :] = jnp.where((pos0 == 0) | (pos0 == n_ctx), 0.0, s_ref[hr - 1:hr, :])
    s_ref[hr + bm:hr + bm + 1, :] = jnp.where(
        (pos0 + bm == n_ctx) | (pos0 + bm == n_seq), 0.0, s_ref[hr + bm:hr + bm + 1, :])
    for r0 in range(0, bm, rc):
        rs = slice(r0, r0 + rc)
        prev = s_ref[hr - 1 + r0:hr - 1 + r0 + rc, :]
        nxt = s_ref[hr + 1 + r0:hr + 1 + r0 + rc, :]
        if _splits_rows(r0, rc, bm, tiles_per_batch, n_ctx):
            pos = pos0 + r0 + lax.broadcasted_iota(jnp.int32, (rc, 1), 0)
            prev = jnp.where(pos == n_ctx, 0.0, prev)
            nxt = jnp.where(pos == n_ctx - 1, 0.0, nxt)
        z = (prev * cw_ref[0:1, :] + s_ref[hr + r0:hr + r0 + rc, :] * cw_ref[1:2, :]
             + nxt * cw_ref[2:3, :] + cb_ref[...])
        o_ref[rs, :] = (z * _sigmoid(z) * v_ref[rs, :]).astype(BF16)


def _ffn_up(x, w, w_down, layer, conv_w, conv_b, *, n_ctx, n_seq, bn=512, bm=ROW_TILE):
    m, k = x.shape
    f = w.shape[2] // 2
    nj, ni = f // bn, m // bm
    slab = f // (nj * ni)
    assert slab * nj * ni == f and slab % BF16_ROWS == 0
    d_out = w_down.shape[2]
    tpb = n_seq // bm
    hb = bm // BF16_ROWS
    last = m // BF16_ROWS - 1
    return pl.pallas_call(
        functools.partial(_up_kernel, n_ctx=n_ctx, n_seq=n_seq, tiles_per_batch=tpb),
        grid=(nj, ni),
        in_specs=[pl.BlockSpec((bm, k), lambda j, i: (i, 0)),
                  pl.BlockSpec((BF16_ROWS, k), lambda j, i: (jnp.maximum(i * hb - 1, 0), 0)),
                  pl.BlockSpec((BF16_ROWS, k), lambda j, i: (jnp.minimum((i + 1) * hb, last), 0)),
                  pl.BlockSpec((None, k, bn), lambda j, i: (layer, 0, j)),
                  pl.BlockSpec((None, k, bn), lambda j, i: (layer, 0, j + nj)),
                  pl.BlockSpec((3, bn), lambda j, i: (0, j)), pl.BlockSpec((1, bn), lambda j, i: (0, j)),
                  pl.BlockSpec((None, slab, d_out), lambda j, i: (layer, j * ni + i, 0))],
        out_specs=[pl.BlockSpec((bm, bn), lambda j, i: (i, j)),
                   pl.BlockSpec((slab, d_out), lambda j, i: (j * ni + i, 0))],
        out_shape=[jax.ShapeDtypeStruct((m, f), BF16), jax.ShapeDtypeStruct((f, d_out), BF16)],
        scratch_shapes=[pltpu.VMEM((k, bn), BF16), pltpu.VMEM((k, bn), BF16),
                        pltpu.VMEM((bm + 2 * BF16_ROWS, k), BF16),
                        pltpu.VMEM((bm + 2 * BF16_ROWS, bn), F32), pltpu.VMEM((bm, bn), F32)],
        compiler_params=pltpu.CompilerParams(
            dimension_semantics=("arbitrary", "arbitrary"), vmem_limit_bytes=VMEM_LIMIT),
        name="ffn_up",
    )(x, x, x, w, w, conv_w, conv_b, w_down)


def _down_kernel(a_ref, w_ref, x_ref, modb_ref, modc_ref, gain_ref, nmodb_ref, nmodc_ref, xo_ref, hn_ref, *,
                 n_ctx, tiles_per_batch):
    part = jnp.dot(a_ref[...], w_ref[...], preferred_element_type=F32)

    @pl.when(pl.program_id(1) == 0)
    def _():
        xo_ref[...] = part

    @pl.when(pl.program_id(1) > 0)
    def _():
        xo_ref[...] += part

    @pl.when(pl.program_id(1) == pl.num_programs(1) - 1)
    def _():
        _residual_norm(xo_ref, hn_ref, x_ref, modb_ref, modc_ref, gain_ref, nmodb_ref, nmodc_ref,
                       n_ctx=n_ctx, tiles_per_batch=tiles_per_batch, gate_idx=5, shift_idx=0)


def _down_proj(a, w, x, mod, gain, next_mod, *, n_ctx, n_seq, bm=ROW_TILE // 2, bk=1408):
    m, f = a.shape
    d = w.shape[1]
    tpb = n_seq // bm
    row = lambda i, k: (i, 0)
    return pl.pallas_call(
        functools.partial(_down_kernel, n_ctx=n_ctx, tiles_per_batch=tpb),
        grid=(m // bm, f // bk),
        in_specs=[pl.BlockSpec((bm, bk), lambda i, k: (i, k)), pl.BlockSpec((bk, d), lambda i, k: (k, 0)),
                  pl.BlockSpec((bm, d), row)]
        + _mod_specs(mod.shape[0] - 1, tpb, d) + [pl.BlockSpec((1, d), lambda i, k: (0, 0))]
        + _mod_specs(mod.shape[0] - 1, tpb, d),
        out_specs=[pl.BlockSpec((bm, d), row), pl.BlockSpec((bm, d), row)],
        out_shape=[jax.ShapeDtypeStruct((m, d), F32), jax.ShapeDtypeStruct((m, d), BF16)],
        compiler_params=pltpu.CompilerParams(
            dimension_semantics=("arbitrary", "arbitrary"), vmem_limit_bytes=VMEM_LIMIT),
        name="down_proj",
    )(a, w, x, mod, mod, gain, next_mod, next_mod)


def _even_finish_kernel(of_ref, ob_ref, r_ref, sx_ref, sb_ref, sg_ref, sxp_ref, sxn_ref, sgp_ref, sgn_ref,
                        gain_ref, cw_ref, o_ref, s_ref, *, n_ctx, n_seq, tiles_per_batch):
    bm = of_ref.shape[0]
    nv = of_ref.shape[1]
    r = r_ref[...]
    o_ref[:, :nv] = (_head_rmsnorm(of_ref[...] + ob_ref[...], GLA_DV) * gain_ref[...]
                     * (r * _sigmoid(r))).astype(BF16)

    _, pos = _chunk_rows(0, bm, tiles_per_batch, bm)
    pos0 = (pl.program_id(0) % tiles_per_batch) * bm
    hr = sxp_ref.shape[0]
    before = (sxp_ref[...] * sgp_ref[...])[hr - 1:hr, :]
    before = jnp.where((pos0 == 0) | (pos0 == n_ctx), 0.0, before)
    after = (sxn_ref[...] * sgn_ref[...])[0:1, :]
    after = jnp.where((pos0 + bm == n_ctx) | (pos0 + bm == n_seq), 0.0, after)
    split = pos if _splits_rows(0, bm, bm, tiles_per_batch, n_ctx) else None
    conv = _dwconv3(s_ref, sg_ref[...] * sx_ref[...], before, after, cw_ref, split, n_ctx)
    o_ref[:, nv:] = (sb_ref[...] * conv).astype(BF16)


def _even_finish(o_f, o_b, pa, pb, gain, conv_w, *, n_ctx, n_seq, bm=ROW_CHUNK):
    m, nv = o_f.shape
    sc = conv_w.shape[1]
    tpb = n_seq // bm
    col = lambda j: (lambda i: (i, j))
    const = lambda i: (0, 0)
    return pl.pallas_call(
        functools.partial(_even_finish_kernel, n_ctx=n_ctx, n_seq=n_seq, tiles_per_batch=tpb),
        grid=(m // bm,),
        in_specs=[pl.BlockSpec((bm, nv), col(0)), pl.BlockSpec((bm, nv), col(0)),
                  pl.BlockSpec((bm, nv), col(2)),
                  pl.BlockSpec((bm, sc), col(0)), pl.BlockSpec((bm, sc), col(1)), pl.BlockSpec((bm, sc), col(2))]
        + _halo_specs(bm, sc, lambda: 0, m, SUBLANES) + _halo_specs(bm, sc, lambda: 2, m, SUBLANES)
        + [pl.BlockSpec((1, nv), const), pl.BlockSpec((3, sc), const)],
        out_specs=pl.BlockSpec((bm, nv + sc), col(0)),
        out_shape=jax.ShapeDtypeStruct((m, nv + sc), BF16),
        scratch_shapes=[pltpu.VMEM((bm + 2 * SUBLANES, sc), F32)],
        compiler_params=pltpu.CompilerParams(vmem_limit_bytes=VMEM_LIMIT),
        name="even_finish",
    )(o_f, o_b, pa, pb, pb, pb, pb, pb, pb, pb, gain, conv_w)


def _mlstm_prep_kernel(x_ref, xp_ref, xn_ref, cw_ref, scale_ref, o_ref, s_ref, *, n_ctx, n_seq,
                       tiles_per_batch):
    bm = x_ref.shape[0]
    _, pos = _chunk_rows(0, bm, tiles_per_batch, bm)
    before, after = _edge_rows(0, bm, bm, tiles_per_batch, n_ctx, n_seq, x_ref, xp_ref, xn_ref)
    split = pos if _splits_rows(0, bm, bm, tiles_per_batch, n_ctx) else None
    z = _dwconv3(s_ref, x_ref[...], before, after, cw_ref, split, n_ctx)
    o_ref[...] = (z * _sigmoid(z) * scale_ref[...]).astype(BF16)


def _mlstm_prep(p, conv_w, scale, *, col, n_ctx, n_seq, bm=ROW_CHUNK):
    m = p.shape[0]
    width = conv_w.shape[1]
    tpb = n_seq // bm
    const = lambda i: (0, 0)
    return pl.pallas_call(
        functools.partial(_mlstm_prep_kernel, n_ctx=n_ctx, n_seq=n_seq, tiles_per_batch=tpb),
        grid=(m // bm,),
        in_specs=[pl.BlockSpec((bm, width), lambda i: (i, col))]
        + _halo_specs(bm, width, lambda: col, m, SUBLANES)
        + [pl.BlockSpec((3, width), const), pl.BlockSpec((1, width), const)],
        out_specs=pl.BlockSpec((bm, width), lambda i: (i, 0)),
        out_shape=jax.ShapeDtypeStruct((m, width), BF16),
        scratch_shapes=[pltpu.VMEM((bm + 2 * SUBLANES, width), F32)],
        compiler_params=pltpu.CompilerParams(vmem_limit_bytes=VMEM_LIMIT),
        name="mlstm_prep",
    )(p, p, p, conv_w, scale)


def _mlstm_finish_kernel(hf_ref, hb_ref, mo_ref, gain_ref, o_ref):
    o_ref[...] = (_head_rmsnorm(hf_ref[...] + hb_ref[...], ML_DV) * gain_ref[...]
                  * _sigmoid(mo_ref[...])).astype(BF16)


def _mlstm_finish(h_f, h_b, p, gain, *, col, bm=ROW_TILE // 2):
    m, nv = h_f.shape
    return pl.pallas_call(
        _mlstm_finish_kernel,
        grid=(m // bm,),
        in_specs=[pl.BlockSpec((bm, nv), lambda i: (i, 0)), pl.BlockSpec((bm, nv), lambda i: (i, 0)),
                  pl.BlockSpec((bm, nv), lambda i: (i, col)), pl.BlockSpec((1, nv), lambda i: (0, 0))],
        out_specs=pl.BlockSpec((bm, nv), lambda i: (i, 0)),
        out_shape=jax.ShapeDtypeStruct((m, nv), BF16),
        compiler_params=pltpu.CompilerParams(vmem_limit_bytes=VMEM_LIMIT),
        name="mlstm_finish",
    )(h_f, h_b, p, gain)


def _qk_prep(x, cos, sin, gain):
    lane = lax.broadcasted_iota(jnp.int32, x.shape, 1)
    lo = lane < DA_DQK
    x2 = x * x
    ss_lo = jnp.sum(jnp.where(lo, x2, 0.0), axis=-1, keepdims=True)
    ss_hi = jnp.sum(jnp.where(lo, 0.0, x2), axis=-1, keepdims=True)
    y = x * lax.rsqrt(jnp.where(lo, ss_lo, ss_hi) * (1.0 / DA_DQK) + EPS) * gain
    swapped = jnp.where((lane & ROPE_PAIRS) == 0, pltpu.roll(y, LANES - ROPE_PAIRS, 1),
                        pltpu.roll(y, ROPE_PAIRS, 1))
    return y * cos + swapped * sin


def _attn_kernel(lam_ref, q_ref, k_ref, v_ref, cos_ref, sin_ref, qg_ref, kg_ref, og_ref, o_ref,
                 kb_ref, vb_ref, *, n_ctx, chunk, prep_rows, out_scale):
    qi = pl.program_id(2)
    lam = lam_ref[0]
    n = k_ref.shape[1]
    bq = q_ref.shape[1]
    heads = q_ref.shape[2] // LANES

    @pl.when(qi == 0)
    def _():
        for r0 in range(0, n, prep_rows):
            rs = slice(r0, r0 + prep_rows)
            for h in range(heads):
                hs = slice(h * LANES, (h + 1) * LANES)
                kb_ref[h, rs, :] = _qk_prep(k_ref[0, rs, hs], cos_ref[rs, :], sin_ref[rs, :],
                                            kg_ref[...]).astype(BF16)
                vb_ref[h, rs, :LANES] = v_ref[0, rs, hs].astype(BF16)
                vb_ref[h, rs, LANES:] = jnp.ones((prep_rows, LANES), BF16)

    def attend(r0, r1, nk):
        nr = r1 - r0
        rows = pl.ds(pl.multiple_of(qi * bq, BF16_ROWS) + r0, nr)
        cos, sin = cos_ref[rows, :], sin_ref[rows, :]
        lane = lax.broadcasted_iota(jnp.int32, (nr, LANES), 1)
        q2, m, acc = [], [], []
        for h in range(heads):
            q = _qk_prep(q_ref[0, r0:r1, h * LANES:(h + 1) * LANES], cos, sin, qg_ref[...])
            q = (q * (DA_DQK ** -0.5 * LOG2E)).astype(BF16)
            zero = jnp.zeros_like(q)
            q2.append(jnp.concatenate([jnp.where(lane < DA_DQK, q, zero),
                                       jnp.where(lane >= DA_DQK, q, zero)], axis=0))
            m.append(jnp.full((2 * nr, 1), -jnp.inf, F32))
            acc.append(jnp.zeros((2 * nr, 2 * LANES), F32))
        for c0 in range(0, nk, chunk):
            c1 = min(c0 + chunk, nk)
            for h in range(heads):
                s = lax.dot_general(q2[h], kb_ref[h, c0:c1, :], NT, preferred_element_type=F32)
                m_new = jnp.maximum(m[h], jnp.max(s, axis=-1, keepdims=True))
                p = jnp.exp2(s - m_new)
                acc[h] = jnp.exp2(m[h] - m_new) * acc[h] + jnp.dot(
                    p.astype(BF16), vb_ref[h, c0:c1, :], preferred_element_type=F32)
                m[h] = m_new
        for h in range(heads):
            o = acc[h][:, :LANES] / acc[h][:, LANES:]
            o = o[:nr] - lam * o[nr:]
            o = o * lax.rsqrt(jnp.mean(o * o, axis=-1, keepdims=True) + EPS) * og_ref[...] * out_scale
            o_ref[0, r0:r1, h * LANES:(h + 1) * LANES] = o.astype(BF16)

    last = pl.num_programs(2) - 1
    last_rows = n - (n // bq) * bq if n % bq else bq

    @pl.when(qi == 0)
    def _():
        attend(0, n_ctx, n_ctx)
        for r0 in range(n_ctx, bq, ATTN_SUB):
            attend(r0, r0 + ATTN_SUB, n)

    @pl.when((qi > 0) & (qi < last))
    def _():
        for r0 in range(0, bq, ATTN_SUB):
            attend(r0, r0 + ATTN_SUB, n)

    @pl.when((qi > 0) & (qi == last))
    def _():
        for r0 in range(0, last_rows, ATTN_SUB):
            attend(r0, r0 + ATTN_SUB, n)


def _diff_attention(p, lam, cos, sin, q_gain, k_gain, out_gain, *, n_ctx, out_scale, bq=ATTN_BQ,
                    chunk=ATTN_CHUNK, prep_rows=ATTN_PREP, heads=ATTN_HEADS):
    b, n, _ = p.shape
    assert n_ctx % ATTN_SUB == 0 and bq % ATTN_SUB == 0 and n % ATTN_SUB == 0 and n_ctx <= bq
    hg = DA_HEADS // heads
    hd = DA_HEADS * LANES
    hw = heads * LANES
    const = lambda bi, hi, qi: (0, 0)
    return pl.pallas_call(
        functools.partial(_attn_kernel, n_ctx=n_ctx, chunk=chunk, prep_rows=prep_rows, out_scale=out_scale),
        grid=(b, hg, pl.cdiv(n, bq)),
        in_specs=[pl.BlockSpec(memory_space=pltpu.SMEM),
                  pl.BlockSpec((1, bq, hw), lambda bi, hi, qi: (bi, qi, hi)),
                  pl.BlockSpec((1, n, hw), lambda bi, hi, qi: (bi, 0, hg + hi)),
                  pl.BlockSpec((1, n, hw), lambda bi, hi, qi: (bi, 0, 2 * hg + hi)),
                  pl.BlockSpec((n, LANES), const), pl.BlockSpec((n, LANES), const),
                  pl.BlockSpec((1, LANES), const), pl.BlockSpec((1, LANES), const),
                  pl.BlockSpec((1, LANES), const)],
        out_specs=pl.BlockSpec((1, bq, hw), lambda bi, hi, qi: (bi, qi, hi)),
        out_shape=jax.ShapeDtypeStruct((b, n, hd), BF16),
        scratch_shapes=[pltpu.VMEM((heads, n, LANES), BF16), pltpu.VMEM((heads, n, 2 * LANES), BF16)],
        compiler_params=pltpu.CompilerParams(
            dimension_semantics=("arbitrary", "arbitrary", "arbitrary"), vmem_limit_bytes=VMEM_LIMIT),
        name="diff_attention",
    )(lam, p, p, p, cos, sin, q_gain, k_gain, out_gain)


def _scan_chunk(t, n_ctx_chunks, n_chunks, reverse):
    if not reverse:
        return t
    return jnp.where(t < n_ctx_chunks, n_ctx_chunks - 1 - t, n_chunks - 1 - (t - n_ctx_chunks))


def _gla_constants(L, reverse):
    nlev = int(math.log2(L))
    idx = np.arange(L)
    i, t = idx[:, None], idx[None, :]
    if reverse:
        i, t = L - 1 - i, L - 1 - t
    rs = [(t <= i), (t > i)]
    am = [(i == t)]
    for lev in range(nlev):
        m = L >> (lev + 1)
        blk_i, blk_t = i // (2 * m), t // (2 * m)
        mid = blk_i * 2 * m + m
        q_role = i >= mid
        rs.append(np.where(q_role, (t >= mid) & (t <= i), (t > i) & (t < mid)) & (blk_i == blk_t))
        am.append((blk_i == blk_t) & q_role & (t < mid))
    return (np.stack(rs).astype(np.float32).reshape((nlev + 2) * L, L),
            np.stack(am).astype(np.float32))


def _gla_kernel(*refs):
    of_ref, ob_ref, st_ref = refs[16:]

    @pl.when(pl.program_id(1) == 0)
    def _():
        st_ref[...] = jnp.zeros_like(st_ref)

    for d, o_ref in enumerate((of_ref, ob_ref)):
        _gla_direction(*refs[8 * d:8 * d + 8], o_ref, st_ref, d * GLA_HEADS, bool(d))


def _gla_direction(q_ref, k_ref, v_ref, glr_ref, w2_ref, gb_ref, rsum_ref, amask_ref, o_ref, st_ref, h0,
                   reverse):
    L = q_ref.shape[1]
    nlev = amask_ref.shape[0] - 1
    n_coarse = nlev + 2 - rsum_ref.shape[0] // L

    z = jnp.dot(glr_ref[0].astype(BF16), w2_ref[...], preferred_element_type=F32) + gb_ref[...]
    g = (jnp.minimum(z, 0.0) - jnp.log(1.0 + jnp.exp(-jnp.abs(z)))) * (1.0 / GLA_TAU)
    g1 = g.astype(BF16)
    g2 = (g - g1.astype(F32)).astype(BF16)
    g3 = (g - g1.astype(F32) - g2.astype(F32)).astype(BF16)
    rsum = rsum_ref[...]
    e_all = (jnp.dot(rsum, g1, preferred_element_type=F32) + jnp.dot(rsum, g2, preferred_element_type=F32)
             + jnp.dot(rsum, g3, preferred_element_type=F32))
    b_tot = jnp.sum(g, axis=0, keepdims=True)

    def level_decay(lev, ck):
        if lev >= n_coarse:
            return jnp.exp(e_all[(2 + lev - n_coarse) * L:(3 + lev - n_coarse) * L, ck])
        m = L >> (lev + 1)
        b3 = e_all[0:L, ck].reshape(L // (2 * m), 2 * m, GLA_DK)
        pivot = m if reverse else m - 1
        return jnp.exp(-jnp.abs(b3 - b3[:, pivot:pivot + 1, :])).reshape(L, GLA_DK)

    for h in range(GLA_HEADS):
        ck = slice(h * GLA_DK, (h + 1) * GLA_DK)
        cv = slice(h * GLA_DV, (h + 1) * GLA_DV)
        q = q_ref[0, :, ck] * (GLA_DK ** -0.5)
        k = k_ref[0, :, ck]
        v = v_ref[0, :, cv].astype(BF16)
        st = st_ref[h0 + h]
        a = amask_ref[0] * lax.dot_general(q.astype(BF16), k.astype(BF16), NT, preferred_element_type=F32)
        for lev in range(nlev):
            e = level_decay(lev, ck)
            a = a + amask_ref[1 + lev] * lax.dot_general(
                (q * e).astype(BF16), (k * e).astype(BF16), NT, preferred_element_type=F32)
        qe = (q * jnp.exp(e_all[0:L, ck])).astype(BF16)
        o = lax.dot_general(qe, st.astype(BF16), NT, preferred_element_type=F32)
        o = o + jnp.dot(a.astype(BF16), v, preferred_element_type=F32)
        o_ref[0, :, cv] = o
        kd = (k * jnp.exp(e_all[L:2 * L, ck])).astype(BF16)
        st_ref[h0 + h] = jnp.exp(b_tot[:, ck]) * st + jnp.dot(
            v.T, kd, preferred_element_type=F32)


def _gla_scan(p, glr, w2s, gbs, *, n_ctx, L=GLA_CHUNK):
    b, n, _ = p.shape
    nc, ncc = n // L, n_ctx // L
    hk, hv = GLA_HEADS * GLA_DK, GLA_HEADS * GLA_DV
    const2 = lambda bi, t: (0, 0)
    const3 = lambda bi, t: (0, 0, 0)
    in_specs, operands, out_specs = [], [], []
    for d in range(2):
        rsum, amask = _gla_constants(L, bool(d))
        n_coarse = sum(2 * (L >> (lev + 1)) >= SUBLANES for lev in range(amask.shape[0] - 1))
        rsum = rsum.reshape(-1, L, L)
        rsum = np.concatenate([rsum[:2], rsum[2 + n_coarse:]]).reshape(-1, L)
        chunk = functools.partial(_scan_chunk, n_ctx_chunks=ncc, n_chunks=nc, reverse=bool(d))
        cols = lambda width, start, chunk=chunk: pl.BlockSpec(
            (1, L, width), lambda bi, t: (bi, chunk(t), start // width))
        in_specs += [cols(hk, 0), cols(hk, hk), cols(hv, 2 * hk), cols(LANES, 0),
                     pl.BlockSpec(w2s[d].shape, const2), pl.BlockSpec(gbs[d].shape, const2),
                     pl.BlockSpec(rsum.shape, const2), pl.BlockSpec(amask.shape, const3)]
        operands += [p, p, p, glr, w2s[d], gbs[d], jnp.asarray(rsum, BF16), jnp.asarray(amask)]
        out_specs.append(cols(hv, 0))
    return pl.pallas_call(
        _gla_kernel,
        grid=(b, nc),
        in_specs=in_specs,
        out_specs=out_specs,
        out_shape=[jax.ShapeDtypeStruct((b, n, hv), F32)] * 2,
        scratch_shapes=[pltpu.VMEM((2 * GLA_HEADS, GLA_DV, GLA_DK), F32)],
        compiler_params=pltpu.CompilerParams(
            dimension_semantics=("arbitrary", "arbitrary"), vmem_limit_bytes=VMEM_LIMIT),
        name="gla_scan",
    )(*operands)


def _log_sigmoid(x):
    return jnp.minimum(x, 0.0) - jnp.log(1.0 + jnp.exp(-jnp.abs(x)))


def _mlstm_kernel(*refs):
    tri_ref, of_ref, ob_ref, c_ref, m_ref = refs[10:]

    @pl.when(pl.program_id(1) == 0)
    def _():
        c_ref[...] = jnp.zeros_like(c_ref)
        m_ref[...] = jnp.zeros_like(m_ref)

    for d, o_ref in enumerate((of_ref, ob_ref)):
        _mlstm_direction(*refs[5 * d:5 * d + 5], tri_ref[d], o_ref, c_ref, m_ref, d * ML_HEADS, bool(d))


def _mlstm_direction(q_ref, k_ref, v_ref, gc_ref, gr_ref, tri, o_ref, c_ref, m_ref, h0, reverse):
    L = q_ref.shape[1]
    H = ML_HEADS
    ones = jnp.ones((L, LANES), BF16)
    gc = gc_ref[0]
    gr = gr_ref[0]
    ic_col, ic_row = gc[:, :H], gr[:H, :]
    b_col = jnp.dot(tri, _log_sigmoid(gc[:, H:]), precision=HI, preferred_element_type=F32)
    b_row = lax.dot_general(_log_sigmoid(gr[H:, :]), tri, NT, precision=HI, preferred_element_type=F32)
    last = 0 if reverse else L - 1
    causal = tri > 0.5

    for h in range(H):
        ck = slice(h * ML_DK, (h + 1) * ML_DK)
        cv = slice(h * ML_DV, (h + 1) * ML_DV)
        q = q_ref[0, :, ck]
        k = k_ref[0, :, ck]
        v = jnp.concatenate([v_ref[0, :, cv].astype(BF16), ones], axis=1)
        c, m = c_ref[h0 + h], m_ref[h0 + h]
        bc, br = b_col[:, h:h + 1], b_row[h:h + 1, :]
        icc, icr = ic_col[:, h:h + 1], ic_row[h:h + 1, :]
        b_last = bc[last:last + 1, :]

        a = bc + m
        dmat = jnp.where(causal, bc - br + icr, -jnp.inf)
        m_t = jnp.maximum(a, jnp.max(dmat, axis=-1, keepdims=True))
        w_inter = jnp.exp(a - m_t)
        s = lax.dot_general(q, k, NT, preferred_element_type=F32) * jnp.exp(dmat - m_t)
        both = w_inter * jnp.dot(q, c.astype(BF16), preferred_element_type=F32) + jnp.dot(
            s.astype(BF16), v, preferred_element_type=F32)
        den = jnp.maximum(jnp.abs(both[:, ML_DV:]), jnp.exp(-m_t))
        o_ref[0, :, cv] = both[:, :ML_DV] / jnp.concatenate([den] * (ML_DV // LANES), axis=1)

        gs_col = b_last - bc + icc
        gs_row = b_last - br + icr
        m_new = jnp.maximum(b_last + m, jnp.max(gs_row, axis=-1, keepdims=True))
        decay = jnp.exp(b_last + m - m_new)
        wk = jnp.exp(gs_col - m_new) * k.astype(F32)
        c_ref[h0 + h] = decay * c + jnp.dot(wk.astype(BF16).T, v, preferred_element_type=F32)
        m_ref[h0 + h] = m_new


def _mlstm_scan(qk, p, gates, *, v_col, n_ctx, L=ML_CHUNK):
    b, n, _ = qk.shape
    hk, hv, ng = ML_HEADS * ML_DK, ML_HEADS * ML_DV, 2 * ML_HEADS
    nc, ncc = n // L, n_ctx // L
    idx = np.arange(L)
    lower = idx[None, :] <= idx[:, None]
    tri = jnp.asarray(np.stack([lower, lower.T]).astype(np.float32))
    in_specs, operands, out_specs = [], [], []
    for d in range(2):
        chunk = functools.partial(_scan_chunk, n_ctx_chunks=ncc, n_chunks=nc, reverse=bool(d))
        col = lambda j, chunk=chunk: (lambda bi, t: (bi, chunk(t), j))
        g_dir = gates[..., d * ng:(d + 1) * ng]
        in_specs += [pl.BlockSpec((1, L, hk), col(0)), pl.BlockSpec((1, L, hk), col(1)),
                     pl.BlockSpec((1, L, hv), col(v_col)), pl.BlockSpec((1, L, ng), col(0)),
                     pl.BlockSpec((1, ng, L), lambda bi, t, chunk=chunk: (bi, 0, chunk(t)))]
        operands += [qk, qk, p, g_dir, jnp.swapaxes(g_dir, 1, 2)]
        out_specs.append(pl.BlockSpec((1, L, hv), col(0)))
    return pl.pallas_call(
        _mlstm_kernel,
        grid=(b, nc),
        in_specs=in_specs + [pl.BlockSpec((2, L, L), lambda bi, t: (0, 0, 0))],
        out_specs=out_specs,
        out_shape=[jax.ShapeDtypeStruct((b, n, hv), F32)] * 2,
        scratch_shapes=[pltpu.VMEM((2 * ML_HEADS, ML_DK, ML_DV + LANES), F32),
                        pltpu.VMEM((2 * ML_HEADS, 1, 1), F32)],
        compiler_params=pltpu.CompilerParams(
            dimension_semantics=("arbitrary", "arbitrary"), vmem_limit_bytes=VMEM_LIMIT),
        name="mlstm_scan",
    )(*operands, tri)


def _silu(x):
    return x * jax.nn.sigmoid(x)


def _rope_tables(n_lat, n_ctx):
    rows = n_lat // GRID_W
    row = jnp.repeat(jnp.arange(rows, dtype=F32), GRID_W)
    col = jnp.tile(jnp.arange(GRID_W, dtype=F32), rows)
    inv = jnp.power(ROPE_BASE, -jnp.arange(ROPE_PAIRS, dtype=F32) / ROPE_PAIRS)
    ang_r, ang_c = row[:, None] * inv, col[:, None] * inv
    cos = jnp.concatenate([jnp.cos(ang_r)] * 2 + [jnp.cos(ang_c)] * 2, axis=-1)
    sin = jnp.concatenate([-jnp.sin(ang_r), jnp.sin(ang_r), -jnp.sin(ang_c), jnp.sin(ang_c)], axis=-1)
    pad = ((n_ctx, 0), (0, 0))
    cos, sin = jnp.pad(cos, pad, constant_values=1.0), jnp.pad(sin, pad)
    return jnp.tile(cos, (1, 2)), jnp.tile(sin, (1, 2))


def _even_mixer(hn, bsz, w_in, i, gate_w2, gate_b, gla_norm_g, sc_conv_w, seq):
    nq = GLA_HEADS * GLA_DK
    nv = GLA_HEADS * GLA_DV
    gate_col = 2 * nq + 2 * nv
    pa = _matmul(hn, w_in, i, bn=1024, ncols=gate_col, transposed=True)
    glr = _matmul(hn, w_in, i, bn=LANES, col0=gate_col, ncols=LANES, transposed=True)
    pb = _matmul(hn, w_in, i, bn=SC_WIDTH, col0=gate_col, ncols=3 * SC_WIDTH,
                 col_shift=2 * GLA_GATE_RANK, transposed=True)
    pa3 = pa.reshape(bsz, -1, pa.shape[1])
    glr3 = glr.reshape(bsz, -1, LANES)
    w2s = [jnp.zeros((LANES, nq), F32).at[d * GLA_GATE_RANK:(d + 1) * GLA_GATE_RANK].set(gate_w2[d])
           .astype(BF16) for d in range(2)]
    outs = [o.reshape(-1, nv) for o in
            _gla_scan(pa3, glr3, w2s, [gate_b[d][None, :] for d in range(2)], n_ctx=seq["n_ctx"])]
    gain = jnp.tile(gla_norm_g, GLA_HEADS)[None, :]
    return [_even_finish(outs[0], outs[1], pa, pb, gain, sc_conv_w, **seq)]


def _odd_mixer(hn, bsz, rope, layer, w_in, i, qn_g, kn_g, lam_p, subln_g, ml_conv_w, ml_gate_b,
               ml_norm_g, seq):
    lam_init = 0.8 - 0.6 * math.exp(-0.3 * layer)
    lam = (jnp.exp(jnp.sum(lam_p[0] * lam_p[1])) - jnp.exp(jnp.sum(lam_p[2] * lam_p[3])) + lam_init)
    na = DA_HEADS * 2 * DA_DQK
    nk = ML_HEADS * ML_DK
    nv = ML_HEADS * ML_DV
    n_main = 3 * na + 2 * nk + 2 * nv
    p = _matmul(hn, w_in, i, bn=1024, ncols=n_main, transposed=True)
    mg = _matmul(hn, w_in, i, bn=LANES, col0=n_main, ncols=LANES, transposed=True)[:, :4 * ML_HEADS]
    p3 = p.reshape(bsz, -1, n_main)
    n = p3.shape[1]

    cos, sin = rope
    tile2 = lambda g: jnp.tile(g, 2)[None, :]
    da = _diff_attention(p3, lam.reshape(1).astype(F32), cos, sin, tile2(qn_g), tile2(kn_g),
                         subln_g[None, :], n_ctx=seq["n_ctx"], out_scale=1.0 - lam_init)

    scale = jnp.concatenate([jnp.ones((nk,), F32), jnp.full((nk,), ML_DK ** -0.5, F32)])[None, :]
    hqk = _mlstm_prep(p, ml_conv_w, scale, col=3 * na // (2 * nk), **seq).reshape(bsz, n, 2 * nk)
    gates = (mg + ml_gate_b).reshape(bsz, n, 4 * ML_HEADS)
    outs = [o.reshape(-1, nv) for o in
            _mlstm_scan(hqk, p3, gates, v_col=(3 * na + 2 * nk) // nv, n_ctx=seq["n_ctx"])]
    m = _mlstm_finish(outs[0], outs[1], p, jnp.tile(ml_norm_g, ML_HEADS)[None, :],
                      col=(3 * na + 2 * nk + nv) // nv)
    return [da.reshape(-1, na), m]


def _modulation(c, c_ctx, ada_w, ada_b, layer):
    cc = jnp.concatenate([c, c_ctx[None, :]], axis=0)
    rows = cc.shape[0]
    act = jnp.pad(_silu(cc), ((0, BF16_ROWS - rows), (0, 0))).astype(BF16)
    mod = _matmul(act, ada_w, layer, bn=1024, bm=BF16_ROWS)[:rows] + ada_b[layer]
    return mod.reshape(rows, 6, -1)


def kernel(x, c, ctx, c_ctx, ada_w, ada_b, norm1_g, norm2_g, ev_w_in, ev_w_out, gla_gate_w2, gla_gate_b, gla_norm_g, sc_conv_w, od_w_in, od_w_out, da_qnorm_g, da_knorm_g, da_lambda, da_subln_g, ml_conv_w, ml_gate_b, ml_norm_g, ffn_w_up, ffn_conv_w, ffn_conv_b, ffn_w_down):
    bsz, n_lat, d = x.shape
    n_ctx = ctx.shape[1]
    n = n_ctx + n_lat
    depth = ada_w.shape[0]
    d_ff = ffn_w_down.shape[1]
    assert n % ROW_TILE == 0 and n_ctx <= ROW_CHUNK and d_ff % (4 * LANES) == 0
    rope = _rope_tables(n_lat, n_ctx)
    seq = dict(n_ctx=n_ctx, n_seq=n)

    ev_w_in = jnp.swapaxes(ev_w_in, 1, 2)
    od_w_in = jnp.swapaxes(od_w_in, 1, 2)
    mods = [_modulation(c, c_ctx, ada_w, ada_b, layer) for layer in range(depth)]
    xs = jnp.concatenate([ctx, x], axis=1).reshape(bsz * n, d)
    hn = _first_norm(xs, norm1_g[0][None, :], mods[0], **seq)
    for layer in range(depth):
        i = layer // 2
        if layer % 2 == 0:
            mix = _even_mixer(hn, bsz, ev_w_in, i, gla_gate_w2[i], gla_gate_b[i], gla_norm_g[i],
                              sc_conv_w[i], seq)
            w_out = ev_w_out
        else:
            mix = _odd_mixer(hn, bsz, rope, layer, od_w_in, i, da_qnorm_g[i], da_knorm_g[i],
                             da_lambda[i], da_subln_g[i], ml_conv_w[i], ml_gate_b[i], ml_norm_g[i], seq)
            w_out = od_w_out
        xs, hn = _out_proj(mix, w_out, i, xs, mods[layer], norm2_g[layer][None, :], **seq)
        act, w_down = _ffn_up(hn, ffn_w_up, ffn_w_down, layer, ffn_conv_w[layer],
                              ffn_conv_b[layer][None, :], **seq)
        nxt = min(layer + 1, depth - 1)
        xs, hn = _down_proj(act, w_down, xs,
                            mods[layer], norm1_g[nxt][None, :], mods[nxt], bk=d_ff // 2, **seq)
    return xs.reshape(bsz, n, d)[:, n_ctx:, :]
```

```python
import functools
import math

import jax
import jax.numpy as jnp
import numpy as np
from jax import lax
from jax.experimental import pallas as pl
from jax.experimental.pallas import tpu as pltpu

GRID_W = 64
EPS = 1e-6
GLA_HEADS, GLA_DK, GLA_DV, GLA_GATE_RANK, GLA_TAU = 4, 128, 256, 16, 16.0
SC_WIDTH = 1024
DA_HEADS, DA_DQK, DA_DV = 8, 64, 128
ROPE_BASE = 10000.0
ROPE_PAIRS = DA_DQK // 4
ML_HEADS, ML_DK, ML_DV = 4, 128, 256

GLA_CHUNK = 128
ML_CHUNK = 256
ATTN_BQ = 512
ATTN_SUB = 256
ATTN_CHUNK = 1024
ATTN_PREP = 544
ATTN_HEADS = 2
ROW_TILE = 1088
ROW_CHUNK = 272
LANES = 128
SUBLANES = 8
BF16_ROWS = 16
ROW_GROUP_UNROLL = 8
LOG2E = 1.4426950408889634
VMEM_LIMIT = 56 * 1024 * 1024

F32 = jnp.float32
BF16 = jnp.bfloat16
HI = lax.Precision.HIGHEST
NT = (((1,), (1,)), ((), ()))


def _mm_kernel(x_ref, w_ref, *rest, row_shift, transposed):
    o_ref, wb_ref = rest[-2:]

    @pl.when(pl.program_id(1) == 0)
    def _():
        if not transposed:
            wb_ref[...] = w_ref[...].astype(BF16)
            return
        bn = wb_ref.shape[1]
        piece = 2 * LANES
        for c0 in range(0, bn, piece):
            c1 = min(c0 + piece, bn)
            if row_shift and c1 == bn:
                rows = jnp.concatenate([w_ref[c0 + row_shift:bn, :], rest[0][...]], axis=0)
            else:
                rows = w_ref[c0 + row_shift:c1 + row_shift, :]
            wb_ref[:, c0:c1] = rows.T.astype(BF16)

    o_ref[...] = jnp.dot(x_ref[...], wb_ref[...], preferred_element_type=F32).astype(o_ref.dtype)


def _matmul(x, w, layer, *, bn, col0=0, ncols=None, col_shift=0, transposed=False, bm=ROW_TILE,
            out_dtype=F32):
    m, k = x.shape
    n_all = w.shape[1] if transposed else w.shape[2]
    ncols = n_all - col0 if ncols is None else ncols
    j0 = col0 // bn
    if transposed:
        w_spec = pl.BlockSpec((None, bn, k), lambda j, i: (layer, j0 + j, 0))
    else:
        w_spec = pl.BlockSpec((None, k, bn), lambda j, i: (layer, 0, j0 + j))
    in_specs = [pl.BlockSpec((bm, k), lambda j, i: (i, 0)), w_spec]
    operands = [x, w]
    if col_shift:
        per = bn // col_shift
        in_specs.append(pl.BlockSpec((None, col_shift, k), lambda j, i: (layer, (j0 + j + 1) * per, 0)))
        operands.append(w)
    return pl.pallas_call(
        functools.partial(_mm_kernel, row_shift=col_shift, transposed=transposed),
        grid=(pl.cdiv(ncols, bn), m // bm),
        in_specs=in_specs,
        out_specs=pl.BlockSpec((bm, bn), lambda j, i: (i, j)),
        out_shape=jax.ShapeDtypeStruct((m, ncols), out_dtype),
        scratch_shapes=[pltpu.VMEM((k, bn), BF16)],
        compiler_params=pltpu.CompilerParams(
            dimension_semantics=("arbitrary", "arbitrary"), vmem_limit_bytes=VMEM_LIMIT),
        name="matmul",
    )(*operands)


def _norm_mod(x, gain_scale, shift):
    return x * lax.rsqrt(jnp.mean(x * x, axis=-1, keepdims=True) + EPS) * gain_scale + shift


def _row_vectors(modb_ref, modc_ref, idx, gain_ref=None):
    def vec(ref):
        row = ref[0, idx:idx + 1, :]
        if gain_ref is not None:
            row = gain_ref[...] * (1.0 + row)
        return jnp.broadcast_to(row, (BF16_ROWS, row.shape[1]))

    lat, ctx = vec(modb_ref), vec(modc_ref)
    return lambda is_ctx: lat if is_ctx is None else jnp.where(is_ctx, ctx, lat)


def _head_rmsnorm(x, width):
    parts = []
    for c0 in range(0, x.shape[1], width):
        xh = x[:, c0:c0 + width]
        parts.append(xh * lax.rsqrt(jnp.mean(xh * xh, axis=-1, keepdims=True) + EPS))
    return parts[0] if len(parts) == 1 else jnp.concatenate(parts, axis=1)


def _sigmoid(x):
    return 1.0 / (1.0 + jnp.exp(-x))


def _chunk_rows(r0, rc, tiles_per_batch, bm):
    rows = r0 + lax.broadcasted_iota(jnp.int32, (rc, 1), 0)
    return rows, (pl.program_id(0) % tiles_per_batch) * bm + rows


def _edge_rows(r0, rc, bm, tiles_per_batch, n_ctx, n_seq, gate_ref, hprev_ref, hnext_ref):
    hr = hprev_ref.shape[0]
    pos0 = (pl.program_id(0) % tiles_per_batch) * bm
    if r0 == 0:
        before = hprev_ref[...].astype(F32)[hr - 1:hr, :]
        before = jnp.where((pos0 == 0) | (pos0 == n_ctx), 0.0, before)
    else:
        before = gate_ref[r0 - hr:r0, :].astype(F32)[hr - 1:hr, :]
    if r0 + rc == bm:
        after = hnext_ref[...].astype(F32)[0:1, :]
        after = jnp.where((pos0 + bm == n_ctx) | (pos0 + bm == n_seq), 0.0, after)
    else:
        after = gate_ref[r0 + rc:r0 + rc + hr, :].astype(F32)[0:1, :]
    return before, after


def _splits_rows(r0, rc, bm, tiles_per_batch, n_ctx):
    return any(r0 <= n_ctx - k * bm <= r0 + rc for k in range(tiles_per_batch))


def _dwconv3(s_ref, g, before, after, cw_ref, pos, n_ctx):
    rc = g.shape[0]
    s_ref[SUBLANES:SUBLANES + rc, :] = g
    s_ref[SUBLANES - 1:SUBLANES, :] = before
    s_ref[SUBLANES + rc:SUBLANES + rc + 1, :] = after
    prev = s_ref[SUBLANES - 1:SUBLANES - 1 + rc, :]
    nxt = s_ref[SUBLANES + 1:SUBLANES + 1 + rc, :]
    if pos is not None:
        prev = jnp.where(pos == n_ctx, 0.0, prev)
        nxt = jnp.where(pos == n_ctx - 1, 0.0, nxt)
    return prev * cw_ref[0:1, :] + g * cw_ref[1:2, :] + nxt * cw_ref[2:3, :]


def _mod_specs(n_batch, tiles_per_batch, d):
    return [pl.BlockSpec((1, 6, d), lambda i, *_: (i // tiles_per_batch, 0, 0)),
            pl.BlockSpec((1, 6, d), lambda i, *_: (n_batch, 0, 0))]


def _halo_specs(bm, width, col, n_rows, hr):
    hb = bm // hr
    last = n_rows // hr - 1
    return [pl.BlockSpec((hr, width), lambda i, *k: (jnp.maximum(i * hb - 1, 0), col(*k))),
            pl.BlockSpec((hr, width), lambda i, *k: (jnp.minimum((i + 1) * hb, last), col(*k)))]


def _norm_kernel(x_ref, gain_ref, modb_ref, modc_ref, o_ref, *, n_ctx, tiles_per_batch):
    gain_scale = _row_vectors(modb_ref, modc_ref, 1, gain_ref)
    shift = _row_vectors(modb_ref, modc_ref, 0)

    def rows(rs, is_ctx):
        o_ref[rs, :] = _norm_mod(x_ref[rs, :], gain_scale(is_ctx), shift(is_ctx)).astype(BF16)

    _for_row_groups(x_ref.shape[0], n_ctx, tiles_per_batch, rows)


def _first_norm(x, gain, mod, *, n_ctx, n_seq, bm=ROW_TILE):
    m, d = x.shape
    tpb = n_seq // bm
    return pl.pallas_call(
        functools.partial(_norm_kernel, n_ctx=n_ctx, tiles_per_batch=tpb),
        grid=(m // bm,),
        in_specs=[pl.BlockSpec((bm, d), lambda i: (i, 0)), pl.BlockSpec((1, d), lambda i: (0, 0))]
        + _mod_specs(mod.shape[0] - 1, tpb, d),
        out_specs=pl.BlockSpec((bm, d), lambda i: (i, 0)),
        out_shape=jax.ShapeDtypeStruct((m, d), BF16),
        compiler_params=pltpu.CompilerParams(vmem_limit_bytes=VMEM_LIMIT),
        name="first_norm",
    )(x, gain, mod, mod)


def _residual_norm(xo_ref, hn_ref, x_ref, modb_ref, modc_ref, gain_ref, nmodb_ref, nmodc_ref, *,
                   n_ctx, tiles_per_batch, gate_idx, shift_idx, inline=False):
    gate = _row_vectors(modb_ref, modc_ref, gate_idx)
    gain_scale = _row_vectors(nmodb_ref, nmodc_ref, shift_idx + 1, gain_ref)
    shift = _row_vectors(nmodb_ref, nmodc_ref, shift_idx)

    def rows(rs, is_ctx):
        x_new = x_ref[rs, :] + gate(is_ctx) * xo_ref[rs, :]
        xo_ref[rs, :] = x_new
        hn_ref[rs, :] = _norm_mod(x_new, gain_scale(is_ctx), shift(is_ctx)).astype(BF16)

    _for_row_groups(x_ref.shape[0], n_ctx, tiles_per_batch, rows, inline)


def _for_row_groups(bm, n_ctx, tiles_per_batch, fn, inline=False):
    rg = BF16_ROWS
    pos0 = (pl.program_id(0) % tiles_per_batch) * bm
    n_lead = -(-min(n_ctx, bm) // rg)
    if inline:
        for g in range(bm // rg):
            is_ctx = (pos0 + g * rg + lax.broadcasted_iota(jnp.int32, (rg, 1), 0) < n_ctx) if g < n_lead else None
            fn(slice(g * rg, (g + 1) * rg), is_ctx)
        return

    def group(with_ctx):
        def body(g, carry):
            r0 = pl.multiple_of(g * rg, rg)
            is_ctx = (pos0 + r0 + lax.broadcasted_iota(jnp.int32, (rg, 1), 0) < n_ctx) if with_ctx else None
            fn(pl.ds(r0, rg), is_ctx)
            return carry
        return body

    lax.fori_loop(0, n_lead, group(True), 0, unroll=ROW_GROUP_UNROLL)
    lax.fori_loop(n_lead, bm // rg, group(False), 0, unroll=ROW_GROUP_UNROLL)


def _out_kernel(*refs, n_in, n_ctx, tiles_per_batch):
    a_refs = refs[:n_in]
    w_ref, x_ref, modb_ref, modc_ref, gain_ref, xo_ref, hn_ref, wb_ref = refs[n_in:]

    @pl.when(pl.program_id(0) == 0)
    def _():
        for r0 in range(0, w_ref.shape[0], ROW_CHUNK):
            r1 = min(r0 + ROW_CHUNK, w_ref.shape[0])
            wb_ref[r0:r1, :] = w_ref[r0:r1, :].astype(BF16)

    k0 = 0
    for n_done, a_ref in enumerate(a_refs):
        part = jnp.dot(a_ref[...], wb_ref[k0:k0 + a_ref.shape[1], :], preferred_element_type=F32)
        xo_ref[...] = part if n_done == 0 else xo_ref[...] + part
        k0 += a_ref.shape[1]
    _residual_norm(xo_ref, hn_ref, x_ref, modb_ref, modc_ref, gain_ref, modb_ref, modc_ref,
                   n_ctx=n_ctx, tiles_per_batch=tiles_per_batch, gate_idx=2, shift_idx=3, inline=True)


def _out_proj(acts, w, layer, x, mod, gain, *, n_ctx, n_seq, bm=ROW_TILE // 2):
    m, d = x.shape
    kk = w.shape[1]
    tpb = n_seq // bm
    rows = lambda width: pl.BlockSpec((bm, width), lambda i: (i, 0))
    return pl.pallas_call(
        functools.partial(_out_kernel, n_in=len(acts), n_ctx=n_ctx, tiles_per_batch=tpb),
        grid=(m // bm,),
        in_specs=[rows(a.shape[1]) for a in acts]
        + [pl.BlockSpec((None, kk, d), lambda i: (layer, 0, 0), pipeline_mode=pl.Buffered(1)), rows(d)]
        + _mod_specs(mod.shape[0] - 1, tpb, d) + [pl.BlockSpec((1, d), lambda i: (0, 0))],
        out_specs=[rows(d), rows(d)],
        out_shape=[jax.ShapeDtypeStruct((m, d), F32), jax.ShapeDtypeStruct((m, d), BF16)],
        scratch_shapes=[pltpu.VMEM((kk, d), BF16)],
        compiler_params=pltpu.CompilerParams(
            dimension_semantics=("arbitrary",), vmem_limit_bytes=VMEM_LIMIT),
        name="out_proj",
    )(*acts, w, x, mod, mod, gain)


def _up_kernel(x_ref, xp_ref, xn_ref, wg_ref, wv_ref, cw_ref, cb_ref, wd_ref, o_ref, wdb_ref, wgb_ref,
               wvb_ref, xe_ref, s_ref, v_ref, *, n_ctx, n_seq, tiles_per_batch):
    i = pl.program_id(1)
    bm, rc, hr = x_ref.shape[0], ROW_CHUNK, xp_ref.shape[0]
    wdb_ref[...] = wd_ref[...].astype(BF16)

    @pl.when(i == 0)
    def _():
        wgb_ref[...] = wg_ref[...].astype(BF16)
        wvb_ref[...] = wv_ref[...].astype(BF16)

    xe_ref[0:hr, :] = xp_ref[...]
    xe_ref[hr:hr + bm, :] = x_ref[...]
    xe_ref[hr + bm:, :] = xn_ref[...]
    s_ref[...] = jnp.dot(xe_ref[...], wgb_ref[...], preferred_element_type=F32)
    v_ref[...] = jnp.dot(x_ref[...], wvb_ref[...], preferred_element_type=F32)
    pos0 = (i % tiles_per_batch) * bm
    s_ref[hr - 1:hr, :] = jnp.where((pos0 == 0) | (pos0 == n_ctx), 0.0, s_ref[hr - 1:hr, :])
    s_ref[hr + bm:hr + bm + 1, :] = jnp.where(
        (pos0 + bm == n_ctx) | (pos0 + bm == n_seq), 0.0, s_ref[hr + bm:hr + bm + 1, :])
    for r0 in range(0, bm, rc):
        rs = slice(r0, r0 + rc)
        prev = s_ref[hr - 1 + r0:hr - 1 + r0 + rc, :]
        nxt = s_ref[hr + 1 + r0:hr + 1 + r0 + rc, :]
        if _splits_rows(r0, rc, bm, tiles_per_batch, n_ctx):
            pos = pos0 + r0 + lax.broadcasted_iota(jnp.int32, (rc, 1), 0)
            prev = jnp.where(pos == n_ctx, 0.0, prev)
            nxt = jnp.where(pos == n_ctx - 1, 0.0, nxt)
        z = (prev * cw_ref[0:1, :] + s_ref[hr + r0:hr + r0 + rc, :] * cw_ref[1:2, :]
             + nxt * cw_ref[2:3, :] + cb_ref[...])
        o_ref[rs, :] = (z * _sigmoid(z) * v_ref[rs, :]).astype(BF16)


def _ffn_up(x, w, w_down, layer, conv_w, conv_b, *, n_ctx, n_seq, bn=512, bm=ROW_TILE):
    m, k = x.shape
    f = w.shape[2] // 2
    nj, ni = f // bn, m // bm
    slab = f // (nj * ni)
    assert slab * nj * ni == f and slab % BF16_ROWS == 0
    d_out = w_down.shape[2]
    tpb = n_seq // bm
    hb = bm // BF16_ROWS
    last = m // BF16_ROWS - 1
    return pl.pallas_call(
        functools.partial(_up_kernel, n_ctx=n_ctx, n_seq=n_seq, tiles_per_batch=tpb),
        grid=(nj, ni),
        in_specs=[pl.BlockSpec((bm, k), lambda j, i: (i, 0)),
                  pl.BlockSpec((BF16_ROWS, k), lambda j, i: (jnp.maximum(i * hb - 1, 0), 0)),
                  pl.BlockSpec((BF16_ROWS, k), lambda j, i: (jnp.minimum((i + 1) * hb, last), 0)),
                  pl.BlockSpec((None, k, bn), lambda j, i: (layer, 0, j)),
                  pl.BlockSpec((None, k, bn), lambda j, i: (layer, 0, j + nj)),
                  pl.BlockSpec((3, bn), lambda j, i: (0, j)), pl.BlockSpec((1, bn), lambda j, i: (0, j)),
                  pl.BlockSpec((None, slab, d_out), lambda j, i: (layer, j * ni + i, 0))],
        out_specs=[pl.BlockSpec((bm, bn), lambda j, i: (i, j)),
                   pl.BlockSpec((slab, d_out), lambda j, i: (j * ni + i, 0))],
        out_shape=[jax.ShapeDtypeStruct((m, f), BF16), jax.ShapeDtypeStruct((f, d_out), BF16)],
        scratch_shapes=[pltpu.VMEM((k, bn), BF16), pltpu.VMEM((k, bn), BF16),
                        pltpu.VMEM((bm + 2 * BF16_ROWS, k), BF16),
                        pltpu.VMEM((bm + 2 * BF16_ROWS, bn), F32), pltpu.VMEM((bm, bn), F32)],
        compiler_params=pltpu.CompilerParams(
            dimension_semantics=("arbitrary", "arbitrary"), vmem_limit_bytes=VMEM_LIMIT),
        name="ffn_up",
    )(x, x, x, w, w, conv_w, conv_b, w_down)


def _down_kernel(a_ref, w_ref, x_ref, modb_ref, modc_ref, gain_ref, nmodb_ref, nmodc_ref, xo_ref, hn_ref, *,
                 n_ctx, tiles_per_batch):
    part = jnp.dot(a_ref[...], w_ref[...], preferred_element_type=F32)

    @pl.when(pl.program_id(1) == 0)
    def _():
        xo_ref[...] = part

    @pl.when(pl.program_id(1) > 0)
    def _():
        xo_ref[...] += part

    @pl.when(pl.program_id(1) == pl.num_programs(1) - 1)
    def _():
        _residual_norm(xo_ref, hn_ref, x_ref, modb_ref, modc_ref, gain_ref, nmodb_ref, nmodc_ref,
                       n_ctx=n_ctx, tiles_per_batch=tiles_per_batch, gate_idx=5, shift_idx=0)


def _down_proj(a, w, x, mod, gain, next_mod, *, n_ctx, n_seq, bm=ROW_TILE // 2, bk=1408):
    m, f = a.shape
    d = w.shape[1]
    tpb = n_seq // bm
    row = lambda i, k: (i, 0)
    return pl.pallas_call(
        functools.partial(_down_kernel, n_ctx=n_ctx, tiles_per_batch=tpb),
        grid=(m // bm, f // bk),
        in_specs=[pl.BlockSpec((bm, bk), lambda i, k: (i, k)), pl.BlockSpec((bk, d), lambda i, k: (k, 0)),
                  pl.BlockSpec((bm, d), row)]
        + _mod_specs(mod.shape[0] - 1, tpb, d) + [pl.BlockSpec((1, d), lambda i, k: (0, 0))]
        + _mod_specs(mod.shape[0] - 1, tpb, d),
        out_specs=[pl.BlockSpec((bm, d), row), pl.BlockSpec((bm, d), row)],
        out_shape=[jax.ShapeDtypeStruct((m, d), F32), jax.ShapeDtypeStruct((m, d), BF16)],
        compiler_params=pltpu.CompilerParams(
            dimension_semantics=("arbitrary", "arbitrary"), vmem_limit_bytes=VMEM_LIMIT),
        name="down_proj",
    )(a, w, x, mod, mod, gain, next_mod, next_mod)


def _even_finish_kernel(of_ref, ob_ref, r_ref, sx_ref, sb_ref, sg_ref, sxp_ref, sxn_ref, sgp_ref, sgn_ref,
                        gain_ref, cw_ref, o_ref, s_ref, *, n_ctx, n_seq, tiles_per_batch):
    bm = of_ref.shape[0]
    nv = of_ref.shape[1]
    r = r_ref[...]
    o_ref[:, :nv] = (_head_rmsnorm(of_ref[...] + ob_ref[...], GLA_DV) * gain_ref[...]
                     * (r * _sigmoid(r))).astype(BF16)

    _, pos = _chunk_rows(0, bm, tiles_per_batch, bm)
    pos0 = (pl.program_id(0) % tiles_per_batch) * bm
    hr = sxp_ref.shape[0]
    before = (sxp_ref[...] * sgp_ref[...])[hr - 1:hr, :]
    before = jnp.where((pos0 == 0) | (pos0 == n_ctx), 0.0, before)
    after = (sxn_ref[...] * sgn_ref[...])[0:1, :]
    after = jnp.where((pos0 + bm == n_ctx) | (pos0 + bm == n_seq), 0.0, after)
    split = pos if _splits_rows(0, bm, bm, tiles_per_batch, n_ctx) else None
    conv = _dwconv3(s_ref, sg_ref[...] * sx_ref[...], before, after, cw_ref, split, n_ctx)
    o_ref[:, nv:] = (sb_ref[...] * conv).astype(BF16)


def _even_finish(o_f, o_b, pa, pb, gain, conv_w, *, n_ctx, n_seq, bm=ROW_CHUNK):
    m, nv = o_f.shape
    sc = conv_w.shape[1]
    tpb = n_seq // bm
    col = lambda j: (lambda i: (i, j))
    const = lambda i: (0, 0)
    return pl.pallas_call(
        functools.partial(_even_finish_kernel, n_ctx=n_ctx, n_seq=n_seq, tiles_per_batch=tpb),
        grid=(m // bm,),
        in_specs=[pl.BlockSpec((bm, nv), col(0)), pl.BlockSpec((bm, nv), col(0)),
                  pl.BlockSpec((bm, nv), col(2)),
                  pl.BlockSpec((bm, sc), col(0)), pl.BlockSpec((bm, sc), col(1)), pl.BlockSpec((bm, sc), col(2))]
        + _halo_specs(bm, sc, lambda: 0, m, SUBLANES) + _halo_specs(bm, sc, lambda: 2, m, SUBLANES)
        + [pl.BlockSpec((1, nv), const), pl.BlockSpec((3, sc), const)],
        out_specs=pl.BlockSpec((bm, nv + sc), col(0)),
        out_shape=jax.ShapeDtypeStruct((m, nv + sc), BF16),
        scratch_shapes=[pltpu.VMEM((bm + 2 * SUBLANES, sc), F32)],
        compiler_params=pltpu.CompilerParams(vmem_limit_bytes=VMEM_LIMIT),
        name="even_finish",
    )(o_f, o_b, pa, pb, pb, pb, pb, pb, pb, pb, gain, conv_w)


def _mlstm_prep_kernel(x_ref, xp_ref, xn_ref, cw_ref, scale_ref, o_ref, s_ref, *, n_ctx, n_seq,
                       tiles_per_batch):
    bm = x_ref.shape[0]
    _, pos = _chunk_rows(0, bm, tiles_per_batch, bm)
    before, after = _edge_rows(0, bm, bm, tiles_per_batch, n_ctx, n_seq, x_ref, xp_ref, xn_ref)
    split = pos if _splits_rows(0, bm, bm, tiles_per_batch, n_ctx) else None
    z = _dwconv3(s_ref, x_ref[...], before, after, cw_ref, split, n_ctx)
    o_ref[...] = (z * _sigmoid(z) * scale_ref[...]).astype(BF16)


def _mlstm_prep(p, conv_w, scale, *, col, n_ctx, n_seq, bm=ROW_CHUNK):
    m = p.shape[0]
    width = conv_w.shape[1]
    tpb = n_seq // bm
    const = lambda i: (0, 0)
    return pl.pallas_call(
        functools.partial(_mlstm_prep_kernel, n_ctx=n_ctx, n_seq=n_seq, tiles_per_batch=tpb),
        grid=(m // bm,),
        in_specs=[pl.BlockSpec((bm, width), lambda i: (i, col))]
        + _halo_specs(bm, width, lambda: col, m, SUBLANES)
        + [pl.BlockSpec((3, width), const), pl.BlockSpec((1, width), const)],
        out_specs=pl.BlockSpec((bm, width), lambda i: (i, 0)),
        out_shape=jax.ShapeDtypeStruct((m, width), BF16),
        scratch_shapes=[pltpu.VMEM((bm + 2 * SUBLANES, width), F32)],
        compiler_params=pltpu.CompilerParams(vmem_limit_bytes=VMEM_LIMIT),
        name="mlstm_prep",
    )(p, p, p, conv_w, scale)


def _mlstm_finish_kernel(hf_ref, hb_ref, mo_ref, gain_ref, o_ref):
    o_ref[...] = (_head_rmsnorm(hf_ref[...] + hb_ref[...], ML_DV) * gain_ref[...]
                  * _sigmoid(mo_ref[...])).astype(BF16)


def _mlstm_finish(h_f, h_b, p, gain, *, col, bm=ROW_TILE // 2):
    m, nv = h_f.shape
    return pl.pallas_call(
        _mlstm_finish_kernel,
        grid=(m // bm,),
        in_specs=[pl.BlockSpec((bm, nv), lambda i: (i, 0)), pl.BlockSpec((bm, nv), lambda i: (i, 0)),
                  pl.BlockSpec((bm, nv), lambda i: (i, col)), pl.BlockSpec((1, nv), lambda i: (0, 0))],
        out_specs=pl.BlockSpec((bm, nv), lambda i: (i, 0)),
        out_shape=jax.ShapeDtypeStruct((m, nv), BF16),
        compiler_params=pltpu.CompilerParams(vmem_limit_bytes=VMEM_LIMIT),
        name="mlstm_finish",
    )(h_f, h_b, p, gain)


def _qk_prep(x, cos, sin, gain):
    lane = lax.broadcasted_iota(jnp.int32, x.shape, 1)
    lo = lane < DA_DQK
    x2 = x * x
    ss_lo = jnp.sum(jnp.where(lo, x2, 0.0), axis=-1, keepdims=True)
    ss_hi = jnp.sum(jnp.where(lo, 0.0, x2), axis=-1, keepdims=True)
    y = x * lax.rsqrt(jnp.where(lo, ss_lo, ss_hi) * (1.0 / DA_DQK) + EPS) * gain
    swapped = jnp.where((lane & ROPE_PAIRS) == 0, pltpu.roll(y, LANES - ROPE_PAIRS, 1),
                        pltpu.roll(y, ROPE_PAIRS, 1))
    return y * cos + swapped * sin


def _attn_kernel(lam_ref, q_ref, k_ref, v_ref, cos_ref, sin_ref, qg_ref, kg_ref, og_ref, o_ref,
                 kb_ref, vb_ref, *, n_ctx, chunk, prep_rows, out_scale):
    qi = pl.program_id(2)
    lam = lam_ref[0]
    n = k_ref.shape[1]
    bq = q_ref.shape[1]
    heads = q_ref.shape[2] // LANES

    @pl.when(qi == 0)
    def _():
        for r0 in range(0, n, prep_rows):
            rs = slice(r0, r0 + prep_rows)
            for h in range(heads):
                hs = slice(h * LANES, (h + 1) * LANES)
                kb_ref[h, rs, :] = _qk_prep(k_ref[0, rs, hs], cos_ref[rs, :], sin_ref[rs, :],
                                            kg_ref[...]).astype(BF16)
                vb_ref[h, rs, :LANES] = v_ref[0, rs, hs].astype(BF16)
                vb_ref[h, rs, LANES:] = jnp.ones((prep_rows, LANES), BF16)

    def attend(r0, r1, nk):
        nr = r1 - r0
        rows = pl.ds(pl.multiple_of(qi * bq, BF16_ROWS) + r0, nr)
        cos, sin = cos_ref[rows, :], sin_ref[rows, :]
        lane = lax.broadcasted_iota(jnp.int32, (nr, LANES), 1)
        q2, m, acc = [], [], []
        for h in range(heads):
            q = _qk_prep(q_ref[0, r0:r1, h * LANES:(h + 1) * LANES], cos, sin, qg_ref[...])
            q = (q * (DA_DQK ** -0.5 * LOG2E)).astype(BF16)
            zero = jnp.zeros_like(q)
            q2.append(jnp.concatenate([jnp.where(lane < DA_DQK, q, zero),
                                       jnp.where(lane >= DA_DQK, q, zero)], axis=0))
            m.append(jnp.full((2 * nr, 1), -jnp.inf, F32))
            acc.append(jnp.zeros((2 * nr, 2 * LANES), F32))
        for c0 in range(0, nk, chunk):
            c1 = min(c0 + chunk, nk)
            for h in range(heads):
                s = lax.dot_general(q2[h], kb_ref[h, c0:c1, :], NT, preferred_element_type=F32)
                m_new = jnp.maximum(m[h], jnp.max(s, axis=-1, keepdims=True))
                p = jnp.exp2(s - m_new)
                acc[h] = jnp.exp2(m[h] - m_new) * acc[h] + jnp.dot(
                    p.astype(BF16), vb_ref[h, c0:c1, :], preferred_element_type=F32)
                m[h] = m_new
        for h in range(heads):
            o = acc[h][:, :LANES] / acc[h][:, LANES:]
            o = o[:nr] - lam * o[nr:]
            o = o * lax.rsqrt(jnp.mean(o * o, axis=-1, keepdims=True) + EPS) * og_ref[...] * out_scale
            o_ref[0, r0:r1, h * LANES:(h + 1) * LANES] = o.astype(BF16)

    last = pl.num_programs(2) - 1
    last_rows = n - (n // bq) * bq if n % bq else bq

    @pl.when(qi == 0)
    def _():
        attend(0, n_ctx, n_ctx)
        for r0 in range(n_ctx, bq, ATTN_SUB):
            attend(r0, r0 + ATTN_SUB, n)

    @pl.when((qi > 0) & (qi < last))
    def _():
        for r0 in range(0, bq, ATTN_SUB):
            attend(r0, r0 + ATTN_SUB, n)

    @pl.when((qi > 0) & (qi == last))
    def _():
        for r0 in range(0, last_rows, ATTN_SUB):
            attend(r0, r0 + ATTN_SUB, n)


def _diff_attention(p, lam, cos, sin, q_gain, k_gain, out_gain, *, n_ctx, out_scale, bq=ATTN_BQ,
                    chunk=ATTN_CHUNK, prep_rows=ATTN_PREP, heads=ATTN_HEADS):
    b, n, _ = p.shape
    assert n_ctx % ATTN_SUB == 0 and bq % ATTN_SUB == 0 and n % ATTN_SUB == 0 and n_ctx <= bq
    hg = DA_HEADS // heads
    hd = DA_HEADS * LANES
    hw = heads * LANES
    const = lambda bi, hi, qi: (0, 0)
    return pl.pallas_call(
        functools.partial(_attn_kernel, n_ctx=n_ctx, chunk=chunk, prep_rows=prep_rows, out_scale=out_scale),
        grid=(b, hg, pl.cdiv(n, bq)),
        in_specs=[pl.BlockSpec(memory_space=pltpu.SMEM),
                  pl.BlockSpec((1, bq, hw), lambda bi, hi, qi: (bi, qi, hi)),
                  pl.BlockSpec((1, n, hw), lambda bi, hi, qi: (bi, 0, hg + hi)),
                  pl.BlockSpec((1, n, hw), lambda bi, hi, qi: (bi, 0, 2 * hg + hi)),
                  pl.BlockSpec((n, LANES), const), pl.BlockSpec((n, LANES), const),
                  pl.BlockSpec((1, LANES), const), pl.BlockSpec((1, LANES), const),
                  pl.BlockSpec((1, LANES), const)],
        out_specs=pl.BlockSpec((1, bq, hw), lambda bi, hi, qi: (bi, qi, hi)),
        out_shape=jax.ShapeDtypeStruct((b, n, hd), BF16),
        scratch_shapes=[pltpu.VMEM((heads, n, LANES), BF16), pltpu.VMEM((heads, n, 2 * LANES), BF16)],
        compiler_params=pltpu.CompilerParams(
            dimension_semantics=("arbitrary", "arbitrary", "arbitrary"), vmem_limit_bytes=VMEM_LIMIT),
        name="diff_attention",
    )(lam, p, p, p, cos, sin, q_gain, k_gain, out_gain)


def _scan_chunk(t, n_ctx_chunks, n_chunks, reverse):
    if not reverse:
        return t
    return jnp.where(t < n_ctx_chunks, n_ctx_chunks - 1 - t, n_chunks - 1 - (t - n_ctx_chunks))


def _gla_constants(L, reverse):
    nlev = int(math.log2(L))
    idx = np.arange(L)
    i, t = idx[:, None], idx[None, :]
    if reverse:
        i, t = L - 1 - i, L - 1 - t
    rs = [(t <= i), (t > i)]
    am = [(i == t)]
    for lev in range(nlev):
        m = L >> (lev + 1)
        blk_i, blk_t = i // (2 * m), t // (2 * m)
        mid = blk_i * 2 * m + m
        q_role = i >= mid
        rs.append(np.where(q_role, (t >= mid) & (t <= i), (t > i) & (t < mid)) & (blk_i == blk_t))
        am.append((blk_i == blk_t) & q_role & (t < mid))
    return (np.stack(rs).astype(np.float32).reshape((nlev + 2) * L, L),
            np.stack(am).astype(np.float32))


def _gla_kernel(*refs):
    of_ref, ob_ref, st_ref = refs[16:]

    @pl.when(pl.program_id(1) == 0)
    def _():
        st_ref[...] = jnp.zeros_like(st_ref)

    for d, o_ref in enumerate((of_ref, ob_ref)):
        _gla_direction(*refs[8 * d:8 * d + 8], o_ref, st_ref, d * GLA_HEADS, bool(d))


def _gla_direction(q_ref, k_ref, v_ref, glr_ref, w2_ref, gb_ref, rsum_ref, amask_ref, o_ref, st_ref, h0,
                   reverse):
    L = q_ref.shape[1]
    nlev = amask_ref.shape[0] - 1
    n_coarse = nlev + 2 - rsum_ref.shape[0] // L

    z = jnp.dot(glr_ref[0].astype(BF16), w2_ref[...], preferred_element_type=F32) + gb_ref[...]
    g = (jnp.minimum(z, 0.0) - jnp.log(1.0 + jnp.exp(-jnp.abs(z)))) * (1.0 / GLA_TAU)
    g1 = g.astype(BF16)
    g2 = (g - g1.astype(F32)).astype(BF16)
    g3 = (g - g1.astype(F32) - g2.astype(F32)).astype(BF16)
    rsum = rsum_ref[...]
    e_all = (jnp.dot(rsum, g1, preferred_element_type=F32) + jnp.dot(rsum, g2, preferred_element_type=F32)
             + jnp.dot(rsum, g3, preferred_element_type=F32))
    b_tot = jnp.sum(g, axis=0, keepdims=True)

    def level_decay(lev, ck):
        if lev >= n_coarse:
            return jnp.exp(e_all[(2 + lev - n_coarse) * L:(3 + lev - n_coarse) * L, ck])
        m = L >> (lev + 1)
        b3 = e_all[0:L, ck].reshape(L // (2 * m), 2 * m, GLA_DK)
        pivot = m if reverse else m - 1
        return jnp.exp(-jnp.abs(b3 - b3[:, pivot:pivot + 1, :])).reshape(L, GLA_DK)

    for h in range(GLA_HEADS):
        ck = slice(h * GLA_DK, (h + 1) * GLA_DK)
        cv = slice(h * GLA_DV, (h + 1) * GLA_DV)
        q = q_ref[0, :, ck] * (GLA_DK ** -0.5)
        k = k_ref[0, :, ck]
        v = v_ref[0, :, cv].astype(BF16)
        st = st_ref[h0 + h]
        a = amask_ref[0] * lax.dot_general(q.astype(BF16), k.astype(BF16), NT, preferred_element_type=F32)
        for lev in range(nlev):
            e = level_decay(lev, ck)
            a = a + amask_ref[1 + lev] * lax.dot_general(
                (q * e).astype(BF16), (k * e).astype(BF16), NT, preferred_element_type=F32)
        qe = (q * jnp.exp(e_all[0:L, ck])).astype(BF16)
        o = lax.dot_general(qe, st.astype(BF16), NT, preferred_element_type=F32)
        o = o + jnp.dot(a.astype(BF16), v, preferred_element_type=F32)
        o_ref[0, :, cv] = o
        kd = (k * jnp.exp(e_all[L:2 * L, ck])).astype(BF16)
        st_ref[h0 + h] = jnp.exp(b_tot[:, ck]) * st + jnp.dot(
            v.T, kd, preferred_element_type=F32)


def _gla_scan(p, glr, w2s, gbs, *, n_ctx, L=GLA_CHUNK):
    b, n, _ = p.shape
    nc, ncc = n // L, n_ctx // L
    hk, hv = GLA_HEADS * GLA_DK, GLA_HEADS * GLA_DV
    const2 = lambda bi, t: (0, 0)
    const3 = lambda bi, t: (0, 0, 0)
    in_specs, operands, out_specs = [], [], []
    for d in range(2):
        rsum, amask = _gla_constants(L, bool(d))
        n_coarse = sum(2 * (L >> (lev + 1)) >= SUBLANES for lev in range(amask.shape[0] - 1))
        rsum = rsum.reshape(-1, L, L)
        rsum = np.concatenate([rsum[:2], rsum[2 + n_coarse:]]).reshape(-1, L)
        chunk = functools.partial(_scan_chunk, n_ctx_chunks=ncc, n_chunks=nc, reverse=bool(d))
        cols = lambda width, start, chunk=chunk: pl.BlockSpec(
            (1, L, width), lambda bi, t: (bi, chunk(t), start // width))
        in_specs += [cols(hk, 0), cols(hk, hk), cols(hv, 2 * hk), cols(LANES, 0),
                     pl.BlockSpec(w2s[d].shape, const2), pl.BlockSpec(gbs[d].shape, const2),
                     pl.BlockSpec(rsum.shape, const2), pl.BlockSpec(amask.shape, const3)]
        operands += [p, p, p, glr, w2s[d], gbs[d], jnp.asarray(rsum, BF16), jnp.asarray(amask)]
        out_specs.append(cols(hv, 0))
    return pl.pallas_call(
        _gla_kernel,
        grid=(b, nc),
        in_specs=in_specs,
        out_specs=out_specs,
        out_shape=[jax.ShapeDtypeStruct((b, n, hv), F32)] * 2,
        scratch_shapes=[pltpu.VMEM((2 * GLA_HEADS, GLA_DV, GLA_DK), F32)],
        compiler_params=pltpu.CompilerParams(
            dimension_semantics=("arbitrary", "arbitrary"), vmem_limit_bytes=VMEM_LIMIT),
        name="gla_scan",
    )(*operands)


def _log_sigmoid(x):
    return jnp.minimum(x, 0.0) - jnp.log(1.0 + jnp.exp(-jnp.abs(x)))


def _mlstm_kernel(*refs):
    tri_ref, of_ref, ob_ref, c_ref, m_ref = refs[10:]

    @pl.when(pl.program_id(0) == 0)
    def _():
        c_ref[...] = jnp.zeros_like(c_ref)
        m_ref[...] = jnp.zeros_like(m_ref)

    for bi in range(of_ref.shape[0]):
        for d, o_ref in enumerate((of_ref, ob_ref)):
            _mlstm_direction(*refs[5 * d:5 * d + 5], tri_ref[d], o_ref, c_ref, m_ref, bi,
                             (2 * bi + d) * ML_HEADS, bool(d))


def _mlstm_direction(q_ref, k_ref, v_ref, gc_ref, gr_ref, tri, o_ref, c_ref, m_ref, bi, h0, reverse):
    L = q_ref.shape[1]
    H = ML_HEADS
    ones = jnp.ones((L, LANES), BF16)
    gc = gc_ref[bi]
    gr = gr_ref[bi]
    ic_col, ic_row = gc[:, :H], gr[:H, :]
    b_col = jnp.dot(tri, _log_sigmoid(gc[:, H:]), precision=HI, preferred_element_type=F32)
    b_row = lax.dot_general(_log_sigmoid(gr[H:, :]), tri, NT, precision=HI, preferred_element_type=F32)
    last = 0 if reverse else L - 1
    causal = tri > 0.5

    for h in range(H):
        ck = slice(h * ML_DK, (h + 1) * ML_DK)
        cv = slice(h * ML_DV, (h + 1) * ML_DV)
        q = q_ref[bi, :, ck]
        k = k_ref[bi, :, ck]
        v = jnp.concatenate([v_ref[bi, :, cv].astype(BF16), ones], axis=1)
        c, m = c_ref[h0 + h], m_ref[h0 + h]
        bc, br = b_col[:, h:h + 1], b_row[h:h + 1, :]
        icc, icr = ic_col[:, h:h + 1], ic_row[h:h + 1, :]
        b_last = bc[last:last + 1, :]

        a = bc + m
        dmat = jnp.where(causal, bc - br + icr, -jnp.inf)
        m_t = jnp.maximum(a, jnp.max(dmat, axis=-1, keepdims=True))
        w_inter = jnp.exp(a - m_t)
        s = lax.dot_general(q, k, NT, preferred_element_type=F32) * jnp.exp(dmat - m_t)
        both = w_inter * jnp.dot(q, c.astype(BF16), preferred_element_type=F32) + jnp.dot(
            s.astype(BF16), v, preferred_element_type=F32)
        den = jnp.maximum(jnp.abs(both[:, ML_DV:]), jnp.exp(-m_t))
        o_ref[bi, :, cv] = both[:, :ML_DV] / jnp.concatenate([den] * (ML_DV // LANES), axis=1)

        gs_col = b_last - bc + icc
        gs_row = b_last - br + icr
        m_new = jnp.maximum(b_last + m, jnp.max(gs_row, axis=-1, keepdims=True))
        decay = jnp.exp(b_last + m - m_new)
        wk = jnp.exp(gs_col - m_new) * k.astype(F32)
        c_ref[h0 + h] = decay * c + jnp.dot(wk.astype(BF16).T, v, preferred_element_type=F32)
        m_ref[h0 + h] = m_new


def _mlstm_scan(qk, p, gates, *, v_col, n_ctx, L=ML_CHUNK):
    b, n, _ = qk.shape
    hk, hv, ng = ML_HEADS * ML_DK, ML_HEADS * ML_DV, 2 * ML_HEADS
    nc, ncc = n // L, n_ctx // L
    idx = np.arange(L)
    lower = idx[None, :] <= idx[:, None]
    tri = jnp.asarray(np.stack([lower, lower.T]).astype(np.float32))
    in_specs, operands, out_specs = [], [], []
    for d in range(2):
        chunk = functools.partial(_scan_chunk, n_ctx_chunks=ncc, n_chunks=nc, reverse=bool(d))
        col = lambda j, chunk=chunk: (lambda t: (0, chunk(t), j))
        g_dir = gates[..., d * ng:(d + 1) * ng]
        in_specs += [pl.BlockSpec((b, L, hk), col(0)), pl.BlockSpec((b, L, hk), col(1)),
                     pl.BlockSpec((b, L, hv), col(v_col)), pl.BlockSpec((b, L, ng), col(0)),
                     pl.BlockSpec((b, ng, L), lambda t, chunk=chunk: (0, 0, chunk(t)))]
        operands += [qk, qk, p, g_dir, jnp.swapaxes(g_dir, 1, 2)]
        out_specs.append(pl.BlockSpec((b, L, hv), col(0)))
    return pl.pallas_call(
        _mlstm_kernel,
        grid=(nc,),
        in_specs=in_specs + [pl.BlockSpec((2, L, L), lambda t: (0, 0, 0))],
        out_specs=out_specs,
        out_shape=[jax.ShapeDtypeStruct((b, n, hv), F32)] * 2,
        scratch_shapes=[pltpu.VMEM((b * 2 * ML_HEADS, ML_DK, ML_DV + LANES), F32),
                        pltpu.VMEM((b * 2 * ML_HEADS, 1, 1), F32)],
        compiler_params=pltpu.CompilerParams(
            dimension_semantics=("arbitrary",), vmem_limit_bytes=VMEM_LIMIT),
        name="mlstm_scan",
    )(*operands, tri)


def _silu(x):
    return x * jax.nn.sigmoid(x)


def _rope_tables(n_lat, n_ctx):
    rows = n_lat // GRID_W
    row = jnp.repeat(jnp.arange(rows, dtype=F32), GRID_W)
    col = jnp.tile(jnp.arange(GRID_W, dtype=F32), rows)
    inv = jnp.power(ROPE_BASE, -jnp.arange(ROPE_PAIRS, dtype=F32) / ROPE_PAIRS)
    ang_r, ang_c = row[:, None] * inv, col[:, None] * inv
    cos = jnp.concatenate([jnp.cos(ang_r)] * 2 + [jnp.cos(ang_c)] * 2, axis=-1)
    sin = jnp.concatenate([-jnp.sin(ang_r), jnp.sin(ang_r), -jnp.sin(ang_c), jnp.sin(ang_c)], axis=-1)
    pad = ((n_ctx, 0), (0, 0))
    cos, sin = jnp.pad(cos, pad, constant_values=1.0), jnp.pad(sin, pad)
    return jnp.tile(cos, (1, 2)), jnp.tile(sin, (1, 2))


def _even_mixer(hn, bsz, w_in, i, gate_w2, gate_b, gla_norm_g, sc_conv_w, seq):
    nq = GLA_HEADS * GLA_DK
    nv = GLA_HEADS * GLA_DV
    gate_col = 2 * nq + 2 * nv
    pa = _matmul(hn, w_in, i, bn=1024, ncols=gate_col, transposed=True)
    glr = _matmul(hn, w_in, i, bn=LANES, col0=gate_col, ncols=LANES, transposed=True)
    pb = _matmul(hn, w_in, i, bn=SC_WIDTH, col0=gate_col, ncols=3 * SC_WIDTH,
                 col_shift=2 * GLA_GATE_RANK, transposed=True)
    pa3 = pa.reshape(bsz, -1, pa.shape[1])
    glr3 = glr.reshape(bsz, -1, LANES)
    w2s = [jnp.zeros((LANES, nq), F32).at[d * GLA_GATE_RANK:(d + 1) * GLA_GATE_RANK].set(gate_w2[d])
           .astype(BF16) for d in range(2)]
    outs = [o.reshape(-1, nv) for o in
            _gla_scan(pa3, glr3, w2s, [gate_b[d][None, :] for d in range(2)], n_ctx=seq["n_ctx"])]
    gain = jnp.tile(gla_norm_g, GLA_HEADS)[None, :]
    return [_even_finish(outs[0], outs[1], pa, pb, gain, sc_conv_w, **seq)]


def _odd_mixer(hn, bsz, rope, layer, w_in, i, qn_g, kn_g, lam_p, subln_g, ml_conv_w, ml_gate_b,
               ml_norm_g, seq):
    lam_init = 0.8 - 0.6 * math.exp(-0.3 * layer)
    lam = (jnp.exp(jnp.sum(lam_p[0] * lam_p[1])) - jnp.exp(jnp.sum(lam_p[2] * lam_p[3])) + lam_init)
    na = DA_HEADS * 2 * DA_DQK
    nk = ML_HEADS * ML_DK
    nv = ML_HEADS * ML_DV
    n_main = 3 * na + 2 * nk + 2 * nv
    p = _matmul(hn, w_in, i, bn=1024, ncols=n_main, transposed=True)
    mg = _matmul(hn, w_in, i, bn=LANES, col0=n_main, ncols=LANES, transposed=True)[:, :4 * ML_HEADS]
    p3 = p.reshape(bsz, -1, n_main)
    n = p3.shape[1]

    cos, sin = rope
    tile2 = lambda g: jnp.tile(g, 2)[None, :]
    da = _diff_attention(p3, lam.reshape(1).astype(F32), cos, sin, tile2(qn_g), tile2(kn_g),
                         subln_g[None, :], n_ctx=seq["n_ctx"], out_scale=1.0 - lam_init)

    scale = jnp.concatenate([jnp.ones((nk,), F32), jnp.full((nk,), ML_DK ** -0.5, F32)])[None, :]
    hqk = _mlstm_prep(p, ml_conv_w, scale, col=3 * na // (2 * nk), **seq).reshape(bsz, n, 2 * nk)
    gates = (mg + ml_gate_b).reshape(bsz, n, 4 * ML_HEADS)
    outs = [o.reshape(-1, nv) for o in
            _mlstm_scan(hqk, p3, gates, v_col=(3 * na + 2 * nk) // nv, n_ctx=seq["n_ctx"])]
    m = _mlstm_finish(outs[0], outs[1], p, jnp.tile(ml_norm_g, ML_HEADS)[None, :],
                      col=(3 * na + 2 * nk + nv) // nv)
    return [da.reshape(-1, na), m]


def _modulation(c, c_ctx, ada_w, ada_b, layer):
    cc = jnp.concatenate([c, c_ctx[None, :]], axis=0)
    rows = cc.shape[0]
    act = jnp.pad(_silu(cc), ((0, BF16_ROWS - rows), (0, 0))).astype(BF16)
    mod = _matmul(act, ada_w, layer, bn=1024, bm=BF16_ROWS)[:rows] + ada_b[layer]
    return mod.reshape(rows, 6, -1)


def kernel(x, c, ctx, c_ctx, ada_w, ada_b, norm1_g, norm2_g, ev_w_in, ev_w_out, gla_gate_w2, gla_gate_b, gla_norm_g, sc_conv_w, od_w_in, od_w_out, da_qnorm_g, da_knorm_g, da_lambda, da_subln_g, ml_conv_w, ml_gate_b, ml_norm_g, ffn_w_up, ffn_conv_w, ffn_conv_b, ffn_w_down):
    bsz, n_lat, d = x.shape
    n_ctx = ctx.shape[1]
    n = n_ctx + n_lat
    depth = ada_w.shape[0]
    d_ff = ffn_w_down.shape[1]
    assert n % ROW_TILE == 0 and n_ctx <= ROW_CHUNK and d_ff % (4 * LANES) == 0
    rope = _rope_tables(n_lat, n_ctx)
    seq = dict(n_ctx=n_ctx, n_seq=n)

    ev_w_in = jnp.swapaxes(ev_w_in, 1, 2)
    od_w_in = jnp.swapaxes(od_w_in, 1, 2)
    mods = [_modulation(c, c_ctx, ada_w, ada_b, layer) for layer in range(depth)]
    xs = jnp.concatenate([ctx, x], axis=1).reshape(bsz * n, d)
    hn = _first_norm(xs, norm1_g[0][None, :], mods[0], **seq)
    for layer in range(depth):
        i = layer // 2
        if layer % 2 == 0:
            mix = _even_mixer(hn, bsz, ev_w_in, i, gla_gate_w2[i], gla_gate_b[i], gla_norm_g[i],
                              sc_conv_w[i], seq)
            w_out = ev_w_out
        else:
            mix = _odd_mixer(hn, bsz, rope, layer, od_w_in, i, da_qnorm_g[i], da_knorm_g[i],
                             da_lambda[i], da_subln_g[i], ml_conv_w[i], ml_gate_b[i], ml_norm_g[i], seq)
            w_out = od_w_out
        xs, hn = _out_proj(mix, w_out, i, xs, mods[layer], norm2_g[layer][None, :], **seq)
        act, w_down = _ffn_up(hn, ffn_w_up, ffn_w_down, layer, ffn_conv_w[layer],
                              ffn_conv_b[layer][None, :], **seq)
        nxt = min(layer + 1, depth - 1)
        xs, hn = _down_proj(act, w_down, xs,
                            mods[layer], norm1_g[nxt][None, :], mods[nxt], bk=d_ff // 2, **seq)
    return xs.reshape(bsz, n, d)[:, n_ctx:, :]
```

```python
import functools
import math

import jax
import jax.numpy as jnp
import numpy as np
from jax import lax
from jax.experimental import pallas as pl
from jax.experimental.pallas import tpu as pltpu

GRID_W = 64
EPS = 1e-6
GLA_HEADS, GLA_DK, GLA_DV, GLA_GATE_RANK, GLA_TAU = 4, 128, 256, 16, 16.0
SC_WIDTH = 1024
DA_HEADS, DA_DQK, DA_DV = 8, 64, 128
ROPE_BASE = 10000.0
ROPE_PAIRS = DA_DQK // 4
ML_HEADS, ML_DK, ML_DV = 4, 128, 256

GLA_CHUNK = 128
ML_CHUNK = 256
ATTN_BQ = 512
ATTN_SUB = 256
ATTN_CHUNK = 1024
ATTN_PREP = 544
ATTN_HEADS = 2
ROW_TILE = 1088
ROW_CHUNK = 272
LANES = 128
SUBLANES = 8
BF16_ROWS = 16
ROW_GROUP_UNROLL = 8
LOG2E = 1.4426950408889634
VMEM_LIMIT = 56 * 1024 * 1024

F32 = jnp.float32
BF16 = jnp.bfloat16
HI = lax.Precision.HIGHEST
NT = (((1,), (1,)), ((), ()))


def _mm_kernel(x_ref, w_ref, *rest, row_shift, transposed):
    o_ref, wb_ref = rest[-2:]

    @pl.when(pl.program_id(1) == 0)
    def _():
        if not transposed:
            wb_ref[...] = w_ref[...].astype(BF16)
            return
        bn = wb_ref.shape[1]
        piece = 2 * LANES
        for c0 in range(0, bn, piece):
            c1 = min(c0 + piece, bn)
            if row_shift and c1 == bn:
                rows = jnp.concatenate([w_ref[c0 + row_shift:bn, :], rest[0][...]], axis=0)
            else:
                rows = w_ref[c0 + row_shift:c1 + row_shift, :]
            wb_ref[:, c0:c1] = rows.T.astype(BF16)

    o_ref[...] = jnp.dot(x_ref[...], wb_ref[...], preferred_element_type=F32).astype(o_ref.dtype)


def _matmul(x, w, layer, *, bn, col0=0, ncols=None, col_shift=0, transposed=False, bm=ROW_TILE,
            out_dtype=F32):
    m, k = x.shape
    n_all = w.shape[1] if transposed else w.shape[2]
    ncols = n_all - col0 if ncols is None else ncols
    j0 = col0 // bn
    if transposed:
        w_spec = pl.BlockSpec((None, bn, k), lambda j, i: (layer, j0 + j, 0))
    else:
        w_spec = pl.BlockSpec((None, k, bn), lambda j, i: (layer, 0, j0 + j))
    in_specs = [pl.BlockSpec((bm, k), lambda j, i: (i, 0)), w_spec]
    operands = [x, w]
    if col_shift:
        per = bn // col_shift
        in_specs.append(pl.BlockSpec((None, col_shift, k), lambda j, i: (layer, (j0 + j + 1) * per, 0)))
        operands.append(w)
    return pl.pallas_call(
        functools.partial(_mm_kernel, row_shift=col_shift, transposed=transposed),
        grid=(pl.cdiv(ncols, bn), m // bm),
        in_specs=in_specs,
        out_specs=pl.BlockSpec((bm, bn), lambda j, i: (i, j)),
        out_shape=jax.ShapeDtypeStruct((m, ncols), out_dtype),
        scratch_shapes=[pltpu.VMEM((k, bn), BF16)],
        compiler_params=pltpu.CompilerParams(
            dimension_semantics=("arbitrary", "arbitrary"), vmem_limit_bytes=VMEM_LIMIT),
        name="matmul",
    )(*operands)


def _norm_mod(x, gain_scale, shift):
    return x * lax.rsqrt(jnp.mean(x * x, axis=-1, keepdims=True) + EPS) * gain_scale + shift


def _row_vectors(modb_ref, modc_ref, idx, gain_ref=None):
    def vec(ref):
        row = ref[0, idx:idx + 1, :]
        if gain_ref is not None:
            row = gain_ref[...] * (1.0 + row)
        return jnp.broadcast_to(row, (BF16_ROWS, row.shape[1]))

    lat, ctx = vec(modb_ref), vec(modc_ref)
    return lambda is_ctx: lat if is_ctx is None else jnp.where(is_ctx, ctx, lat)


def _head_rmsnorm(x, width):
    parts = []
    for c0 in range(0, x.shape[1], width):
        xh = x[:, c0:c0 + width]
        parts.append(xh * lax.rsqrt(jnp.mean(xh * xh, axis=-1, keepdims=True) + EPS))
    return parts[0] if len(parts) == 1 else jnp.concatenate(parts, axis=1)


def _sigmoid(x):
    return 1.0 / (1.0 + jnp.exp(-x))


def _chunk_rows(r0, rc, tiles_per_batch, bm):
    rows = r0 + lax.broadcasted_iota(jnp.int32, (rc, 1), 0)
    return rows, (pl.program_id(0) % tiles_per_batch) * bm + rows


def _edge_rows(r0, rc, bm, tiles_per_batch, n_ctx, n_seq, gate_ref, hprev_ref, hnext_ref):
    hr = hprev_ref.shape[0]
    pos0 = (pl.program_id(0) % tiles_per_batch) * bm
    if r0 == 0:
        before = hprev_ref[...].astype(F32)[hr - 1:hr, :]
        before = jnp.where((pos0 == 0) | (pos0 == n_ctx), 0.0, before)
    else:
        before = gate_ref[r0 - hr:r0, :].astype(F32)[hr - 1:hr, :]
    if r0 + rc == bm:
        after = hnext_ref[...].astype(F32)[0:1, :]
        after = jnp.where((pos0 + bm == n_ctx) | (pos0 + bm == n_seq), 0.0, after)
    else:
        after = gate_ref[r0 + rc:r0 + rc + hr, :].astype(F32)[0:1, :]
    return before, after


def _splits_rows(r0, rc, bm, tiles_per_batch, n_ctx):
    return any(r0 <= n_ctx - k * bm <= r0 + rc for k in range(tiles_per_batch))


def _dwconv3(s_ref, g, before, after, cw_ref, pos, n_ctx):
    rc = g.shape[0]
    s_ref[SUBLANES:SUBLANES + rc, :] = g
    s_ref[SUBLANES - 1:SUBLANES, :] = before
    s_ref[SUBLANES + rc:SUBLANES + rc + 1, :] = after
    prev = s_ref[SUBLANES - 1:SUBLANES - 1 + rc, :]
    nxt = s_ref[SUBLANES + 1:SUBLANES + 1 + rc, :]
    if pos is not None:
        prev = jnp.where(pos == n_ctx, 0.0, prev)
        nxt = jnp.where(pos == n_ctx - 1, 0.0, nxt)
    return prev * cw_ref[0:1, :] + g * cw_ref[1:2, :] + nxt * cw_ref[2:3, :]


def _mod_specs(n_batch, tiles_per_batch, d):
    return [pl.BlockSpec((1, 6, d), lambda i, *_: (i // tiles_per_batch, 0, 0)),
            pl.BlockSpec((1, 6, d), lambda i, *_: (n_batch, 0, 0))]


def _halo_specs(bm, width, col, n_rows, hr):
    hb = bm // hr
    last = n_rows // hr - 1
    return [pl.BlockSpec((hr, width), lambda i, *k: (jnp.maximum(i * hb - 1, 0), col(*k))),
            pl.BlockSpec((hr, width), lambda i, *k: (jnp.minimum((i + 1) * hb, last), col(*k)))]


def _norm_kernel(x_ref, gain_ref, modb_ref, modc_ref, o_ref, *, n_ctx, tiles_per_batch):
    gain_scale = _row_vectors(modb_ref, modc_ref, 1, gain_ref)
    shift = _row_vectors(modb_ref, modc_ref, 0)

    def rows(rs, is_ctx):
        o_ref[rs, :] = _norm_mod(x_ref[rs, :], gain_scale(is_ctx), shift(is_ctx)).astype(BF16)

    _for_row_groups(x_ref.shape[0], n_ctx, tiles_per_batch, rows)


def _first_norm(x, gain, mod, *, n_ctx, n_seq, bm=ROW_TILE):
    m, d = x.shape
    tpb = n_seq // bm
    return pl.pallas_call(
        functools.partial(_norm_kernel, n_ctx=n_ctx, tiles_per_batch=tpb),
        grid=(m // bm,),
        in_specs=[pl.BlockSpec((bm, d), lambda i: (i, 0)), pl.BlockSpec((1, d), lambda i: (0, 0))]
        + _mod_specs(mod.shape[0] - 1, tpb, d),
        out_specs=pl.BlockSpec((bm, d), lambda i: (i, 0)),
        out_shape=jax.ShapeDtypeStruct((m, d), BF16),
        compiler_params=pltpu.CompilerParams(vmem_limit_bytes=VMEM_LIMIT),
        name="first_norm",
    )(x, gain, mod, mod)


def _residual_norm(xo_ref, hn_ref, x_ref, modb_ref, modc_ref, gain_ref, nmodb_ref, nmodc_ref, *,
                   n_ctx, tiles_per_batch, gate_idx, shift_idx, inline=False):
    gate = _row_vectors(modb_ref, modc_ref, gate_idx)
    gain_scale = _row_vectors(nmodb_ref, nmodc_ref, shift_idx + 1, gain_ref)
    shift = _row_vectors(nmodb_ref, nmodc_ref, shift_idx)

    def rows(rs, is_ctx):
        x_new = x_ref[rs, :] + gate(is_ctx) * xo_ref[rs, :]
        xo_ref[rs, :] = x_new
        hn_ref[rs, :] = _norm_mod(x_new, gain_scale(is_ctx), shift(is_ctx)).astype(BF16)

    _for_row_groups(x_ref.shape[0], n_ctx, tiles_per_batch, rows, inline)


def _for_row_groups(bm, n_ctx, tiles_per_batch, fn, inline=False):
    rg = BF16_ROWS
    pos0 = (pl.program_id(0) % tiles_per_batch) * bm
    n_lead = -(-min(n_ctx, bm) // rg)
    if inline:
        for g in range(bm // rg):
            is_ctx = (pos0 + g * rg + lax.broadcasted_iota(jnp.int32, (rg, 1), 0) < n_ctx) if g < n_lead else None
            fn(slice(g * rg, (g + 1) * rg), is_ctx)
        return

    def group(with_ctx):
        def body(g, carry):
            r0 = pl.multiple_of(g * rg, rg)
            is_ctx = (pos0 + r0 + lax.broadcasted_iota(jnp.int32, (rg, 1), 0) < n_ctx) if with_ctx else None
            fn(pl.ds(r0, rg), is_ctx)
            return carry
        return body

    lax.fori_loop(0, n_lead, group(True), 0, unroll=ROW_GROUP_UNROLL)
    lax.fori_loop(n_lead, bm // rg, group(False), 0, unroll=ROW_GROUP_UNROLL)


def _out_kernel(*refs, n_in, n_ctx, tiles_per_batch):
    a_refs = refs[:n_in]
    w_ref, x_ref, modb_ref, modc_ref, gain_ref, xo_ref, hn_ref, wb_ref = refs[n_in:]

    @pl.when(pl.program_id(0) == 0)
    def _():
        for r0 in range(0, w_ref.shape[0], ROW_CHUNK):
            r1 = min(r0 + ROW_CHUNK, w_ref.shape[0])
            wb_ref[r0:r1, :] = w_ref[r0:r1, :].astype(BF16)

    k0 = 0
    for n_done, a_ref in enumerate(a_refs):
        part = jnp.dot(a_ref[...], wb_ref[k0:k0 + a_ref.shape[1], :], preferred_element_type=F32)
        xo_ref[...] = part if n_done == 0 else xo_ref[...] + part
        k0 += a_ref.shape[1]
    _residual_norm(xo_ref, hn_ref, x_ref, modb_ref, modc_ref, gain_ref, modb_ref, modc_ref,
                   n_ctx=n_ctx, tiles_per_batch=tiles_per_batch, gate_idx=2, shift_idx=3, inline=True)


def _out_proj(acts, w, layer, x, mod, gain, *, n_ctx, n_seq, bm=ROW_TILE // 2):
    m, d = x.shape
    kk = w.shape[1]
    tpb = n_seq // bm
    rows = lambda width: pl.BlockSpec((bm, width), lambda i: (i, 0))
    return pl.pallas_call(
        functools.partial(_out_kernel, n_in=len(acts), n_ctx=n_ctx, tiles_per_batch=tpb),
        grid=(m // bm,),
        in_specs=[rows(a.shape[1]) for a in acts]
        + [pl.BlockSpec((None, kk, d), lambda i: (layer, 0, 0), pipeline_mode=pl.Buffered(1)), rows(d)]
        + _mod_specs(mod.shape[0] - 1, tpb, d) + [pl.BlockSpec((1, d), lambda i: (0, 0))],
        out_specs=[rows(d), rows(d)],
        out_shape=[jax.ShapeDtypeStruct((m, d), F32), jax.ShapeDtypeStruct((m, d), BF16)],
        scratch_shapes=[pltpu.VMEM((kk, d), BF16)],
        compiler_params=pltpu.CompilerParams(
            dimension_semantics=("arbitrary",), vmem_limit_bytes=VMEM_LIMIT),
        name="out_proj",
    )(*acts, w, x, mod, mod, gain)


def _up_kernel(x_ref, xp_ref, xn_ref, wg_ref, wv_ref, cw_ref, cb_ref, wd_ref, o_ref, wdb_ref, wgb_ref,
               wvb_ref, xe_ref, s_ref, v_ref, *, n_ctx, n_seq, tiles_per_batch):
    i = pl.program_id(1)
    bm, rc, hr = x_ref.shape[0], ROW_CHUNK, xp_ref.shape[0]
    wdb_ref[...] = wd_ref[...].astype(BF16)

    @pl.when(i == 0)
    def _():
        wgb_ref[...] = wg_ref[...].astype(BF16)
        wvb_ref[...] = wv_ref[...].astype(BF16)

    xe_ref[0:hr, :] = xp_ref[...]
    xe_ref[hr:hr + bm, :] = x_ref[...]
    xe_ref[hr + bm:, :] = xn_ref[...]
    s_ref[...] = jnp.dot(xe_ref[...], wgb_ref[...], preferred_element_type=F32)
    v_ref[...] = jnp.dot(x_ref[...], wvb_ref[...], preferred_element_type=F32)
    pos0 = (i % tiles_per_batch) * bm
    s_ref[hr - 1:hr, :] = jnp.where((pos0 == 0) | (pos0 == n_ctx), 0.0, s_ref[hr - 1:hr, :])
    s_ref[hr + bm:hr + bm + 1, :] = jnp.where(
        (pos0 + bm == n_ctx) | (pos0 + bm == n_seq), 0.0, s_ref[hr + bm:hr + bm + 1, :])
    for r0 in range(0, bm, rc):
        rs = slice(r0, r0 + rc)
        prev = s_ref[hr - 1 + r0:hr - 1 + r0 + rc, :]
        nxt = s_ref[hr + 1 + r0:hr + 1 + r0 + rc, :]
        if _splits_rows(r0, rc, bm, tiles_per_batch, n_ctx):
            pos = pos0 + r0 + lax.broadcasted_iota(jnp.int32, (rc, 1), 0)
            prev = jnp.where(pos == n_ctx, 0.0, prev)
            nxt = jnp.where(pos == n_ctx - 1, 0.0, nxt)
        z = (prev * cw_ref[0:1, :] + s_ref[hr + r0:hr + r0 + rc, :] * cw_ref[1:2, :]
             + nxt * cw_ref[2:3, :] + cb_ref[...])
        o_ref[rs, :] = (z * _sigmoid(z) * v_ref[rs, :]).astype(BF16)


def _ffn_up(x, w, w_down, layer, conv_w, conv_b, *, n_ctx, n_seq, bn=512, bm=ROW_TILE):
    m, k = x.shape
    f = w.shape[2] // 2
    nj, ni = f // bn, m // bm
    slab = f // (nj * ni)
    assert slab * nj * ni == f and slab % BF16_ROWS == 0
    d_out = w_down.shape[2]
    tpb = n_seq // bm
    hb = bm // BF16_ROWS
    last = m // BF16_ROWS - 1
    return pl.pallas_call(
        functools.partial(_up_kernel, n_ctx=n_ctx, n_seq=n_seq, tiles_per_batch=tpb),
        grid=(nj, ni),
        in_specs=[pl.BlockSpec((bm, k), lambda j, i: (i, 0)),
                  pl.BlockSpec((BF16_ROWS, k), lambda j, i: (jnp.maximum(i * hb - 1, 0), 0)),
                  pl.BlockSpec((BF16_ROWS, k), lambda j, i: (jnp.minimum((i + 1) * hb, last), 0)),
                  pl.BlockSpec((None, k, bn), lambda j, i: (layer, 0, j)),
                  pl.BlockSpec((None, k, bn), lambda j, i: (layer, 0, j + nj)),
                  pl.BlockSpec((3, bn), lambda j, i: (0, j)), pl.BlockSpec((1, bn), lambda j, i: (0, j)),
                  pl.BlockSpec((None, slab, d_out), lambda j, i: (layer, j * ni + i, 0))],
        out_specs=[pl.BlockSpec((bm, bn), lambda j, i: (i, j)),
                   pl.BlockSpec((slab, d_out), lambda j, i: (j * ni + i, 0))],
        out_shape=[jax.ShapeDtypeStruct((m, f), BF16), jax.ShapeDtypeStruct((f, d_out), BF16)],
        scratch_shapes=[pltpu.VMEM((k, bn), BF16), pltpu.VMEM((k, bn), BF16),
                        pltpu.VMEM((bm + 2 * BF16_ROWS, k), BF16),
                        pltpu.VMEM((bm + 2 * BF16_ROWS, bn), F32), pltpu.VMEM((bm, bn), F32)],
        compiler_params=pltpu.CompilerParams(
            dimension_semantics=("arbitrary", "arbitrary"), vmem_limit_bytes=VMEM_LIMIT),
        name="ffn_up",
    )(x, x, x, w, w, conv_w, conv_b, w_down)


def _down_kernel(a_ref, w_ref, x_ref, modb_ref, modc_ref, gain_ref, nmodb_ref, nmodc_ref, xo_ref, hn_ref, *,
                 n_ctx, tiles_per_batch):
    part = jnp.dot(a_ref[...], w_ref[...], preferred_element_type=F32)

    @pl.when(pl.program_id(1) == 0)
    def _():
        xo_ref[...] = part

    @pl.when(pl.program_id(1) > 0)
    def _():
        xo_ref[...] += part

    @pl.when(pl.program_id(1) == pl.num_programs(1) - 1)
    def _():
        _residual_norm(xo_ref, hn_ref, x_ref, modb_ref, modc_ref, gain_ref, nmodb_ref, nmodc_ref,
                       n_ctx=n_ctx, tiles_per_batch=tiles_per_batch, gate_idx=5, shift_idx=0)


def _down_proj(a, w, x, mod, gain, next_mod, *, n_ctx, n_seq, bm=ROW_TILE // 2, bk=1408):
    m, f = a.shape
    d = w.shape[1]
    tpb = n_seq // bm
    row = lambda i, k: (i, 0)
    return pl.pallas_call(
        functools.partial(_down_kernel, n_ctx=n_ctx, tiles_per_batch=tpb),
        grid=(m // bm, f // bk),
        in_specs=[pl.BlockSpec((bm, bk), lambda i, k: (i, k)), pl.BlockSpec((bk, d), lambda i, k: (k, 0)),
                  pl.BlockSpec((bm, d), row)]
        + _mod_specs(mod.shape[0] - 1, tpb, d) + [pl.BlockSpec((1, d), lambda i, k: (0, 0))]
        + _mod_specs(mod.shape[0] - 1, tpb, d),
        out_specs=[pl.BlockSpec((bm, d), row), pl.BlockSpec((bm, d), row)],
        out_shape=[jax.ShapeDtypeStruct((m, d), F32), jax.ShapeDtypeStruct((m, d), BF16)],
        compiler_params=pltpu.CompilerParams(
            dimension_semantics=("arbitrary", "arbitrary"), vmem_limit_bytes=VMEM_LIMIT),
        name="down_proj",
    )(a, w, x, mod, mod, gain, next_mod, next_mod)


def _even_finish_kernel(of_ref, ob_ref, r_ref, sx_ref, sb_ref, sg_ref, sxp_ref, sxn_ref, sgp_ref, sgn_ref,
                        gain_ref, cw_ref, o_ref, s_ref, *, n_ctx, n_seq, tiles_per_batch):
    bm = of_ref.shape[0]
    nv = of_ref.shape[1]
    r = r_ref[...]
    o_ref[:, :nv] = (_head_rmsnorm(of_ref[...] + ob_ref[...], GLA_DV) * gain_ref[...]
                     * (r * _sigmoid(r))).astype(BF16)

    _, pos = _chunk_rows(0, bm, tiles_per_batch, bm)
    pos0 = (pl.program_id(0) % tiles_per_batch) * bm
    hr = sxp_ref.shape[0]
    f32 = lambda ref: ref[...].astype(F32)
    before = (f32(sxp_ref) * f32(sgp_ref))[hr - 1:hr, :]
    before = jnp.where((pos0 == 0) | (pos0 == n_ctx), 0.0, before)
    after = (f32(sxn_ref) * f32(sgn_ref))[0:1, :]
    after = jnp.where((pos0 + bm == n_ctx) | (pos0 + bm == n_seq), 0.0, after)
    split = pos if _splits_rows(0, bm, bm, tiles_per_batch, n_ctx) else None
    conv = _dwconv3(s_ref, f32(sg_ref) * f32(sx_ref), before, after, cw_ref, split, n_ctx)
    o_ref[:, nv:] = (f32(sb_ref) * conv).astype(BF16)


def _even_finish(o_f, o_b, pa, pb, gain, conv_w, *, n_ctx, n_seq, bm=ROW_CHUNK):
    m, nv = o_f.shape
    sc = conv_w.shape[1]
    tpb = n_seq // bm
    col = lambda j: (lambda i: (i, j))
    const = lambda i: (0, 0)
    return pl.pallas_call(
        functools.partial(_even_finish_kernel, n_ctx=n_ctx, n_seq=n_seq, tiles_per_batch=tpb),
        grid=(m // bm,),
        in_specs=[pl.BlockSpec((bm, nv), col(0)), pl.BlockSpec((bm, nv), col(0)),
                  pl.BlockSpec((bm, nv), col(2)),
                  pl.BlockSpec((bm, sc), col(0)), pl.BlockSpec((bm, sc), col(1)), pl.BlockSpec((bm, sc), col(2))]
        + _halo_specs(bm, sc, lambda: 0, m, BF16_ROWS) + _halo_specs(bm, sc, lambda: 2, m, BF16_ROWS)
        + [pl.BlockSpec((1, nv), const), pl.BlockSpec((3, sc), const)],
        out_specs=pl.BlockSpec((bm, nv + sc), col(0)),
        out_shape=jax.ShapeDtypeStruct((m, nv + sc), BF16),
        scratch_shapes=[pltpu.VMEM((bm + 2 * SUBLANES, sc), F32)],
        compiler_params=pltpu.CompilerParams(vmem_limit_bytes=VMEM_LIMIT),
        name="even_finish",
    )(o_f, o_b, pa, pb, pb, pb, pb, pb, pb, pb, gain, conv_w)


def _mlstm_prep_kernel(x_ref, xp_ref, xn_ref, cw_ref, scale_ref, o_ref, s_ref, *, n_ctx, n_seq,
                       tiles_per_batch):
    bm = x_ref.shape[0]
    _, pos = _chunk_rows(0, bm, tiles_per_batch, bm)
    before, after = _edge_rows(0, bm, bm, tiles_per_batch, n_ctx, n_seq, x_ref, xp_ref, xn_ref)
    split = pos if _splits_rows(0, bm, bm, tiles_per_batch, n_ctx) else None
    z = _dwconv3(s_ref, x_ref[...], before, after, cw_ref, split, n_ctx)
    o_ref[...] = (z * _sigmoid(z) * scale_ref[...]).astype(BF16)


def _mlstm_prep(p, conv_w, scale, *, col, n_ctx, n_seq, bm=ROW_CHUNK):
    m = p.shape[0]
    width = conv_w.shape[1]
    tpb = n_seq // bm
    const = lambda i: (0, 0)
    return pl.pallas_call(
        functools.partial(_mlstm_prep_kernel, n_ctx=n_ctx, n_seq=n_seq, tiles_per_batch=tpb),
        grid=(m // bm,),
        in_specs=[pl.BlockSpec((bm, width), lambda i: (i, col))]
        + _halo_specs(bm, width, lambda: col, m, SUBLANES)
        + [pl.BlockSpec((3, width), const), pl.BlockSpec((1, width), const)],
        out_specs=pl.BlockSpec((bm, width), lambda i: (i, 0)),
        out_shape=jax.ShapeDtypeStruct((m, width), BF16),
        scratch_shapes=[pltpu.VMEM((bm + 2 * SUBLANES, width), F32)],
        compiler_params=pltpu.CompilerParams(vmem_limit_bytes=VMEM_LIMIT),
        name="mlstm_prep",
    )(p, p, p, conv_w, scale)


def _mlstm_finish_kernel(hf_ref, hb_ref, mo_ref, gain_ref, o_ref):
    o_ref[...] = (_head_rmsnorm(hf_ref[...] + hb_ref[...], ML_DV) * gain_ref[...]
                  * _sigmoid(mo_ref[...])).astype(BF16)


def _mlstm_finish(h_f, h_b, p, gain, *, col, bm=ROW_TILE // 2):
    m, nv = h_f.shape
    return pl.pallas_call(
        _mlstm_finish_kernel,
        grid=(m // bm,),
        in_specs=[pl.BlockSpec((bm, nv), lambda i: (i, 0)), pl.BlockSpec((bm, nv), lambda i: (i, 0)),
                  pl.BlockSpec((bm, nv), lambda i: (i, col)), pl.BlockSpec((1, nv), lambda i: (0, 0))],
        out_specs=pl.BlockSpec((bm, nv), lambda i: (i, 0)),
        out_shape=jax.ShapeDtypeStruct((m, nv), BF16),
        compiler_params=pltpu.CompilerParams(vmem_limit_bytes=VMEM_LIMIT),
        name="mlstm_finish",
    )(h_f, h_b, p, gain)


def _qk_prep(x, cos, sin, gain):
    lane = lax.broadcasted_iota(jnp.int32, x.shape, 1)
    lo = lane < DA_DQK
    x2 = x * x
    ss_lo = jnp.sum(jnp.where(lo, x2, 0.0), axis=-1, keepdims=True)
    ss_hi = jnp.sum(jnp.where(lo, 0.0, x2), axis=-1, keepdims=True)
    y = x * lax.rsqrt(jnp.where(lo, ss_lo, ss_hi) * (1.0 / DA_DQK) + EPS) * gain
    swapped = jnp.where((lane & ROPE_PAIRS) == 0, pltpu.roll(y, LANES - ROPE_PAIRS, 1),
                        pltpu.roll(y, ROPE_PAIRS, 1))
    return y * cos + swapped * sin


def _attn_kernel(lam_ref, q_ref, k_ref, v_ref, cos_ref, sin_ref, qg_ref, kg_ref, og_ref, o_ref,
                 kb_ref, vb_ref, *, n_ctx, chunk, prep_rows, out_scale):
    qi = pl.program_id(2)
    lam = lam_ref[0]
    n = k_ref.shape[1]
    bq = q_ref.shape[1]
    heads = q_ref.shape[2] // LANES

    @pl.when(qi == 0)
    def _():
        for r0 in range(0, n, prep_rows):
            rs = slice(r0, r0 + prep_rows)
            for h in range(heads):
                hs = slice(h * LANES, (h + 1) * LANES)
                kb_ref[h, rs, :] = _qk_prep(k_ref[0, rs, hs], cos_ref[rs, :], sin_ref[rs, :],
                                            kg_ref[...]).astype(BF16)
                vb_ref[h, rs, :LANES] = v_ref[0, rs, hs].astype(BF16)
                vb_ref[h, rs, LANES:] = jnp.ones((prep_rows, LANES), BF16)

    def attend(r0, r1, nk):
        nr = r1 - r0
        rows = pl.ds(pl.multiple_of(qi * bq, BF16_ROWS) + r0, nr)
        cos, sin = cos_ref[rows, :], sin_ref[rows, :]
        lane = lax.broadcasted_iota(jnp.int32, (nr, LANES), 1)
        q2, m, acc = [], [], []
        for h in range(heads):
            q = _qk_prep(q_ref[0, r0:r1, h * LANES:(h + 1) * LANES], cos, sin, qg_ref[...])
            q = (q * (DA_DQK ** -0.5 * LOG2E)).astype(BF16)
            zero = jnp.zeros_like(q)
            q2.append(jnp.concatenate([jnp.where(lane < DA_DQK, q, zero),
                                       jnp.where(lane >= DA_DQK, q, zero)], axis=0))
            m.append(jnp.full((2 * nr, 1), -jnp.inf, F32))
            acc.append(jnp.zeros((2 * nr, 2 * LANES), F32))
        for c0 in range(0, nk, chunk):
            c1 = min(c0 + chunk, nk)
            for h in range(heads):
                s = lax.dot_general(q2[h], kb_ref[h, c0:c1, :], NT, preferred_element_type=F32)
                m_new = jnp.maximum(m[h], jnp.max(s, axis=-1, keepdims=True))
                p = jnp.exp2(s - m_new)
                acc[h] = jnp.exp2(m[h] - m_new) * acc[h] + jnp.dot(
                    p.astype(BF16), vb_ref[h, c0:c1, :], preferred_element_type=F32)
                m[h] = m_new
        for h in range(heads):
            o = acc[h][:, :LANES] / acc[h][:, LANES:]
            o = o[:nr] - lam * o[nr:]
            o = o * lax.rsqrt(jnp.mean(o * o, axis=-1, keepdims=True) + EPS) * og_ref[...] * out_scale
            o_ref[0, r0:r1, h * LANES:(h + 1) * LANES] = o.astype(BF16)

    last = pl.num_programs(2) - 1
    last_rows = n - (n // bq) * bq if n % bq else bq

    @pl.when(qi == 0)
    def _():
        attend(0, n_ctx, n_ctx)
        for r0 in range(n_ctx, bq, ATTN_SUB):
            attend(r0, r0 + ATTN_SUB, n)

    @pl.when((qi > 0) & (qi < last))
    def _():
        for r0 in range(0, bq, ATTN_SUB):
            attend(r0, r0 + ATTN_SUB, n)

    @pl.when((qi > 0) & (qi == last))
    def _():
        for r0 in range(0, last_rows, ATTN_SUB):
            attend(r0, r0 + ATTN_SUB, n)


def _diff_attention(p, lam, cos, sin, q_gain, k_gain, out_gain, *, n_ctx, out_scale, bq=ATTN_BQ,
                    chunk=ATTN_CHUNK, prep_rows=ATTN_PREP, heads=ATTN_HEADS):
    b, n, _ = p.shape
    assert n_ctx % ATTN_SUB == 0 and bq % ATTN_SUB == 0 and n % ATTN_SUB == 0 and n_ctx <= bq
    hg = DA_HEADS // heads
    hd = DA_HEADS * LANES
    hw = heads * LANES
    const = lambda bi, hi, qi: (0, 0)
    return pl.pallas_call(
        functools.partial(_attn_kernel, n_ctx=n_ctx, chunk=chunk, prep_rows=prep_rows, out_scale=out_scale),
        grid=(b, hg, pl.cdiv(n, bq)),
        in_specs=[pl.BlockSpec(memory_space=pltpu.SMEM),
                  pl.BlockSpec((1, bq, hw), lambda bi, hi, qi: (bi, qi, hi)),
                  pl.BlockSpec((1, n, hw), lambda bi, hi, qi: (bi, 0, hg + hi)),
                  pl.BlockSpec((1, n, hw), lambda bi, hi, qi: (bi, 0, 2 * hg + hi)),
                  pl.BlockSpec((n, LANES), const), pl.BlockSpec((n, LANES), const),
                  pl.BlockSpec((1, LANES), const), pl.BlockSpec((1, LANES), const),
                  pl.BlockSpec((1, LANES), const)],
        out_specs=pl.BlockSpec((1, bq, hw), lambda bi, hi, qi: (bi, qi, hi)),
        out_shape=jax.ShapeDtypeStruct((b, n, hd), BF16),
        scratch_shapes=[pltpu.VMEM((heads, n, LANES), BF16), pltpu.VMEM((heads, n, 2 * LANES), BF16)],
        compiler_params=pltpu.CompilerParams(
            dimension_semantics=("arbitrary", "arbitrary", "arbitrary"), vmem_limit_bytes=VMEM_LIMIT),
        name="diff_attention",
    )(lam, p, p, p, cos, sin, q_gain, k_gain, out_gain)


def _scan_chunk(t, n_ctx_chunks, n_chunks, reverse):
    if not reverse:
        return t
    return jnp.where(t < n_ctx_chunks, n_ctx_chunks - 1 - t, n_chunks - 1 - (t - n_ctx_chunks))


def _gla_constants(L, reverse):
    nlev = int(math.log2(L))
    idx = np.arange(L)
    i, t = idx[:, None], idx[None, :]
    if reverse:
        i, t = L - 1 - i, L - 1 - t
    rs = [(t <= i), (t > i)]
    am = [(i == t)]
    for lev in range(nlev):
        m = L >> (lev + 1)
        blk_i, blk_t = i // (2 * m), t // (2 * m)
        mid = blk_i * 2 * m + m
        q_role = i >= mid
        rs.append(np.where(q_role, (t >= mid) & (t <= i), (t > i) & (t < mid)) & (blk_i == blk_t))
        am.append((blk_i == blk_t) & q_role & (t < mid))
    return (np.stack(rs).astype(np.float32).reshape((nlev + 2) * L, L),
            np.stack(am).astype(np.float32))


def _gla_kernel(*refs):
    of_ref, ob_ref, st_ref = refs[16:]

    @pl.when(pl.program_id(1) == 0)
    def _():
        st_ref[...] = jnp.zeros_like(st_ref)

    for d, o_ref in enumerate((of_ref, ob_ref)):
        _gla_direction(*refs[8 * d:8 * d + 8], o_ref, st_ref, d * GLA_HEADS, bool(d))


def _gla_direction(q_ref, k_ref, v_ref, glr_ref, w2_ref, gb_ref, rsum_ref, amask_ref, o_ref, st_ref, h0,
                   reverse):
    L = q_ref.shape[1]
    nlev = amask_ref.shape[0] - 1
    n_coarse = nlev + 2 - rsum_ref.shape[0] // L

    z = jnp.dot(glr_ref[0].astype(BF16), w2_ref[...], preferred_element_type=F32) + gb_ref[...]
    g = (jnp.minimum(z, 0.0) - jnp.log(1.0 + jnp.exp(-jnp.abs(z)))) * (1.0 / GLA_TAU)
    g1 = g.astype(BF16)
    g2 = (g - g1.astype(F32)).astype(BF16)
    rsum = rsum_ref[...]
    e_all = (jnp.dot(rsum, g1, preferred_element_type=F32)
             + jnp.dot(rsum, g2, preferred_element_type=F32))
    b_tot = jnp.sum(g, axis=0, keepdims=True)

    def level_decay(lev, ck):
        if lev >= n_coarse:
            return jnp.exp(e_all[(2 + lev - n_coarse) * L:(3 + lev - n_coarse) * L, ck])
        m = L >> (lev + 1)
        b3 = e_all[0:L, ck].reshape(L // (2 * m), 2 * m, GLA_DK)
        pivot = m if reverse else m - 1
        return jnp.exp(-jnp.abs(b3 - b3[:, pivot:pivot + 1, :])).reshape(L, GLA_DK)

    for h in range(GLA_HEADS):
        ck = slice(h * GLA_DK, (h + 1) * GLA_DK)
        cv = slice(h * GLA_DV, (h + 1) * GLA_DV)
        q = q_ref[0, :, ck] * (GLA_DK ** -0.5)
        k = k_ref[0, :, ck]
        v = v_ref[0, :, cv].astype(BF16)
        st = st_ref[h0 + h]
        a = amask_ref[0] * lax.dot_general(q.astype(BF16), k.astype(BF16), NT, preferred_element_type=F32)
        for lev in range(nlev):
            e = level_decay(lev, ck)
            a = a + amask_ref[1 + lev] * lax.dot_general(
                (q * e).astype(BF16), (k * e).astype(BF16), NT, preferred_element_type=F32)
        qe = (q * jnp.exp(e_all[0:L, ck])).astype(BF16)
        o = lax.dot_general(qe, st.astype(BF16), NT, preferred_element_type=F32)
        o = o + jnp.dot(a.astype(BF16), v, preferred_element_type=F32)
        o_ref[0, :, cv] = o
        kd = (k * jnp.exp(e_all[L:2 * L, ck])).astype(BF16)
        st_ref[h0 + h] = jnp.exp(b_tot[:, ck]) * st + jnp.dot(
            v.T, kd, preferred_element_type=F32)


def _gla_scan(p, glr, w2s, gbs, *, n_ctx, L=GLA_CHUNK):
    b, n, _ = p.shape
    nc, ncc = n // L, n_ctx // L
    hk, hv = GLA_HEADS * GLA_DK, GLA_HEADS * GLA_DV
    const2 = lambda bi, t: (0, 0)
    const3 = lambda bi, t: (0, 0, 0)
    in_specs, operands, out_specs = [], [], []
    for d in range(2):
        rsum, amask = _gla_constants(L, bool(d))
        n_coarse = sum(2 * (L >> (lev + 1)) >= SUBLANES for lev in range(amask.shape[0] - 1))
        rsum = rsum.reshape(-1, L, L)
        rsum = np.concatenate([rsum[:2], rsum[2 + n_coarse:]]).reshape(-1, L)
        chunk = functools.partial(_scan_chunk, n_ctx_chunks=ncc, n_chunks=nc, reverse=bool(d))
        cols = lambda width, start, chunk=chunk: pl.BlockSpec(
            (1, L, width), lambda bi, t: (bi, chunk(t), start // width))
        in_specs += [cols(hk, 0), cols(hk, hk), cols(hv, 2 * hk), cols(LANES, 0),
                     pl.BlockSpec(w2s[d].shape, const2), pl.BlockSpec(gbs[d].shape, const2),
                     pl.BlockSpec(rsum.shape, const2), pl.BlockSpec(amask.shape, const3)]
        operands += [p, p, p, glr, w2s[d], gbs[d], jnp.asarray(rsum, BF16), jnp.asarray(amask)]
        out_specs.append(cols(hv, 0))
    return pl.pallas_call(
        _gla_kernel,
        grid=(b, nc),
        in_specs=in_specs,
        out_specs=out_specs,
        out_shape=[jax.ShapeDtypeStruct((b, n, hv), F32)] * 2,
        scratch_shapes=[pltpu.VMEM((2 * GLA_HEADS, GLA_DV, GLA_DK), F32)],
        compiler_params=pltpu.CompilerParams(
            dimension_semantics=("arbitrary", "arbitrary"), vmem_limit_bytes=VMEM_LIMIT),
        name="gla_scan",
    )(*operands)


def _log_sigmoid(x):
    return jnp.minimum(x, 0.0) - jnp.log(1.0 + jnp.exp(-jnp.abs(x)))


def _mlstm_kernel(*refs):
    tri_ref, of_ref, ob_ref, c_ref, m_ref = refs[10:]

    @pl.when(pl.program_id(0) == 0)
    def _():
        c_ref[...] = jnp.zeros_like(c_ref)
        m_ref[...] = jnp.zeros_like(m_ref)

    for bi in range(of_ref.shape[0]):
        for d, o_ref in enumerate((of_ref, ob_ref)):
            _mlstm_direction(*refs[5 * d:5 * d + 5], tri_ref[d], o_ref, c_ref, m_ref, bi,
                             (2 * bi + d) * ML_HEADS, bool(d))


def _mlstm_direction(q_ref, k_ref, v_ref, gc_ref, gr_ref, tri, o_ref, c_ref, m_ref, bi, h0, reverse):
    L = q_ref.shape[1]
    H = ML_HEADS
    ones = jnp.ones((L, LANES), BF16)
    gc = gc_ref[bi]
    gr = gr_ref[bi]
    ic_col, ic_row = gc[:, :H], gr[:H, :]
    b_col = jnp.dot(tri, _log_sigmoid(gc[:, H:]), precision=HI, preferred_element_type=F32)
    b_row = lax.dot_general(_log_sigmoid(gr[H:, :]), tri, NT, precision=HI, preferred_element_type=F32)
    last = 0 if reverse else L - 1
    causal = tri > 0.5

    for h in range(H):
        ck = slice(h * ML_DK, (h + 1) * ML_DK)
        cv = slice(h * ML_DV, (h + 1) * ML_DV)
        q = q_ref[bi, :, ck]
        k = k_ref[bi, :, ck]
        v = jnp.concatenate([v_ref[bi, :, cv].astype(BF16), ones], axis=1)
        c, m = c_ref[h0 + h], m_ref[h0 + h]
        bc, br = b_col[:, h:h + 1], b_row[h:h + 1, :]
        icc, icr = ic_col[:, h:h + 1], ic_row[h:h + 1, :]
        b_last = bc[last:last + 1, :]

        a = bc + m
        dmat = jnp.where(causal, bc - br + icr, -jnp.inf)
        m_t = jnp.maximum(a, jnp.max(dmat, axis=-1, keepdims=True))
        w_inter = jnp.exp(a - m_t)
        s = lax.dot_general(q, k, NT, preferred_element_type=F32) * jnp.exp(dmat - m_t)
        both = w_inter * jnp.dot(q, c.astype(BF16), preferred_element_type=F32) + jnp.dot(
            s.astype(BF16), v, preferred_element_type=F32)
        den = jnp.maximum(jnp.abs(both[:, ML_DV:]), jnp.exp(-m_t))
        o_ref[bi, :, cv] = both[:, :ML_DV] / jnp.concatenate([den] * (ML_DV // LANES), axis=1)

        gs_col = b_last - bc + icc
        gs_row = b_last - br + icr
        m_new = jnp.maximum(b_last + m, jnp.max(gs_row, axis=-1, keepdims=True))
        decay = jnp.exp(b_last + m - m_new)
        wk = jnp.exp(gs_col - m_new) * k.astype(F32)
        c_ref[h0 + h] = decay * c + jnp.dot(wk.astype(BF16).T, v, preferred_element_type=F32)
        m_ref[h0 + h] = m_new


def _mlstm_scan(qk, p, gates, *, v_col, n_ctx, L=ML_CHUNK):
    b, n, _ = qk.shape
    hk, hv, ng = ML_HEADS * ML_DK, ML_HEADS * ML_DV, 2 * ML_HEADS
    nc, ncc = n // L, n_ctx // L
    idx = np.arange(L)
    lower = idx[None, :] <= idx[:, None]
    tri = jnp.asarray(np.stack([lower, lower.T]).astype(np.float32))
    in_specs, operands, out_specs = [], [], []
    for d in range(2):
        chunk = functools.partial(_scan_chunk, n_ctx_chunks=ncc, n_chunks=nc, reverse=bool(d))
        col = lambda j, chunk=chunk: (lambda t: (0, chunk(t), j))
        g_dir = gates[..., d * ng:(d + 1) * ng]
        in_specs += [pl.BlockSpec((b, L, hk), col(0)), pl.BlockSpec((b, L, hk), col(1)),
                     pl.BlockSpec((b, L, hv), col(v_col)), pl.BlockSpec((b, L, ng), col(0)),
                     pl.BlockSpec((b, ng, L), lambda t, chunk=chunk: (0, 0, chunk(t)))]
        operands += [qk, qk, p, g_dir, jnp.swapaxes(g_dir, 1, 2)]
        out_specs.append(pl.BlockSpec((b, L, hv), col(0)))
    return pl.pallas_call(
        _mlstm_kernel,
        grid=(nc,),
        in_specs=in_specs + [pl.BlockSpec((2, L, L), lambda t: (0, 0, 0))],
        out_specs=out_specs,
        out_shape=[jax.ShapeDtypeStruct((b, n, hv), F32)] * 2,
        scratch_shapes=[pltpu.VMEM((b * 2 * ML_HEADS, ML_DK, ML_DV + LANES), F32),
                        pltpu.VMEM((b * 2 * ML_HEADS, 1, 1), F32)],
        compiler_params=pltpu.CompilerParams(
            dimension_semantics=("arbitrary",), vmem_limit_bytes=VMEM_LIMIT),
        name="mlstm_scan",
    )(*operands, tri)


def _silu(x):
    return x * jax.nn.sigmoid(x)


def _rope_tables(n_lat, n_ctx):
    rows = n_lat // GRID_W
    row = jnp.repeat(jnp.arange(rows, dtype=F32), GRID_W)
    col = jnp.tile(jnp.arange(GRID_W, dtype=F32), rows)
    inv = jnp.power(ROPE_BASE, -jnp.arange(ROPE_PAIRS, dtype=F32) / ROPE_PAIRS)
    ang_r, ang_c = row[:, None] * inv, col[:, None] * inv
    cos = jnp.concatenate([jnp.cos(ang_r)] * 2 + [jnp.cos(ang_c)] * 2, axis=-1)
    sin = jnp.concatenate([-jnp.sin(ang_r), jnp.sin(ang_r), -jnp.sin(ang_c), jnp.sin(ang_c)], axis=-1)
    pad = ((n_ctx, 0), (0, 0))
    cos, sin = jnp.pad(cos, pad, constant_values=1.0), jnp.pad(sin, pad)
    return jnp.tile(cos, (1, 2)), jnp.tile(sin, (1, 2))


def _even_mixer(hn, bsz, w_in, i, gate_w2, gate_b, gla_norm_g, sc_conv_w, seq):
    nq = GLA_HEADS * GLA_DK
    nv = GLA_HEADS * GLA_DV
    gate_col = 2 * nq + 2 * nv
    pa = _matmul(hn, w_in, i, bn=1024, ncols=gate_col, transposed=True)
    glr = _matmul(hn, w_in, i, bn=LANES, col0=gate_col, ncols=LANES, transposed=True)
    pb = _matmul(hn, w_in, i, bn=SC_WIDTH, col0=gate_col, ncols=3 * SC_WIDTH,
                 col_shift=2 * GLA_GATE_RANK, transposed=True, out_dtype=BF16)
    pa3 = pa.reshape(bsz, -1, pa.shape[1])
    glr3 = glr.reshape(bsz, -1, LANES)
    w2s = [jnp.zeros((LANES, nq), F32).at[d * GLA_GATE_RANK:(d + 1) * GLA_GATE_RANK].set(gate_w2[d])
           .astype(BF16) for d in range(2)]
    outs = [o.reshape(-1, nv) for o in
            _gla_scan(pa3, glr3, w2s, [gate_b[d][None, :] for d in range(2)], n_ctx=seq["n_ctx"])]
    gain = jnp.tile(gla_norm_g, GLA_HEADS)[None, :]
    return [_even_finish(outs[0], outs[1], pa, pb, gain, sc_conv_w, **seq)]


def _odd_mixer(hn, bsz, rope, layer, w_in, i, qn_g, kn_g, lam_p, subln_g, ml_conv_w, ml_gate_b,
               ml_norm_g, seq):
    lam_init = 0.8 - 0.6 * math.exp(-0.3 * layer)
    lam = (jnp.exp(jnp.sum(lam_p[0] * lam_p[1])) - jnp.exp(jnp.sum(lam_p[2] * lam_p[3])) + lam_init)
    na = DA_HEADS * 2 * DA_DQK
    nk = ML_HEADS * ML_DK
    nv = ML_HEADS * ML_DV
    n_main = 3 * na + 2 * nk + 2 * nv
    p = _matmul(hn, w_in, i, bn=1024, ncols=n_main, transposed=True)
    mg = _matmul(hn, w_in, i, bn=LANES, col0=n_main, ncols=LANES, transposed=True)[:, :4 * ML_HEADS]
    p3 = p.reshape(bsz, -1, n_main)
    n = p3.shape[1]

    cos, sin = rope
    tile2 = lambda g: jnp.tile(g, 2)[None, :]
    da = _diff_attention(p3, lam.reshape(1).astype(F32), cos, sin, tile2(qn_g), tile2(kn_g),
                         subln_g[None, :], n_ctx=seq["n_ctx"], out_scale=1.0 - lam_init)

    scale = jnp.concatenate([jnp.ones((nk,), F32), jnp.full((nk,), ML_DK ** -0.5, F32)])[None, :]
    hqk = _mlstm_prep(p, ml_conv_w, scale, col=3 * na // (2 * nk), **seq).reshape(bsz, n, 2 * nk)
    gates = (mg + ml_gate_b).reshape(bsz, n, 4 * ML_HEADS)
    outs = [o.reshape(-1, nv) for o in
            _mlstm_scan(hqk, p3, gates, v_col=(3 * na + 2 * nk) // nv, n_ctx=seq["n_ctx"])]
    m = _mlstm_finish(outs[0], outs[1], p, jnp.tile(ml_norm_g, ML_HEADS)[None, :],
                      col=(3 * na + 2 * nk + nv) // nv)
    return [da.reshape(-1, na), m]


def _modulation(c, c_ctx, ada_w, ada_b, layer):
    cc = jnp.concatenate([c, c_ctx[None, :]], axis=0)
    rows = cc.shape[0]
    act = jnp.pad(_silu(cc), ((0, BF16_ROWS - rows), (0, 0))).astype(BF16)
    mod = _matmul(act, ada_w, layer, bn=1024, bm=BF16_ROWS)[:rows] + ada_b[layer]
    return mod.reshape(rows, 6, -1)


def kernel(x, c, ctx, c_ctx, ada_w, ada_b, norm1_g, norm2_g, ev_w_in, ev_w_out, gla_gate_w2, gla_gate_b, gla_norm_g, sc_conv_w, od_w_in, od_w_out, da_qnorm_g, da_knorm_g, da_lambda, da_subln_g, ml_conv_w, ml_gate_b, ml_norm_g, ffn_w_up, ffn_conv_w, ffn_conv_b, ffn_w_down):
    bsz, n_lat, d = x.shape
    n_ctx = ctx.shape[1]
    n = n_ctx + n_lat
    depth = ada_w.shape[0]
    d_ff = ffn_w_down.shape[1]
    assert n % ROW_TILE == 0 and n_ctx <= ROW_CHUNK and d_ff % (4 * LANES) == 0
    rope = _rope_tables(n_lat, n_ctx)
    seq = dict(n_ctx=n_ctx, n_seq=n)

    ev_w_in = jnp.swapaxes(ev_w_in, 1, 2)
    od_w_in = jnp.swapaxes(od_w_in, 1, 2)
    mods = [_modulation(c, c_ctx, ada_w, ada_b, layer) for layer in range(depth)]
    xs = jnp.concatenate([ctx, x], axis=1).reshape(bsz * n, d)
    hn = _first_norm(xs, norm1_g[0][None, :], mods[0], **seq)
    for layer in range(depth):
        i = layer // 2
        if layer % 2 == 0:
            mix = _even_mixer(hn, bsz, ev_w_in, i, gla_gate_w2[i], gla_gate_b[i], gla_norm_g[i],
                              sc_conv_w[i], seq)
            w_out = ev_w_out
        else:
            mix = _odd_mixer(hn, bsz, rope, layer, od_w_in, i, da_qnorm_g[i], da_knorm_g[i],
                             da_lambda[i], da_subln_g[i], ml_conv_w[i], ml_gate_b[i], ml_norm_g[i], seq)
            w_out = od_w_out
        xs, hn = _out_proj(mix, w_out, i, xs, mods[layer], norm2_g[layer][None, :], **seq)
        act, w_down = _ffn_up(hn, ffn_w_up, ffn_w_down, layer, ffn_conv_w[layer],
                              ffn_conv_b[layer][None, :], **seq)
        nxt = min(layer + 1, depth - 1)
        xs, hn = _down_proj(act, w_down, xs,
                            mods[layer], norm1_g[nxt][None, :], mods[nxt], bk=d_ff // 2, **seq)
    return xs.reshape(bsz, n, d)[:, n_ctx:, :]
```

```python
import functools
import math

import jax
import jax.numpy as jnp
import numpy as np
from jax import lax
from jax.experimental import pallas as pl
from jax.experimental.pallas import tpu as pltpu

GRID_W = 64
EPS = 1e-6
GLA_HEADS, GLA_DK, GLA_DV, GLA_GATE_RANK, GLA_TAU = 4, 128, 256, 16, 16.0
SC_WIDTH = 1024
DA_HEADS, DA_DQK, DA_DV = 8, 64, 128
ROPE_BASE = 10000.0
ROPE_PAIRS = DA_DQK // 4
ML_HEADS, ML_DK, ML_DV = 4, 128, 256

GLA_CHUNK = 128
ML_CHUNK = 256
ATTN_BQ = 512
ATTN_SUB = 256
ATTN_CHUNK = 1024
ATTN_PREP = 544
ATTN_HEADS = 2
ROW_TILE = 1088
ROW_CHUNK = 272
LANES = 128
SUBLANES = 8
BF16_ROWS = 16
ROW_GROUP_UNROLL = 8
LOG2E = 1.4426950408889634
VMEM_LIMIT = 56 * 1024 * 1024

F32 = jnp.float32
BF16 = jnp.bfloat16
HI = lax.Precision.HIGHEST
NT = (((1,), (1,)), ((), ()))


def _mm_kernel(x_ref, w_ref, *rest, row_shift, transposed):
    o_ref, wb_ref = rest[-2:]

    @pl.when(pl.program_id(1) == 0)
    def _():
        if not transposed:
            wb_ref[...] = w_ref[...].astype(BF16)
            return
        bn = wb_ref.shape[1]
        piece = 2 * LANES
        for c0 in range(0, bn, piece):
            c1 = min(c0 + piece, bn)
            if row_shift and c1 == bn:
                rows = jnp.concatenate([w_ref[c0 + row_shift:bn, :], rest[0][...]], axis=0)
            else:
                rows = w_ref[c0 + row_shift:c1 + row_shift, :]
            wb_ref[:, c0:c1] = rows.T.astype(BF16)

    o_ref[...] = jnp.dot(x_ref[...], wb_ref[...], preferred_element_type=F32).astype(o_ref.dtype)


def _matmul(x, w, layer, *, bn, col0=0, ncols=None, col_shift=0, transposed=False, bm=ROW_TILE,
            out_dtype=F32):
    m, k = x.shape
    n_all = w.shape[1] if transposed else w.shape[2]
    ncols = n_all - col0 if ncols is None else ncols
    j0 = col0 // bn
    if transposed:
        w_spec = pl.BlockSpec((None, bn, k), lambda j, i: (layer, j0 + j, 0))
    else:
        w_spec = pl.BlockSpec((None, k, bn), lambda j, i: (layer, 0, j0 + j))
    in_specs = [pl.BlockSpec((bm, k), lambda j, i: (i, 0)), w_spec]
    operands = [x, w]
    if col_shift:
        per = bn // col_shift
        in_specs.append(pl.BlockSpec((None, col_shift, k), lambda j, i: (layer, (j0 + j + 1) * per, 0)))
        operands.append(w)
    return pl.pallas_call(
        functools.partial(_mm_kernel, row_shift=col_shift, transposed=transposed),
        grid=(pl.cdiv(ncols, bn), m // bm),
        in_specs=in_specs,
        out_specs=pl.BlockSpec((bm, bn), lambda j, i: (i, j)),
        out_shape=jax.ShapeDtypeStruct((m, ncols), out_dtype),
        scratch_shapes=[pltpu.VMEM((k, bn), BF16)],
        compiler_params=pltpu.CompilerParams(
            dimension_semantics=("arbitrary", "arbitrary"), vmem_limit_bytes=VMEM_LIMIT),
        name="matmul",
    )(*operands)


def _norm_mod(x, gain_scale, shift):
    return x * lax.rsqrt(jnp.mean(x * x, axis=-1, keepdims=True) + EPS) * gain_scale + shift


def _row_vectors(modb_ref, modc_ref, idx, gain_ref=None):
    def vec(ref):
        row = ref[0, idx:idx + 1, :]
        if gain_ref is not None:
            row = gain_ref[...] * (1.0 + row)
        return jnp.broadcast_to(row, (BF16_ROWS, row.shape[1]))

    lat, ctx = vec(modb_ref), vec(modc_ref)
    return lambda is_ctx: lat if is_ctx is None else jnp.where(is_ctx, ctx, lat)


def _head_rmsnorm(x, width):
    parts = []
    for c0 in range(0, x.shape[1], width):
        xh = x[:, c0:c0 + width]
        parts.append(xh * lax.rsqrt(jnp.mean(xh * xh, axis=-1, keepdims=True) + EPS))
    return parts[0] if len(parts) == 1 else jnp.concatenate(parts, axis=1)


def _sigmoid(x):
    return 1.0 / (1.0 + jnp.exp(-x))


def _chunk_rows(r0, rc, tiles_per_batch, bm):
    rows = r0 + lax.broadcasted_iota(jnp.int32, (rc, 1), 0)
    return rows, (pl.program_id(0) % tiles_per_batch) * bm + rows


def _edge_rows(r0, rc, bm, tiles_per_batch, n_ctx, n_seq, gate_ref, hprev_ref, hnext_ref):
    hr = hprev_ref.shape[0]
    pos0 = (pl.program_id(0) % tiles_per_batch) * bm
    if r0 == 0:
        before = hprev_ref[...].astype(F32)[hr - 1:hr, :]
        before = jnp.where((pos0 == 0) | (pos0 == n_ctx), 0.0, before)
    else:
        before = gate_ref[r0 - hr:r0, :].astype(F32)[hr - 1:hr, :]
    if r0 + rc == bm:
        after = hnext_ref[...].astype(F32)[0:1, :]
        after = jnp.where((pos0 + bm == n_ctx) | (pos0 + bm == n_seq), 0.0, after)
    else:
        after = gate_ref[r0 + rc:r0 + rc + hr, :].astype(F32)[0:1, :]
    return before, after


def _splits_rows(r0, rc, bm, tiles_per_batch, n_ctx):
    return any(r0 <= n_ctx - k * bm <= r0 + rc for k in range(tiles_per_batch))


def _dwconv3(s_ref, g, before, after, cw_ref, pos, n_ctx):
    rc = g.shape[0]
    s_ref[SUBLANES:SUBLANES + rc, :] = g
    s_ref[SUBLANES - 1:SUBLANES, :] = before
    s_ref[SUBLANES + rc:SUBLANES + rc + 1, :] = after
    prev = s_ref[SUBLANES - 1:SUBLANES - 1 + rc, :]
    nxt = s_ref[SUBLANES + 1:SUBLANES + 1 + rc, :]
    if pos is not None:
        prev = jnp.where(pos == n_ctx, 0.0, prev)
        nxt = jnp.where(pos == n_ctx - 1, 0.0, nxt)
    return prev * cw_ref[0:1, :] + g * cw_ref[1:2, :] + nxt * cw_ref[2:3, :]


def _mod_specs(n_batch, tiles_per_batch, d):
    return [pl.BlockSpec((1, 6, d), lambda i, *_: (i // tiles_per_batch, 0, 0)),
            pl.BlockSpec((1, 6, d), lambda i, *_: (n_batch, 0, 0))]


def _halo_specs(bm, width, col, n_rows, hr):
    hb = bm // hr
    last = n_rows // hr - 1
    return [pl.BlockSpec((hr, width), lambda i, *k: (jnp.maximum(i * hb - 1, 0), col(*k))),
            pl.BlockSpec((hr, width), lambda i, *k: (jnp.minimum((i + 1) * hb, last), col(*k)))]


def _norm_kernel(x_ref, gain_ref, modb_ref, modc_ref, o_ref, *, n_ctx, tiles_per_batch):
    gain_scale = _row_vectors(modb_ref, modc_ref, 1, gain_ref)
    shift = _row_vectors(modb_ref, modc_ref, 0)

    def rows(rs, is_ctx):
        o_ref[rs, :] = _norm_mod(x_ref[rs, :], gain_scale(is_ctx), shift(is_ctx)).astype(BF16)

    _for_row_groups(x_ref.shape[0], n_ctx, tiles_per_batch, rows)


def _first_norm(x, gain, mod, *, n_ctx, n_seq, bm=ROW_TILE):
    m, d = x.shape
    tpb = n_seq // bm
    return pl.pallas_call(
        functools.partial(_norm_kernel, n_ctx=n_ctx, tiles_per_batch=tpb),
        grid=(m // bm,),
        in_specs=[pl.BlockSpec((bm, d), lambda i: (i, 0)), pl.BlockSpec((1, d), lambda i: (0, 0))]
        + _mod_specs(mod.shape[0] - 1, tpb, d),
        out_specs=pl.BlockSpec((bm, d), lambda i: (i, 0)),
        out_shape=jax.ShapeDtypeStruct((m, d), BF16),
        compiler_params=pltpu.CompilerParams(vmem_limit_bytes=VMEM_LIMIT),
        name="first_norm",
    )(x, gain, mod, mod)


def _residual_norm(xo_ref, hn_ref, x_ref, modb_ref, modc_ref, gain_ref, nmodb_ref, nmodc_ref, *,
                   n_ctx, tiles_per_batch, gate_idx, shift_idx, inline=False):
    gate = _row_vectors(modb_ref, modc_ref, gate_idx)
    gain_scale = _row_vectors(nmodb_ref, nmodc_ref, shift_idx + 1, gain_ref)
    shift = _row_vectors(nmodb_ref, nmodc_ref, shift_idx)

    def rows(rs, is_ctx):
        x_new = x_ref[rs, :] + gate(is_ctx) * xo_ref[rs, :]
        xo_ref[rs, :] = x_new
        hn_ref[rs, :] = _norm_mod(x_new, gain_scale(is_ctx), shift(is_ctx)).astype(BF16)

    _for_row_groups(x_ref.shape[0], n_ctx, tiles_per_batch, rows, inline)


def _for_row_groups(bm, n_ctx, tiles_per_batch, fn, inline=False):
    rg = BF16_ROWS
    pos0 = (pl.program_id(0) % tiles_per_batch) * bm
    n_lead = -(-min(n_ctx, bm) // rg)
    if inline:
        for g in range(bm // rg):
            is_ctx = (pos0 + g * rg + lax.broadcasted_iota(jnp.int32, (rg, 1), 0) < n_ctx) if g < n_lead else None
            fn(slice(g * rg, (g + 1) * rg), is_ctx)
        return

    def group(with_ctx):
        def body(g, carry):
            r0 = pl.multiple_of(g * rg, rg)
            is_ctx = (pos0 + r0 + lax.broadcasted_iota(jnp.int32, (rg, 1), 0) < n_ctx) if with_ctx else None
            fn(pl.ds(r0, rg), is_ctx)
            return carry
        return body

    lax.fori_loop(0, n_lead, group(True), 0, unroll=ROW_GROUP_UNROLL)
    lax.fori_loop(n_lead, bm // rg, group(False), 0, unroll=ROW_GROUP_UNROLL)


def _out_kernel(*refs, n_in, n_ctx, tiles_per_batch):
    a_refs = refs[:n_in]
    w_ref, x_ref, modb_ref, modc_ref, gain_ref, xo_ref, hn_ref, wb_ref = refs[n_in:]

    @pl.when(pl.program_id(0) == 0)
    def _():
        for r0 in range(0, w_ref.shape[0], ROW_CHUNK):
            r1 = min(r0 + ROW_CHUNK, w_ref.shape[0])
            wb_ref[r0:r1, :] = w_ref[r0:r1, :].astype(BF16)

    k0 = 0
    for n_done, a_ref in enumerate(a_refs):
        part = jnp.dot(a_ref[...], wb_ref[k0:k0 + a_ref.shape[1], :], preferred_element_type=F32)
        xo_ref[...] = part if n_done == 0 else xo_ref[...] + part
        k0 += a_ref.shape[1]
    _residual_norm(xo_ref, hn_ref, x_ref, modb_ref, modc_ref, gain_ref, modb_ref, modc_ref,
                   n_ctx=n_ctx, tiles_per_batch=tiles_per_batch, gate_idx=2, shift_idx=3, inline=True)


def _out_proj(acts, w, layer, x, mod, gain, *, n_ctx, n_seq, bm=ROW_TILE // 2):
    m, d = x.shape
    kk = w.shape[1]
    tpb = n_seq // bm
    rows = lambda width: pl.BlockSpec((bm, width), lambda i: (i, 0))
    return pl.pallas_call(
        functools.partial(_out_kernel, n_in=len(acts), n_ctx=n_ctx, tiles_per_batch=tpb),
        grid=(m // bm,),
        in_specs=[rows(a.shape[1]) for a in acts]
        + [pl.BlockSpec((None, kk, d), lambda i: (layer, 0, 0), pipeline_mode=pl.Buffered(1)), rows(d)]
        + _mod_specs(mod.shape[0] - 1, tpb, d) + [pl.BlockSpec((1, d), lambda i: (0, 0))],
        out_specs=[rows(d), rows(d)],
        out_shape=[jax.ShapeDtypeStruct((m, d), F32), jax.ShapeDtypeStruct((m, d), BF16)],
        scratch_shapes=[pltpu.VMEM((kk, d), BF16)],
        compiler_params=pltpu.CompilerParams(
            dimension_semantics=("arbitrary",), vmem_limit_bytes=VMEM_LIMIT),
        name="out_proj",
    )(*acts, w, x, mod, mod, gain)


def _up_kernel(x_ref, xp_ref, xn_ref, wg_ref, wv_ref, cw_ref, cb_ref, wd_ref, o_ref, wdb_ref, wgb_ref,
               wvb_ref, xe_ref, s_ref, v_ref, *, n_ctx, n_seq, tiles_per_batch):
    i = pl.program_id(1)
    bm, rc, hr = x_ref.shape[0], ROW_CHUNK, xp_ref.shape[0]
    wdb_ref[...] = wd_ref[...].astype(BF16)

    @pl.when(i == 0)
    def _():
        wgb_ref[...] = wg_ref[...].astype(BF16)
        wvb_ref[...] = wv_ref[...].astype(BF16)

    xe_ref[0:hr, :] = xp_ref[...]
    xe_ref[hr:hr + bm, :] = x_ref[...]
    xe_ref[hr + bm:, :] = xn_ref[...]
    s_ref[...] = jnp.dot(xe_ref[...], wgb_ref[...], preferred_element_type=F32)
    v_ref[...] = jnp.dot(x_ref[...], wvb_ref[...], preferred_element_type=F32)
    pos0 = (i % tiles_per_batch) * bm
    s_ref[hr - 1:hr, :] = jnp.where((pos0 == 0) | (pos0 == n_ctx), 0.0, s_ref[hr - 1:hr, :])
    s_ref[hr + bm:hr + bm + 1, :] = jnp.where(
        (pos0 + bm == n_ctx) | (pos0 + bm == n_seq), 0.0, s_ref[hr + bm:hr + bm + 1, :])
    for r0 in range(0, bm, rc):
        rs = slice(r0, r0 + rc)
        prev = s_ref[hr - 1 + r0:hr - 1 + r0 + rc, :]
        nxt = s_ref[hr + 1 + r0:hr + 1 + r0 + rc, :]
        if _splits_rows(r0, rc, bm, tiles_per_batch, n_ctx):
            pos = pos0 + r0 + lax.broadcasted_iota(jnp.int32, (rc, 1), 0)
            prev = jnp.where(pos == n_ctx, 0.0, prev)
            nxt = jnp.where(pos == n_ctx - 1, 0.0, nxt)
        z = (prev * cw_ref[0:1, :] + s_ref[hr + r0:hr + r0 + rc, :] * cw_ref[1:2, :]
             + nxt * cw_ref[2:3, :] + cb_ref[...])
        o_ref[rs, :] = (z * _sigmoid(z) * v_ref[rs, :]).astype(BF16)


def _ffn_up(x, w, w_down, layer, conv_w, conv_b, *, n_ctx, n_seq, bn=512, bm=ROW_TILE):
    m, k = x.shape
    f = w.shape[2] // 2
    nj, ni = f // bn, m // bm
    slab = f // (nj * ni)
    assert slab * nj * ni == f and slab % BF16_ROWS == 0
    d_out = w_down.shape[2]
    tpb = n_seq // bm
    hb = bm // BF16_ROWS
    last = m // BF16_ROWS - 1
    return pl.pallas_call(
        functools.partial(_up_kernel, n_ctx=n_ctx, n_seq=n_seq, tiles_per_batch=tpb),
        grid=(nj, ni),
        in_specs=[pl.BlockSpec((bm, k), lambda j, i: (i, 0)),
                  pl.BlockSpec((BF16_ROWS, k), lambda j, i: (jnp.maximum(i * hb - 1, 0), 0)),
                  pl.BlockSpec((BF16_ROWS, k), lambda j, i: (jnp.minimum((i + 1) * hb, last), 0)),
                  pl.BlockSpec((None, k, bn), lambda j, i: (layer, 0, j)),
                  pl.BlockSpec((None, k, bn), lambda j, i: (layer, 0, j + nj)),
                  pl.BlockSpec((3, bn), lambda j, i: (0, j)), pl.BlockSpec((1, bn), lambda j, i: (0, j)),
                  pl.BlockSpec((None, slab, d_out), lambda j, i: (layer, j * ni + i, 0))],
        out_specs=[pl.BlockSpec((bm, bn), lambda j, i: (i, j)),
                   pl.BlockSpec((slab, d_out), lambda j, i: (j * ni + i, 0))],
        out_shape=[jax.ShapeDtypeStruct((m, f), BF16), jax.ShapeDtypeStruct((f, d_out), BF16)],
        scratch_shapes=[pltpu.VMEM((k, bn), BF16), pltpu.VMEM((k, bn), BF16),
                        pltpu.VMEM((bm + 2 * BF16_ROWS, k), BF16),
                        pltpu.VMEM((bm + 2 * BF16_ROWS, bn), F32), pltpu.VMEM((bm, bn), F32)],
        compiler_params=pltpu.CompilerParams(
            dimension_semantics=("arbitrary", "arbitrary"), vmem_limit_bytes=VMEM_LIMIT),
        name="ffn_up",
    )(x, x, x, w, w, conv_w, conv_b, w_down)


def _down_kernel(a_ref, w_ref, x_ref, modb_ref, modc_ref, gain_ref, nmodb_ref, nmodc_ref, xo_ref, hn_ref, *,
                 n_ctx, tiles_per_batch):
    part = jnp.dot(a_ref[...], w_ref[...], preferred_element_type=F32)

    @pl.when(pl.program_id(1) == 0)
    def _():
        xo_ref[...] = part

    @pl.when(pl.program_id(1) > 0)
    def _():
        xo_ref[...] += part

    @pl.when(pl.program_id(1) == pl.num_programs(1) - 1)
    def _():
        _residual_norm(xo_ref, hn_ref, x_ref, modb_ref, modc_ref, gain_ref, nmodb_ref, nmodc_ref,
                       n_ctx=n_ctx, tiles_per_batch=tiles_per_batch, gate_idx=5, shift_idx=0)


def _down_proj(a, w, x, mod, gain, next_mod, *, n_ctx, n_seq, bm=ROW_TILE // 2, bk=1408):
    m, f = a.shape
    d = w.shape[1]
    tpb = n_seq // bm
    row = lambda i, k: (i, 0)
    return pl.pallas_call(
        functools.partial(_down_kernel, n_ctx=n_ctx, tiles_per_batch=tpb),
        grid=(m // bm, f // bk),
        in_specs=[pl.BlockSpec((bm, bk), lambda i, k: (i, k)), pl.BlockSpec((bk, d), lambda i, k: (k, 0)),
                  pl.BlockSpec((bm, d), row)]
        + _mod_specs(mod.shape[0] - 1, tpb, d) + [pl.BlockSpec((1, d), lambda i, k: (0, 0))]
        + _mod_specs(mod.shape[0] - 1, tpb, d),
        out_specs=[pl.BlockSpec((bm, d), row), pl.BlockSpec((bm, d), row)],
        out_shape=[jax.ShapeDtypeStruct((m, d), F32), jax.ShapeDtypeStruct((m, d), BF16)],
        compiler_params=pltpu.CompilerParams(
            dimension_semantics=("arbitrary", "arbitrary"), vmem_limit_bytes=VMEM_LIMIT),
        name="down_proj",
    )(a, w, x, mod, mod, gain, next_mod, next_mod)


def _even_finish_kernel(of_ref, ob_ref, r_ref, sx_ref, sb_ref, sg_ref, sxp_ref, sxn_ref, sgp_ref, sgn_ref,
                        gain_ref, cw_ref, o_ref, s_ref, *, n_ctx, n_seq, tiles_per_batch):
    bm = of_ref.shape[0]
    nv = of_ref.shape[1]
    r = r_ref[...]
    o_ref[:, :nv] = (_head_rmsnorm(of_ref[...] + ob_ref[...], GLA_DV) * gain_ref[...]
                     * (r * _sigmoid(r))).astype(BF16)

    _, pos = _chunk_rows(0, bm, tiles_per_batch, bm)
    pos0 = (pl.program_id(0) % tiles_per_batch) * bm
    hr = sxp_ref.shape[0]
    f32 = lambda ref: ref[...].astype(F32)
    before = (f32(sxp_ref) * f32(sgp_ref))[hr - 1:hr, :]
    before = jnp.where((pos0 == 0) | (pos0 == n_ctx), 0.0, before)
    after = (f32(sxn_ref) * f32(sgn_ref))[0:1, :]
    after = jnp.where((pos0 + bm == n_ctx) | (pos0 + bm == n_seq), 0.0, after)
    split = pos if _splits_rows(0, bm, bm, tiles_per_batch, n_ctx) else None
    conv = _dwconv3(s_ref, f32(sg_ref) * f32(sx_ref), before, after, cw_ref, split, n_ctx)
    o_ref[:, nv:] = (f32(sb_ref) * conv).astype(BF16)


def _even_finish(o_f, o_b, pa, pb, gain, conv_w, *, n_ctx, n_seq, bm=ROW_CHUNK):
    m, nv = o_f.shape
    sc = conv_w.shape[1]
    tpb = n_seq // bm
    col = lambda j: (lambda i: (i, j))
    const = lambda i: (0, 0)
    return pl.pallas_call(
        functools.partial(_even_finish_kernel, n_ctx=n_ctx, n_seq=n_seq, tiles_per_batch=tpb),
        grid=(m // bm,),
        in_specs=[pl.BlockSpec((bm, nv), col(0)), pl.BlockSpec((bm, nv), col(0)),
                  pl.BlockSpec((bm, nv), col(2)),
                  pl.BlockSpec((bm, sc), col(0)), pl.BlockSpec((bm, sc), col(1)), pl.BlockSpec((bm, sc), col(2))]
        + _halo_specs(bm, sc, lambda: 0, m, BF16_ROWS) + _halo_specs(bm, sc, lambda: 2, m, BF16_ROWS)
        + [pl.BlockSpec((1, nv), const), pl.BlockSpec((3, sc), const)],
        out_specs=pl.BlockSpec((bm, nv + sc), col(0)),
        out_shape=jax.ShapeDtypeStruct((m, nv + sc), BF16),
        scratch_shapes=[pltpu.VMEM((bm + 2 * SUBLANES, sc), F32)],
        compiler_params=pltpu.CompilerParams(vmem_limit_bytes=VMEM_LIMIT),
        name="even_finish",
    )(o_f, o_b, pa, pb, pb, pb, pb, pb, pb, pb, gain, conv_w)


def _mlstm_prep_kernel(x_ref, xp_ref, xn_ref, cw_ref, scale_ref, o_ref, s_ref, *, n_ctx, n_seq,
                       tiles_per_batch):
    bm = x_ref.shape[0]
    _, pos = _chunk_rows(0, bm, tiles_per_batch, bm)
    before, after = _edge_rows(0, bm, bm, tiles_per_batch, n_ctx, n_seq, x_ref, xp_ref, xn_ref)
    split = pos if _splits_rows(0, bm, bm, tiles_per_batch, n_ctx) else None
    z = _dwconv3(s_ref, x_ref[...].astype(F32), before, after, cw_ref, split, n_ctx)
    o_ref[...] = (z * _sigmoid(z) * scale_ref[...]).astype(BF16)


def _mlstm_prep(p, conv_w, scale, *, col, n_ctx, n_seq, bm=ROW_CHUNK):
    m = p.shape[0]
    width = conv_w.shape[1]
    tpb = n_seq // bm
    const = lambda i: (0, 0)
    return pl.pallas_call(
        functools.partial(_mlstm_prep_kernel, n_ctx=n_ctx, n_seq=n_seq, tiles_per_batch=tpb),
        grid=(m // bm,),
        in_specs=[pl.BlockSpec((bm, width), lambda i: (i, col))]
        + _halo_specs(bm, width, lambda: col, m, BF16_ROWS)
        + [pl.BlockSpec((3, width), const), pl.BlockSpec((1, width), const)],
        out_specs=pl.BlockSpec((bm, width), lambda i: (i, 0)),
        out_shape=jax.ShapeDtypeStruct((m, width), BF16),
        scratch_shapes=[pltpu.VMEM((bm + 2 * SUBLANES, width), F32)],
        compiler_params=pltpu.CompilerParams(vmem_limit_bytes=VMEM_LIMIT),
        name="mlstm_prep",
    )(p, p, p, conv_w, scale)


def _mlstm_finish_kernel(hf_ref, hb_ref, mo_ref, gain_ref, o_ref):
    o_ref[...] = (_head_rmsnorm(hf_ref[...] + hb_ref[...], ML_DV) * gain_ref[...]
                  * _sigmoid(mo_ref[...].astype(F32))).astype(BF16)


def _mlstm_finish(h_f, h_b, p, gain, *, col, bm=ROW_TILE // 2):
    m, nv = h_f.shape
    return pl.pallas_call(
        _mlstm_finish_kernel,
        grid=(m // bm,),
        in_specs=[pl.BlockSpec((bm, nv), lambda i: (i, 0)), pl.BlockSpec((bm, nv), lambda i: (i, 0)),
                  pl.BlockSpec((bm, nv), lambda i: (i, col)), pl.BlockSpec((1, nv), lambda i: (0, 0))],
        out_specs=pl.BlockSpec((bm, nv), lambda i: (i, 0)),
        out_shape=jax.ShapeDtypeStruct((m, nv), BF16),
        compiler_params=pltpu.CompilerParams(vmem_limit_bytes=VMEM_LIMIT),
        name="mlstm_finish",
    )(h_f, h_b, p, gain)


def _qk_prep(x, cos, sin, gain):
    lane = lax.broadcasted_iota(jnp.int32, x.shape, 1)
    lo = lane < DA_DQK
    x2 = x * x
    ss_lo = jnp.sum(jnp.where(lo, x2, 0.0), axis=-1, keepdims=True)
    ss_hi = jnp.sum(jnp.where(lo, 0.0, x2), axis=-1, keepdims=True)
    y = x * lax.rsqrt(jnp.where(lo, ss_lo, ss_hi) * (1.0 / DA_DQK) + EPS) * gain
    swapped = jnp.where((lane & ROPE_PAIRS) == 0, pltpu.roll(y, LANES - ROPE_PAIRS, 1),
                        pltpu.roll(y, ROPE_PAIRS, 1))
    return y * cos + swapped * sin


def _attn_kernel(lam_ref, q_ref, k_ref, v_ref, cos_ref, sin_ref, qg_ref, kg_ref, og_ref, o_ref,
                 kb_ref, vb_ref, *, n_ctx, chunk, prep_rows, out_scale):
    qi = pl.program_id(2)
    lam = lam_ref[0]
    n = k_ref.shape[1]
    bq = q_ref.shape[1]
    heads = q_ref.shape[2] // LANES

    @pl.when(qi == 0)
    def _():
        for r0 in range(0, n, prep_rows):
            rs = slice(r0, r0 + prep_rows)
            for h in range(heads):
                hs = slice(h * LANES, (h + 1) * LANES)
                kb_ref[h, rs, :] = _qk_prep(k_ref[0, rs, hs].astype(F32), cos_ref[rs, :], sin_ref[rs, :],
                                            kg_ref[...]).astype(BF16)
                vb_ref[h, rs, :LANES] = v_ref[0, rs, hs].astype(BF16)
                vb_ref[h, rs, LANES:] = jnp.ones((prep_rows, LANES), BF16)

    def attend(r0, r1, nk):
        nr = r1 - r0
        rows = pl.ds(pl.multiple_of(qi * bq, BF16_ROWS) + r0, nr)
        cos, sin = cos_ref[rows, :], sin_ref[rows, :]
        lane = lax.broadcasted_iota(jnp.int32, (nr, LANES), 1)
        q2, m, acc = [], [], []
        for h in range(heads):
            q = _qk_prep(q_ref[0, r0:r1, h * LANES:(h + 1) * LANES].astype(F32), cos, sin, qg_ref[...])
            q = (q * (DA_DQK ** -0.5 * LOG2E)).astype(BF16)
            zero = jnp.zeros_like(q)
            q2.append(jnp.concatenate([jnp.where(lane < DA_DQK, q, zero),
                                       jnp.where(lane >= DA_DQK, q, zero)], axis=0))
            m.append(jnp.full((2 * nr, 1), -jnp.inf, F32))
            acc.append(jnp.zeros((2 * nr, 2 * LANES), F32))
        for c0 in range(0, nk, chunk):
            c1 = min(c0 + chunk, nk)
            for h in range(heads):
                s = lax.dot_general(q2[h], kb_ref[h, c0:c1, :], NT, preferred_element_type=F32)
                m_new = jnp.maximum(m[h], jnp.max(s, axis=-1, keepdims=True))
                p = jnp.exp2(s - m_new)
                acc[h] = jnp.exp2(m[h] - m_new) * acc[h] + jnp.dot(
                    p.astype(BF16), vb_ref[h, c0:c1, :], preferred_element_type=F32)
                m[h] = m_new
        for h in range(heads):
            o = acc[h][:, :LANES] / acc[h][:, LANES:]
            o = o[:nr] - lam * o[nr:]
            o = o * lax.rsqrt(jnp.mean(o * o, axis=-1, keepdims=True) + EPS) * og_ref[...] * out_scale
            o_ref[0, r0:r1, h * LANES:(h + 1) * LANES] = o.astype(BF16)

    last = pl.num_programs(2) - 1
    last_rows = n - (n // bq) * bq if n % bq else bq

    @pl.when(qi == 0)
    def _():
        attend(0, n_ctx, n_ctx)
        for r0 in range(n_ctx, bq, ATTN_SUB):
            attend(r0, r0 + ATTN_SUB, n)

    @pl.when((qi > 0) & (qi < last))
    def _():
        for r0 in range(0, bq, ATTN_SUB):
            attend(r0, r0 + ATTN_SUB, n)

    @pl.when((qi > 0) & (qi == last))
    def _():
        for r0 in range(0, last_rows, ATTN_SUB):
            attend(r0, r0 + ATTN_SUB, n)


def _diff_attention(p, lam, cos, sin, q_gain, k_gain, out_gain, *, n_ctx, out_scale, bq=ATTN_BQ,
                    chunk=ATTN_CHUNK, prep_rows=ATTN_PREP, heads=ATTN_HEADS):
    b, n, _ = p.shape
    assert n_ctx % ATTN_SUB == 0 and bq % ATTN_SUB == 0 and n % ATTN_SUB == 0 and n_ctx <= bq
    hg = DA_HEADS // heads
    hd = DA_HEADS * LANES
    hw = heads * LANES
    const = lambda bi, hi, qi: (0, 0)
    return pl.pallas_call(
        functools.partial(_attn_kernel, n_ctx=n_ctx, chunk=chunk, prep_rows=prep_rows, out_scale=out_scale),
        grid=(b, hg, pl.cdiv(n, bq)),
        in_specs=[pl.BlockSpec(memory_space=pltpu.SMEM),
                  pl.BlockSpec((1, bq, hw), lambda bi, hi, qi: (bi, qi, hi)),
                  pl.BlockSpec((1, n, hw), lambda bi, hi, qi: (bi, 0, hg + hi)),
                  pl.BlockSpec((1, n, hw), lambda bi, hi, qi: (bi, 0, 2 * hg + hi)),
                  pl.BlockSpec((n, LANES), const), pl.BlockSpec((n, LANES), const),
                  pl.BlockSpec((1, LANES), const), pl.BlockSpec((1, LANES), const),
                  pl.BlockSpec((1, LANES), const)],
        out_specs=pl.BlockSpec((1, bq, hw), lambda bi, hi, qi: (bi, qi, hi)),
        out_shape=jax.ShapeDtypeStruct((b, n, hd), BF16),
        scratch_shapes=[pltpu.VMEM((heads, n, LANES), BF16), pltpu.VMEM((heads, n, 2 * LANES), BF16)],
        compiler_params=pltpu.CompilerParams(
            dimension_semantics=("arbitrary", "arbitrary", "arbitrary"), vmem_limit_bytes=VMEM_LIMIT),
        name="diff_attention",
    )(lam, p, p, p, cos, sin, q_gain, k_gain, out_gain)


def _scan_chunk(t, n_ctx_chunks, n_chunks, reverse):
    if not reverse:
        return t
    return jnp.where(t < n_ctx_chunks, n_ctx_chunks - 1 - t, n_chunks - 1 - (t - n_ctx_chunks))


def _gla_constants(L, reverse):
    nlev = int(math.log2(L))
    idx = np.arange(L)
    i, t = idx[:, None], idx[None, :]
    if reverse:
        i, t = L - 1 - i, L - 1 - t
    rs = [(t <= i), (t > i)]
    am = [(i == t)]
    for lev in range(nlev):
        m = L >> (lev + 1)
        blk_i, blk_t = i // (2 * m), t // (2 * m)
        mid = blk_i * 2 * m + m
        q_role = i >= mid
        rs.append(np.where(q_role, (t >= mid) & (t <= i), (t > i) & (t < mid)) & (blk_i == blk_t))
        am.append((blk_i == blk_t) & q_role & (t < mid))
    return (np.stack(rs).astype(np.float32).reshape((nlev + 2) * L, L),
            np.stack(am).astype(np.float32))


def _gla_kernel(*refs):
    of_ref, ob_ref, st_ref = refs[16:]

    @pl.when(pl.program_id(1) == 0)
    def _():
        st_ref[...] = jnp.zeros_like(st_ref)

    for d, o_ref in enumerate((of_ref, ob_ref)):
        _gla_direction(*refs[8 * d:8 * d + 8], o_ref, st_ref, d * GLA_HEADS, bool(d))


def _gla_direction(q_ref, k_ref, v_ref, glr_ref, w2_ref, gb_ref, rsum_ref, amask_ref, o_ref, st_ref, h0,
                   reverse):
    L = q_ref.shape[1]
    nlev = amask_ref.shape[0] - 1
    n_coarse = nlev + 2 - rsum_ref.shape[0] // L

    z = jnp.dot(glr_ref[0].astype(BF16), w2_ref[...], preferred_element_type=F32) + gb_ref[...]
    g = (jnp.minimum(z, 0.0) - jnp.log(1.0 + jnp.exp(-jnp.abs(z)))) * (1.0 / GLA_TAU)
    g1 = g.astype(BF16)
    g2 = (g - g1.astype(F32)).astype(BF16)
    rsum = rsum_ref[...]
    e_all = (jnp.dot(rsum, g1, preferred_element_type=F32)
             + jnp.dot(rsum, g2, preferred_element_type=F32))
    b_tot = jnp.sum(g, axis=0, keepdims=True)

    def level_decay(lev, ck):
        if lev >= n_coarse:
            return jnp.exp(e_all[(2 + lev - n_coarse) * L:(3 + lev - n_coarse) * L, ck])
        m = L >> (lev + 1)
        b3 = e_all[0:L, ck].reshape(L // (2 * m), 2 * m, GLA_DK)
        pivot = m if reverse else m - 1
        return jnp.exp(-jnp.abs(b3 - b3[:, pivot:pivot + 1, :])).reshape(L, GLA_DK)

    for h in range(GLA_HEADS):
        ck = slice(h * GLA_DK, (h + 1) * GLA_DK)
        cv = slice(h * GLA_DV, (h + 1) * GLA_DV)
        q = q_ref[0, :, ck] * (GLA_DK ** -0.5)
        k = k_ref[0, :, ck]
        v = v_ref[0, :, cv].astype(BF16)
        st = st_ref[h0 + h]
        a = amask_ref[0] * lax.dot_general(q.astype(BF16), k.astype(BF16), NT, preferred_element_type=F32)
        for lev in range(nlev):
            e = level_decay(lev, ck)
            a = a + amask_ref[1 + lev] * lax.dot_general(
                (q * e).astype(BF16), (k * e).astype(BF16), NT, preferred_element_type=F32)
        qe = (q * jnp.exp(e_all[0:L, ck])).astype(BF16)
        o = lax.dot_general(qe, st.astype(BF16), NT, preferred_element_type=F32)
        o = o + jnp.dot(a.astype(BF16), v, preferred_element_type=F32)
        o_ref[0, :, cv] = o
        kd = (k * jnp.exp(e_all[L:2 * L, ck])).astype(BF16)
        st_ref[h0 + h] = jnp.exp(b_tot[:, ck]) * st + jnp.dot(
            v.T, kd, preferred_element_type=F32)


def _gla_scan(p, glr, w2s, gbs, *, n_ctx, L=GLA_CHUNK):
    b, n, _ = p.shape
    nc, ncc = n // L, n_ctx // L
    hk, hv = GLA_HEADS * GLA_DK, GLA_HEADS * GLA_DV
    const2 = lambda bi, t: (0, 0)
    const3 = lambda bi, t: (0, 0, 0)
    in_specs, operands, out_specs = [], [], []
    for d in range(2):
        rsum, amask = _gla_constants(L, bool(d))
        n_coarse = sum(2 * (L >> (lev + 1)) >= SUBLANES for lev in range(amask.shape[0] - 1))
        rsum = rsum.reshape(-1, L, L)
        rsum = np.concatenate([rsum[:2], rsum[2 + n_coarse:]]).reshape(-1, L)
        chunk = functools.partial(_scan_chunk, n_ctx_chunks=ncc, n_chunks=nc, reverse=bool(d))
        cols = lambda width, start, chunk=chunk: pl.BlockSpec(
            (1, L, width), lambda bi, t: (bi, chunk(t), start // width))
        in_specs += [cols(hk, 0), cols(hk, hk), cols(hv, 2 * hk), cols(LANES, 0),
                     pl.BlockSpec(w2s[d].shape, const2), pl.BlockSpec(gbs[d].shape, const2),
                     pl.BlockSpec(rsum.shape, const2), pl.BlockSpec(amask.shape, const3)]
        operands += [p, p, p, glr, w2s[d], gbs[d], jnp.asarray(rsum, BF16), jnp.asarray(amask)]
        out_specs.append(cols(hv, 0))
    return pl.pallas_call(
        _gla_kernel,
        grid=(b, nc),
        in_specs=in_specs,
        out_specs=out_specs,
        out_shape=[jax.ShapeDtypeStruct((b, n, hv), F32)] * 2,
        scratch_shapes=[pltpu.VMEM((2 * GLA_HEADS, GLA_DV, GLA_DK), F32)],
        compiler_params=pltpu.CompilerParams(
            dimension_semantics=("arbitrary", "arbitrary"), vmem_limit_bytes=VMEM_LIMIT),
        name="gla_scan",
    )(*operands)


def _log_sigmoid(x):
    return jnp.minimum(x, 0.0) - jnp.log(1.0 + jnp.exp(-jnp.abs(x)))


def _mlstm_kernel(*refs):
    tri_ref, of_ref, ob_ref, c_ref, m_ref = refs[10:]

    @pl.when(pl.program_id(0) == 0)
    def _():
        c_ref[...] = jnp.zeros_like(c_ref)
        m_ref[...] = jnp.zeros_like(m_ref)

    for bi in range(of_ref.shape[0]):
        for d, o_ref in enumerate((of_ref, ob_ref)):
            _mlstm_direction(*refs[5 * d:5 * d + 5], tri_ref[d], o_ref, c_ref, m_ref, bi,
                             (2 * bi + d) * ML_HEADS, bool(d))


def _mlstm_direction(q_ref, k_ref, v_ref, gc_ref, gr_ref, tri, o_ref, c_ref, m_ref, bi, h0, reverse):
    L = q_ref.shape[1]
    H = ML_HEADS
    ones = jnp.ones((L, LANES), BF16)
    gc = gc_ref[bi]
    gr = gr_ref[bi]
    ic_col, ic_row = gc[:, :H], gr[:H, :]
    b_col = jnp.dot(tri, _log_sigmoid(gc[:, H:]), precision=HI, preferred_element_type=F32)
    b_row = lax.dot_general(_log_sigmoid(gr[H:, :]), tri, NT, precision=HI, preferred_element_type=F32)
    last = 0 if reverse else L - 1
    causal = tri > 0.5

    for h in range(H):
        ck = slice(h * ML_DK, (h + 1) * ML_DK)
        cv = slice(h * ML_DV, (h + 1) * ML_DV)
        q = q_ref[bi, :, ck]
        k = k_ref[bi, :, ck]
        v = jnp.concatenate([v_ref[bi, :, cv].astype(BF16), ones], axis=1)
        c, m = c_ref[h0 + h], m_ref[h0 + h]
        bc, br = b_col[:, h:h + 1], b_row[h:h + 1, :]
        icc, icr = ic_col[:, h:h + 1], ic_row[h:h + 1, :]
        b_last = bc[last:last + 1, :]

        a = bc + m
        dmat = jnp.where(causal, bc - br + icr, -jnp.inf)
        m_t = jnp.maximum(a, jnp.max(dmat, axis=-1, keepdims=True))
        w_inter = jnp.exp(a - m_t)
        s = lax.dot_general(q, k, NT, preferred_element_type=F32) * jnp.exp(dmat - m_t)
        both = w_inter * jnp.dot(q, c.astype(BF16), preferred_element_type=F32) + jnp.dot(
            s.astype(BF16), v, preferred_element_type=F32)
        den = jnp.maximum(jnp.abs(both[:, ML_DV:]), jnp.exp(-m_t))
        o_ref[bi, :, cv] = both[:, :ML_DV] / jnp.concatenate([den] * (ML_DV // LANES), axis=1)

        gs_col = b_last - bc + icc
        gs_row = b_last - br + icr
        m_new = jnp.maximum(b_last + m, jnp.max(gs_row, axis=-1, keepdims=True))
        decay = jnp.exp(b_last + m - m_new)
        wk = jnp.exp(gs_col - m_new) * k.astype(F32)
        c_ref[h0 + h] = decay * c + jnp.dot(wk.astype(BF16).T, v, preferred_element_type=F32)
        m_ref[h0 + h] = m_new


def _mlstm_scan(qk, p, gates, *, v_col, n_ctx, L=ML_CHUNK):
    b, n, _ = qk.shape
    hk, hv, ng = ML_HEADS * ML_DK, ML_HEADS * ML_DV, 2 * ML_HEADS
    nc, ncc = n // L, n_ctx // L
    idx = np.arange(L)
    lower = idx[None, :] <= idx[:, None]
    tri = jnp.asarray(np.stack([lower, lower.T]).astype(np.float32))
    in_specs, operands, out_specs = [], [], []
    for d in range(2):
        chunk = functools.partial(_scan_chunk, n_ctx_chunks=ncc, n_chunks=nc, reverse=bool(d))
        col = lambda j, chunk=chunk: (lambda t: (0, chunk(t), j))
        g_dir = gates[..., d * ng:(d + 1) * ng]
        in_specs += [pl.BlockSpec((b, L, hk), col(0)), pl.BlockSpec((b, L, hk), col(1)),
                     pl.BlockSpec((b, L, hv), col(v_col)), pl.BlockSpec((b, L, ng), col(0)),
                     pl.BlockSpec((b, ng, L), lambda t, chunk=chunk: (0, 0, chunk(t)))]
        operands += [qk, qk, p, g_dir, jnp.swapaxes(g_dir, 1, 2)]
        out_specs.append(pl.BlockSpec((b, L, hv), col(0)))
    return pl.pallas_call(
        _mlstm_kernel,
        grid=(nc,),
        in_specs=in_specs + [pl.BlockSpec((2, L, L), lambda t: (0, 0, 0))],
        out_specs=out_specs,
        out_shape=[jax.ShapeDtypeStruct((b, n, hv), F32)] * 2,
        scratch_shapes=[pltpu.VMEM((b * 2 * ML_HEADS, ML_DK, ML_DV + LANES), F32),
                        pltpu.VMEM((b * 2 * ML_HEADS, 1, 1), F32)],
        compiler_params=pltpu.CompilerParams(
            dimension_semantics=("arbitrary",), vmem_limit_bytes=VMEM_LIMIT),
        name="mlstm_scan",
    )(*operands, tri)


def _silu(x):
    return x * jax.nn.sigmoid(x)


def _rope_tables(n_lat, n_ctx):
    rows = n_lat // GRID_W
    row = jnp.repeat(jnp.arange(rows, dtype=F32), GRID_W)
    col = jnp.tile(jnp.arange(GRID_W, dtype=F32), rows)
    inv = jnp.power(ROPE_BASE, -jnp.arange(ROPE_PAIRS, dtype=F32) / ROPE_PAIRS)
    ang_r, ang_c = row[:, None] * inv, col[:, None] * inv
    cos = jnp.concatenate([jnp.cos(ang_r)] * 2 + [jnp.cos(ang_c)] * 2, axis=-1)
    sin = jnp.concatenate([-jnp.sin(ang_r), jnp.sin(ang_r), -jnp.sin(ang_c), jnp.sin(ang_c)], axis=-1)
    pad = ((n_ctx, 0), (0, 0))
    cos, sin = jnp.pad(cos, pad, constant_values=1.0), jnp.pad(sin, pad)
    return jnp.tile(cos, (1, 2)), jnp.tile(sin, (1, 2))


def _even_mixer(hn, bsz, w_in, i, gate_w2, gate_b, gla_norm_g, sc_conv_w, seq):
    nq = GLA_HEADS * GLA_DK
    nv = GLA_HEADS * GLA_DV
    gate_col = 2 * nq + 2 * nv
    pa = _matmul(hn, w_in, i, bn=1024, ncols=gate_col, transposed=True)
    glr = _matmul(hn, w_in, i, bn=LANES, col0=gate_col, ncols=LANES, transposed=True)
    pb = _matmul(hn, w_in, i, bn=SC_WIDTH, col0=gate_col, ncols=3 * SC_WIDTH,
                 col_shift=2 * GLA_GATE_RANK, transposed=True, out_dtype=BF16)
    pa3 = pa.reshape(bsz, -1, pa.shape[1])
    glr3 = glr.reshape(bsz, -1, LANES)
    w2s = [jnp.zeros((LANES, nq), F32).at[d * GLA_GATE_RANK:(d + 1) * GLA_GATE_RANK].set(gate_w2[d])
           .astype(BF16) for d in range(2)]
    outs = [o.reshape(-1, nv) for o in
            _gla_scan(pa3, glr3, w2s, [gate_b[d][None, :] for d in range(2)], n_ctx=seq["n_ctx"])]
    gain = jnp.tile(gla_norm_g, GLA_HEADS)[None, :]
    return [_even_finish(outs[0], outs[1], pa, pb, gain, sc_conv_w, **seq)]


def _odd_mixer(hn, bsz, rope, layer, w_in, i, qn_g, kn_g, lam_p, subln_g, ml_conv_w, ml_gate_b,
               ml_norm_g, seq):
    lam_init = 0.8 - 0.6 * math.exp(-0.3 * layer)
    lam = (jnp.exp(jnp.sum(lam_p[0] * lam_p[1])) - jnp.exp(jnp.sum(lam_p[2] * lam_p[3])) + lam_init)
    na = DA_HEADS * 2 * DA_DQK
    nk = ML_HEADS * ML_DK
    nv = ML_HEADS * ML_DV
    n_main = 3 * na + 2 * nk + 2 * nv
    p = _matmul(hn, w_in, i, bn=1024, ncols=n_main, transposed=True, out_dtype=BF16)
    mg = _matmul(hn, w_in, i, bn=LANES, col0=n_main, ncols=LANES, transposed=True)[:, :4 * ML_HEADS]
    p3 = p.reshape(bsz, -1, n_main)
    n = p3.shape[1]

    cos, sin = rope
    tile2 = lambda g: jnp.tile(g, 2)[None, :]
    da = _diff_attention(p3, lam.reshape(1).astype(F32), cos, sin, tile2(qn_g), tile2(kn_g),
                         subln_g[None, :], n_ctx=seq["n_ctx"], out_scale=1.0 - lam_init)

    scale = jnp.concatenate([jnp.ones((nk,), F32), jnp.full((nk,), ML_DK ** -0.5, F32)])[None, :]
    hqk = _mlstm_prep(p, ml_conv_w, scale, col=3 * na // (2 * nk), **seq).reshape(bsz, n, 2 * nk)
    gates = (mg + ml_gate_b).reshape(bsz, n, 4 * ML_HEADS)
    outs = [o.reshape(-1, nv) for o in
            _mlstm_scan(hqk, p3, gates, v_col=(3 * na + 2 * nk) // nv, n_ctx=seq["n_ctx"])]
    m = _mlstm_finish(outs[0], outs[1], p, jnp.tile(ml_norm_g, ML_HEADS)[None, :],
                      col=(3 * na + 2 * nk + nv) // nv)
    return [da.reshape(-1, na), m]


def _modulation(c, c_ctx, ada_w, ada_b, layer):
    cc = jnp.concatenate([c, c_ctx[None, :]], axis=0)
    rows = cc.shape[0]
    act = jnp.pad(_silu(cc), ((0, BF16_ROWS - rows), (0, 0))).astype(BF16)
    mod = _matmul(act, ada_w, layer, bn=1024, bm=BF16_ROWS)[:rows] + ada_b[layer]
    return mod.reshape(rows, 6, -1)


def kernel(x, c, ctx, c_ctx, ada_w, ada_b, norm1_g, norm2_g, ev_w_in, ev_w_out, gla_gate_w2, gla_gate_b, gla_norm_g, sc_conv_w, od_w_in, od_w_out, da_qnorm_g, da_knorm_g, da_lambda, da_subln_g, ml_conv_w, ml_gate_b, ml_norm_g, ffn_w_up, ffn_conv_w, ffn_conv_b, ffn_w_down):
    bsz, n_lat, d = x.shape
    n_ctx = ctx.shape[1]
    n = n_ctx + n_lat
    depth = ada_w.shape[0]
    d_ff = ffn_w_down.shape[1]
    assert n % ROW_TILE == 0 and n_ctx <= ROW_CHUNK and d_ff % (4 * LANES) == 0
    rope = _rope_tables(n_lat, n_ctx)
    seq = dict(n_ctx=n_ctx, n_seq=n)

    ev_w_in = jnp.swapaxes(ev_w_in, 1, 2)
    od_w_in = jnp.swapaxes(od_w_in, 1, 2)
    mods = [_modulation(c, c_ctx, ada_w, ada_b, layer) for layer in range(depth)]
    xs = jnp.concatenate([ctx, x], axis=1).reshape(bsz * n, d)
    hn = _first_norm(xs, norm1_g[0][None, :], mods[0], **seq)
    for layer in range(depth):
        i = layer // 2
        if layer % 2 == 0:
            mix = _even_mixer(hn, bsz, ev_w_in, i, gla_gate_w2[i], gla_gate_b[i], gla_norm_g[i],
                              sc_conv_w[i], seq)
            w_out = ev_w_out
        else:
            mix = _odd_mixer(hn, bsz, rope, layer, od_w_in, i, da_qnorm_g[i], da_knorm_g[i],
                             da_lambda[i], da_subln_g[i], ml_conv_w[i], ml_gate_b[i], ml_norm_g[i], seq)
            w_out = od_w_out
        xs, hn = _out_proj(mix, w_out, i, xs, mods[layer], norm2_g[layer][None, :], **seq)
        act, w_down = _ffn_up(hn, ffn_w_up, ffn_w_down, layer, ffn_conv_w[layer],
                              ffn_conv_b[layer][None, :], **seq)
        nxt = min(layer + 1, depth - 1)
        xs, hn = _down_proj(act, w_down, xs,
                            mods[layer], norm1_g[nxt][None, :], mods[nxt], bk=d_ff // 2, **seq)
    return xs.reshape(bsz, n, d)[:, n_ctx:, :]
```

```python
import functools
import math

import jax
import jax.numpy as jnp
import numpy as np
from jax import lax
from jax.experimental import pallas as pl
from jax.experimental.pallas import tpu as pltpu

GRID_W = 64
EPS = 1e-6
GLA_HEADS, GLA_DK, GLA_DV, GLA_GATE_RANK, GLA_TAU = 4, 128, 256, 16, 16.0
SC_WIDTH = 1024
DA_HEADS, DA_DQK, DA_DV = 8, 64, 128
ROPE_BASE = 10000.0
ROPE_PAIRS = DA_DQK // 4
ML_HEADS, ML_DK, ML_DV = 4, 128, 256

GLA_CHUNK = 128
ML_CHUNK = 256
ATTN_BQ = 512
ATTN_SUB = 256
ATTN_CHUNK = 1024
ATTN_PREP = 544
ATTN_HEADS = 2
ROW_TILE = 1088
ROW_CHUNK = 272
LANES = 128
SUBLANES = 8
BF16_ROWS = 16
ROW_GROUP_UNROLL = 8
LOG2E = 1.4426950408889634
VMEM_LIMIT = 56 * 1024 * 1024

F32 = jnp.float32
BF16 = jnp.bfloat16
HI = lax.Precision.HIGHEST
NT = (((1,), (1,)), ((), ()))


def _mm_kernel(x_ref, w_ref, *rest, row_shift, transposed):
    o_ref, wb_ref = rest[-2:]

    @pl.when(pl.program_id(1) == 0)
    def _():
        if not transposed:
            wb_ref[...] = w_ref[...].astype(BF16)
            return
        bn = wb_ref.shape[1]
        piece = 2 * LANES
        for c0 in range(0, bn, piece):
            c1 = min(c0 + piece, bn)
            if row_shift and c1 == bn:
                rows = jnp.concatenate([w_ref[c0 + row_shift:bn, :], rest[0][...]], axis=0)
            else:
                rows = w_ref[c0 + row_shift:c1 + row_shift, :]
            wb_ref[:, c0:c1] = rows.T.astype(BF16)

    o_ref[...] = jnp.dot(x_ref[...], wb_ref[...], preferred_element_type=F32).astype(o_ref.dtype)


def _matmul(x, w, layer, *, bn, col0=0, ncols=None, col_shift=0, transposed=False, bm=ROW_TILE,
            out_dtype=F32):
    m, k = x.shape
    n_all = w.shape[1] if transposed else w.shape[2]
    ncols = n_all - col0 if ncols is None else ncols
    j0 = col0 // bn
    if transposed:
        w_spec = pl.BlockSpec((None, bn, k), lambda j, i: (layer, j0 + j, 0))
    else:
        w_spec = pl.BlockSpec((None, k, bn), lambda j, i: (layer, 0, j0 + j))
    in_specs = [pl.BlockSpec((bm, k), lambda j, i: (i, 0)), w_spec]
    operands = [x, w]
    if col_shift:
        per = bn // col_shift
        in_specs.append(pl.BlockSpec((None, col_shift, k), lambda j, i: (layer, (j0 + j + 1) * per, 0)))
        operands.append(w)
    return pl.pallas_call(
        functools.partial(_mm_kernel, row_shift=col_shift, transposed=transposed),
        grid=(pl.cdiv(ncols, bn), m // bm),
        in_specs=in_specs,
        out_specs=pl.BlockSpec((bm, bn), lambda j, i: (i, j)),
        out_shape=jax.ShapeDtypeStruct((m, ncols), out_dtype),
        scratch_shapes=[pltpu.VMEM((k, bn), BF16)],
        compiler_params=pltpu.CompilerParams(
            dimension_semantics=("arbitrary", "arbitrary"), vmem_limit_bytes=VMEM_LIMIT),
        name="matmul",
    )(*operands)


def _norm_mod(x, gain_scale, shift):
    return x * lax.rsqrt(jnp.mean(x * x, axis=-1, keepdims=True) + EPS) * gain_scale + shift


def _row_vectors(modb_ref, modc_ref, idx, gain_ref=None):
    def vec(ref):
        row = ref[0, idx:idx + 1, :]
        if gain_ref is not None:
            row = gain_ref[...] * (1.0 + row)
        return jnp.broadcast_to(row, (BF16_ROWS, row.shape[1]))

    lat, ctx = vec(modb_ref), vec(modc_ref)
    return lambda is_ctx: lat if is_ctx is None else jnp.where(is_ctx, ctx, lat)


def _head_rmsnorm(x, width):
    parts = []
    for c0 in range(0, x.shape[1], width):
        xh = x[:, c0:c0 + width]
        parts.append(xh * lax.rsqrt(jnp.mean(xh * xh, axis=-1, keepdims=True) + EPS))
    return parts[0] if len(parts) == 1 else jnp.concatenate(parts, axis=1)


def _sigmoid(x):
    return 1.0 / (1.0 + jnp.exp(-x))


def _chunk_rows(r0, rc, tiles_per_batch, bm):
    rows = r0 + lax.broadcasted_iota(jnp.int32, (rc, 1), 0)
    return rows, (pl.program_id(0) % tiles_per_batch) * bm + rows


def _edge_rows(r0, rc, bm, tiles_per_batch, n_ctx, n_seq, gate_ref, hprev_ref, hnext_ref):
    hr = hprev_ref.shape[0]
    pos0 = (pl.program_id(0) % tiles_per_batch) * bm
    if r0 == 0:
        before = hprev_ref[...].astype(F32)[hr - 1:hr, :]
        before = jnp.where((pos0 == 0) | (pos0 == n_ctx), 0.0, before)
    else:
        before = gate_ref[r0 - hr:r0, :].astype(F32)[hr - 1:hr, :]
    if r0 + rc == bm:
        after = hnext_ref[...].astype(F32)[0:1, :]
        after = jnp.where((pos0 + bm == n_ctx) | (pos0 + bm == n_seq), 0.0, after)
    else:
        after = gate_ref[r0 + rc:r0 + rc + hr, :].astype(F32)[0:1, :]
    return before, after


def _splits_rows(r0, rc, bm, tiles_per_batch, n_ctx):
    return any(r0 <= n_ctx - k * bm <= r0 + rc for k in range(tiles_per_batch))


def _dwconv3(s_ref, g, before, after, cw_ref, pos, n_ctx):
    rc = g.shape[0]
    s_ref[SUBLANES:SUBLANES + rc, :] = g
    s_ref[SUBLANES - 1:SUBLANES, :] = before
    s_ref[SUBLANES + rc:SUBLANES + rc + 1, :] = after
    prev = s_ref[SUBLANES - 1:SUBLANES - 1 + rc, :]
    nxt = s_ref[SUBLANES + 1:SUBLANES + 1 + rc, :]
    if pos is not None:
        prev = jnp.where(pos == n_ctx, 0.0, prev)
        nxt = jnp.where(pos == n_ctx - 1, 0.0, nxt)
    return prev * cw_ref[0:1, :] + g * cw_ref[1:2, :] + nxt * cw_ref[2:3, :]


def _mod_specs(n_batch, tiles_per_batch, d):
    return [pl.BlockSpec((1, 6, d), lambda i, *_: (i // tiles_per_batch, 0, 0)),
            pl.BlockSpec((1, 6, d), lambda i, *_: (n_batch, 0, 0))]


def _halo_specs(bm, width, col, n_rows, hr):
    hb = bm // hr
    last = n_rows // hr - 1
    return [pl.BlockSpec((hr, width), lambda i, *k: (jnp.maximum(i * hb - 1, 0), col(*k))),
            pl.BlockSpec((hr, width), lambda i, *k: (jnp.minimum((i + 1) * hb, last), col(*k)))]


def _norm_kernel(x_ref, gain_ref, modb_ref, modc_ref, o_ref, *, n_ctx, tiles_per_batch):
    gain_scale = _row_vectors(modb_ref, modc_ref, 1, gain_ref)
    shift = _row_vectors(modb_ref, modc_ref, 0)

    def rows(rs, is_ctx):
        o_ref[rs, :] = _norm_mod(x_ref[rs, :], gain_scale(is_ctx), shift(is_ctx)).astype(BF16)

    _for_row_groups(x_ref.shape[0], n_ctx, tiles_per_batch, rows)


def _first_norm(x, gain, mod, *, n_ctx, n_seq, bm=ROW_TILE):
    m, d = x.shape
    tpb = n_seq // bm
    return pl.pallas_call(
        functools.partial(_norm_kernel, n_ctx=n_ctx, tiles_per_batch=tpb),
        grid=(m // bm,),
        in_specs=[pl.BlockSpec((bm, d), lambda i: (i, 0)), pl.BlockSpec((1, d), lambda i: (0, 0))]
        + _mod_specs(mod.shape[0] - 1, tpb, d),
        out_specs=pl.BlockSpec((bm, d), lambda i: (i, 0)),
        out_shape=jax.ShapeDtypeStruct((m, d), BF16),
        compiler_params=pltpu.CompilerParams(vmem_limit_bytes=VMEM_LIMIT),
        name="first_norm",
    )(x, gain, mod, mod)


def _residual_norm(xo_ref, hn_ref, x_ref, modb_ref, modc_ref, gain_ref, nmodb_ref, nmodc_ref, *,
                   n_ctx, tiles_per_batch, gate_idx, shift_idx, inline=False):
    gate = _row_vectors(modb_ref, modc_ref, gate_idx)
    gain_scale = _row_vectors(nmodb_ref, nmodc_ref, shift_idx + 1, gain_ref)
    shift = _row_vectors(nmodb_ref, nmodc_ref, shift_idx)

    def rows(rs, is_ctx):
        x_new = x_ref[rs, :] + gate(is_ctx) * xo_ref[rs, :]
        xo_ref[rs, :] = x_new
        hn_ref[rs, :] = _norm_mod(x_new, gain_scale(is_ctx), shift(is_ctx)).astype(BF16)

    _for_row_groups(x_ref.shape[0], n_ctx, tiles_per_batch, rows, inline)


def _for_row_groups(bm, n_ctx, tiles_per_batch, fn, inline=False):
    rg = BF16_ROWS
    pos0 = (pl.program_id(0) % tiles_per_batch) * bm
    n_lead = -(-min(n_ctx, bm) // rg)
    if inline:
        for g in range(bm // rg):
            is_ctx = (pos0 + g * rg + lax.broadcasted_iota(jnp.int32, (rg, 1), 0) < n_ctx) if g < n_lead else None
            fn(slice(g * rg, (g + 1) * rg), is_ctx)
        return

    def group(with_ctx):
        def body(g, carry):
            r0 = pl.multiple_of(g * rg, rg)
            is_ctx = (pos0 + r0 + lax.broadcasted_iota(jnp.int32, (rg, 1), 0) < n_ctx) if with_ctx else None
            fn(pl.ds(r0, rg), is_ctx)
            return carry
        return body

    lax.fori_loop(0, n_lead, group(True), 0, unroll=ROW_GROUP_UNROLL)
    lax.fori_loop(n_lead, bm // rg, group(False), 0, unroll=ROW_GROUP_UNROLL)


def _out_kernel(*refs, n_in, n_ctx, tiles_per_batch):
    a_refs = refs[:n_in]
    w_ref, x_ref, modb_ref, modc_ref, gain_ref, xo_ref, hn_ref, wb_ref = refs[n_in:]

    @pl.when(pl.program_id(0) == 0)
    def _():
        for r0 in range(0, w_ref.shape[0], ROW_CHUNK):
            r1 = min(r0 + ROW_CHUNK, w_ref.shape[0])
            wb_ref[r0:r1, :] = w_ref[r0:r1, :].astype(BF16)

    k0 = 0
    for n_done, a_ref in enumerate(a_refs):
        part = jnp.dot(a_ref[...], wb_ref[k0:k0 + a_ref.shape[1], :], preferred_element_type=F32)
        xo_ref[...] = part if n_done == 0 else xo_ref[...] + part
        k0 += a_ref.shape[1]
    _residual_norm(xo_ref, hn_ref, x_ref, modb_ref, modc_ref, gain_ref, modb_ref, modc_ref,
                   n_ctx=n_ctx, tiles_per_batch=tiles_per_batch, gate_idx=2, shift_idx=3, inline=True)


def _out_proj(acts, w, layer, x, mod, gain, *, n_ctx, n_seq, bm=ROW_TILE // 2):
    m, d = x.shape
    kk = w.shape[1]
    tpb = n_seq // bm
    rows = lambda width: pl.BlockSpec((bm, width), lambda i: (i, 0))
    return pl.pallas_call(
        functools.partial(_out_kernel, n_in=len(acts), n_ctx=n_ctx, tiles_per_batch=tpb),
        grid=(m // bm,),
        in_specs=[rows(a.shape[1]) for a in acts]
        + [pl.BlockSpec((None, kk, d), lambda i: (layer, 0, 0), pipeline_mode=pl.Buffered(1)), rows(d)]
        + _mod_specs(mod.shape[0] - 1, tpb, d) + [pl.BlockSpec((1, d), lambda i: (0, 0))],
        out_specs=[rows(d), rows(d)],
        out_shape=[jax.ShapeDtypeStruct((m, d), F32), jax.ShapeDtypeStruct((m, d), BF16)],
        scratch_shapes=[pltpu.VMEM((kk, d), BF16)],
        compiler_params=pltpu.CompilerParams(
            dimension_semantics=("arbitrary",), vmem_limit_bytes=VMEM_LIMIT),
        name="out_proj",
    )(*acts, w, x, mod, mod, gain)


def _up_kernel(x_ref, xp_ref, xn_ref, wg_ref, wv_ref, cw_ref, cb_ref, wd_ref, o_ref, wdb_ref, wgb_ref,
               wvb_ref, xe_ref, s_ref, v_ref, *, n_ctx, n_seq, tiles_per_batch):
    i = pl.program_id(1)
    bm, rc, hr = x_ref.shape[0], ROW_CHUNK, xp_ref.shape[0]
    wdb_ref[...] = wd_ref[...].astype(BF16)

    @pl.when(i == 0)
    def _():
        wgb_ref[...] = wg_ref[...].astype(BF16)
        wvb_ref[...] = wv_ref[...].astype(BF16)

    xe_ref[0:hr, :] = xp_ref[...]
    xe_ref[hr:hr + bm, :] = x_ref[...]
    xe_ref[hr + bm:, :] = xn_ref[...]
    s_ref[...] = jnp.dot(xe_ref[...], wgb_ref[...], preferred_element_type=F32)
    v_ref[...] = jnp.dot(x_ref[...], wvb_ref[...], preferred_element_type=F32)
    pos0 = (i % tiles_per_batch) * bm
    s_ref[hr - 1:hr, :] = jnp.where((pos0 == 0) | (pos0 == n_ctx), 0.0, s_ref[hr - 1:hr, :])
    s_ref[hr + bm:hr + bm + 1, :] = jnp.where(
        (pos0 + bm == n_ctx) | (pos0 + bm == n_seq), 0.0, s_ref[hr + bm:hr + bm + 1, :])
    for r0 in range(0, bm, rc):
        rs = slice(r0, r0 + rc)
        prev = s_ref[hr - 1 + r0:hr - 1 + r0 + rc, :]
        nxt = s_ref[hr + 1 + r0:hr + 1 + r0 + rc, :]
        if _splits_rows(r0, rc, bm, tiles_per_batch, n_ctx):
            pos = pos0 + r0 + lax.broadcasted_iota(jnp.int32, (rc, 1), 0)
            prev = jnp.where(pos == n_ctx, 0.0, prev)
            nxt = jnp.where(pos == n_ctx - 1, 0.0, nxt)
        z = (prev * cw_ref[0:1, :] + s_ref[hr + r0:hr + r0 + rc, :] * cw_ref[1:2, :]
             + nxt * cw_ref[2:3, :] + cb_ref[...])
        o_ref[rs, :] = (z * _sigmoid(z) * v_ref[rs, :]).astype(BF16)


def _ffn_up(x, w, w_down, layer, conv_w, conv_b, *, n_ctx, n_seq, bn=512, bm=ROW_TILE):
    m, k = x.shape
    f = w.shape[2] // 2
    nj, ni = f // bn, m // bm
    slab = f // (nj * ni)
    assert slab * nj * ni == f and slab % BF16_ROWS == 0
    d_out = w_down.shape[2]
    tpb = n_seq // bm
    hb = bm // BF16_ROWS
    last = m // BF16_ROWS - 1
    return pl.pallas_call(
        functools.partial(_up_kernel, n_ctx=n_ctx, n_seq=n_seq, tiles_per_batch=tpb),
        grid=(nj, ni),
        in_specs=[pl.BlockSpec((bm, k), lambda j, i: (i, 0)),
                  pl.BlockSpec((BF16_ROWS, k), lambda j, i: (jnp.maximum(i * hb - 1, 0), 0)),
                  pl.BlockSpec((BF16_ROWS, k), lambda j, i: (jnp.minimum((i + 1) * hb, last), 0)),
                  pl.BlockSpec((None, k, bn), lambda j, i: (layer, 0, j)),
                  pl.BlockSpec((None, k, bn), lambda j, i: (layer, 0, j + nj)),
                  pl.BlockSpec((3, bn), lambda j, i: (0, j)), pl.BlockSpec((1, bn), lambda j, i: (0, j)),
                  pl.BlockSpec((None, slab, d_out), lambda j, i: (layer, j * ni + i, 0))],
        out_specs=[pl.BlockSpec((bm, bn), lambda j, i: (i, j)),
                   pl.BlockSpec((slab, d_out), lambda j, i: (j * ni + i, 0))],
        out_shape=[jax.ShapeDtypeStruct((m, f), BF16), jax.ShapeDtypeStruct((f, d_out), BF16)],
        scratch_shapes=[pltpu.VMEM((k, bn), BF16), pltpu.VMEM((k, bn), BF16),
                        pltpu.VMEM((bm + 2 * BF16_ROWS, k), BF16),
                        pltpu.VMEM((bm + 2 * BF16_ROWS, bn), F32), pltpu.VMEM((bm, bn), F32)],
        compiler_params=pltpu.CompilerParams(
            dimension_semantics=("arbitrary", "arbitrary"), vmem_limit_bytes=VMEM_LIMIT),
        name="ffn_up",
    )(x, x, x, w, w, conv_w, conv_b, w_down)


def _down_kernel(a_ref, w_ref, x_ref, modb_ref, modc_ref, gain_ref, nmodb_ref, nmodc_ref, xo_ref, hn_ref, *,
                 n_ctx, tiles_per_batch):
    part = jnp.dot(a_ref[...], w_ref[...], preferred_element_type=F32)

    @pl.when(pl.program_id(1) == 0)
    def _():
        xo_ref[...] = part

    @pl.when(pl.program_id(1) > 0)
    def _():
        xo_ref[...] += part

    @pl.when(pl.program_id(1) == pl.num_programs(1) - 1)
    def _():
        _residual_norm(xo_ref, hn_ref, x_ref, modb_ref, modc_ref, gain_ref, nmodb_ref, nmodc_ref,
                       n_ctx=n_ctx, tiles_per_batch=tiles_per_batch, gate_idx=5, shift_idx=0)


def _down_proj(a, w, x, mod, gain, next_mod, *, n_ctx, n_seq, bm=ROW_TILE // 2, bk=1408):
    m, f = a.shape
    d = w.shape[1]
    tpb = n_seq // bm
    row = lambda i, k: (i, 0)
    return pl.pallas_call(
        functools.partial(_down_kernel, n_ctx=n_ctx, tiles_per_batch=tpb),
        grid=(m // bm, f // bk),
        in_specs=[pl.BlockSpec((bm, bk), lambda i, k: (i, k)), pl.BlockSpec((bk, d), lambda i, k: (k, 0)),
                  pl.BlockSpec((bm, d), row)]
        + _mod_specs(mod.shape[0] - 1, tpb, d) + [pl.BlockSpec((1, d), lambda i, k: (0, 0))]
        + _mod_specs(mod.shape[0] - 1, tpb, d),
        out_specs=[pl.BlockSpec((bm, d), row), pl.BlockSpec((bm, d), row)],
        out_shape=[jax.ShapeDtypeStruct((m, d), F32), jax.ShapeDtypeStruct((m, d), BF16)],
        compiler_params=pltpu.CompilerParams(
            dimension_semantics=("arbitrary", "arbitrary"), vmem_limit_bytes=VMEM_LIMIT),
        name="down_proj",
    )(a, w, x, mod, mod, gain, next_mod, next_mod)


def _even_finish_kernel(of_ref, ob_ref, r_ref, sx_ref, sb_ref, sg_ref, sxp_ref, sxn_ref, sgp_ref, sgn_ref,
                        gain_ref, cw_ref, o_ref, s_ref, *, n_ctx, n_seq, tiles_per_batch):
    bm = of_ref.shape[0]
    nv = of_ref.shape[1]
    r = r_ref[...].astype(F32)
    o_ref[:, :nv] = (_head_rmsnorm(of_ref[...] + ob_ref[...], GLA_DV) * gain_ref[...]
                     * (r * _sigmoid(r))).astype(BF16)

    _, pos = _chunk_rows(0, bm, tiles_per_batch, bm)
    pos0 = (pl.program_id(0) % tiles_per_batch) * bm
    hr = sxp_ref.shape[0]
    f32 = lambda ref: ref[...].astype(F32)
    before = (f32(sxp_ref) * f32(sgp_ref))[hr - 1:hr, :]
    before = jnp.where((pos0 == 0) | (pos0 == n_ctx), 0.0, before)
    after = (f32(sxn_ref) * f32(sgn_ref))[0:1, :]
    after = jnp.where((pos0 + bm == n_ctx) | (pos0 + bm == n_seq), 0.0, after)
    split = pos if _splits_rows(0, bm, bm, tiles_per_batch, n_ctx) else None
    conv = _dwconv3(s_ref, f32(sg_ref) * f32(sx_ref), before, after, cw_ref, split, n_ctx)
    o_ref[:, nv:] = (f32(sb_ref) * conv).astype(BF16)


def _even_finish(o_f, o_b, pa, pb, gain, conv_w, *, n_ctx, n_seq, bm=ROW_CHUNK):
    m, nv = o_f.shape
    sc = conv_w.shape[1]
    tpb = n_seq // bm
    col = lambda j: (lambda i: (i, j))
    const = lambda i: (0, 0)
    return pl.pallas_call(
        functools.partial(_even_finish_kernel, n_ctx=n_ctx, n_seq=n_seq, tiles_per_batch=tpb),
        grid=(m // bm,),
        in_specs=[pl.BlockSpec((bm, nv), col(0)), pl.BlockSpec((bm, nv), col(0)),
                  pl.BlockSpec((bm, nv), col(2)),
                  pl.BlockSpec((bm, sc), col(0)), pl.BlockSpec((bm, sc), col(1)), pl.BlockSpec((bm, sc), col(2))]
        + _halo_specs(bm, sc, lambda: 0, m, BF16_ROWS) + _halo_specs(bm, sc, lambda: 2, m, BF16_ROWS)
        + [pl.BlockSpec((1, nv), const), pl.BlockSpec((3, sc), const)],
        out_specs=pl.BlockSpec((bm, nv + sc), col(0)),
        out_shape=jax.ShapeDtypeStruct((m, nv + sc), BF16),
        scratch_shapes=[pltpu.VMEM((bm + 2 * SUBLANES, sc), F32)],
        compiler_params=pltpu.CompilerParams(vmem_limit_bytes=VMEM_LIMIT),
        name="even_finish",
    )(o_f, o_b, pa, pb, pb, pb, pb, pb, pb, pb, gain, conv_w)


def _mlstm_prep_kernel(x_ref, xp_ref, xn_ref, cw_ref, scale_ref, o_ref, s_ref, *, n_ctx, n_seq,
                       tiles_per_batch):
    bm = x_ref.shape[0]
    _, pos = _chunk_rows(0, bm, tiles_per_batch, bm)
    before, after = _edge_rows(0, bm, bm, tiles_per_batch, n_ctx, n_seq, x_ref, xp_ref, xn_ref)
    split = pos if _splits_rows(0, bm, bm, tiles_per_batch, n_ctx) else None
    z = _dwconv3(s_ref, x_ref[...].astype(F32), before, after, cw_ref, split, n_ctx)
    o_ref[...] = (z * _sigmoid(z) * scale_ref[...]).astype(BF16)


def _mlstm_prep(p, conv_w, scale, *, col, n_ctx, n_seq, bm=ROW_CHUNK):
    m = p.shape[0]
    width = conv_w.shape[1]
    tpb = n_seq // bm
    const = lambda i: (0, 0)
    return pl.pallas_call(
        functools.partial(_mlstm_prep_kernel, n_ctx=n_ctx, n_seq=n_seq, tiles_per_batch=tpb),
        grid=(m // bm,),
        in_specs=[pl.BlockSpec((bm, width), lambda i: (i, col))]
        + _halo_specs(bm, width, lambda: col, m, BF16_ROWS)
        + [pl.BlockSpec((3, width), const), pl.BlockSpec((1, width), const)],
        out_specs=pl.BlockSpec((bm, width), lambda i: (i, 0)),
        out_shape=jax.ShapeDtypeStruct((m, width), BF16),
        scratch_shapes=[pltpu.VMEM((bm + 2 * SUBLANES, width), F32)],
        compiler_params=pltpu.CompilerParams(vmem_limit_bytes=VMEM_LIMIT),
        name="mlstm_prep",
    )(p, p, p, conv_w, scale)


def _mlstm_finish_kernel(hf_ref, hb_ref, mo_ref, gain_ref, o_ref):
    o_ref[...] = (_head_rmsnorm(hf_ref[...] + hb_ref[...], ML_DV) * gain_ref[...]
                  * _sigmoid(mo_ref[...].astype(F32))).astype(BF16)


def _mlstm_finish(h_f, h_b, p, gain, *, col, bm=ROW_TILE // 2):
    m, nv = h_f.shape
    return pl.pallas_call(
        _mlstm_finish_kernel,
        grid=(m // bm,),
        in_specs=[pl.BlockSpec((bm, nv), lambda i: (i, 0)), pl.BlockSpec((bm, nv), lambda i: (i, 0)),
                  pl.BlockSpec((bm, nv), lambda i: (i, col)), pl.BlockSpec((1, nv), lambda i: (0, 0))],
        out_specs=pl.BlockSpec((bm, nv), lambda i: (i, 0)),
        out_shape=jax.ShapeDtypeStruct((m, nv), BF16),
        compiler_params=pltpu.CompilerParams(vmem_limit_bytes=VMEM_LIMIT),
        name="mlstm_finish",
    )(h_f, h_b, p, gain)


def _qk_prep(x, cos, sin, gain):
    lane = lax.broadcasted_iota(jnp.int32, x.shape, 1)
    lo = lane < DA_DQK
    x2 = x * x
    ss_lo = jnp.sum(jnp.where(lo, x2, 0.0), axis=-1, keepdims=True)
    ss_hi = jnp.sum(jnp.where(lo, 0.0, x2), axis=-1, keepdims=True)
    y = x * lax.rsqrt(jnp.where(lo, ss_lo, ss_hi) * (1.0 / DA_DQK) + EPS) * gain
    swapped = jnp.where((lane & ROPE_PAIRS) == 0, pltpu.roll(y, LANES - ROPE_PAIRS, 1),
                        pltpu.roll(y, ROPE_PAIRS, 1))
    return y * cos + swapped * sin


def _attn_kernel(lam_ref, q_ref, k_ref, v_ref, cos_ref, sin_ref, qg_ref, kg_ref, og_ref, o_ref,
                 kb_ref, vb_ref, *, n_ctx, chunk, prep_rows, out_scale):
    qi = pl.program_id(2)
    lam = lam_ref[0]
    n = k_ref.shape[1]
    bq = q_ref.shape[1]
    heads = q_ref.shape[2] // LANES

    @pl.when(qi == 0)
    def _():
        for r0 in range(0, n, prep_rows):
            rs = slice(r0, r0 + prep_rows)
            for h in range(heads):
                hs = slice(h * LANES, (h + 1) * LANES)
                kb_ref[h, rs, :] = _qk_prep(k_ref[0, rs, hs].astype(F32), cos_ref[rs, :], sin_ref[rs, :],
                                            kg_ref[...]).astype(BF16)
                vb_ref[h, rs, :LANES] = v_ref[0, rs, hs].astype(BF16)
                vb_ref[h, rs, LANES:] = jnp.ones((prep_rows, LANES), BF16)

    def attend(r0, r1, nk):
        nr = r1 - r0
        rows = pl.ds(pl.multiple_of(qi * bq, BF16_ROWS) + r0, nr)
        cos, sin = cos_ref[rows, :], sin_ref[rows, :]
        lane = lax.broadcasted_iota(jnp.int32, (nr, LANES), 1)
        q2, m, acc = [], [], []
        for h in range(heads):
            q = _qk_prep(q_ref[0, r0:r1, h * LANES:(h + 1) * LANES].astype(F32), cos, sin, qg_ref[...])
            q = (q * (DA_DQK ** -0.5 * LOG2E)).astype(BF16)
            zero = jnp.zeros_like(q)
            q2.append(jnp.concatenate([jnp.where(lane < DA_DQK, q, zero),
                                       jnp.where(lane >= DA_DQK, q, zero)], axis=0))
            m.append(jnp.full((2 * nr, 1), -jnp.inf, F32))
            acc.append(jnp.zeros((2 * nr, 2 * LANES), F32))
        for c0 in range(0, nk, chunk):
            c1 = min(c0 + chunk, nk)
            for h in range(heads):
                s = lax.dot_general(q2[h], kb_ref[h, c0:c1, :], NT, preferred_element_type=F32)
                m_new = jnp.maximum(m[h], jnp.max(s, axis=-1, keepdims=True))
                p = jnp.exp2(s - m_new)
                acc[h] = jnp.exp2(m[h] - m_new) * acc[h] + jnp.dot(
                    p.astype(BF16), vb_ref[h, c0:c1, :], preferred_element_type=F32)
                m[h] = m_new
        for h in range(heads):
            o = acc[h][:, :LANES] / acc[h][:, LANES:]
            o = o[:nr] - lam * o[nr:]
            o = o * lax.rsqrt(jnp.mean(o * o, axis=-1, keepdims=True) + EPS) * og_ref[...] * out_scale
            o_ref[0, r0:r1, h * LANES:(h + 1) * LANES] = o.astype(BF16)

    last = pl.num_programs(2) - 1
    last_rows = n - (n // bq) * bq if n % bq else bq

    @pl.when(qi == 0)
    def _():
        attend(0, n_ctx, n_ctx)
        for r0 in range(n_ctx, bq, ATTN_SUB):
            attend(r0, r0 + ATTN_SUB, n)

    @pl.when((qi > 0) & (qi < last))
    def _():
        for r0 in range(0, bq, ATTN_SUB):
            attend(r0, r0 + ATTN_SUB, n)

    @pl.when((qi > 0) & (qi == last))
    def _():
        for r0 in range(0, last_rows, ATTN_SUB):
            attend(r0, r0 + ATTN_SUB, n)


def _diff_attention(p, lam, cos, sin, q_gain, k_gain, out_gain, *, n_ctx, out_scale, bq=ATTN_BQ,
                    chunk=ATTN_CHUNK, prep_rows=ATTN_PREP, heads=ATTN_HEADS):
    b, n, _ = p.shape
    assert n_ctx % ATTN_SUB == 0 and bq % ATTN_SUB == 0 and n % ATTN_SUB == 0 and n_ctx <= bq
    hg = DA_HEADS // heads
    hd = DA_HEADS * LANES
    hw = heads * LANES
    const = lambda bi, hi, qi: (0, 0)
    return pl.pallas_call(
        functools.partial(_attn_kernel, n_ctx=n_ctx, chunk=chunk, prep_rows=prep_rows, out_scale=out_scale),
        grid=(b, hg, pl.cdiv(n, bq)),
        in_specs=[pl.BlockSpec(memory_space=pltpu.SMEM),
                  pl.BlockSpec((1, bq, hw), lambda bi, hi, qi: (bi, qi, hi)),
                  pl.BlockSpec((1, n, hw), lambda bi, hi, qi: (bi, 0, hg + hi)),
                  pl.BlockSpec((1, n, hw), lambda bi, hi, qi: (bi, 0, 2 * hg + hi)),
                  pl.BlockSpec((n, LANES), const), pl.BlockSpec((n, LANES), const),
                  pl.BlockSpec((1, LANES), const), pl.BlockSpec((1, LANES), const),
                  pl.BlockSpec((1, LANES), const)],
        out_specs=pl.BlockSpec((1, bq, hw), lambda bi, hi, qi: (bi, qi, hi)),
        out_shape=jax.ShapeDtypeStruct((b, n, hd), BF16),
        scratch_shapes=[pltpu.VMEM((heads, n, LANES), BF16), pltpu.VMEM((heads, n, 2 * LANES), BF16)],
        compiler_params=pltpu.CompilerParams(
            dimension_semantics=("arbitrary", "arbitrary", "arbitrary"), vmem_limit_bytes=VMEM_LIMIT),
        name="diff_attention",
    )(lam, p, p, p, cos, sin, q_gain, k_gain, out_gain)


def _scan_chunk(t, n_ctx_chunks, n_chunks, reverse):
    if not reverse:
        return t
    return jnp.where(t < n_ctx_chunks, n_ctx_chunks - 1 - t, n_chunks - 1 - (t - n_ctx_chunks))


def _gla_constants(L, reverse):
    nlev = int(math.log2(L))
    idx = np.arange(L)
    i, t = idx[:, None], idx[None, :]
    if reverse:
        i, t = L - 1 - i, L - 1 - t
    rs = [(t <= i), (t > i)]
    am = [(i == t)]
    for lev in range(nlev):
        m = L >> (lev + 1)
        blk_i, blk_t = i // (2 * m), t // (2 * m)
        mid = blk_i * 2 * m + m
        q_role = i >= mid
        rs.append(np.where(q_role, (t >= mid) & (t <= i), (t > i) & (t < mid)) & (blk_i == blk_t))
        am.append((blk_i == blk_t) & q_role & (t < mid))
    return (np.stack(rs).astype(np.float32).reshape((nlev + 2) * L, L),
            np.stack(am).astype(np.float32))


def _gla_kernel(*refs):
    of_ref, ob_ref, st_ref = refs[16:]

    @pl.when(pl.program_id(1) == 0)
    def _():
        st_ref[...] = jnp.zeros_like(st_ref)

    for d, o_ref in enumerate((of_ref, ob_ref)):
        _gla_direction(*refs[8 * d:8 * d + 8], o_ref, st_ref, d * GLA_HEADS, bool(d))


def _gla_direction(q_ref, k_ref, v_ref, glr_ref, w2_ref, gb_ref, rsum_ref, amask_ref, o_ref, st_ref, h0,
                   reverse):
    L = q_ref.shape[1]
    nlev = amask_ref.shape[0] - 1
    n_coarse = nlev + 2 - rsum_ref.shape[0] // L

    z = jnp.dot(glr_ref[0].astype(BF16), w2_ref[...], preferred_element_type=F32) + gb_ref[...]
    g = (jnp.minimum(z, 0.0) - jnp.log(1.0 + jnp.exp(-jnp.abs(z)))) * (1.0 / GLA_TAU)
    g1 = g.astype(BF16)
    g2 = (g - g1.astype(F32)).astype(BF16)
    rsum = rsum_ref[...]
    e_all = (jnp.dot(rsum, g1, preferred_element_type=F32)
             + jnp.dot(rsum, g2, preferred_element_type=F32))
    b_tot = jnp.sum(g, axis=0, keepdims=True)

    def level_decay(lev, ck):
        if lev >= n_coarse:
            return jnp.exp(e_all[(2 + lev - n_coarse) * L:(3 + lev - n_coarse) * L, ck])
        m = L >> (lev + 1)
        b3 = e_all[0:L, ck].reshape(L // (2 * m), 2 * m, GLA_DK)
        pivot = m if reverse else m - 1
        return jnp.exp(-jnp.abs(b3 - b3[:, pivot:pivot + 1, :])).reshape(L, GLA_DK)

    for h in range(GLA_HEADS):
        ck = slice(h * GLA_DK, (h + 1) * GLA_DK)
        cv = slice(h * GLA_DV, (h + 1) * GLA_DV)
        q = q_ref[0, :, ck].astype(F32) * (GLA_DK ** -0.5)
        k = k_ref[0, :, ck].astype(F32)
        v = v_ref[0, :, cv].astype(BF16)
        st = st_ref[h0 + h]
        a = amask_ref[0] * lax.dot_general(q.astype(BF16), k.astype(BF16), NT, preferred_element_type=F32)
        for lev in range(nlev):
            e = level_decay(lev, ck)
            a = a + amask_ref[1 + lev] * lax.dot_general(
                (q * e).astype(BF16), (k * e).astype(BF16), NT, preferred_element_type=F32)
        qe = (q * jnp.exp(e_all[0:L, ck])).astype(BF16)
        o = lax.dot_general(qe, st.astype(BF16), NT, preferred_element_type=F32)
        o = o + jnp.dot(a.astype(BF16), v, preferred_element_type=F32)
        o_ref[0, :, cv] = o
        kd = (k * jnp.exp(e_all[L:2 * L, ck])).astype(BF16)
        st_ref[h0 + h] = jnp.exp(b_tot[:, ck]) * st + jnp.dot(
            v.T, kd, preferred_element_type=F32)


def _gla_scan(p, glr, w2s, gbs, *, n_ctx, L=GLA_CHUNK):
    b, n, _ = p.shape
    nc, ncc = n // L, n_ctx // L
    hk, hv = GLA_HEADS * GLA_DK, GLA_HEADS * GLA_DV
    const2 = lambda bi, t: (0, 0)
    const3 = lambda bi, t: (0, 0, 0)
    in_specs, operands, out_specs = [], [], []
    for d in range(2):
        rsum, amask = _gla_constants(L, bool(d))
        n_coarse = sum(2 * (L >> (lev + 1)) >= SUBLANES for lev in range(amask.shape[0] - 1))
        rsum = rsum.reshape(-1, L, L)
        rsum = np.concatenate([rsum[:2], rsum[2 + n_coarse:]]).reshape(-1, L)
        chunk = functools.partial(_scan_chunk, n_ctx_chunks=ncc, n_chunks=nc, reverse=bool(d))
        cols = lambda width, start, chunk=chunk: pl.BlockSpec(
            (1, L, width), lambda bi, t: (bi, chunk(t), start // width))
        in_specs += [cols(hk, 0), cols(hk, hk), cols(hv, 2 * hk), cols(LANES, 0),
                     pl.BlockSpec(w2s[d].shape, const2), pl.BlockSpec(gbs[d].shape, const2),
                     pl.BlockSpec(rsum.shape, const2), pl.BlockSpec(amask.shape, const3)]
        operands += [p, p, p, glr, w2s[d], gbs[d], jnp.asarray(rsum, BF16), jnp.asarray(amask)]
        out_specs.append(cols(hv, 0))
    return pl.pallas_call(
        _gla_kernel,
        grid=(b, nc),
        in_specs=in_specs,
        out_specs=out_specs,
        out_shape=[jax.ShapeDtypeStruct((b, n, hv), F32)] * 2,
        scratch_shapes=[pltpu.VMEM((2 * GLA_HEADS, GLA_DV, GLA_DK), F32)],
        compiler_params=pltpu.CompilerParams(
            dimension_semantics=("arbitrary", "arbitrary"), vmem_limit_bytes=VMEM_LIMIT),
        name="gla_scan",
    )(*operands)


def _log_sigmoid(x):
    return jnp.minimum(x, 0.0) - jnp.log(1.0 + jnp.exp(-jnp.abs(x)))


def _mlstm_kernel(*refs):
    tri_ref, of_ref, ob_ref, c_ref, m_ref = refs[10:]

    @pl.when(pl.program_id(0) == 0)
    def _():
        c_ref[...] = jnp.zeros_like(c_ref)
        m_ref[...] = jnp.zeros_like(m_ref)

    for bi in range(of_ref.shape[0]):
        for d, o_ref in enumerate((of_ref, ob_ref)):
            _mlstm_direction(*refs[5 * d:5 * d + 5], tri_ref[d], o_ref, c_ref, m_ref, bi,
                             (2 * bi + d) * ML_HEADS, bool(d))


def _mlstm_direction(q_ref, k_ref, v_ref, gc_ref, gr_ref, tri, o_ref, c_ref, m_ref, bi, h0, reverse):
    L = q_ref.shape[1]
    H = ML_HEADS
    ones = jnp.ones((L, LANES), BF16)
    gc = gc_ref[bi]
    gr = gr_ref[bi]
    ic_col, ic_row = gc[:, :H], gr[:H, :]
    b_col = jnp.dot(tri, _log_sigmoid(gc[:, H:]), precision=HI, preferred_element_type=F32)
    b_row = lax.dot_general(_log_sigmoid(gr[H:, :]), tri, NT, precision=HI, preferred_element_type=F32)
    last = 0 if reverse else L - 1
    causal = tri > 0.5

    for h in range(H):
        ck = slice(h * ML_DK, (h + 1) * ML_DK)
        cv = slice(h * ML_DV, (h + 1) * ML_DV)
        q = q_ref[bi, :, ck]
        k = k_ref[bi, :, ck]
        v = jnp.concatenate([v_ref[bi, :, cv].astype(BF16), ones], axis=1)
        c, m = c_ref[h0 + h], m_ref[h0 + h]
        bc, br = b_col[:, h:h + 1], b_row[h:h + 1, :]
        icc, icr = ic_col[:, h:h + 1], ic_row[h:h + 1, :]
        b_last = bc[last:last + 1, :]

        a = bc + m
        dmat = jnp.where(causal, bc - br + icr, -jnp.inf)
        m_t = jnp.maximum(a, jnp.max(dmat, axis=-1, keepdims=True))
        w_inter = jnp.exp(a - m_t)
        s = lax.dot_general(q, k, NT, preferred_element_type=F32) * jnp.exp(dmat - m_t)
        both = w_inter * jnp.dot(q, c.astype(BF16), preferred_element_type=F32) + jnp.dot(
            s.astype(BF16), v, preferred_element_type=F32)
        den = jnp.maximum(jnp.abs(both[:, ML_DV:]), jnp.exp(-m_t))
        o_ref[bi, :, cv] = both[:, :ML_DV] / jnp.concatenate([den] * (ML_DV // LANES), axis=1)

        gs_col = b_last - bc + icc
        gs_row = b_last - br + icr
        m_new = jnp.maximum(b_last + m, jnp.max(gs_row, axis=-1, keepdims=True))
        decay = jnp.exp(b_last + m - m_new)
        wk = jnp.exp(gs_col - m_new) * k.astype(F32)
        c_ref[h0 + h] = decay * c + jnp.dot(wk.astype(BF16).T, v, preferred_element_type=F32)
        m_ref[h0 + h] = m_new


def _mlstm_scan(qk, p, gates, *, v_col, n_ctx, L=ML_CHUNK):
    b, n, _ = qk.shape
    hk, hv, ng = ML_HEADS * ML_DK, ML_HEADS * ML_DV, 2 * ML_HEADS
    nc, ncc = n // L, n_ctx // L
    idx = np.arange(L)
    lower = idx[None, :] <= idx[:, None]
    tri = jnp.asarray(np.stack([lower, lower.T]).astype(np.float32))
    in_specs, operands, out_specs = [], [], []
    for d in range(2):
        chunk = functools.partial(_scan_chunk, n_ctx_chunks=ncc, n_chunks=nc, reverse=bool(d))
        col = lambda j, chunk=chunk: (lambda t: (0, chunk(t), j))
        g_dir = gates[..., d * ng:(d + 1) * ng]
        in_specs += [pl.BlockSpec((b, L, hk), col(0)), pl.BlockSpec((b, L, hk), col(1)),
                     pl.BlockSpec((b, L, hv), col(v_col)), pl.BlockSpec((b, L, ng), col(0)),
                     pl.BlockSpec((b, ng, L), lambda t, chunk=chunk: (0, 0, chunk(t)))]
        operands += [qk, qk, p, g_dir, jnp.swapaxes(g_dir, 1, 2)]
        out_specs.append(pl.BlockSpec((b, L, hv), col(0)))
    return pl.pallas_call(
        _mlstm_kernel,
        grid=(nc,),
        in_specs=in_specs + [pl.BlockSpec((2, L, L), lambda t: (0, 0, 0))],
        out_specs=out_specs,
        out_shape=[jax.ShapeDtypeStruct((b, n, hv), F32)] * 2,
        scratch_shapes=[pltpu.VMEM((b * 2 * ML_HEADS, ML_DK, ML_DV + LANES), F32),
                        pltpu.VMEM((b * 2 * ML_HEADS, 1, 1), F32)],
        compiler_params=pltpu.CompilerParams(
            dimension_semantics=("arbitrary",), vmem_limit_bytes=VMEM_LIMIT),
        name="mlstm_scan",
    )(*operands, tri)


def _silu(x):
    return x * jax.nn.sigmoid(x)


def _rope_tables(n_lat, n_ctx):
    rows = n_lat // GRID_W
    row = jnp.repeat(jnp.arange(rows, dtype=F32), GRID_W)
    col = jnp.tile(jnp.arange(GRID_W, dtype=F32), rows)
    inv = jnp.power(ROPE_BASE, -jnp.arange(ROPE_PAIRS, dtype=F32) / ROPE_PAIRS)
    ang_r, ang_c = row[:, None] * inv, col[:, None] * inv
    cos = jnp.concatenate([jnp.cos(ang_r)] * 2 + [jnp.cos(ang_c)] * 2, axis=-1)
    sin = jnp.concatenate([-jnp.sin(ang_r), jnp.sin(ang_r), -jnp.sin(ang_c), jnp.sin(ang_c)], axis=-1)
    pad = ((n_ctx, 0), (0, 0))
    cos, sin = jnp.pad(cos, pad, constant_values=1.0), jnp.pad(sin, pad)
    return jnp.tile(cos, (1, 2)), jnp.tile(sin, (1, 2))


def _even_mixer(hn, bsz, w_in, i, gate_w2, gate_b, gla_norm_g, sc_conv_w, seq):
    nq = GLA_HEADS * GLA_DK
    nv = GLA_HEADS * GLA_DV
    gate_col = 2 * nq + 2 * nv
    pa = _matmul(hn, w_in, i, bn=1024, ncols=gate_col, transposed=True, out_dtype=BF16)
    glr = _matmul(hn, w_in, i, bn=LANES, col0=gate_col, ncols=LANES, transposed=True)
    pb = _matmul(hn, w_in, i, bn=SC_WIDTH, col0=gate_col, ncols=3 * SC_WIDTH,
                 col_shift=2 * GLA_GATE_RANK, transposed=True, out_dtype=BF16)
    pa3 = pa.reshape(bsz, -1, pa.shape[1])
    glr3 = glr.reshape(bsz, -1, LANES)
    w2s = [jnp.zeros((LANES, nq), F32).at[d * GLA_GATE_RANK:(d + 1) * GLA_GATE_RANK].set(gate_w2[d])
           .astype(BF16) for d in range(2)]
    outs = [o.reshape(-1, nv) for o in
            _gla_scan(pa3, glr3, w2s, [gate_b[d][None, :] for d in range(2)], n_ctx=seq["n_ctx"])]
    gain = jnp.tile(gla_norm_g, GLA_HEADS)[None, :]
    return [_even_finish(outs[0], outs[1], pa, pb, gain, sc_conv_w, **seq)]


def _odd_mixer(hn, bsz, rope, layer, w_in, i, qn_g, kn_g, lam_p, subln_g, ml_conv_w, ml_gate_b,
               ml_norm_g, seq):
    lam_init = 0.8 - 0.6 * math.exp(-0.3 * layer)
    lam = (jnp.exp(jnp.sum(lam_p[0] * lam_p[1])) - jnp.exp(jnp.sum(lam_p[2] * lam_p[3])) + lam_init)
    na = DA_HEADS * 2 * DA_DQK
    nk = ML_HEADS * ML_DK
    nv = ML_HEADS * ML_DV
    n_main = 3 * na + 2 * nk + 2 * nv
    p = _matmul(hn, w_in, i, bn=1024, ncols=n_main, transposed=True, out_dtype=BF16)
    mg = _matmul(hn, w_in, i, bn=LANES, col0=n_main, ncols=LANES, transposed=True)[:, :4 * ML_HEADS]
    p3 = p.reshape(bsz, -1, n_main)
    n = p3.shape[1]

    cos, sin = rope
    tile2 = lambda g: jnp.tile(g, 2)[None, :]
    da = _diff_attention(p3, lam.reshape(1).astype(F32), cos, sin, tile2(qn_g), tile2(kn_g),
                         subln_g[None, :], n_ctx=seq["n_ctx"], out_scale=1.0 - lam_init)

    scale = jnp.concatenate([jnp.ones((nk,), F32), jnp.full((nk,), ML_DK ** -0.5, F32)])[None, :]
    hqk = _mlstm_prep(p, ml_conv_w, scale, col=3 * na // (2 * nk), **seq).reshape(bsz, n, 2 * nk)
    gates = (mg + ml_gate_b).reshape(bsz, n, 4 * ML_HEADS)
    outs = [o.reshape(-1, nv) for o in
            _mlstm_scan(hqk, p3, gates, v_col=(3 * na + 2 * nk) // nv, n_ctx=seq["n_ctx"])]
    m = _mlstm_finish(outs[0], outs[1], p, jnp.tile(ml_norm_g, ML_HEADS)[None, :],
                      col=(3 * na + 2 * nk + nv) // nv)
    return [da.reshape(-1, na), m]


def _modulation(c, c_ctx, ada_w, ada_b, layer):
    cc = jnp.concatenate([c, c_ctx[None, :]], axis=0)
    rows = cc.shape[0]
    act = jnp.pad(_silu(cc), ((0, BF16_ROWS - rows), (0, 0))).astype(BF16)
    mod = _matmul(act, ada_w, layer, bn=1024, bm=BF16_ROWS)[:rows] + ada_b[layer]
    return mod.reshape(rows, 6, -1)


def kernel(x, c, ctx, c_ctx, ada_w, ada_b, norm1_g, norm2_g, ev_w_in, ev_w_out, gla_gate_w2, gla_gate_b, gla_norm_g, sc_conv_w, od_w_in, od_w_out, da_qnorm_g, da_knorm_g, da_lambda, da_subln_g, ml_conv_w, ml_gate_b, ml_norm_g, ffn_w_up, ffn_conv_w, ffn_conv_b, ffn_w_down):
    bsz, n_lat, d = x.shape
    n_ctx = ctx.shape[1]
    n = n_ctx + n_lat
    depth = ada_w.shape[0]
    d_ff = ffn_w_down.shape[1]
    assert n % ROW_TILE == 0 and n_ctx <= ROW_CHUNK and d_ff % (4 * LANES) == 0
    rope = _rope_tables(n_lat, n_ctx)
    seq = dict(n_ctx=n_ctx, n_seq=n)

    ev_w_in = jnp.swapaxes(ev_w_in, 1, 2)
    od_w_in = jnp.swapaxes(od_w_in, 1, 2)
    mods = [_modulation(c, c_ctx, ada_w, ada_b, layer) for layer in range(depth)]
    xs = jnp.concatenate([ctx, x], axis=1).reshape(bsz * n, d)
    hn = _first_norm(xs, norm1_g[0][None, :], mods[0], **seq)
    for layer in range(depth):
        i = layer // 2
        if layer % 2 == 0:
            mix = _even_mixer(hn, bsz, ev_w_in, i, gla_gate_w2[i], gla_gate_b[i], gla_norm_g[i],
                              sc_conv_w[i], seq)
            w_out = ev_w_out
        else:
            mix = _odd_mixer(hn, bsz, rope, layer, od_w_in, i, da_qnorm_g[i], da_knorm_g[i],
                             da_lambda[i], da_subln_g[i], ml_conv_w[i], ml_gate_b[i], ml_norm_g[i], seq)
            w_out = od_w_out
        xs, hn = _out_proj(mix, w_out, i, xs, mods[layer], norm2_g[layer][None, :], **seq)
        act, w_down = _ffn_up(hn, ffn_w_up, ffn_w_down, layer, ffn_conv_w[layer],
                              ffn_conv_b[layer][None, :], **seq)
        nxt = min(layer + 1, depth - 1)
        xs, hn = _down_proj(act, w_down, xs,
                            mods[layer], norm1_g[nxt][None, :], mods[nxt], bk=d_ff // 2, **seq)
    return xs.reshape(bsz, n, d)[:, n_ctx:, :]
```

```python
import functools
import math

import jax
import jax.numpy as jnp
import numpy as np
from jax import lax
from jax.experimental import pallas as pl
from jax.experimental.pallas import tpu as pltpu

GRID_W = 64
EPS = 1e-6
GLA_HEADS, GLA_DK, GLA_DV, GLA_GATE_RANK, GLA_TAU = 4, 128, 256, 16, 16.0
SC_WIDTH = 1024
DA_HEADS, DA_DQK, DA_DV = 8, 64, 128
ROPE_BASE = 10000.0
ROPE_PAIRS = DA_DQK // 4
ML_HEADS, ML_DK, ML_DV = 4, 128, 256

GLA_CHUNK = 128
ML_CHUNK = 256
ATTN_BQ = 512
ATTN_SUB = 256
ATTN_CHUNK = 1024
ATTN_PREP = 544
ATTN_HEADS = 2
ROW_TILE = 1088
ROW_CHUNK = 272
LANES = 128
SUBLANES = 8
BF16_ROWS = 16
ROW_GROUP_UNROLL = 8
LOG2E = 1.4426950408889634
VMEM_LIMIT = 56 * 1024 * 1024

F32 = jnp.float32
BF16 = jnp.bfloat16
HI = lax.Precision.HIGHEST
NT = (((1,), (1,)), ((), ()))


def _mm_kernel(x_ref, w_ref, *rest, row_shift, transposed):
    o_ref, wb_ref = rest[-2:]

    @pl.when(pl.program_id(1) == 0)
    def _():
        if not transposed:
            wb_ref[...] = w_ref[...].astype(BF16)
            return
        bn = wb_ref.shape[1]
        piece = 2 * LANES
        for c0 in range(0, bn, piece):
            c1 = min(c0 + piece, bn)
            if row_shift and c1 == bn:
                rows = jnp.concatenate([w_ref[c0 + row_shift:bn, :], rest[0][...]], axis=0)
            else:
                rows = w_ref[c0 + row_shift:c1 + row_shift, :]
            wb_ref[:, c0:c1] = rows.T.astype(BF16)

    o_ref[...] = jnp.dot(x_ref[...], wb_ref[...], preferred_element_type=F32).astype(o_ref.dtype)


def _matmul(x, w, layer, *, bn, col0=0, ncols=None, col_shift=0, transposed=False, bm=ROW_TILE,
            out_dtype=F32):
    m, k = x.shape
    n_all = w.shape[1] if transposed else w.shape[2]
    ncols = n_all - col0 if ncols is None else ncols
    j0 = col0 // bn
    if transposed:
        w_spec = pl.BlockSpec((None, bn, k), lambda j, i: (layer, j0 + j, 0))
    else:
        w_spec = pl.BlockSpec((None, k, bn), lambda j, i: (layer, 0, j0 + j))
    in_specs = [pl.BlockSpec((bm, k), lambda j, i: (i, 0)), w_spec]
    operands = [x, w]
    if col_shift:
        per = bn // col_shift
        in_specs.append(pl.BlockSpec((None, col_shift, k), lambda j, i: (layer, (j0 + j + 1) * per, 0)))
        operands.append(w)
    return pl.pallas_call(
        functools.partial(_mm_kernel, row_shift=col_shift, transposed=transposed),
        grid=(pl.cdiv(ncols, bn), m // bm),
        in_specs=in_specs,
        out_specs=pl.BlockSpec((bm, bn), lambda j, i: (i, j)),
        out_shape=jax.ShapeDtypeStruct((m, ncols), out_dtype),
        scratch_shapes=[pltpu.VMEM((k, bn), BF16)],
        compiler_params=pltpu.CompilerParams(
            dimension_semantics=("arbitrary", "arbitrary"), vmem_limit_bytes=VMEM_LIMIT),
        name="matmul",
    )(*operands)


def _norm_mod(x, gain_scale, shift):
    return x * lax.rsqrt(jnp.mean(x * x, axis=-1, keepdims=True) + EPS) * gain_scale + shift


def _row_vectors(modb_ref, modc_ref, idx, gain_ref=None):
    def vec(ref):
        row = ref[0, idx:idx + 1, :]
        if gain_ref is not None:
            row = gain_ref[...] * (1.0 + row)
        return jnp.broadcast_to(row, (BF16_ROWS, row.shape[1]))

    lat, ctx = vec(modb_ref), vec(modc_ref)
    return lambda is_ctx: lat if is_ctx is None else jnp.where(is_ctx, ctx, lat)


def _head_rmsnorm(x, width):
    parts = []
    for c0 in range(0, x.shape[1], width):
        xh = x[:, c0:c0 + width]
        parts.append(xh * lax.rsqrt(jnp.mean(xh * xh, axis=-1, keepdims=True) + EPS))
    return parts[0] if len(parts) == 1 else jnp.concatenate(parts, axis=1)


def _sigmoid(x):
    return 1.0 / (1.0 + jnp.exp(-x))


def _chunk_rows(r0, rc, tiles_per_batch, bm):
    rows = r0 + lax.broadcasted_iota(jnp.int32, (rc, 1), 0)
    return rows, (pl.program_id(0) % tiles_per_batch) * bm + rows


def _edge_rows(r0, rc, bm, tiles_per_batch, n_ctx, n_seq, gate_ref, hprev_ref, hnext_ref):
    hr = hprev_ref.shape[0]
    pos0 = (pl.program_id(0) % tiles_per_batch) * bm
    if r0 == 0:
        before = hprev_ref[...].astype(F32)[hr - 1:hr, :]
        before = jnp.where((pos0 == 0) | (pos0 == n_ctx), 0.0, before)
    else:
        before = gate_ref[r0 - hr:r0, :].astype(F32)[hr - 1:hr, :]
    if r0 + rc == bm:
        after = hnext_ref[...].astype(F32)[0:1, :]
        after = jnp.where((pos0 + bm == n_ctx) | (pos0 + bm == n_seq), 0.0, after)
    else:
        after = gate_ref[r0 + rc:r0 + rc + hr, :].astype(F32)[0:1, :]
    return before, after


def _splits_rows(r0, rc, bm, tiles_per_batch, n_ctx):
    return any(r0 <= n_ctx - k * bm <= r0 + rc for k in range(tiles_per_batch))


def _dwconv3(s_ref, g, before, after, cw_ref, pos, n_ctx):
    rc = g.shape[0]
    s_ref[SUBLANES:SUBLANES + rc, :] = g
    s_ref[SUBLANES - 1:SUBLANES, :] = before
    s_ref[SUBLANES + rc:SUBLANES + rc + 1, :] = after
    prev = s_ref[SUBLANES - 1:SUBLANES - 1 + rc, :]
    nxt = s_ref[SUBLANES + 1:SUBLANES + 1 + rc, :]
    if pos is not None:
        prev = jnp.where(pos == n_ctx, 0.0, prev)
        nxt = jnp.where(pos == n_ctx - 1, 0.0, nxt)
    return prev * cw_ref[0:1, :] + g * cw_ref[1:2, :] + nxt * cw_ref[2:3, :]


def _mod_specs(n_batch, tiles_per_batch, d):
    return [pl.BlockSpec((1, 6, d), lambda i, *_: (i // tiles_per_batch, 0, 0)),
            pl.BlockSpec((1, 6, d), lambda i, *_: (n_batch, 0, 0))]


def _halo_specs(bm, width, col, n_rows, hr):
    hb = bm // hr
    last = n_rows // hr - 1
    return [pl.BlockSpec((hr, width), lambda i, *k: (jnp.maximum(i * hb - 1, 0), col(*k))),
            pl.BlockSpec((hr, width), lambda i, *k: (jnp.minimum((i + 1) * hb, last), col(*k)))]


def _norm_kernel(x_ref, gain_ref, modb_ref, modc_ref, o_ref, *, n_ctx, tiles_per_batch):
    gain_scale = _row_vectors(modb_ref, modc_ref, 1, gain_ref)
    shift = _row_vectors(modb_ref, modc_ref, 0)

    def rows(rs, is_ctx):
        o_ref[rs, :] = _norm_mod(x_ref[rs, :], gain_scale(is_ctx), shift(is_ctx)).astype(BF16)

    _for_row_groups(x_ref.shape[0], n_ctx, tiles_per_batch, rows)


def _first_norm(x, gain, mod, *, n_ctx, n_seq, bm=ROW_TILE):
    m, d = x.shape
    tpb = n_seq // bm
    return pl.pallas_call(
        functools.partial(_norm_kernel, n_ctx=n_ctx, tiles_per_batch=tpb),
        grid=(m // bm,),
        in_specs=[pl.BlockSpec((bm, d), lambda i: (i, 0)), pl.BlockSpec((1, d), lambda i: (0, 0))]
        + _mod_specs(mod.shape[0] - 1, tpb, d),
        out_specs=pl.BlockSpec((bm, d), lambda i: (i, 0)),
        out_shape=jax.ShapeDtypeStruct((m, d), BF16),
        compiler_params=pltpu.CompilerParams(vmem_limit_bytes=VMEM_LIMIT),
        name="first_norm",
    )(x, gain, mod, mod)


def _residual_norm(xo_ref, hn_ref, x_ref, modb_ref, modc_ref, gain_ref, nmodb_ref, nmodc_ref, *,
                   n_ctx, tiles_per_batch, gate_idx, shift_idx, inline=False):
    gate = _row_vectors(modb_ref, modc_ref, gate_idx)
    gain_scale = _row_vectors(nmodb_ref, nmodc_ref, shift_idx + 1, gain_ref)
    shift = _row_vectors(nmodb_ref, nmodc_ref, shift_idx)

    def rows(rs, is_ctx):
        x_new = x_ref[rs, :] + gate(is_ctx) * xo_ref[rs, :]
        xo_ref[rs, :] = x_new
        hn_ref[rs, :] = _norm_mod(x_new, gain_scale(is_ctx), shift(is_ctx)).astype(BF16)

    _for_row_groups(x_ref.shape[0], n_ctx, tiles_per_batch, rows, inline)


def _for_row_groups(bm, n_ctx, tiles_per_batch, fn, inline=False):
    rg = BF16_ROWS
    pos0 = (pl.program_id(0) % tiles_per_batch) * bm
    n_lead = -(-min(n_ctx, bm) // rg)
    if inline:
        for g in range(bm // rg):
            is_ctx = (pos0 + g * rg + lax.broadcasted_iota(jnp.int32, (rg, 1), 0) < n_ctx) if g < n_lead else None
            fn(slice(g * rg, (g + 1) * rg), is_ctx)
        return

    def group(with_ctx):
        def body(g, carry):
            r0 = pl.multiple_of(g * rg, rg)
            is_ctx = (pos0 + r0 + lax.broadcasted_iota(jnp.int32, (rg, 1), 0) < n_ctx) if with_ctx else None
            fn(pl.ds(r0, rg), is_ctx)
            return carry
        return body

    lax.fori_loop(0, n_lead, group(True), 0, unroll=ROW_GROUP_UNROLL)
    lax.fori_loop(n_lead, bm // rg, group(False), 0, unroll=ROW_GROUP_UNROLL)


def _out_kernel(*refs, n_in, n_ctx, tiles_per_batch):
    a_refs = refs[:n_in]
    w_ref, x_ref, modb_ref, modc_ref, gain_ref, xo_ref, hn_ref, wb_ref = refs[n_in:]

    @pl.when(pl.program_id(0) == 0)
    def _():
        for r0 in range(0, w_ref.shape[0], ROW_CHUNK):
            r1 = min(r0 + ROW_CHUNK, w_ref.shape[0])
            wb_ref[r0:r1, :] = w_ref[r0:r1, :].astype(BF16)

    k0 = 0
    for n_done, a_ref in enumerate(a_refs):
        part = jnp.dot(a_ref[...], wb_ref[k0:k0 + a_ref.shape[1], :], preferred_element_type=F32)
        xo_ref[...] = part if n_done == 0 else xo_ref[...] + part
        k0 += a_ref.shape[1]
    _residual_norm(xo_ref, hn_ref, x_ref, modb_ref, modc_ref, gain_ref, modb_ref, modc_ref,
                   n_ctx=n_ctx, tiles_per_batch=tiles_per_batch, gate_idx=2, shift_idx=3, inline=True)


def _out_proj(acts, w, layer, x, mod, gain, *, n_ctx, n_seq, bm=ROW_TILE // 2):
    m, d = x.shape
    kk = w.shape[1]
    tpb = n_seq // bm
    rows = lambda width: pl.BlockSpec((bm, width), lambda i: (i, 0))
    return pl.pallas_call(
        functools.partial(_out_kernel, n_in=len(acts), n_ctx=n_ctx, tiles_per_batch=tpb),
        grid=(m // bm,),
        in_specs=[rows(a.shape[1]) for a in acts]
        + [pl.BlockSpec((None, kk, d), lambda i: (layer, 0, 0), pipeline_mode=pl.Buffered(1)), rows(d)]
        + _mod_specs(mod.shape[0] - 1, tpb, d) + [pl.BlockSpec((1, d), lambda i: (0, 0))],
        out_specs=[rows(d), rows(d)],
        out_shape=[jax.ShapeDtypeStruct((m, d), F32), jax.ShapeDtypeStruct((m, d), BF16)],
        scratch_shapes=[pltpu.VMEM((kk, d), BF16)],
        compiler_params=pltpu.CompilerParams(
            dimension_semantics=("arbitrary",), vmem_limit_bytes=VMEM_LIMIT),
        name="out_proj",
    )(*acts, w, x, mod, mod, gain)


def _up_kernel(x_ref, xp_ref, xn_ref, wg_ref, wv_ref, cw_ref, cb_ref, wd_ref, o_ref, wdb_ref, wgb_ref,
               wvb_ref, xe_ref, s_ref, v_ref, *, n_ctx, n_seq, tiles_per_batch):
    i = pl.program_id(1)
    bm, rc, hr = x_ref.shape[0], ROW_CHUNK, xp_ref.shape[0]
    wdb_ref[...] = wd_ref[...].astype(BF16)

    @pl.when(i == 0)
    def _():
        wgb_ref[...] = wg_ref[...].astype(BF16)
        wvb_ref[...] = wv_ref[...].astype(BF16)

    xe_ref[0:hr, :] = xp_ref[...]
    xe_ref[hr:hr + bm, :] = x_ref[...]
    xe_ref[hr + bm:, :] = xn_ref[...]
    s_ref[...] = jnp.dot(xe_ref[...], wgb_ref[...], preferred_element_type=F32)
    v_ref[...] = jnp.dot(x_ref[...], wvb_ref[...], preferred_element_type=F32)
    pos0 = (i % tiles_per_batch) * bm
    s_ref[hr - 1:hr, :] = jnp.where((pos0 == 0) | (pos0 == n_ctx), 0.0, s_ref[hr - 1:hr, :])
    s_ref[hr + bm:hr + bm + 1, :] = jnp.where(
        (pos0 + bm == n_ctx) | (pos0 + bm == n_seq), 0.0, s_ref[hr + bm:hr + bm + 1, :])
    for r0 in range(0, bm, rc):
        rs = slice(r0, r0 + rc)
        prev = s_ref[hr - 1 + r0:hr - 1 + r0 + rc, :]
        nxt = s_ref[hr + 1 + r0:hr + 1 + r0 + rc, :]
        if _splits_rows(r0, rc, bm, tiles_per_batch, n_ctx):
            pos = pos0 + r0 + lax.broadcasted_iota(jnp.int32, (rc, 1), 0)
            prev = jnp.where(pos == n_ctx, 0.0, prev)
            nxt = jnp.where(pos == n_ctx - 1, 0.0, nxt)
        z = (prev * cw_ref[0:1, :] + s_ref[hr + r0:hr + r0 + rc, :] * cw_ref[1:2, :]
             + nxt * cw_ref[2:3, :] + cb_ref[...])
        o_ref[rs, :] = (z * _sigmoid(z) * v_ref[rs, :]).astype(BF16)


def _ffn_up(x, w, w_down, layer, conv_w, conv_b, *, n_ctx, n_seq, bn=512, bm=ROW_TILE):
    m, k = x.shape
    f = w.shape[2] // 2
    nj, ni = f // bn, m // bm
    slab = f // (nj * ni)
    assert slab * nj * ni == f and slab % BF16_ROWS == 0
    d_out = w_down.shape[2]
    tpb = n_seq // bm
    hb = bm // BF16_ROWS
    last = m // BF16_ROWS - 1
    return pl.pallas_call(
        functools.partial(_up_kernel, n_ctx=n_ctx, n_seq=n_seq, tiles_per_batch=tpb),
        grid=(nj, ni),
        in_specs=[pl.BlockSpec((bm, k), lambda j, i: (i, 0)),
                  pl.BlockSpec((BF16_ROWS, k), lambda j, i: (jnp.maximum(i * hb - 1, 0), 0)),
                  pl.BlockSpec((BF16_ROWS, k), lambda j, i: (jnp.minimum((i + 1) * hb, last), 0)),
                  pl.BlockSpec((None, k, bn), lambda j, i: (layer, 0, j)),
                  pl.BlockSpec((None, k, bn), lambda j, i: (layer, 0, j + nj)),
                  pl.BlockSpec((3, bn), lambda j, i: (0, j)), pl.BlockSpec((1, bn), lambda j, i: (0, j)),
                  pl.BlockSpec((None, slab, d_out), lambda j, i: (layer, j * ni + i, 0))],
        out_specs=[pl.BlockSpec((bm, bn), lambda j, i: (i, j)),
                   pl.BlockSpec((slab, d_out), lambda j, i: (j * ni + i, 0))],
        out_shape=[jax.ShapeDtypeStruct((m, f), BF16), jax.ShapeDtypeStruct((f, d_out), BF16)],
        scratch_shapes=[pltpu.VMEM((k, bn), BF16), pltpu.VMEM((k, bn), BF16),
                        pltpu.VMEM((bm + 2 * BF16_ROWS, k), BF16),
                        pltpu.VMEM((bm + 2 * BF16_ROWS, bn), F32), pltpu.VMEM((bm, bn), F32)],
        compiler_params=pltpu.CompilerParams(
            dimension_semantics=("arbitrary", "arbitrary"), vmem_limit_bytes=VMEM_LIMIT),
        name="ffn_up",
    )(x, x, x, w, w, conv_w, conv_b, w_down)


def _down_kernel(a_ref, w_ref, x_ref, modb_ref, modc_ref, gain_ref, nmodb_ref, nmodc_ref, xo_ref, hn_ref, *,
                 n_ctx, tiles_per_batch):
    part = jnp.dot(a_ref[...], w_ref[...], preferred_element_type=F32)

    @pl.when(pl.program_id(1) == 0)
    def _():
        xo_ref[...] = part

    @pl.when(pl.program_id(1) > 0)
    def _():
        xo_ref[...] += part

    @pl.when(pl.program_id(1) == pl.num_programs(1) - 1)
    def _():
        _residual_norm(xo_ref, hn_ref, x_ref, modb_ref, modc_ref, gain_ref, nmodb_ref, nmodc_ref,
                       n_ctx=n_ctx, tiles_per_batch=tiles_per_batch, gate_idx=5, shift_idx=0)


def _down_proj(a, w, x, mod, gain, next_mod, *, n_ctx, n_seq, bm=ROW_TILE // 2, bk=1408):
    m, f = a.shape
    d = w.shape[1]
    tpb = n_seq // bm
    row = lambda i, k: (i, 0)
    return pl.pallas_call(
        functools.partial(_down_kernel, n_ctx=n_ctx, tiles_per_batch=tpb),
        grid=(m // bm, f // bk),
        in_specs=[pl.BlockSpec((bm, bk), lambda i, k: (i, k)), pl.BlockSpec((bk, d), lambda i, k: (k, 0)),
                  pl.BlockSpec((bm, d), row)]
        + _mod_specs(mod.shape[0] - 1, tpb, d) + [pl.BlockSpec((1, d), lambda i, k: (0, 0))]
        + _mod_specs(mod.shape[0] - 1, tpb, d),
        out_specs=[pl.BlockSpec((bm, d), row), pl.BlockSpec((bm, d), row)],
        out_shape=[jax.ShapeDtypeStruct((m, d), F32), jax.ShapeDtypeStruct((m, d), BF16)],
        compiler_params=pltpu.CompilerParams(
            dimension_semantics=("arbitrary", "arbitrary"), vmem_limit_bytes=VMEM_LIMIT),
        name="down_proj",
    )(a, w, x, mod, mod, gain, next_mod, next_mod)


def _even_finish_kernel(of_ref, ob_ref, r_ref, sx_ref, sb_ref, sg_ref, sxp_ref, sxn_ref, sgp_ref, sgn_ref,
                        gain_ref, cw_ref, o_ref, s_ref, *, n_ctx, n_seq, tiles_per_batch):
    bm = of_ref.shape[0]
    nv = of_ref.shape[1]
    r = r_ref[...].astype(F32)
    o_ref[:, :nv] = (_head_rmsnorm(of_ref[...] + ob_ref[...], GLA_DV) * gain_ref[...]
                     * (r * _sigmoid(r))).astype(BF16)

    _, pos = _chunk_rows(0, bm, tiles_per_batch, bm)
    pos0 = (pl.program_id(0) % tiles_per_batch) * bm
    hr = sxp_ref.shape[0]
    f32 = lambda ref: ref[...].astype(F32)
    before = (f32(sxp_ref) * f32(sgp_ref))[hr - 1:hr, :]
    before = jnp.where((pos0 == 0) | (pos0 == n_ctx), 0.0, before)
    after = (f32(sxn_ref) * f32(sgn_ref))[0:1, :]
    after = jnp.where((pos0 + bm == n_ctx) | (pos0 + bm == n_seq), 0.0, after)
    split = pos if _splits_rows(0, bm, bm, tiles_per_batch, n_ctx) else None
    conv = _dwconv3(s_ref, f32(sg_ref) * f32(sx_ref), before, after, cw_ref, split, n_ctx)
    o_ref[:, nv:] = (f32(sb_ref) * conv).astype(BF16)


def _even_finish(o_f, o_b, pa, pb, gain, conv_w, *, n_ctx, n_seq, bm=ROW_CHUNK):
    m, nv = o_f.shape
    sc = conv_w.shape[1]
    tpb = n_seq // bm
    col = lambda j: (lambda i: (i, j))
    const = lambda i: (0, 0)
    return pl.pallas_call(
        functools.partial(_even_finish_kernel, n_ctx=n_ctx, n_seq=n_seq, tiles_per_batch=tpb),
        grid=(m // bm,),
        in_specs=[pl.BlockSpec((bm, nv), col(0)), pl.BlockSpec((bm, nv), col(0)),
                  pl.BlockSpec((bm, nv), col(2)),
                  pl.BlockSpec((bm, sc), col(0)), pl.BlockSpec((bm, sc), col(1)), pl.BlockSpec((bm, sc), col(2))]
        + _halo_specs(bm, sc, lambda: 0, m, BF16_ROWS) + _halo_specs(bm, sc, lambda: 2, m, BF16_ROWS)
        + [pl.BlockSpec((1, nv), const), pl.BlockSpec((3, sc), const)],
        out_specs=pl.BlockSpec((bm, nv + sc), col(0)),
        out_shape=jax.ShapeDtypeStruct((m, nv + sc), BF16),
        scratch_shapes=[pltpu.VMEM((bm + 2 * SUBLANES, sc), F32)],
        compiler_params=pltpu.CompilerParams(vmem_limit_bytes=VMEM_LIMIT),
        name="even_finish",
    )(o_f, o_b, pa, pb, pb, pb, pb, pb, pb, pb, gain, conv_w)


def _mlstm_prep_kernel(x_ref, xp_ref, xn_ref, cw_ref, scale_ref, o_ref, s_ref, *, n_ctx, n_seq,
                       tiles_per_batch):
    bm = x_ref.shape[0]
    _, pos = _chunk_rows(0, bm, tiles_per_batch, bm)
    before, after = _edge_rows(0, bm, bm, tiles_per_batch, n_ctx, n_seq, x_ref, xp_ref, xn_ref)
    split = pos if _splits_rows(0, bm, bm, tiles_per_batch, n_ctx) else None
    z = _dwconv3(s_ref, x_ref[...].astype(F32), before, after, cw_ref, split, n_ctx)
    o_ref[...] = (z * _sigmoid(z) * scale_ref[...]).astype(BF16)


def _mlstm_prep(p, conv_w, scale, *, col, n_ctx, n_seq, bm=ROW_CHUNK):
    m = p.shape[0]
    width = conv_w.shape[1]
    tpb = n_seq // bm
    const = lambda i: (0, 0)
    return pl.pallas_call(
        functools.partial(_mlstm_prep_kernel, n_ctx=n_ctx, n_seq=n_seq, tiles_per_batch=tpb),
        grid=(m // bm,),
        in_specs=[pl.BlockSpec((bm, width), lambda i: (i, col))]
        + _halo_specs(bm, width, lambda: col, m, BF16_ROWS)
        + [pl.BlockSpec((3, width), const), pl.BlockSpec((1, width), const)],
        out_specs=pl.BlockSpec((bm, width), lambda i: (i, 0)),
        out_shape=jax.ShapeDtypeStruct((m, width), BF16),
        scratch_shapes=[pltpu.VMEM((bm + 2 * SUBLANES, width), F32)],
        compiler_params=pltpu.CompilerParams(vmem_limit_bytes=VMEM_LIMIT),
        name="mlstm_prep",
    )(p, p, p, conv_w, scale)


def _mlstm_finish_kernel(hf_ref, hb_ref, mo_ref, gain_ref, o_ref):
    o_ref[...] = (_head_rmsnorm(hf_ref[...] + hb_ref[...], ML_DV) * gain_ref[...]
                  * _sigmoid(mo_ref[...].astype(F32))).astype(BF16)


def _mlstm_finish(h_f, h_b, p, gain, *, col, bm=ROW_TILE // 2):
    m, nv = h_f.shape
    return pl.pallas_call(
        _mlstm_finish_kernel,
        grid=(m // bm,),
        in_specs=[pl.BlockSpec((bm, nv), lambda i: (i, 0)), pl.BlockSpec((bm, nv), lambda i: (i, 0)),
                  pl.BlockSpec((bm, nv), lambda i: (i, col)), pl.BlockSpec((1, nv), lambda i: (0, 0))],
        out_specs=pl.BlockSpec((bm, nv), lambda i: (i, 0)),
        out_shape=jax.ShapeDtypeStruct((m, nv), BF16),
        compiler_params=pltpu.CompilerParams(vmem_limit_bytes=VMEM_LIMIT),
        name="mlstm_finish",
    )(h_f, h_b, p, gain)


def _qk_prep(x, cos, sin, gain):
    lane = lax.broadcasted_iota(jnp.int32, x.shape, 1)
    lo = lane < DA_DQK
    x2 = x * x
    ss_lo = jnp.sum(jnp.where(lo, x2, 0.0), axis=-1, keepdims=True)
    ss_hi = jnp.sum(jnp.where(lo, 0.0, x2), axis=-1, keepdims=True)
    y = x * lax.rsqrt(jnp.where(lo, ss_lo, ss_hi) * (1.0 / DA_DQK) + EPS) * gain
    swapped = jnp.where((lane & ROPE_PAIRS) == 0, pltpu.roll(y, LANES - ROPE_PAIRS, 1),
                        pltpu.roll(y, ROPE_PAIRS, 1))
    return y * cos + swapped * sin


def _attn_kernel(lam_ref, q_ref, k_ref, v_ref, cos_ref, sin_ref, qg_ref, kg_ref, og_ref, o_ref,
                 kb_ref, vb_ref, *, n_ctx, chunk, prep_rows, out_scale):
    qi = pl.program_id(2)
    lam = lam_ref[0]
    n = k_ref.shape[1]
    bq = q_ref.shape[1]
    heads = q_ref.shape[2] // LANES

    @pl.when(qi == 0)
    def _():
        for r0 in range(0, n, prep_rows):
            rs = slice(r0, r0 + prep_rows)
            for h in range(heads):
                hs = slice(h * LANES, (h + 1) * LANES)
                kb_ref[h, rs, :] = _qk_prep(k_ref[0, rs, hs].astype(F32), cos_ref[rs, :], sin_ref[rs, :],
                                            kg_ref[...]).astype(BF16)
                vb_ref[h, rs, :LANES] = v_ref[0, rs, hs].astype(BF16)
                vb_ref[h, rs, LANES:] = jnp.ones((prep_rows, LANES), BF16)

    def attend(r0, r1, nk):
        nr = r1 - r0
        rows = pl.ds(pl.multiple_of(qi * bq, BF16_ROWS) + r0, nr)
        cos, sin = cos_ref[rows, :], sin_ref[rows, :]
        lane = lax.broadcasted_iota(jnp.int32, (nr, LANES), 1)
        q2, m, acc = [], [], []
        for h in range(heads):
            q = _qk_prep(q_ref[0, r0:r1, h * LANES:(h + 1) * LANES].astype(F32), cos, sin, qg_ref[...])
            q = (q * (DA_DQK ** -0.5 * LOG2E)).astype(BF16)
            zero = jnp.zeros_like(q)
            q2.append(jnp.concatenate([jnp.where(lane < DA_DQK, q, zero),
                                       jnp.where(lane >= DA_DQK, q, zero)], axis=0))
            m.append(jnp.full((2 * nr, 1), -jnp.inf, F32))
            acc.append(jnp.zeros((2 * nr, 2 * LANES), F32))
        for c0 in range(0, nk, chunk):
            c1 = min(c0 + chunk, nk)
            for h in range(heads):
                s = lax.dot_general(q2[h], kb_ref[h, c0:c1, :], NT, preferred_element_type=F32)
                m_new = jnp.maximum(m[h], jnp.max(s, axis=-1, keepdims=True))
                p = jnp.exp2(s - m_new)
                acc[h] = jnp.exp2(m[h] - m_new) * acc[h] + jnp.dot(
                    p.astype(BF16), vb_ref[h, c0:c1, :], preferred_element_type=F32)
                m[h] = m_new
        for h in range(heads):
            o = acc[h][:, :LANES] / acc[h][:, LANES:]
            o = o[:nr] - lam * o[nr:]
            o = o * lax.rsqrt(jnp.mean(o * o, axis=-1, keepdims=True) + EPS) * og_ref[...] * out_scale
            o_ref[0, r0:r1, h * LANES:(h + 1) * LANES] = o.astype(BF16)

    last = pl.num_programs(2) - 1
    last_rows = n - (n // bq) * bq if n % bq else bq

    @pl.when(qi == 0)
    def _():
        attend(0, n_ctx, n_ctx)
        for r0 in range(n_ctx, bq, ATTN_SUB):
            attend(r0, r0 + ATTN_SUB, n)

    @pl.when((qi > 0) & (qi < last))
    def _():
        for r0 in range(0, bq, ATTN_SUB):
            attend(r0, r0 + ATTN_SUB, n)

    @pl.when((qi > 0) & (qi == last))
    def _():
        for r0 in range(0, last_rows, ATTN_SUB):
            attend(r0, r0 + ATTN_SUB, n)


def _diff_attention(p, lam, cos, sin, q_gain, k_gain, out_gain, *, n_ctx, out_scale, bq=ATTN_BQ,
                    chunk=ATTN_CHUNK, prep_rows=ATTN_PREP, heads=ATTN_HEADS):
    b, n, _ = p.shape
    assert n_ctx % ATTN_SUB == 0 and bq % ATTN_SUB == 0 and n % ATTN_SUB == 0 and n_ctx <= bq
    hg = DA_HEADS // heads
    hd = DA_HEADS * LANES
    hw = heads * LANES
    const = lambda bi, hi, qi: (0, 0)
    return pl.pallas_call(
        functools.partial(_attn_kernel, n_ctx=n_ctx, chunk=chunk, prep_rows=prep_rows, out_scale=out_scale),
        grid=(b, hg, pl.cdiv(n, bq)),
        in_specs=[pl.BlockSpec(memory_space=pltpu.SMEM),
                  pl.BlockSpec((1, bq, hw), lambda bi, hi, qi: (bi, qi, hi)),
                  pl.BlockSpec((1, n, hw), lambda bi, hi, qi: (bi, 0, hg + hi)),
                  pl.BlockSpec((1, n, hw), lambda bi, hi, qi: (bi, 0, 2 * hg + hi)),
                  pl.BlockSpec((n, LANES), const), pl.BlockSpec((n, LANES), const),
                  pl.BlockSpec((1, LANES), const), pl.BlockSpec((1, LANES), const),
                  pl.BlockSpec((1, LANES), const)],
        out_specs=pl.BlockSpec((1, bq, hw), lambda bi, hi, qi: (bi, qi, hi)),
        out_shape=jax.ShapeDtypeStruct((b, n, hd), BF16),
        scratch_shapes=[pltpu.VMEM((heads, n, LANES), BF16), pltpu.VMEM((heads, n, 2 * LANES), BF16)],
        compiler_params=pltpu.CompilerParams(
            dimension_semantics=("arbitrary", "arbitrary", "arbitrary"), vmem_limit_bytes=VMEM_LIMIT),
        name="diff_attention",
    )(lam, p, p, p, cos, sin, q_gain, k_gain, out_gain)


def _scan_chunk(t, n_ctx_chunks, n_chunks, reverse):
    if not reverse:
        return t
    return jnp.where(t < n_ctx_chunks, n_ctx_chunks - 1 - t, n_chunks - 1 - (t - n_ctx_chunks))


def _gla_constants(L, reverse):
    nlev = int(math.log2(L))
    idx = np.arange(L)
    i, t = idx[:, None], idx[None, :]
    if reverse:
        i, t = L - 1 - i, L - 1 - t
    rs = [(t <= i), (t > i)]
    am = [(i == t)]
    for lev in range(nlev):
        m = L >> (lev + 1)
        blk_i, blk_t = i // (2 * m), t // (2 * m)
        mid = blk_i * 2 * m + m
        q_role = i >= mid
        rs.append(np.where(q_role, (t >= mid) & (t <= i), (t > i) & (t < mid)) & (blk_i == blk_t))
        am.append((blk_i == blk_t) & q_role & (t < mid))
    return (np.stack(rs).astype(np.float32).reshape((nlev + 2) * L, L),
            np.stack(am).astype(np.float32))


def _gla_kernel(*refs):
    of_ref, ob_ref, st_ref = refs[16:]

    @pl.when(pl.program_id(1) == 0)
    def _():
        st_ref[...] = jnp.zeros_like(st_ref)

    for d, o_ref in enumerate((of_ref, ob_ref)):
        _gla_direction(*refs[8 * d:8 * d + 8], o_ref, st_ref, d * GLA_HEADS, bool(d))


def _gla_direction(q_ref, k_ref, v_ref, glr_ref, w2_ref, gb_ref, rsum_ref, amask_ref, o_ref, st_ref, h0,
                   reverse):
    L = q_ref.shape[1]
    nlev = amask_ref.shape[0] - 1
    n_coarse = nlev + 2 - rsum_ref.shape[0] // L

    z = jnp.dot(glr_ref[0].astype(BF16), w2_ref[...], preferred_element_type=F32) + gb_ref[...]
    g = (jnp.minimum(z, 0.0) - jnp.log(1.0 + jnp.exp(-jnp.abs(z)))) * (1.0 / GLA_TAU)
    g1 = g.astype(BF16)
    g2 = (g - g1.astype(F32)).astype(BF16)
    rsum = rsum_ref[...]
    e_all = (jnp.dot(rsum, g1, preferred_element_type=F32)
             + jnp.dot(rsum, g2, preferred_element_type=F32))
    b_tot = jnp.sum(g, axis=0, keepdims=True)

    def level_decay(lev, ck):
        if lev >= n_coarse:
            return jnp.exp(e_all[(2 + lev - n_coarse) * L:(3 + lev - n_coarse) * L, ck])
        m = L >> (lev + 1)
        b3 = e_all[0:L, ck].reshape(L // (2 * m), 2 * m, GLA_DK)
        pivot = m if reverse else m - 1
        return jnp.exp(-jnp.abs(b3 - b3[:, pivot:pivot + 1, :])).reshape(L, GLA_DK)

    for h in range(GLA_HEADS):
        ck = slice(h * GLA_DK, (h + 1) * GLA_DK)
        cv = slice(h * GLA_DV, (h + 1) * GLA_DV)
        q = q_ref[0, :, ck].astype(F32) * (GLA_DK ** -0.5)
        k = k_ref[0, :, ck].astype(F32)
        v = v_ref[0, :, cv].astype(BF16)
        st = st_ref[h0 + h]
        a = amask_ref[0] * lax.dot_general(q.astype(BF16), k.astype(BF16), NT, preferred_element_type=F32)
        for lev in range(nlev):
            e = level_decay(lev, ck)
            a = a + amask_ref[1 + lev] * lax.dot_general(
                (q * e).astype(BF16), (k * e).astype(BF16), NT, preferred_element_type=F32)
        qe = (q * jnp.exp(e_all[0:L, ck])).astype(BF16)
        o = lax.dot_general(qe, st.astype(BF16), NT, preferred_element_type=F32)
        o = o + jnp.dot(a.astype(BF16), v, preferred_element_type=F32)
        o_ref[0, :, cv] = o
        kd = (k * jnp.exp(e_all[L:2 * L, ck])).astype(BF16)
        st_ref[h0 + h] = jnp.exp(b_tot[:, ck]) * st + jnp.dot(
            v.T, kd, preferred_element_type=F32)


def _gla_scan(p, glr, w2s, gbs, *, n_ctx, L=GLA_CHUNK):
    b, n, _ = p.shape
    nc, ncc = n // L, n_ctx // L
    hk, hv = GLA_HEADS * GLA_DK, GLA_HEADS * GLA_DV
    const2 = lambda bi, t: (0, 0)
    const3 = lambda bi, t: (0, 0, 0)
    in_specs, operands, out_specs = [], [], []
    for d in range(2):
        rsum, amask = _gla_constants(L, bool(d))
        n_coarse = sum(2 * (L >> (lev + 1)) >= SUBLANES for lev in range(amask.shape[0] - 1))
        rsum = rsum.reshape(-1, L, L)
        rsum = np.concatenate([rsum[:2], rsum[2 + n_coarse:]]).reshape(-1, L)
        chunk = functools.partial(_scan_chunk, n_ctx_chunks=ncc, n_chunks=nc, reverse=bool(d))
        cols = lambda width, start, chunk=chunk: pl.BlockSpec(
            (1, L, width), lambda bi, t: (bi, chunk(t), start // width))
        in_specs += [cols(hk, 0), cols(hk, hk), cols(hv, 2 * hk), cols(LANES, 0),
                     pl.BlockSpec(w2s[d].shape, const2), pl.BlockSpec(gbs[d].shape, const2),
                     pl.BlockSpec(rsum.shape, const2), pl.BlockSpec(amask.shape, const3)]
        operands += [p, p, p, glr, w2s[d], gbs[d], jnp.asarray(rsum, BF16), jnp.asarray(amask)]
        out_specs.append(cols(hv, 0))
    return pl.pallas_call(
        _gla_kernel,
        grid=(b, nc),
        in_specs=in_specs,
        out_specs=out_specs,
        out_shape=[jax.ShapeDtypeStruct((b, n, hv), F32)] * 2,
        scratch_shapes=[pltpu.VMEM((2 * GLA_HEADS, GLA_DV, GLA_DK), F32)],
        compiler_params=pltpu.CompilerParams(
            dimension_semantics=("arbitrary", "arbitrary"), vmem_limit_bytes=VMEM_LIMIT),
        name="gla_scan",
    )(*operands)


def _log_sigmoid(x):
    return jnp.minimum(x, 0.0) - jnp.log(1.0 + jnp.exp(-jnp.abs(x)))


def _mlstm_kernel(*refs):
    tri_ref, of_ref, ob_ref, c_ref, m_ref = refs[10:]

    @pl.when(pl.program_id(0) == 0)
    def _():
        c_ref[...] = jnp.zeros_like(c_ref)
        m_ref[...] = jnp.zeros_like(m_ref)

    for bi in range(of_ref.shape[0]):
        for d, o_ref in enumerate((of_ref, ob_ref)):
            _mlstm_direction(*refs[5 * d:5 * d + 5], tri_ref[d], o_ref, c_ref, m_ref, bi,
                             (2 * bi + d) * ML_HEADS, bool(d))


def _mlstm_direction(q_ref, k_ref, v_ref, gc_ref, gr_ref, tri, o_ref, c_ref, m_ref, bi, h0, reverse):
    L = q_ref.shape[1]
    H = ML_HEADS
    ones = jnp.ones((L, LANES), BF16)
    gc = gc_ref[bi]
    gr = gr_ref[bi]
    ic_col, ic_row = gc[:, :H], gr[:H, :]
    b_col = jnp.dot(tri, _log_sigmoid(gc[:, H:]), precision=HI, preferred_element_type=F32)
    b_row = lax.dot_general(_log_sigmoid(gr[H:, :]), tri, NT, precision=HI, preferred_element_type=F32)
    last = 0 if reverse else L - 1
    causal = tri > 0.5

    for h in range(H):
        ck = slice(h * ML_DK, (h + 1) * ML_DK)
        cv = slice(h * ML_DV, (h + 1) * ML_DV)
        q = q_ref[bi, :, ck]
        k = k_ref[bi, :, ck]
        v = jnp.concatenate([v_ref[bi, :, cv].astype(BF16), ones], axis=1)
        c, m = c_ref[h0 + h], m_ref[h0 + h]
        bc, br = b_col[:, h:h + 1], b_row[h:h + 1, :]
        icc, icr = ic_col[:, h:h + 1], ic_row[h:h + 1, :]
        b_last = bc[last:last + 1, :]

        a = bc + m
        dmat = jnp.where(causal, bc - br + icr, -jnp.inf)
        m_t = jnp.maximum(a, jnp.max(dmat, axis=-1, keepdims=True))
        w_inter = jnp.exp(a - m_t)
        s = lax.dot_general(q, k, NT, preferred_element_type=F32) * jnp.exp(dmat - m_t)
        both = w_inter * jnp.dot(q, c.astype(BF16), preferred_element_type=F32) + jnp.dot(
            s.astype(BF16), v, preferred_element_type=F32)
        den = jnp.maximum(jnp.abs(both[:, ML_DV:]), jnp.exp(-m_t))
        o_ref[bi, :, cv] = both[:, :ML_DV] / jnp.concatenate([den] * (ML_DV // LANES), axis=1)

        gs_col = b_last - bc + icc
        gs_row = b_last - br + icr
        m_new = jnp.maximum(b_last + m, jnp.max(gs_row, axis=-1, keepdims=True))
        decay = jnp.exp(b_last + m - m_new)
        wk = jnp.exp(gs_col - m_new) * k.astype(F32)
        c_ref[h0 + h] = decay * c + jnp.dot(wk.astype(BF16).T, v, preferred_element_type=F32)
        m_ref[h0 + h] = m_new


def _mlstm_scan(qk, p, gates, *, v_col, n_ctx, L=ML_CHUNK):
    b, n, _ = qk.shape
    hk, hv, ng = ML_HEADS * ML_DK, ML_HEADS * ML_DV, 2 * ML_HEADS
    nc, ncc = n // L, n_ctx // L
    idx = np.arange(L)
    lower = idx[None, :] <= idx[:, None]
    tri = jnp.asarray(np.stack([lower, lower.T]).astype(np.float32))
    in_specs, operands, out_specs = [], [], []
    for d in range(2):
        chunk = functools.partial(_scan_chunk, n_ctx_chunks=ncc, n_chunks=nc, reverse=bool(d))
        col = lambda j, chunk=chunk: (lambda t: (0, chunk(t), j))
        g_dir = gates[..., d * ng:(d + 1) * ng]
        in_specs += [pl.BlockSpec((b, L, hk), col(0)), pl.BlockSpec((b, L, hk), col(1)),
                     pl.BlockSpec((b, L, hv), col(v_col)), pl.BlockSpec((b, L, ng), col(0)),
                     pl.BlockSpec((b, ng, L), lambda t, chunk=chunk: (0, 0, chunk(t)))]
        operands += [qk, qk, p, g_dir, jnp.swapaxes(g_dir, 1, 2)]
        out_specs.append(pl.BlockSpec((b, L, hv), col(0)))
    return pl.pallas_call(
        _mlstm_kernel,
        grid=(nc,),
        in_specs=in_specs + [pl.BlockSpec((2, L, L), lambda t: (0, 0, 0))],
        out_specs=out_specs,
        out_shape=[jax.ShapeDtypeStruct((b, n, hv), F32)] * 2,
        scratch_shapes=[pltpu.VMEM((b * 2 * ML_HEADS, ML_DK, ML_DV + LANES), F32),
                        pltpu.VMEM((b * 2 * ML_HEADS, 1, 1), F32)],
        compiler_params=pltpu.CompilerParams(
            dimension_semantics=("arbitrary",), vmem_limit_bytes=VMEM_LIMIT),
        name="mlstm_scan",
    )(*operands, tri)


def _silu(x):
    return x * jax.nn.sigmoid(x)


def _rope_tables(n_lat, n_ctx):
    rows = n_lat // GRID_W
    row = jnp.repeat(jnp.arange(rows, dtype=F32), GRID_W)
    col = jnp.tile(jnp.arange(GRID_W, dtype=F32), rows)
    inv = jnp.power(ROPE_BASE, -jnp.arange(ROPE_PAIRS, dtype=F32) / ROPE_PAIRS)
    ang_r, ang_c = row[:, None] * inv, col[:, None] * inv
    cos = jnp.concatenate([jnp.cos(ang_r)] * 2 + [jnp.cos(ang_c)] * 2, axis=-1)
    sin = jnp.concatenate([-jnp.sin(ang_r), jnp.sin(ang_r), -jnp.sin(ang_c), jnp.sin(ang_c)], axis=-1)
    pad = ((n_ctx, 0), (0, 0))
    cos, sin = jnp.pad(cos, pad, constant_values=1.0), jnp.pad(sin, pad)
    return jnp.tile(cos, (1, 2)), jnp.tile(sin, (1, 2))


def _even_mixer(hn, bsz, w_in, i, gate_w2, gate_b, gla_norm_g, sc_conv_w, seq):
    nq = GLA_HEADS * GLA_DK
    nv = GLA_HEADS * GLA_DV
    gate_col = 2 * nq + 2 * nv
    pa = _matmul(hn, w_in, i, bn=1024, ncols=gate_col, transposed=True, out_dtype=BF16)
    glr = _matmul(hn, w_in, i, bn=LANES, col0=gate_col, ncols=LANES, transposed=True, out_dtype=BF16)
    pb = _matmul(hn, w_in, i, bn=SC_WIDTH, col0=gate_col, ncols=3 * SC_WIDTH,
                 col_shift=2 * GLA_GATE_RANK, transposed=True, out_dtype=BF16)
    pa3 = pa.reshape(bsz, -1, pa.shape[1])
    glr3 = glr.reshape(bsz, -1, LANES)
    w2s = [jnp.zeros((LANES, nq), F32).at[d * GLA_GATE_RANK:(d + 1) * GLA_GATE_RANK].set(gate_w2[d])
           .astype(BF16) for d in range(2)]
    outs = [o.reshape(-1, nv) for o in
            _gla_scan(pa3, glr3, w2s, [gate_b[d][None, :] for d in range(2)], n_ctx=seq["n_ctx"])]
    gain = jnp.tile(gla_norm_g, GLA_HEADS)[None, :]
    return [_even_finish(outs[0], outs[1], pa, pb, gain, sc_conv_w, **seq)]


def _odd_mixer(hn, bsz, rope, layer, w_in, i, qn_g, kn_g, lam_p, subln_g, ml_conv_w, ml_gate_b,
               ml_norm_g, seq):
    lam_init = 0.8 - 0.6 * math.exp(-0.3 * layer)
    lam = (jnp.exp(jnp.sum(lam_p[0] * lam_p[1])) - jnp.exp(jnp.sum(lam_p[2] * lam_p[3])) + lam_init)
    na = DA_HEADS * 2 * DA_DQK
    nk = ML_HEADS * ML_DK
    nv = ML_HEADS * ML_DV
    n_main = 3 * na + 2 * nk + 2 * nv
    p = _matmul(hn, w_in, i, bn=1024, ncols=n_main, transposed=True, out_dtype=BF16)
    mg = _matmul(hn, w_in, i, bn=LANES, col0=n_main, ncols=LANES, transposed=True)[:, :4 * ML_HEADS]
    p3 = p.reshape(bsz, -1, n_main)
    n = p3.shape[1]

    cos, sin = rope
    tile2 = lambda g: jnp.tile(g, 2)[None, :]
    da = _diff_attention(p3, lam.reshape(1).astype(F32), cos, sin, tile2(qn_g), tile2(kn_g),
                         subln_g[None, :], n_ctx=seq["n_ctx"], out_scale=1.0 - lam_init)

    scale = jnp.concatenate([jnp.ones((nk,), F32), jnp.full((nk,), ML_DK ** -0.5, F32)])[None, :]
    hqk = _mlstm_prep(p, ml_conv_w, scale, col=3 * na // (2 * nk), **seq).reshape(bsz, n, 2 * nk)
    gates = (mg + ml_gate_b).reshape(bsz, n, 4 * ML_HEADS)
    outs = [o.reshape(-1, nv) for o in
            _mlstm_scan(hqk, p3, gates, v_col=(3 * na + 2 * nk) // nv, n_ctx=seq["n_ctx"])]
    m = _mlstm_finish(outs[0], outs[1], p, jnp.tile(ml_norm_g, ML_HEADS)[None, :],
                      col=(3 * na + 2 * nk + nv) // nv)
    return [da.reshape(-1, na), m]


def _modulation(c, c_ctx, ada_w, ada_b, layer):
    cc = jnp.concatenate([c, c_ctx[None, :]], axis=0)
    rows = cc.shape[0]
    act = jnp.pad(_silu(cc), ((0, BF16_ROWS - rows), (0, 0))).astype(BF16)
    mod = _matmul(act, ada_w, layer, bn=1024, bm=BF16_ROWS)[:rows] + ada_b[layer]
    return mod.reshape(rows, 6, -1)


def kernel(x, c, ctx, c_ctx, ada_w, ada_b, norm1_g, norm2_g, ev_w_in, ev_w_out, gla_gate_w2, gla_gate_b, gla_norm_g, sc_conv_w, od_w_in, od_w_out, da_qnorm_g, da_knorm_g, da_lambda, da_subln_g, ml_conv_w, ml_gate_b, ml_norm_g, ffn_w_up, ffn_conv_w, ffn_conv_b, ffn_w_down):
    bsz, n_lat, d = x.shape
    n_ctx = ctx.shape[1]
    n = n_ctx + n_lat
    depth = ada_w.shape[0]
    d_ff = ffn_w_down.shape[1]
    assert n % ROW_TILE == 0 and n_ctx <= ROW_CHUNK and d_ff % (4 * LANES) == 0
    rope = _rope_tables(n_lat, n_ctx)
    seq = dict(n_ctx=n_ctx, n_seq=n)

    ev_w_in = jnp.swapaxes(ev_w_in, 1, 2)
    od_w_in = jnp.swapaxes(od_w_in, 1, 2)
    mods = [_modulation(c, c_ctx, ada_w, ada_b, layer) for layer in range(depth)]
    xs = jnp.concatenate([ctx, x], axis=1).reshape(bsz * n, d)
    hn = _first_norm(xs, norm1_g[0][None, :], mods[0], **seq)
    for layer in range(depth):
        i = layer // 2
        if layer % 2 == 0:
            mix = _even_mixer(hn, bsz, ev_w_in, i, gla_gate_w2[i], gla_gate_b[i], gla_norm_g[i],
                              sc_conv_w[i], seq)
            w_out = ev_w_out
        else:
            mix = _odd_mixer(hn, bsz, rope, layer, od_w_in, i, da_qnorm_g[i], da_knorm_g[i],
                             da_lambda[i], da_subln_g[i], ml_conv_w[i], ml_gate_b[i], ml_norm_g[i], seq)
            w_out = od_w_out
        xs, hn = _out_proj(mix, w_out, i, xs, mods[layer], norm2_g[layer][None, :], **seq)
        act, w_down = _ffn_up(hn, ffn_w_up, ffn_w_down, layer, ffn_conv_w[layer],
                              ffn_conv_b[layer][None, :], **seq)
        nxt = min(layer + 1, depth - 1)
        xs, hn = _down_proj(act, w_down, xs,
                            mods[layer], norm1_g[nxt][None, :], mods[nxt], bk=d_ff // 2, **seq)
    return xs.reshape(bsz, n, d)[:, n_ctx:, :]
```
